```python
import math
import jax, jax.numpy as jnp
from jax import lax
import numpy as np

D_MODEL = 1024
BATCH = 4
SEQ = 8192
DEPTH = 2

GRID_W = 64
CTX_LEN = 256
HEAD_DIM = 64
NA_HEADS = 4
NA_ROWS = 8
NA_COLS = 16
MLA_HEADS = 4
MLA_Q_LORA = 256
MLA_KV_LORA = 192
MLA_NOPE = 64
MLA_ROPE = 32
MLA_V = 64
SG_GROUPS = 4
SG_CHUNK = 128
SG_WIDTH = 256
GQA_HEADS = 4
GQA_KV_HEADS = 2
N_BRANCH = 4
BRANCH_WIDTH = 256
FFN_DIM = 2816
CONV_WIDTH = 3

Q_BLOCK = 128
ROPE_THETA = 10000.0
EPS = 1e-6
NEG_INF = -1e30

NA_IN = 3 * NA_HEADS * HEAD_DIM
MLA_IN = MLA_Q_LORA + MLA_KV_LORA + MLA_ROPE
SG_IN = 2 * SG_WIDTH
GQA_IN = (GQA_HEADS + 2 * GQA_KV_HEADS) * HEAD_DIM
IN_WIDTH = NA_IN + MLA_IN + SG_IN + GQA_IN

kernel_name = "hybrid_natten_mla_gmlp_gqa_diffusion_block"


def rms(x):
    xf = x.astype(jnp.float32)
    return (xf * lax.rsqrt(jnp.mean(xf * xf, axis=-1, keepdims=True) + EPS)).astype(x.dtype)


def rope_1d(x, pos):
    half = x.shape[-1] // 2
    inv = ROPE_THETA ** (-jnp.arange(half, dtype=jnp.float32) / half)
    ang = pos.astype(jnp.float32)[:, None] * inv[None, :]
    shape = (ang.shape[0],) + (1,) * (x.ndim - 3) + (half,)
    cos = jnp.cos(ang).reshape(shape).astype(x.dtype)
    sin = jnp.sin(ang).reshape(shape).astype(x.dtype)
    x1, x2 = x[..., :half], x[..., half:]
    return jnp.concatenate([x1 * cos - x2 * sin, x2 * cos + x1 * sin], axis=-1)


def rope_2d(x, rows, cols):
    h = x.shape[-1] // 2
    return jnp.concatenate([rope_1d(x[..., :h], rows), rope_1d(x[..., h:], cols)], axis=-1)


def modulate(x, shift, scale):
    return rms(x) * (1 + scale[:, None]) + shift[:, None]


def softmax_f32(s):
    return jax.nn.softmax(s.astype(jnp.float32), axis=-1)


def prep_na(z, p):
    B, T, _ = z.shape
    qkv = z.reshape(B, T, 3, NA_HEADS, HEAD_DIM)
    q = rms(qkv[:, :, 0]) * p['na_q_norm']
    k = rms(qkv[:, :, 1]) * p['na_k_norm']
    return q, k, qkv[:, :, 2]


def prep_mla(z, p, pos):
    B, T, _ = z.shape
    cq, ckv, k_rope = jnp.split(z, [MLA_Q_LORA, MLA_Q_LORA + MLA_KV_LORA], axis=-1)
    q = ((rms(cq) * p['mla_cq_norm']) @ p['mla_w_uq']).reshape(B, T, MLA_HEADS, MLA_NOPE + MLA_ROPE)
    kv = ((rms(ckv) * p['mla_ckv_norm']) @ p['mla_w_ukv']).reshape(B, T, MLA_HEADS, MLA_NOPE + MLA_V)
    qg, kg = p['mla_q_norm'], p['mla_k_norm']
    q_nope = rms(q[..., :MLA_NOPE]) * qg[:MLA_NOPE]
    q_rope = rms(q[..., MLA_NOPE:]) * qg[MLA_NOPE:]
    k_nope = rms(kv[..., :MLA_NOPE]) * kg[:MLA_NOPE]
    k_rope = rms(k_rope) * kg[MLA_NOPE:]
    v = kv[..., MLA_NOPE:]
    if pos is not None:
        q_rope = rope_2d(q_rope, *pos)
        k_rope = rope_2d(k_rope, *pos)
    q = jnp.concatenate([q_nope, q_rope], axis=-1)[:, :, :, None]
    k = jnp.concatenate([k_nope, jnp.broadcast_to(k_rope[:, :, None], (B, T, MLA_HEADS, MLA_ROPE))], axis=-1)
    return q, k, v


def prep_gqa(z, p, pos):
    B, T, _ = z.shape
    q, k, v = jnp.split(z, [GQA_HEADS * HEAD_DIM, (GQA_HEADS + GQA_KV_HEADS) * HEAD_DIM], axis=-1)
    q = rms(q.reshape(B, T, GQA_KV_HEADS, GQA_HEADS // GQA_KV_HEADS, HEAD_DIM)) * p['gqa_q_norm']
    k = rms(k.reshape(B, T, GQA_KV_HEADS, HEAD_DIM)) * p['gqa_k_norm']
    v = v.reshape(B, T, GQA_KV_HEADS, HEAD_DIM)
    if pos is not None:
        q = rope_2d(q, *pos)
        k = rope_2d(k, *pos)
    return q, k, v


def token_mixing_inputs(h, p, pos):
    z = h @ p['w_in']
    z_na, z_mla, z_sg, z_gqa = jnp.split(z, [NA_IN, NA_IN + MLA_IN, NA_IN + MLA_IN + SG_IN], axis=-1)
    return prep_na(z_na, p), prep_mla(z_mla, p, pos), z_sg, prep_gqa(z_gqa, p, pos)


def attend_ctx(q, k, v, scale):
    s = jnp.einsum('bqgrd,bkgd->bgrqk', q, k) * scale
    p = softmax_f32(s).astype(v.dtype)
    o = jnp.einsum('bgrqk,bkgv->bqgrv', p, v)
    return o.reshape(o.shape[0], o.shape[1], -1)


def attend_latent_blocked(q, k_lat, v_lat, k_ctx, v_ctx, scale):
    B, T, G, R, dq = q.shape
    k_all = jnp.concatenate([k_ctx, k_lat], axis=1)
    v_all = jnp.concatenate([v_ctx, v_lat], axis=1)
    nblk = T // Q_BLOCK
    qb = jnp.moveaxis(q.reshape(B, nblk, Q_BLOCK, G, R, dq), 1, 0)

    def one_block(q_blk):
        s = jnp.einsum('bqgrd,bkgd->bgrqk', q_blk, k_all) * scale
        p = softmax_f32(s).astype(v_all.dtype)
        return jnp.einsum('bgrqk,bkgv->bqgrv', p, v_all)

    o = lax.map(one_block, qb)
    return jnp.moveaxis(o, 0, 1).reshape(B, T, -1)


def neighbourhood_attention(q, k, v, k_ctx, v_ctx, rpb, rows):
    B, T, H, d = q.shape
    kr, kc = min(NA_ROWS, rows), NA_COLS
    scale = d ** -0.5
    qg = q.reshape(B, rows, GRID_W, H, d)
    kg = k.reshape(B, rows, GRID_W, H, d)
    vg = v.reshape(B, rows, GRID_W, H, d)
    r = jnp.arange(rows)
    row_idx = jnp.clip(r - kr // 2, 0, rows - kr)[:, None] + jnp.arange(kr)[None, :]
    col = jnp.arange(GRID_W)
    c_start = jnp.clip(col - kc // 2, 0, GRID_W - kc)
    col_mask = (col[None, :] >= c_start[:, None]) & (col[None, :] < c_start[:, None] + kc)
    k_band = kg[:, row_idx]
    v_band = vg[:, row_idx]
    s_nb = (jnp.einsum('brqhd,brkwhd->bhrqkw', qg, k_band) * scale).astype(jnp.float32)
    dr = row_idx - r[:, None] + (NA_ROWS - 1)
    dc = jnp.clip(col[None, :] - col[:, None] + (NA_COLS - 1), 0, 2 * NA_COLS - 2)
    bias = jnp.take(rpb[:, dr], dc, axis=-1).transpose(0, 1, 3, 2, 4)
    s_nb = jnp.where(col_mask[:, None, :], s_nb + bias.astype(jnp.float32), NEG_INF)
    s_nb = s_nb.reshape(B, H, rows, GRID_W, kr * GRID_W)
    s_cx = (jnp.einsum('brqhd,bkhd->bhrqk', qg, k_ctx) * scale).astype(jnp.float32)
    p = softmax_f32(jnp.concatenate([s_nb, s_cx], axis=-1))
    p_nb = p[..., :kr * GRID_W].reshape(B, H, rows, GRID_W, kr, GRID_W).astype(v.dtype)
    p_cx = p[..., kr * GRID_W:].astype(v.dtype)
    o = (jnp.einsum('bhrqkw,brkwhd->brqhd', p_nb, v_band)
         + jnp.einsum('bhrqk,bkhd->brqhd', p_cx, v_ctx))
    return o.reshape(B, T, H * d)


def spatial_gating(z_sg, p):
    B, T, _ = z_sg.shape
    uv = jax.nn.gelu(z_sg)
    u, v = uv[..., :SG_WIDTH], uv[..., SG_WIDTH:]
    v = rms(v) * p['sg_v_norm']
    vc = v.reshape(B, T // SG_CHUNK, SG_CHUNK, SG_GROUPS, SG_WIDTH // SG_GROUPS)
    mixed = jnp.einsum('gpq,bnqgc->bnpgc', p['sg_w_s'], vc) + p['sg_b_s'].T[None, None, :, :, None]
    return u * mixed.reshape(B, T, SG_WIDTH)


def merge_branches(h, branches, p):
    y = 0
    for i, o in enumerate(branches):
        y = y + jax.nn.sigmoid(h @ p['w_gate'][i] + p['b_gate'][i]) * (o @ p['w_branch'][i])
    return y @ p['w_out']


def dwconv3(x, w, b):
    xp = jnp.pad(x, ((0, 0), (1, 1), (0, 0)))
    return xp[:, :-2] * w[0] + xp[:, 1:-1] * w[1] + xp[:, 2:] * w[2] + b


def conv_ffn(h, p):
    up = dwconv3(h @ p['w_up'], p['conv_w'], p['conv_b'])
    a, g = jnp.split(up, 2, axis=-1)
    return (jax.nn.silu(g) * a) @ p['w_down']


def layer(x, cx, c, c_ctx, p, pos, ctx_out):
    rows = x.shape[1] // GRID_W
    mod = jax.nn.silu(c) @ p['w_ada'] + p['b_ada']
    mod_c = (jax.nn.silu(c_ctx) @ p['w_ada'] + p['b_ada'])[None]
    sh1, sc1, g1, sh2, sc2, g2 = jnp.split(mod, 6, axis=-1)
    csh1, csc1, cg1, csh2, csc2, cg2 = jnp.split(mod_c, 6, axis=-1)

    h = modulate(x, sh1, sc1)
    hc = modulate(cx, csh1, csc1)
    na, mla, sg, gqa = token_mixing_inputs(h, p, pos)
    na_c, mla_c, sg_c, gqa_c = token_mixing_inputs(hc, p, None)

    o_na = neighbourhood_attention(na[0], na[1], na[2], na_c[1], na_c[2], p['na_rpb'], rows)
    o_mla = attend_latent_blocked(mla[0], mla[1], mla[2], mla_c[1], mla_c[2], (MLA_NOPE + MLA_ROPE) ** -0.5)
    o_sg = spatial_gating(sg, p)
    o_gqa = attend_latent_blocked(gqa[0], gqa[1], gqa[2], gqa_c[1], gqa_c[2], HEAD_DIM ** -0.5)
    x = x + g1[:, None] * merge_branches(h, (o_na, o_mla, o_sg, o_gqa), p)
    x = x + g2[:, None] * conv_ffn(modulate(x, sh2, sc2), p)

    if ctx_out:
        oc_na = attend_ctx(na_c[0][:, :, :, None], na_c[1], na_c[2], HEAD_DIM ** -0.5)
        oc_mla = attend_ctx(mla_c[0], mla_c[1], mla_c[2], (MLA_NOPE + MLA_ROPE) ** -0.5)
        oc_sg = spatial_gating(sg_c, p)
        oc_gqa = attend_ctx(gqa_c[0], gqa_c[1], gqa_c[2], HEAD_DIM ** -0.5)
        cx = cx + cg1[:, None] * merge_branches(hc, (oc_na, oc_mla, oc_sg, oc_gqa), p)
        cx = cx + cg2[:, None] * conv_ffn(modulate(cx, csh2, csc2), p)
    return x, cx


def setup_inputs(seed: int = 0) -> dict:
    key = jax.random.key(seed)
    ks = jax.random.split(key, 32)
    L, D = DEPTH, D_MODEL

    def nrm(k, shape, fan_in):
        return jax.random.normal(k, shape, jnp.float32) * fan_in ** -0.5

    def gain(k, shape):
        return 1.0 + 0.02 * jax.random.normal(k, shape, jnp.float32)

    def small(k, shape):
        return 0.02 * jax.random.normal(k, shape, jnp.float32)

    return {
        'x': jax.random.normal(ks[0], (BATCH, SEQ, D), jnp.float32),
        'c': jax.random.normal(ks[1], (BATCH, D), jnp.float32),
        'ctx': jax.random.normal(ks[2], (BATCH, CTX_LEN, D), jnp.float32),
        'c_ctx': jax.random.normal(ks[3], (D,), jnp.float32),
        'w_ada': nrm(ks[4], (L, D, 6 * D), D),
        'b_ada': small(ks[5], (L, 6 * D)),
        'w_in': nrm(ks[6], (L, D, IN_WIDTH), D),
        'na_q_norm': gain(ks[7], (L, HEAD_DIM)),
        'na_k_norm': gain(ks[8], (L, HEAD_DIM)),
        'na_rpb': small(ks[9], (L, NA_HEADS, 2 * NA_ROWS - 1, 2 * NA_COLS - 1)),
        'mla_cq_norm': gain(ks[10], (L, MLA_Q_LORA)),
        'mla_ckv_norm': gain(ks[11], (L, MLA_KV_LORA)),
        'mla_w_uq': nrm(ks[12], (L, MLA_Q_LORA, MLA_HEADS * (MLA_NOPE + MLA_ROPE)), MLA_Q_LORA),
        'mla_w_ukv': nrm(ks[13], (L, MLA_KV_LORA, MLA_HEADS * (MLA_NOPE + MLA_V)), MLA_KV_LORA),
        'mla_q_norm': gain(ks[14], (L, MLA_NOPE + MLA_ROPE)),
        'mla_k_norm': gain(ks[15], (L, MLA_NOPE + MLA_ROPE)),
        'sg_v_norm': gain(ks[16], (L, SG_WIDTH)),
        'sg_w_s': nrm(ks[17], (L, SG_GROUPS, SG_CHUNK, SG_CHUNK), SG_CHUNK),
        'sg_b_s': gain(ks[18], (L, SG_GROUPS, SG_CHUNK)),
        'gqa_q_norm': gain(ks[19], (L, HEAD_DIM)),
        'gqa_k_norm': gain(ks[20], (L, HEAD_DIM)),
        'w_branch': nrm(ks[21], (L, N_BRANCH, BRANCH_WIDTH, D), BRANCH_WIDTH),
        'w_gate': nrm(ks[22], (L, N_BRANCH, D, D), D),
        'b_gate': small(ks[23], (L, N_BRANCH, D)),
        'w_out': nrm(ks[24], (L, D, D), D),
        'w_up': nrm(ks[25], (L, D, 2 * FFN_DIM), D),
        'conv_w': nrm(ks[26], (L, CONV_WIDTH, 2 * FFN_DIM), CONV_WIDTH),
        'conv_b': small(ks[27], (L, 2 * FFN_DIM)),
        'w_down': nrm(ks[28], (L, FFN_DIM, D), FFN_DIM),
    }


def reference(x, c, ctx, c_ctx, w_ada, b_ada, w_in, na_q_norm, na_k_norm, na_rpb,
              mla_cq_norm, mla_ckv_norm, mla_w_uq, mla_w_ukv, mla_q_norm, mla_k_norm,
              sg_v_norm, sg_w_s, sg_b_s, gqa_q_norm, gqa_k_norm,
              w_branch, w_gate, b_gate, w_out, w_up, conv_w, conv_b, w_down):
    T = x.shape[1]
    t = jnp.arange(T)
    pos = (t // GRID_W, t % GRID_W)
    cx = ctx
    for l in range(DEPTH):
        p = {
            'w_ada': w_ada[l], 'b_ada': b_ada[l], 'w_in': w_in[l],
            'na_q_norm': na_q_norm[l], 'na_k_norm': na_k_norm[l], 'na_rpb': na_rpb[l],
            'mla_cq_norm': mla_cq_norm[l], 'mla_ckv_norm': mla_ckv_norm[l],
            'mla_w_uq': mla_w_uq[l], 'mla_w_ukv': mla_w_ukv[l],
            'mla_q_norm': mla_q_norm[l], 'mla_k_norm': mla_k_norm[l],
            'sg_v_norm': sg_v_norm[l], 'sg_w_s': sg_w_s[l], 'sg_b_s': sg_b_s[l],
            'gqa_q_norm': gqa_q_norm[l], 'gqa_k_norm': gqa_k_norm[l],
            'w_branch': w_branch[l], 'w_gate': w_gate[l], 'b_gate': b_gate[l], 'w_out': w_out[l],
            'w_up': w_up[l], 'conv_w': conv_w[l], 'conv_b': conv_b[l], 'w_down': w_down[l],
        }
        x, cx = layer(x, cx, c, c_ctx, p, pos, l < DEPTH - 1)
    return x
```

```python
import functools
import math

import numpy as np
import jax
import jax.numpy as jnp
from jax import lax
from jax.experimental import pallas as pl
from jax.experimental.pallas import tpu as pltpu

F32 = jnp.float32
BF16 = jnp.bfloat16

D_MODEL = 1024
GRID_W = 64
HEAD_DIM = 64
NA_HEADS = 4
NA_ROWS = 8
NA_COLS = 16
MLA_HEADS = 4
MLA_Q_LORA = 256
MLA_KV_LORA = 192
MLA_NOPE = 64
MLA_ROPE = 32
MLA_V = 64
SG_GROUPS = 4
SG_CHUNK = 128
SG_WIDTH = 256
GQA_HEADS = 4
GQA_KV_HEADS = 2
N_BRANCH = 4
BRANCH_WIDTH = 256
FFN_DIM = 2816
ROPE_THETA = 10000.0
EPS = 1e-6
NEG_INF = -1e30

NA_IN = 3 * NA_HEADS * HEAD_DIM
MLA_IN = MLA_Q_LORA + MLA_KV_LORA + MLA_ROPE
SG_IN = 2 * SG_WIDTH
GQA_IN = (GQA_HEADS + 2 * GQA_KV_HEADS) * HEAD_DIM

LANES = 128
MXU_DIM = 256
BF16_SUBLANES = 16
V_AUG = HEAD_DIM + BF16_SUBLANES
LOG2E = math.log2(math.e)

C_NA = 0
C_MLA = C_NA + NA_IN
C_SG = C_MLA + 640
C_GQA = C_SG + SG_IN
IN_ARR = C_GQA + 640

G_NA, G_CQ, G_CKV, G_MQ, G_MKN, G_MKR, G_SG, G_GQA, G_TOT = 0, 512, 768, 1024, 1536, 2048, 2176, 2432, 2944

FFN_CHUNK = 256
N_FFN_CHUNKS = FFN_DIM // FFN_CHUNK
HALO = BF16_SUBLANES


def _const_spec(shape):
    nd = len(shape)
    return pl.BlockSpec(shape, lambda *_: (0,) * nd, pipeline_mode=pl.Buffered(1))


def _params(vmem_mb, n_grid):
    return pltpu.CompilerParams(dimension_semantics=("arbitrary",) * n_grid,
                                vmem_limit_bytes=vmem_mb * 1024 * 1024)


def _dot(a, b):
    return jnp.dot(a, b, preferred_element_type=F32)


def _dot_nt(a, b):
    return lax.dot_general(a, b, (((1,), (1,)), ((), ())), preferred_element_type=F32)


def _rms(xf):
    return xf * lax.rsqrt(jnp.mean(xf * xf, axis=-1, keepdims=True) + EPS)


def _segsum(x2, s_ref):
    hi = x2.astype(BF16)
    lo = (x2 - hi.astype(F32)).astype(BF16)
    s = s_ref[...]
    parts = []
    for j in range(x2.shape[1] // MXU_DIM):
        sl = slice(MXU_DIM * j, MXU_DIM * (j + 1))
        parts.append(_dot(hi[:, sl], s) + _dot(lo[:, sl], s))
    return parts[0] if len(parts) == 1 else jnp.concatenate(parts, axis=1)


def _rope(x, cos, sin_signed, shift, first_half):
    w = x.shape[1]
    partner = jnp.where(first_half, pltpu.roll(x, w - shift, 1), pltpu.roll(x, shift, 1))
    return x * cos + partner * sin_signed


def _ada_kernel(c_ref, w_ref, b_ref, o_ref):
    c = c_ref[...]
    a = (c * jax.nn.sigmoid(c)).astype(BF16)
    o_ref[...] = _dot(a, w_ref[...].astype(BF16)) + b_ref[...]


def _ada_call(cvec, w_ada, b_ada):
    n_layers, d, n = w_ada.shape
    tn = 1536
    return pl.pallas_call(
        _ada_kernel,
        grid=(n_layers, n // tn),
        in_specs=[pl.BlockSpec((8, d), lambda l, j: (0, 0)),
                  pl.BlockSpec((None, d, tn), lambda l, j: (l, 0, j)),
                  pl.BlockSpec((None, 1, tn), lambda l, j: (l, 0, j))],
        out_specs=pl.BlockSpec((None, 8, tn), lambda l, j: (l, 0, j)),
        out_shape=jax.ShapeDtypeStruct((n_layers, 8, n), F32),
        compiler_params=_params(32, 2),
        name="ada",
    )(cvec, w_ada, b_ada.reshape(n_layers, 1, n))


def _in_kernel(x_ref, mod_ref, w_in_ref, w_uq_ref, w_ukv_ref, g_ref, s64_ref, smq_ref, invq_ref,
               gc_ref, gs_ref, mc_ref, ms_ref, sgw_ref, sgb_ref,
               naq_ref, nak_ref, nav_ref, mq_ref, mk_ref, mvt_ref, sgo_ref, gq_ref, gk_ref, gvt_ref):
    tm = x_ref.shape[0]
    x = x_ref[...]
    h = (_rms(x) * (1.0 + mod_ref[1:2, :]) + mod_ref[0:1, :]).astype(BF16)

    def gain(off, width):
        return g_ref[:, off:off + width]

    inv_head = 1.0 / HEAD_DIM

    z = _dot(h, w_in_ref[:, C_NA:C_NA + NA_IN])
    qk = z[:, 0:512]
    qk = qk * lax.rsqrt(_segsum(qk * qk, s64_ref) * inv_head + EPS) * gain(G_NA, 512)
    naq_ref[...] = qk[:, 0:256].astype(BF16)
    nak_ref[...] = qk[:, 256:512].astype(BF16)
    nav_ref[...] = z[:, 512:768].astype(BF16)

    z = _dot(h, w_in_ref[:, C_MLA:C_MLA + 640])
    cq, ckv, kr = z[:, 0:256], z[:, 256:512], z[:, 512:640]
    cq = cq * lax.rsqrt(jnp.mean(cq * cq, axis=-1, keepdims=True) + EPS) * gain(G_CQ, 256)
    ckv = ckv * lax.rsqrt(jnp.sum(ckv * ckv, axis=-1, keepdims=True) * (1.0 / MLA_KV_LORA) + EPS) * gain(G_CKV, 256)
    q = _dot(cq.astype(BF16), w_uq_ref[...])
    q = q * lax.rsqrt(_segsum(q * q, smq_ref) * invq_ref[...] + EPS) * gain(G_MQ, 512)
    kv = _dot(ckv.astype(BF16), w_ukv_ref[...])
    kn = kv[:, 0:512]
    kn = kn * lax.rsqrt(_segsum(kn * kn, s64_ref) * inv_head + EPS) * gain(G_MKN, 512)
    kr = kr * lax.rsqrt(jnp.sum(kr * kr, axis=-1, keepdims=True) * (1.0 / MLA_ROPE) + EPS) * gain(G_MKR, 128)
    lane = lax.broadcasted_iota(jnp.int32, (tm, LANES), 1)
    first8 = (lane & 15) < 8
    mc, ms = mc_ref[...], ms_ref[...]
    kr = _rope(kr, mc, ms, 8, first8)
    for hh in range(MLA_HEADS):
        sl = slice(LANES * hh, LANES * (hh + 1))
        mq_ref[hh] = _rope(q[:, sl], mc, ms, 8, first8).astype(BF16)
        mk_ref[hh] = (kn[:, sl] + kr).astype(BF16)
    vt = kv[:, 512:768].T
    ones = jnp.ones((BF16_SUBLANES, tm), BF16)
    for hh in range(MLA_HEADS):
        mvt_ref[hh, 0:HEAD_DIM, :] = vt[HEAD_DIM * hh:HEAD_DIM * (hh + 1), :].astype(BF16)
        mvt_ref[hh, HEAD_DIM:V_AUG, :] = ones

    z = _dot(h, w_in_ref[:, C_SG:C_SG + SG_IN])
    uv = jax.nn.gelu(z)
    u, v = uv[:, 0:SG_WIDTH], uv[:, SG_WIDTH:]
    v = v * lax.rsqrt(jnp.mean(v * v, axis=-1, keepdims=True) + EPS) * gain(G_SG, 256)
    grp = lax.broadcasted_iota(jnp.int32, (SG_CHUNK, SG_WIDTH), 1) >> 6
    for c in range(tm // SG_CHUNK):
        rows = slice(SG_CHUNK * c, SG_CHUNK * (c + 1))
        vc = v[rows, :]
        mixed = sgb_ref[...]
        for gi in range(SG_GROUPS):
            mixed = mixed + _dot(sgw_ref[gi], jnp.where(grp == gi, vc, 0.0).astype(BF16))
        sgo_ref[rows, :] = (u[rows, :] * mixed).astype(BF16)

    z = _dot(h, w_in_ref[:, C_GQA:C_GQA + 640])
    qk = z[:, 0:512]
    qk = qk * lax.rsqrt(_segsum(qk * qk, s64_ref) * inv_head + EPS) * gain(G_GQA, 512)
    lane2 = lax.broadcasted_iota(jnp.int32, (tm, 2 * LANES), 1)
    first16 = (lane2 & 31) < 16
    gc, gs = gc_ref[...], gs_ref[...]
    qn = _rope(qk[:, 0:256], gc, gs, 16, first16)
    kn = _rope(qk[:, 256:512], gc, gs, 16, first16)
    half = lane >> 6
    for g in range(GQA_KV_HEADS):
        sl = slice(LANES * g, LANES * (g + 1))
        for r in range(GQA_HEADS // GQA_KV_HEADS):
            gq_ref[2 * g + r] = jnp.where(half == r, qn[:, sl], 0.0).astype(BF16)
        gk_ref[g] = kn[:, sl].astype(BF16)
    vt = z[:, 512:640].T
    for g in range(GQA_KV_HEADS):
        gvt_ref[g, 0:HEAD_DIM, :] = vt[HEAD_DIM * g:HEAD_DIM * (g + 1), :].astype(BF16)
        gvt_ref[g, HEAD_DIM:V_AUG, :] = ones


def _in_call(x, mod, mod_row, lw, tabs, tm):
    b, t, d = x.shape
    nt = t // tm
    if mod_row is None:
        mod_map = lambda i, j: (j, 0, 0)
    else:
        mod_map = lambda i, j: (mod_row, 0, 0)
    tok = lambda i, j: (j, i, 0)
    tab = lambda i, j: (i, 0)
    head_tok = lambda i, j: (j, 0, i, 0)
    head_t = lambda i, j: (j, 0, 0, i)
    in_specs = [
        pl.BlockSpec((None, tm, d), tok),
        pl.BlockSpec((None, 6, d), mod_map),
        _const_spec((d, IN_ARR)),
        _const_spec((MLA_Q_LORA, 512)),
        _const_spec((256, 768)),
        _const_spec((1, G_TOT)),
        _const_spec((MXU_DIM, MXU_DIM)),
        _const_spec((MXU_DIM, MXU_DIM)),
        _const_spec((1, 512)),
        pl.BlockSpec((tm, 256), tab), pl.BlockSpec((tm, 256), tab),
        pl.BlockSpec((tm, LANES), tab), pl.BlockSpec((tm, LANES), tab),
        _const_spec((SG_GROUPS, SG_CHUNK, SG_CHUNK)),
        _const_spec((SG_CHUNK, SG_WIDTH)),
    ]
    out_specs = [
        pl.BlockSpec((None, tm, 256), tok), pl.BlockSpec((None, tm, 256), tok), pl.BlockSpec((None, tm, 256), tok),
        pl.BlockSpec((None, MLA_HEADS, tm, LANES), head_tok),
        pl.BlockSpec((None, MLA_HEADS, tm, LANES), head_tok),
        pl.BlockSpec((None, MLA_HEADS, V_AUG, tm), head_t),
        pl.BlockSpec((None, tm, 256), tok),
        pl.BlockSpec((None, GQA_HEADS, tm, LANES), head_tok),
        pl.BlockSpec((None, GQA_KV_HEADS, tm, LANES), head_tok),
        pl.BlockSpec((None, GQA_KV_HEADS, V_AUG, tm), head_t),
    ]
    sds = jax.ShapeDtypeStruct
    out_shape = [
        sds((b, t, 256), BF16), sds((b, t, 256), BF16), sds((b, t, 256), BF16),
        sds((b, MLA_HEADS, t, LANES), BF16), sds((b, MLA_HEADS, t, LANES), BF16),
        sds((b, MLA_HEADS, V_AUG, t), BF16),
        sds((b, t, 256), BF16),
        sds((b, GQA_HEADS, t, LANES), BF16), sds((b, GQA_KV_HEADS, t, LANES), BF16),
        sds((b, GQA_KV_HEADS, V_AUG, t), BF16),
    ]
    outs = pl.pallas_call(
        _in_kernel, grid=(nt, b), in_specs=in_specs, out_specs=out_specs, out_shape=out_shape,
        compiler_params=_params(48, 2), name="in_proj",
    )(x, mod, lw["w_in"], lw["w_uq"], lw["w_ukv"], lw["gains"], lw["s64"], lw["smq"], lw["invq"],
      tabs["gc"], tabs["gs"], tabs["mc"], tabs["ms"], lw["sgw"], lw["sgb"])
    names = ("na_q", "na_k", "na_v", "mla_q", "mla_k", "mla_vt", "sg_o", "gqa_q", "gqa_k", "gqa_vt")
    return dict(zip(names, outs))


def _na_heads(q, k_blocks, v_blocks, biases):
    tq = q.shape[0]
    head_of_lane = lax.broadcasted_iota(jnp.int32, (tq, 256), 1) >> 6
    qf = q.astype(F32) * (HEAD_DIM ** -0.5)
    out = jnp.zeros((tq, 256), F32)
    for hh in range(NA_HEADS):
        qh = jnp.where(head_of_lane == hh, qf, 0.0).astype(BF16)
        scores = []
        for kb, bias in zip(k_blocks, biases):
            s = _dot_nt(qh, kb)
            if bias is not None:
                s = s + bias[hh]
            scores.append(s)
        m = scores[0].max(axis=-1, keepdims=True)
        for s in scores[1:]:
            m = jnp.maximum(m, s.max(axis=-1, keepdims=True))
        l = jnp.zeros((tq, 1), F32)
        o = jnp.zeros((tq, 256), F32)
        for s, vb in zip(scores, v_blocks):
            p = jnp.exp(s - m)
            l = l + p.sum(axis=-1, keepdims=True)
            o = o + _dot(p.astype(BF16), vb)
        out = out + jnp.where(head_of_lane == hh, o * (1.0 / l), 0.0)
    return out


def _na_kernel(q_ref, kp_ref, kc_ref, kn_ref, vp_ref, vc_ref, vn_ref, kx_ref, vx_ref, bias_ref, o_ref):
    biases = [bias_ref.at[:, :, 0:256], bias_ref.at[:, :, 256:512], bias_ref.at[:, :, 512:768], None]
    out = _na_heads(q_ref[...], [kp_ref[...], kc_ref[...], kn_ref[...], kx_ref[...]],
                    [vp_ref[...], vc_ref[...], vn_ref[...], vx_ref[...]], biases)
    o_ref[...] = out.astype(BF16)


def _na_call(q, k, v, kx, vx, bias):
    b, t, w = q.shape
    c = kx.shape[1]
    tq = 4 * GRID_W
    nt = t // tq
    cur = lambda bi, i: (bi, i, 0)
    prev = lambda bi, i: (bi, jnp.maximum(i - 1, 0), 0)
    nxt = lambda bi, i: (bi, jnp.minimum(i + 1, nt - 1), 0)
    ctx = lambda bi, i: (bi, 0, 0)
    variant = lambda bi, i: (jnp.where(i == 0, 0, jnp.where(i == nt - 1, 2, 1)), 0, 0, 0)
    blk = lambda m: pl.BlockSpec((None, tq, w), m)
    return pl.pallas_call(
        _na_kernel, grid=(b, nt),
        in_specs=[blk(cur), blk(prev), blk(cur), blk(nxt), blk(prev), blk(cur), blk(nxt),
                  pl.BlockSpec((None, c, w), ctx), pl.BlockSpec((None, c, w), ctx),
                  pl.BlockSpec((None, NA_HEADS, tq, 3 * tq), variant)],
        out_specs=blk(cur),
        out_shape=jax.ShapeDtypeStruct((b, t, w), BF16),
        compiler_params=_params(48, 2), name="na_attn",
    )(q, k, k, k, v, v, v, kx, vx, bias)


def _na_ctx_kernel(q_ref, k_ref, v_ref, o_ref):
    o_ref[...] = _na_heads(q_ref[...], [k_ref[...]], [v_ref[...]], [None]).astype(BF16)


def _na_ctx_call(q, k, v):
    b, c, w = q.shape
    spec = pl.BlockSpec((None, c, w), lambda bi: (bi, 0, 0))
    return pl.pallas_call(
        _na_ctx_kernel, grid=(b,), in_specs=[spec, spec, spec], out_specs=spec,
        out_shape=jax.ShapeDtypeStruct((b, c, w), BF16),
        compiler_params=_params(32, 1), name="na_ctx_attn",
    )(q, k, v)


def _na_bias(rpb, nt):
    rows = 4 * nt
    col = np.arange(GRID_W)
    c_start = np.clip(col - NA_COLS // 2, 0, GRID_W - NA_COLS)
    valid_c = (col[None, :] >= c_start[:, None]) & (col[None, :] < c_start[:, None] + NA_COLS)
    dc = np.clip(col[None, :] - col[:, None] + (NA_COLS - 1), 0, 2 * NA_COLS - 2)
    variants = []
    for i in (0, min(1, nt - 1), nt - 1):
        rq = 4 * i + np.arange(4)
        start = np.clip(rq - NA_ROWS // 2, 0, rows - NA_ROWS)
        blocks = np.array([i - 1, i, i + 1])
        rk = (4 * blocks[:, None] + np.arange(4)[None, :]).reshape(-1)
        block_ok = np.repeat((blocks >= 0) & (blocks < nt), 4)
        valid_r = (rk[None, :] >= start[:, None]) & (rk[None, :] < start[:, None] + NA_ROWS) & block_ok[None, :]
        dr = np.clip(rk[None, :] - rq[:, None] + (NA_ROWS - 1), 0, 2 * NA_ROWS - 2)
        vals = rpb[:, dr[:, None, :, None], dc[None, :, None, :]]
        ok = valid_r[:, None, :, None] & valid_c[None, :, None, :]
        variants.append(jnp.where(ok[None], vals, NEG_INF).reshape(NA_HEADS, 4 * GRID_W, 12 * GRID_W))
    return jnp.stack(variants).astype(F32)


def _flash_kernel(*refs, n_heads, group, tk, c_scale, has_ctx):
    if has_ctx:
        q_ref, k_ref, vt_ref, kx_ref, vtx_ref, o_ref, ot_scr = refs
    else:
        q_ref, k_ref, vt_ref, o_ref, ot_scr = refs
    tq = q_ref.shape[1]
    n_chunks = k_ref.shape[1] // tk

    for hh in range(n_heads):
        slab = hh // group
        q = q_ref[hh]

        def step(k, vt, m, acc, q=q):
            s = _dot_nt(k, q)
            m_new = jnp.maximum(m, jnp.max(s, axis=0, keepdims=True))
            p = jnp.exp2((s - m_new) * c_scale)
            alpha = jnp.exp2((m - m_new) * c_scale)
            return m_new, acc * alpha + _dot(vt, p.astype(BF16))

        m = jnp.full((1, tq), NEG_INF, F32)
        acc = jnp.zeros((V_AUG, tq), F32)
        if has_ctx:
            m, acc = step(kx_ref[slab], vtx_ref[slab], m, acc)

        def body(j, carry, slab=slab, step=step):
            off = pl.multiple_of(j * tk, tk)
            return step(k_ref[slab, pl.ds(off, tk), :], vt_ref[slab, :, pl.ds(off, tk)], *carry)

        m, acc = lax.fori_loop(0, n_chunks, body, (m, acc))
        ot_scr[HEAD_DIM * hh:HEAD_DIM * (hh + 1), :] = acc[0:HEAD_DIM] * (1.0 / acc[HEAD_DIM:HEAD_DIM + 1])
    o_ref[...] = ot_scr[...].T.astype(BF16)


def _flash_call(q, k, vt, kx, vtx, scale, tq, tk):
    b, n_heads, t_q, w = q.shape
    slabs, t_k = k.shape[1], k.shape[2]
    has_ctx = kx is not None
    full4 = lambda bi, i: (bi, 0, 0, 0)
    in_specs = [pl.BlockSpec((None, n_heads, tq, w), lambda bi, i: (bi, 0, i, 0)),
                pl.BlockSpec((None, slabs, t_k, w), full4),
                pl.BlockSpec((None, slabs, V_AUG, t_k), full4)]
    args = [q, k, vt]
    if has_ctx:
        c = kx.shape[2]
        in_specs += [pl.BlockSpec((None, slabs, c, w), full4), pl.BlockSpec((None, slabs, V_AUG, c), full4)]
        args += [kx, vtx]
    kern = functools.partial(_flash_kernel, n_heads=n_heads, group=n_heads // slabs, tk=tk,
                             c_scale=scale * LOG2E, has_ctx=has_ctx)
    return pl.pallas_call(
        kern, grid=(b, t_q // tq), in_specs=in_specs,
        out_specs=pl.BlockSpec((None, tq, n_heads * HEAD_DIM), lambda bi, i: (bi, i, 0)),
        out_shape=jax.ShapeDtypeStruct((b, t_q, n_heads * HEAD_DIM), BF16),
        scratch_shapes=[pltpu.VMEM((n_heads * HEAD_DIM, tq), F32)],
        compiler_params=_params(56, 2), name="flash",
    )(*args)


def _merge_kernel(x_ref, mod_ref, o0_ref, o1_ref, o2_ref, o3_ref, wg_ref, bg_ref, wb_ref, wo_ref, out_ref):
    x = x_ref[...]
    h = (_rms(x) * (1.0 + mod_ref[1:2, :]) + mod_ref[0:1, :]).astype(BF16)
    y = None
    for i, o_ref in enumerate((o0_ref, o1_ref, o2_ref, o3_ref)):
        gate = jax.nn.sigmoid(_dot(h, wg_ref[i]) + bg_ref[i])
        term = gate * _dot(o_ref[...], wb_ref[i])
        y = term if y is None else y + term
    out_ref[...] = x + mod_ref[2:3, :] * _dot(y.astype(BF16), wo_ref[...])


def _merge_call(x, mod, mod_row, branches, lw, tm):
    b, t, d = x.shape
    if mod_row is None:
        mod_map = lambda i, j: (j, 0, 0)
    else:
        mod_map = lambda i, j: (mod_row, 0, 0)
    tok = lambda i, j: (j, i, 0)
    br = pl.BlockSpec((None, tm, BRANCH_WIDTH), tok)
    return pl.pallas_call(
        _merge_kernel, grid=(t // tm, b),
        in_specs=[pl.BlockSpec((None, tm, d), tok), pl.BlockSpec((None, 6, d), mod_map), br, br, br, br,
                  _const_spec((N_BRANCH, d, d)), _const_spec((N_BRANCH, 1, d)),
                  _const_spec((N_BRANCH, BRANCH_WIDTH, d)), _const_spec((d, d))],
        out_specs=pl.BlockSpec((None, tm, d), tok),
        out_shape=jax.ShapeDtypeStruct((b, t, d), F32),
        compiler_params=_params(56, 2), name="merge",
    )(x, mod, *branches, lw["w_gate"], lw["b_gate"], lw["w_branch"], lw["w_out"])


def _ffn_kernel(x_ref, xp_ref, xn_ref, mod_ref, wu_ref, cw_ref, cb_ref, wd_ref, out_ref, h_scr, acc_scr):
    tm = x_ref.shape[0]
    i = pl.program_id(0)
    nt = pl.num_programs(0)
    sh, sc = mod_ref[3:4, :], mod_ref[4:5, :]

    def modulated(v):
        return _rms(v) * (1.0 + sc) + sh

    x = x_ref[...]
    h_scr[0:HALO, :] = jnp.where(i > 0, modulated(xp_ref[...]), 0.0).astype(BF16)
    h_scr[HALO:HALO + tm, :] = modulated(x).astype(BF16)
    h_scr[HALO + tm:, :] = jnp.where(i < nt - 1, modulated(xn_ref[...]), 0.0).astype(BF16)
    acc_scr[...] = jnp.zeros_like(acc_scr)
    rows = tm + 2 * HALO

    def chunk(j, carry):
        up = _dot(h_scr[...], wu_ref[j])
        cw = cw_ref[j]
        conv = (pltpu.roll(up, 1, 0)[HALO:HALO + tm] * cw[0:1, :] + up[HALO:HALO + tm] * cw[1:2, :]
                + pltpu.roll(up, rows - 1, 0)[HALO:HALO + tm] * cw[2:3, :] + cb_ref[j])
        a, g = conv[:, 0:FFN_CHUNK], conv[:, FFN_CHUNK:]
        act = (g * jax.nn.sigmoid(g) * a).astype(BF16)
        acc_scr[...] += _dot(act, wd_ref[j])
        return carry

    lax.fori_loop(0, N_FFN_CHUNKS, chunk, 0)
    out_ref[...] = x + mod_ref[5:6, :] * acc_scr[...]


def _ffn_call(x, mod, mod_row, lw, tm):
    b, t, d = x.shape
    nt = t // tm
    hb = tm // HALO
    last_halo = t // HALO - 1
    if mod_row is None:
        mod_map = lambda i, j: (j, 0, 0)
    else:
        mod_map = lambda i, j: (mod_row, 0, 0)
    tok = lambda i, j: (j, i, 0)
    return pl.pallas_call(
        _ffn_kernel, grid=(nt, b),
        in_specs=[pl.BlockSpec((None, tm, d), tok),
                  pl.BlockSpec((None, HALO, d), lambda i, j: (j, jnp.maximum(i * hb - 1, 0), 0)),
                  pl.BlockSpec((None, HALO, d), lambda i, j: (j, jnp.minimum((i + 1) * hb, last_halo), 0)),
                  pl.BlockSpec((None, 6, d), mod_map),
                  _const_spec((N_FFN_CHUNKS, d, 2 * FFN_CHUNK)),
                  _const_spec((N_FFN_CHUNKS, 3, 2 * FFN_CHUNK)),
                  _const_spec((N_FFN_CHUNKS, 1, 2 * FFN_CHUNK)),
                  _const_spec((N_FFN_CHUNKS, FFN_CHUNK, d))],
        out_specs=pl.BlockSpec((None, tm, d), tok),
        out_shape=jax.ShapeDtypeStruct((b, t, d), F32),
        scratch_shapes=[pltpu.VMEM((tm + 2 * HALO, d), BF16), pltpu.VMEM((tm, d), F32)],
        compiler_params=_params(56, 2), name="ffn",
    )(x, x, x, mod, lw["w_up"], lw["conv_w"], lw["conv_b"], lw["w_down"])


def _block_diag_ones(width, segs):
    m = np.zeros((width, width), np.float32)
    for lo, hi in segs:
        m[lo:hi, lo:hi] = 1.0
    return jnp.asarray(m, BF16)


def _rope_pattern(pos, width):
    half = width // 2
    inv = ROPE_THETA ** (-jnp.arange(half, dtype=F32) / half)
    ang = pos.astype(F32)[:, None] * inv[None, :]
    cos, sin = jnp.cos(ang), jnp.sin(ang)
    return jnp.concatenate([cos, cos], axis=1), jnp.concatenate([-sin, sin], axis=1)


def _rope_tables(t, identity):
    if identity:
        return {"gc": jnp.ones((t, 256), F32), "gs": jnp.zeros((t, 256), F32),
                "mc": jnp.ones((t, LANES), F32), "ms": jnp.zeros((t, LANES), F32)}
    pos = jnp.arange(t)
    rows, cols = pos // GRID_W, pos % GRID_W
    rc, rs = _rope_pattern(rows, HEAD_DIM // 2)
    cc, cs = _rope_pattern(cols, HEAD_DIM // 2)
    gc = jnp.tile(jnp.concatenate([rc, cc], axis=1), (1, 4))
    gs = jnp.tile(jnp.concatenate([rs, cs], axis=1), (1, 4))
    rc, rs = _rope_pattern(rows, MLA_ROPE // 2)
    cc, cs = _rope_pattern(cols, MLA_ROPE // 2)
    one, zero = jnp.ones((t, MLA_NOPE), F32), jnp.zeros((t, MLA_NOPE), F32)
    pad1, pad0 = jnp.ones((t, LANES - MLA_NOPE - MLA_ROPE), F32), jnp.zeros((t, LANES - MLA_NOPE - MLA_ROPE), F32)
    return {"gc": gc, "gs": gs,
            "mc": jnp.concatenate([one, rc, cc, pad1], axis=1),
            "ms": jnp.concatenate([zero, rs, cs, pad0], axis=1)}


def _arrange_layer(p):
    d = D_MODEL
    w_in = p["w_in"]
    na, mla = w_in[:, :NA_IN], w_in[:, NA_IN:NA_IN + MLA_IN]
    sg = w_in[:, NA_IN + MLA_IN:NA_IN + MLA_IN + SG_IN]
    gqa = w_in[:, NA_IN + MLA_IN + SG_IN:]
    z = lambda n: jnp.zeros((d, n), F32)
    cq, ckv, kr = mla[:, :MLA_Q_LORA], mla[:, MLA_Q_LORA:MLA_Q_LORA + MLA_KV_LORA], mla[:, MLA_Q_LORA + MLA_KV_LORA:]
    gq, gk, gv = gqa[:, :256], gqa[:, 256:384], gqa[:, 384:]
    gk_dup = jnp.concatenate([gk[:, :64], gk[:, :64], gk[:, 64:], gk[:, 64:]], axis=1)
    w_in_arr = jnp.concatenate([na, cq, ckv, z(256 - MLA_KV_LORA), z(MLA_NOPE), kr, z(LANES - MLA_NOPE - MLA_ROPE),
                                sg, gq, gk_dup, gv], axis=1).astype(BF16)

    w_uq = p["mla_w_uq"].reshape(MLA_Q_LORA, MLA_HEADS, MLA_NOPE + MLA_ROPE)
    w_uq = jnp.pad(w_uq, ((0, 0), (0, 0), (0, LANES - MLA_NOPE - MLA_ROPE))).reshape(MLA_Q_LORA, MLA_HEADS * LANES)
    w_ukv = p["mla_w_ukv"].reshape(MLA_KV_LORA, MLA_HEADS, MLA_NOPE + MLA_V)
    k_part = jnp.pad(w_ukv[:, :, :MLA_NOPE], ((0, 0), (0, 0), (0, LANES - MLA_NOPE))).reshape(MLA_KV_LORA, MLA_HEADS * LANES)
    v_part = w_ukv[:, :, MLA_NOPE:].reshape(MLA_KV_LORA, MLA_HEADS * MLA_V)
    w_ukv = jnp.pad(jnp.concatenate([k_part, v_part], axis=1), ((0, 256 - MLA_KV_LORA), (0, 0)))

    qg, kg = p["mla_q_norm"], p["mla_k_norm"]
    z1 = lambda n: jnp.zeros((n,), F32)
    gains = jnp.concatenate([
        jnp.tile(p["na_q_norm"], 4), jnp.tile(p["na_k_norm"], 4),
        p["mla_cq_norm"],
        p["mla_ckv_norm"], z1(256 - MLA_KV_LORA),
        jnp.tile(jnp.concatenate([qg, z1(LANES - MLA_NOPE - MLA_ROPE)]), 4),
        jnp.tile(jnp.concatenate([kg[:MLA_NOPE], z1(LANES - MLA_NOPE)]), 4),
        z1(MLA_NOPE), kg[MLA_NOPE:], z1(LANES - MLA_NOPE - MLA_ROPE),
        p["sg_v_norm"],
        jnp.tile(p["gqa_q_norm"], 4), jnp.tile(p["gqa_k_norm"], 4),
    ]).reshape(1, G_TOT)

    invq = np.tile(np.concatenate([np.full(MLA_NOPE, 1.0 / MLA_NOPE), np.full(MLA_ROPE, 1.0 / MLA_ROPE),
                                   np.ones(LANES - MLA_NOPE - MLA_ROPE)]), 4).astype(np.float32).reshape(1, 512)
    sgb = jnp.repeat(p["sg_b_s"].T, SG_WIDTH // SG_GROUPS, axis=1)

    w_up = p["w_up"]
    a_part = w_up[:, :FFN_DIM].reshape(d, N_FFN_CHUNKS, FFN_CHUNK)
    g_part = w_up[:, FFN_DIM:].reshape(d, N_FFN_CHUNKS, FFN_CHUNK)
    w_up_arr = jnp.concatenate([a_part, g_part], axis=2).transpose(1, 0, 2).astype(BF16)

    def chunked(v):
        lead = v.shape[:-1]
        a = v[..., :FFN_DIM].reshape(*lead, N_FFN_CHUNKS, FFN_CHUNK)
        g = v[..., FFN_DIM:].reshape(*lead, N_FFN_CHUNKS, FFN_CHUNK)
        return jnp.moveaxis(jnp.concatenate([a, g], axis=-1), -2, 0)

    return {
        "w_in": w_in_arr, "w_uq": w_uq.astype(BF16), "w_ukv": w_ukv.astype(BF16), "gains": gains,
        "s64": _block_diag_ones(MXU_DIM, [(64 * i, 64 * i + 64) for i in range(4)]),
        "smq": _block_diag_ones(MXU_DIM, [(0, 64), (64, 96), (128, 192), (192, 224)]),
        "invq": jnp.asarray(invq),
        "sgw": p["sg_w_s"].astype(BF16), "sgb": sgb,
        "w_gate": p["w_gate"].astype(BF16), "b_gate": p["b_gate"].reshape(N_BRANCH, 1, d),
        "w_branch": p["w_branch"].astype(BF16), "w_out": p["w_out"].astype(BF16),
        "w_up": w_up_arr, "conv_w": chunked(p["conv_w"]), "conv_b": chunked(p["conv_b"].reshape(1, -1)),
        "w_down": p["w_down"].reshape(N_FFN_CHUNKS, FFN_CHUNK, d).astype(BF16),
    }


def kernel(x, c, ctx, c_ctx, w_ada, b_ada, w_in, na_q_norm, na_k_norm, na_rpb, mla_cq_norm, mla_ckv_norm,
           mla_w_uq, mla_w_ukv, mla_q_norm, mla_k_norm, sg_v_norm, sg_w_s, sg_b_s, gqa_q_norm, gqa_k_norm,
           w_branch, w_gate, b_gate, w_out, w_up, conv_w, conv_b, w_down):
    b, t, d = x.shape
    n_ctx = ctx.shape[1]
    depth = w_in.shape[0]
    ctx_row = b
    cvec = jnp.zeros((8, d), F32).at[:b].set(c).at[ctx_row].set(c_ctx)
    mod_all = _ada_call(cvec, w_ada, b_ada).reshape(depth, 8, 6, d)

    tm = min(512, t)
    tq = min(256, t)
    tk = min(512, t)
    tabs_lat = _rope_tables(t, identity=False)
    tabs_ctx = _rope_tables(n_ctx, identity=True)
    mla_scale = (MLA_NOPE + MLA_ROPE) ** -0.5
    gqa_scale = HEAD_DIM ** -0.5

    cx = ctx
    for l in range(depth):
        lw = _arrange_layer({
            "w_in": w_in[l], "na_q_norm": na_q_norm[l], "na_k_norm": na_k_norm[l],
            "mla_cq_norm": mla_cq_norm[l], "mla_ckv_norm": mla_ckv_norm[l], "mla_w_uq": mla_w_uq[l],
            "mla_w_ukv": mla_w_ukv[l], "mla_q_norm": mla_q_norm[l], "mla_k_norm": mla_k_norm[l],
            "sg_v_norm": sg_v_norm[l], "sg_w_s": sg_w_s[l], "sg_b_s": sg_b_s[l],
            "gqa_q_norm": gqa_q_norm[l], "gqa_k_norm": gqa_k_norm[l],
            "w_branch": w_branch[l], "w_gate": w_gate[l], "b_gate": b_gate[l], "w_out": w_out[l],
            "w_up": w_up[l], "conv_w": conv_w[l], "conv_b": conv_b[l], "w_down": w_down[l]})
        mod = mod_all[l]
        pc = _in_call(cx, mod, ctx_row, lw, tabs_ctx, n_ctx)
        pz = _in_call(x, mod, None, lw, tabs_lat, tm)

        o_na = _na_call(pz["na_q"], pz["na_k"], pz["na_v"], pc["na_k"], pc["na_v"], _na_bias(na_rpb[l], t // (4 * GRID_W)))
        o_mla = _flash_call(pz["mla_q"], pz["mla_k"], pz["mla_vt"], pc["mla_k"], pc["mla_vt"], mla_scale, tq, tk)
        o_gqa = _flash_call(pz["gqa_q"], pz["gqa_k"], pz["gqa_vt"], pc["gqa_k"], pc["gqa_vt"], gqa_scale, tq, tk)
        x = _merge_call(x, mod, None, (o_na, o_mla, pz["sg_o"], o_gqa), lw, tm)
        x = _ffn_call(x, mod, None, lw, tm)

        if l < depth - 1:
            oc_na = _na_ctx_call(pc["na_q"], pc["na_k"], pc["na_v"])
            oc_mla = _flash_call(pc["mla_q"], pc["mla_k"], pc["mla_vt"], None, None, mla_scale, n_ctx, n_ctx)
            oc_gqa = _flash_call(pc["gqa_q"], pc["gqa_k"], pc["gqa_vt"], None, None, gqa_scale, n_ctx, n_ctx)
            cx = _merge_call(cx, mod, ctx_row, (oc_na, oc_mla, pc["sg_o"], oc_gqa), lw, n_ctx)
            cx = _ffn_call(cx, mod, ctx_row, lw, n_ctx)
    return x
```

```python
import functools
import math

import numpy as np
import jax
import jax.numpy as jnp
from jax import lax
from jax.experimental import pallas as pl
from jax.experimental.pallas import tpu as pltpu

F32 = jnp.float32
BF16 = jnp.bfloat16

D_MODEL = 1024
GRID_W = 64
HEAD_DIM = 64
NA_HEADS = 4
NA_ROWS = 8
NA_COLS = 16
MLA_HEADS = 4
MLA_Q_LORA = 256
MLA_KV_LORA = 192
MLA_NOPE = 64
MLA_ROPE = 32
MLA_V = 64
SG_GROUPS = 4
SG_CHUNK = 128
SG_WIDTH = 256
GQA_HEADS = 4
GQA_KV_HEADS = 2
N_BRANCH = 4
BRANCH_WIDTH = 256
FFN_DIM = 2816
ROPE_THETA = 10000.0
EPS = 1e-6
NEG_INF = -1e30

NA_IN = 3 * NA_HEADS * HEAD_DIM
MLA_IN = MLA_Q_LORA + MLA_KV_LORA + MLA_ROPE
SG_IN = 2 * SG_WIDTH
GQA_IN = (GQA_HEADS + 2 * GQA_KV_HEADS) * HEAD_DIM

LANES = 128
MXU_DIM = 256
BF16_SUBLANES = 16
V_AUG = HEAD_DIM + BF16_SUBLANES
LOG2E = math.log2(math.e)

C_NA = 0
C_MLA = C_NA + NA_IN
C_SG = C_MLA + 640
C_GQA = C_SG + SG_IN
IN_ARR = C_GQA + 640

G_NA, G_CQ, G_CKV, G_MQ, G_MKN, G_MKR, G_SG, G_GQA, G_TOT = 0, 512, 768, 1024, 1536, 2048, 2176, 2432, 2944

FFN_CHUNK = 256
N_FFN_CHUNKS = FFN_DIM // FFN_CHUNK
HALO = BF16_SUBLANES


def _const_spec(shape):
    nd = len(shape)
    return pl.BlockSpec(shape, lambda *_: (0,) * nd, pipeline_mode=pl.Buffered(1))


def _params(vmem_mb, n_grid):
    return pltpu.CompilerParams(dimension_semantics=("arbitrary",) * n_grid,
                                vmem_limit_bytes=vmem_mb * 1024 * 1024)


def _dot(a, b):
    return jnp.dot(a, b, preferred_element_type=F32)


def _dot_nt(a, b):
    return lax.dot_general(a, b, (((1,), (1,)), ((), ())), preferred_element_type=F32)


def _rms(xf):
    return xf * lax.rsqrt(jnp.mean(xf * xf, axis=-1, keepdims=True) + EPS)


def _segsum(x2, s_ref):
    hi = x2.astype(BF16)
    lo = (x2 - hi.astype(F32)).astype(BF16)
    s = s_ref[...]
    parts = []
    for j in range(x2.shape[1] // MXU_DIM):
        sl = slice(MXU_DIM * j, MXU_DIM * (j + 1))
        parts.append(_dot(hi[:, sl], s) + _dot(lo[:, sl], s))
    return parts[0] if len(parts) == 1 else jnp.concatenate(parts, axis=1)


def _rope(x, cos, sin_signed, shift, first_half):
    w = x.shape[1]
    partner = jnp.where(first_half, pltpu.roll(x, w - shift, 1), pltpu.roll(x, shift, 1))
    return x * cos + partner * sin_signed


def _ada_kernel(c_ref, w_ref, b_ref, o_ref):
    c = c_ref[...]
    a = (c * jax.nn.sigmoid(c)).astype(BF16)
    o_ref[...] = _dot(a, w_ref[...].astype(BF16)) + b_ref[...]


def _ada_call(cvec, w_ada, b_ada):
    n_layers, d, n = w_ada.shape
    tn = 1536
    return pl.pallas_call(
        _ada_kernel,
        grid=(n_layers, n // tn),
        in_specs=[pl.BlockSpec((8, d), lambda l, j: (0, 0)),
                  pl.BlockSpec((None, d, tn), lambda l, j: (l, 0, j)),
                  pl.BlockSpec((None, 1, tn), lambda l, j: (l, 0, j))],
        out_specs=pl.BlockSpec((None, 8, tn), lambda l, j: (l, 0, j)),
        out_shape=jax.ShapeDtypeStruct((n_layers, 8, n), F32),
        compiler_params=_params(32, 2),
        name="ada",
    )(cvec, w_ada, b_ada.reshape(n_layers, 1, n))


def _in_kernel(x_ref, mod_ref, w_in_ref, w_uq_ref, w_ukv_ref, g_ref, s64_ref, smq_ref, invq_ref,
               gc_ref, gs_ref, mc_ref, ms_ref, sgw_ref, sgb_ref,
               naq_ref, nak_ref, nav_ref, mq_ref, mk_ref, mvt_ref, sgo_ref, gq_ref, gk_ref, gvt_ref):
    tm = x_ref.shape[0]
    x = x_ref[...]
    h = (_rms(x) * (1.0 + mod_ref[1:2, :]) + mod_ref[0:1, :]).astype(BF16)

    def gain(off, width):
        return g_ref[:, off:off + width]

    inv_head = 1.0 / HEAD_DIM

    z = _dot(h, w_in_ref[:, C_NA:C_NA + NA_IN])
    qk = z[:, 0:512]
    qk = qk * lax.rsqrt(_segsum(qk * qk, s64_ref) * inv_head + EPS) * gain(G_NA, 512)
    naq_ref[...] = qk[:, 0:256].astype(BF16)
    nak_ref[...] = qk[:, 256:512].astype(BF16)
    nav_ref[...] = z[:, 512:768].astype(BF16)

    z = _dot(h, w_in_ref[:, C_MLA:C_MLA + 640])
    cq, ckv, kr = z[:, 0:256], z[:, 256:512], z[:, 512:640]
    cq = cq * lax.rsqrt(jnp.mean(cq * cq, axis=-1, keepdims=True) + EPS) * gain(G_CQ, 256)
    ckv = ckv * lax.rsqrt(jnp.sum(ckv * ckv, axis=-1, keepdims=True) * (1.0 / MLA_KV_LORA) + EPS) * gain(G_CKV, 256)
    q = _dot(cq.astype(BF16), w_uq_ref[...])
    q = q * lax.rsqrt(_segsum(q * q, smq_ref) * invq_ref[...] + EPS) * gain(G_MQ, 512)
    kv = _dot(ckv.astype(BF16), w_ukv_ref[...])
    kn = kv[:, 0:512]
    kn = kn * lax.rsqrt(_segsum(kn * kn, s64_ref) * inv_head + EPS) * gain(G_MKN, 512)
    kr = kr * lax.rsqrt(jnp.sum(kr * kr, axis=-1, keepdims=True) * (1.0 / MLA_ROPE) + EPS) * gain(G_MKR, 128)
    lane = lax.broadcasted_iota(jnp.int32, (tm, LANES), 1)
    first8 = (lane & 15) < 8
    mc, ms = mc_ref[...], ms_ref[...]
    kr = _rope(kr, mc, ms, 8, first8)
    for hh in range(MLA_HEADS):
        sl = slice(LANES * hh, LANES * (hh + 1))
        mq_ref[hh] = _rope(q[:, sl], mc, ms, 8, first8).astype(BF16)
        mk_ref[hh] = (kn[:, sl] + kr).astype(BF16)
    vt = kv[:, 512:768].T
    ones = jnp.ones((BF16_SUBLANES, tm), BF16)
    for hh in range(MLA_HEADS):
        mvt_ref[hh, 0:HEAD_DIM, :] = vt[HEAD_DIM * hh:HEAD_DIM * (hh + 1), :].astype(BF16)
        mvt_ref[hh, HEAD_DIM:V_AUG, :] = ones

    z = _dot(h, w_in_ref[:, C_SG:C_SG + SG_IN])
    uv = jax.nn.gelu(z)
    u, v = uv[:, 0:SG_WIDTH], uv[:, SG_WIDTH:]
    v = v * lax.rsqrt(jnp.mean(v * v, axis=-1, keepdims=True) + EPS) * gain(G_SG, 256)
    grp = lax.broadcasted_iota(jnp.int32, (SG_CHUNK, SG_WIDTH), 1) >> 6
    for c in range(tm // SG_CHUNK):
        rows = slice(SG_CHUNK * c, SG_CHUNK * (c + 1))
        vc = v[rows, :]
        mixed = sgb_ref[...]
        for gi in range(SG_GROUPS):
            mixed = mixed + _dot(sgw_ref[gi], jnp.where(grp == gi, vc, 0.0).astype(BF16))
        sgo_ref[rows, :] = (u[rows, :] * mixed).astype(BF16)

    z = _dot(h, w_in_ref[:, C_GQA:C_GQA + 640])
    qk = z[:, 0:512]
    qk = qk * lax.rsqrt(_segsum(qk * qk, s64_ref) * inv_head + EPS) * gain(G_GQA, 512)
    lane2 = lax.broadcasted_iota(jnp.int32, (tm, 2 * LANES), 1)
    first16 = (lane2 & 31) < 16
    gc, gs = gc_ref[...], gs_ref[...]
    qn = _rope(qk[:, 0:256], gc, gs, 16, first16)
    kn = _rope(qk[:, 256:512], gc, gs, 16, first16)
    half = lane >> 6
    for g in range(GQA_KV_HEADS):
        sl = slice(LANES * g, LANES * (g + 1))
        for r in range(GQA_HEADS // GQA_KV_HEADS):
            gq_ref[2 * g + r] = jnp.where(half == r, qn[:, sl], 0.0).astype(BF16)
        gk_ref[g] = kn[:, sl].astype(BF16)
    vt = z[:, 512:640].T
    for g in range(GQA_KV_HEADS):
        gvt_ref[g, 0:HEAD_DIM, :] = vt[HEAD_DIM * g:HEAD_DIM * (g + 1), :].astype(BF16)
        gvt_ref[g, HEAD_DIM:V_AUG, :] = ones


def _in_call(x, mod, mod_row, lw, tabs, tm):
    b, t, d = x.shape
    nt = t // tm
    if mod_row is None:
        mod_map = lambda i, j: (j, 0, 0)
    else:
        mod_map = lambda i, j: (mod_row, 0, 0)
    tok = lambda i, j: (j, i, 0)
    tab = lambda i, j: (i, 0)
    head_tok = lambda i, j: (j, 0, i, 0)
    head_t = lambda i, j: (j, 0, 0, i)
    in_specs = [
        pl.BlockSpec((None, tm, d), tok),
        pl.BlockSpec((None, 6, d), mod_map),
        _const_spec((d, IN_ARR)),
        _const_spec((MLA_Q_LORA, 512)),
        _const_spec((256, 768)),
        _const_spec((1, G_TOT)),
        _const_spec((MXU_DIM, MXU_DIM)),
        _const_spec((MXU_DIM, MXU_DIM)),
        _const_spec((1, 512)),
        pl.BlockSpec((tm, 256), tab), pl.BlockSpec((tm, 256), tab),
        pl.BlockSpec((tm, LANES), tab), pl.BlockSpec((tm, LANES), tab),
        _const_spec((SG_GROUPS, SG_CHUNK, SG_CHUNK)),
        _const_spec((SG_CHUNK, SG_WIDTH)),
    ]
    out_specs = [
        pl.BlockSpec((None, tm, 256), tok), pl.BlockSpec((None, tm, 256), tok), pl.BlockSpec((None, tm, 256), tok),
        pl.BlockSpec((None, MLA_HEADS, tm, LANES), head_tok),
        pl.BlockSpec((None, MLA_HEADS, tm, LANES), head_tok),
        pl.BlockSpec((None, MLA_HEADS, V_AUG, tm), head_t),
        pl.BlockSpec((None, tm, 256), tok),
        pl.BlockSpec((None, GQA_HEADS, tm, LANES), head_tok),
        pl.BlockSpec((None, GQA_KV_HEADS, tm, LANES), head_tok),
        pl.BlockSpec((None, GQA_KV_HEADS, V_AUG, tm), head_t),
    ]
    sds = jax.ShapeDtypeStruct
    out_shape = [
        sds((b, t, 256), BF16), sds((b, t, 256), BF16), sds((b, t, 256), BF16),
        sds((b, MLA_HEADS, t, LANES), BF16), sds((b, MLA_HEADS, t, LANES), BF16),
        sds((b, MLA_HEADS, V_AUG, t), BF16),
        sds((b, t, 256), BF16),
        sds((b, GQA_HEADS, t, LANES), BF16), sds((b, GQA_KV_HEADS, t, LANES), BF16),
        sds((b, GQA_KV_HEADS, V_AUG, t), BF16),
    ]
    outs = pl.pallas_call(
        _in_kernel, grid=(nt, b), in_specs=in_specs, out_specs=out_specs, out_shape=out_shape,
        compiler_params=_params(48, 2), name="in_proj",
    )(x, mod, lw["w_in"], lw["w_uq"], lw["w_ukv"], lw["gains"], lw["s64"], lw["smq"], lw["invq"],
      tabs["gc"], tabs["gs"], tabs["mc"], tabs["ms"], lw["sgw"], lw["sgb"])
    names = ("na_q", "na_k", "na_v", "mla_q", "mla_k", "mla_vt", "sg_o", "gqa_q", "gqa_k", "gqa_vt")
    return dict(zip(names, outs))


def _na_heads(q, k_blocks, v_blocks, biases):
    tq = q.shape[0]
    head_of_lane = lax.broadcasted_iota(jnp.int32, (tq, 256), 1) >> 6
    qf = q.astype(F32) * (HEAD_DIM ** -0.5)
    out = jnp.zeros((tq, 256), F32)
    for hh in range(NA_HEADS):
        qh = jnp.where(head_of_lane == hh, qf, 0.0).astype(BF16)
        scores = []
        for kb, bias in zip(k_blocks, biases):
            s = _dot_nt(qh, kb)
            if bias is not None:
                s = s + bias[hh]
            scores.append(s)
        m = scores[0].max(axis=-1, keepdims=True)
        for s in scores[1:]:
            m = jnp.maximum(m, s.max(axis=-1, keepdims=True))
        l = jnp.zeros((tq, 1), F32)
        o = jnp.zeros((tq, 256), F32)
        for s, vb in zip(scores, v_blocks):
            p = jnp.exp(s - m)
            l = l + p.sum(axis=-1, keepdims=True)
            o = o + _dot(p.astype(BF16), vb)
        out = out + jnp.where(head_of_lane == hh, o * (1.0 / l), 0.0)
    return out


def _na_kernel(q_ref, kp_ref, kc_ref, kn_ref, vp_ref, vc_ref, vn_ref, kx_ref, vx_ref, bias_ref, o_ref):
    biases = [bias_ref.at[:, :, 0:256], bias_ref.at[:, :, 256:512], bias_ref.at[:, :, 512:768], None]
    out = _na_heads(q_ref[...], [kp_ref[...], kc_ref[...], kn_ref[...], kx_ref[...]],
                    [vp_ref[...], vc_ref[...], vn_ref[...], vx_ref[...]], biases)
    o_ref[...] = out.astype(BF16)


def _na_call(q, k, v, kx, vx, bias):
    b, t, w = q.shape
    c = kx.shape[1]
    tq = 4 * GRID_W
    nt = t // tq
    cur = lambda bi, i: (bi, i, 0)
    prev = lambda bi, i: (bi, jnp.maximum(i - 1, 0), 0)
    nxt = lambda bi, i: (bi, jnp.minimum(i + 1, nt - 1), 0)
    ctx = lambda bi, i: (bi, 0, 0)
    variant = lambda bi, i: (jnp.where(i == 0, 0, jnp.where(i == nt - 1, 2, 1)), 0, 0, 0)
    blk = lambda m: pl.BlockSpec((None, tq, w), m)
    return pl.pallas_call(
        _na_kernel, grid=(b, nt),
        in_specs=[blk(cur), blk(prev), blk(cur), blk(nxt), blk(prev), blk(cur), blk(nxt),
                  pl.BlockSpec((None, c, w), ctx), pl.BlockSpec((None, c, w), ctx),
                  pl.BlockSpec((None, NA_HEADS, tq, 3 * tq), variant)],
        out_specs=blk(cur),
        out_shape=jax.ShapeDtypeStruct((b, t, w), BF16),
        compiler_params=_params(48, 2), name="na_attn",
    )(q, k, k, k, v, v, v, kx, vx, bias)


def _na_ctx_kernel(q_ref, k_ref, v_ref, o_ref):
    o_ref[...] = _na_heads(q_ref[...], [k_ref[...]], [v_ref[...]], [None]).astype(BF16)


def _na_ctx_call(q, k, v):
    b, c, w = q.shape
    spec = pl.BlockSpec((None, c, w), lambda bi: (bi, 0, 0))
    return pl.pallas_call(
        _na_ctx_kernel, grid=(b,), in_specs=[spec, spec, spec], out_specs=spec,
        out_shape=jax.ShapeDtypeStruct((b, c, w), BF16),
        compiler_params=_params(32, 1), name="na_ctx_attn",
    )(q, k, v)


def _na_bias(rpb, nt):
    rows = 4 * nt
    col = np.arange(GRID_W)
    c_start = np.clip(col - NA_COLS // 2, 0, GRID_W - NA_COLS)
    valid_c = (col[None, :] >= c_start[:, None]) & (col[None, :] < c_start[:, None] + NA_COLS)
    dc = np.clip(col[None, :] - col[:, None] + (NA_COLS - 1), 0, 2 * NA_COLS - 2)
    pick_dc = jnp.asarray(dc[:, :, None] == np.arange(2 * NA_COLS - 1), F32)
    by_col = jnp.einsum("hrd,qkd->hrqk", rpb, pick_dc, precision=lax.Precision.HIGHEST)
    variants = []
    for i in (0, min(1, nt - 1), nt - 1):
        rq = 4 * i + np.arange(4)
        start = np.clip(rq - NA_ROWS // 2, 0, rows - NA_ROWS)
        blocks = np.array([i - 1, i, i + 1])
        rk = (4 * blocks[:, None] + np.arange(4)[None, :]).reshape(-1)
        block_ok = np.repeat((blocks >= 0) & (blocks < nt), 4)
        valid_r = (rk[None, :] >= start[:, None]) & (rk[None, :] < start[:, None] + NA_ROWS) & block_ok[None, :]
        dr = np.clip(rk[None, :] - rq[:, None] + (NA_ROWS - 1), 0, 2 * NA_ROWS - 2)
        pick_dr = jnp.asarray(dr[:, :, None] == np.arange(2 * NA_ROWS - 1), F32)
        vals = jnp.einsum("abr,hrqk->haqbk", pick_dr, by_col, precision=lax.Precision.HIGHEST)
        ok = valid_r[:, None, :, None] & valid_c[None, :, None, :]
        variants.append(jnp.where(ok[None], vals, NEG_INF).reshape(NA_HEADS, 4 * GRID_W, 12 * GRID_W))
    return jnp.stack(variants).astype(F32)


MAX_LOG2_GAP = 64.0


def _col_max(s):
    keys = s.shape[0]
    if keys % 64 == 0 and keys > 64:
        s = jnp.max(s.reshape(keys // 64, 64, s.shape[1]), axis=0)
    return jnp.max(s, axis=0, keepdims=True)


def _flash_kernel(*refs, n_heads, group, tk, c_scale, has_ctx):
    if has_ctx:
        q_ref, k_ref, vt_ref, kx_ref, vtx_ref, o_ref, acc_scr, ot_scr = refs
    else:
        q_ref, k_ref, vt_ref, o_ref, acc_scr, ot_scr = refs
    tq = q_ref.shape[1]
    n_chunks = k_ref.shape[1] // tk
    n_slabs = n_heads // group
    heads = range(n_heads)

    def load_chunk(j):
        off = pl.multiple_of(j * tk, tk)
        return ([k_ref[sl, pl.ds(off, tk), :] for sl in range(n_slabs)],
                [vt_ref[sl, :, pl.ds(off, tk)] for sl in range(n_slabs)])

    def exact_step(ks, vts, ms):
        new_ms = []
        for hh in heads:
            s = _dot_nt(ks[hh // group], q_ref[hh])
            m_new = jnp.maximum(ms[hh], _col_max(s))
            p = jnp.exp2((s - m_new) * c_scale)
            alpha = jnp.exp2((ms[hh] - m_new) * c_scale)
            acc_scr[hh] = acc_scr[hh] * alpha + _dot(vts[hh // group], p.astype(BF16))
            new_ms.append(m_new)
        return tuple(new_ms)

    def streamed_step(ks, vts, carry):
        ms, gaps = carry
        new_ms, new_gaps = [], []
        for hh in heads:
            s = _dot_nt(ks[hh // group], q_ref[hh])
            p = jnp.exp2((s - ms[hh]) * c_scale)
            m_chunk = _col_max(s)
            m_new = jnp.maximum(ms[hh], m_chunk)
            alpha = jnp.exp2((ms[hh] - m_new) * c_scale)
            acc_scr[hh] = (acc_scr[hh] + _dot(vts[hh // group], p.astype(BF16))) * alpha
            new_ms.append(m_new)
            new_gaps.append(jnp.maximum(gaps[hh], (m_chunk - ms[hh]) * c_scale))
        return tuple(new_ms), tuple(new_gaps)

    def first_chunk():
        acc_scr[...] = jnp.zeros_like(acc_scr)
        ms = tuple(jnp.full((1, tq), NEG_INF, F32) for _ in heads)
        if has_ctx:
            return exact_step([kx_ref[sl] for sl in range(n_slabs)], [vtx_ref[sl] for sl in range(n_slabs)], ms), 0
        return exact_step(*load_chunk(0), ms), 1

    ms, first = first_chunk()
    gaps = tuple(jnp.zeros((1, tq), F32) for _ in heads)
    _, gaps = lax.fori_loop(first, n_chunks, lambda j, c: streamed_step(*load_chunk(j), c), (ms, gaps))
    worst = gaps[0]
    for g in gaps[1:]:
        worst = jnp.maximum(worst, g)

    @pl.when(jnp.max(worst) > MAX_LOG2_GAP)
    def _():
        ms, first = first_chunk()
        lax.fori_loop(first, n_chunks, lambda j, m: exact_step(*load_chunk(j), m), ms)

    for hh in heads:
        acc = acc_scr[hh]
        ot_scr[HEAD_DIM * hh:HEAD_DIM * (hh + 1), :] = acc[0:HEAD_DIM] * (1.0 / acc[HEAD_DIM:HEAD_DIM + 1])
    o_ref[...] = ot_scr[...].T.astype(BF16)


def _flash_call(q, k, vt, kx, vtx, scale, tq, tk):
    b, n_heads, t_q, w = q.shape
    slabs, t_k = k.shape[1], k.shape[2]
    has_ctx = kx is not None
    full4 = lambda bi, i: (bi, 0, 0, 0)
    in_specs = [pl.BlockSpec((None, n_heads, tq, w), lambda bi, i: (bi, 0, i, 0)),
                pl.BlockSpec((None, slabs, t_k, w), full4),
                pl.BlockSpec((None, slabs, V_AUG, t_k), full4)]
    args = [q, k, vt]
    if has_ctx:
        c = kx.shape[2]
        in_specs += [pl.BlockSpec((None, slabs, c, w), full4), pl.BlockSpec((None, slabs, V_AUG, c), full4)]
        args += [kx, vtx]
    kern = functools.partial(_flash_kernel, n_heads=n_heads, group=n_heads // slabs, tk=tk,
                             c_scale=scale * LOG2E, has_ctx=has_ctx)
    return pl.pallas_call(
        kern, grid=(b, t_q // tq), in_specs=in_specs,
        out_specs=pl.BlockSpec((None, tq, n_heads * HEAD_DIM), lambda bi, i: (bi, i, 0)),
        out_shape=jax.ShapeDtypeStruct((b, t_q, n_heads * HEAD_DIM), BF16),
        scratch_shapes=[pltpu.VMEM((n_heads, V_AUG, tq), F32), pltpu.VMEM((n_heads * HEAD_DIM, tq), F32)],
        compiler_params=_params(56, 2), name="flash",
    )(*args)


def _merge_kernel(x_ref, mod_ref, o0_ref, o1_ref, o2_ref, o3_ref, wg_ref, bg_ref, wb_ref, wo_ref, out_ref):
    x = x_ref[...]
    h = (_rms(x) * (1.0 + mod_ref[1:2, :]) + mod_ref[0:1, :]).astype(BF16)
    y = None
    for i, o_ref in enumerate((o0_ref, o1_ref, o2_ref, o3_ref)):
        gate = jax.nn.sigmoid(_dot(h, wg_ref[i]) + bg_ref[i])
        term = gate * _dot(o_ref[...], wb_ref[i])
        y = term if y is None else y + term
    out_ref[...] = x + mod_ref[2:3, :] * _dot(y.astype(BF16), wo_ref[...])


def _merge_call(x, mod, mod_row, branches, lw, tm):
    b, t, d = x.shape
    if mod_row is None:
        mod_map = lambda i, j: (j, 0, 0)
    else:
        mod_map = lambda i, j: (mod_row, 0, 0)
    tok = lambda i, j: (j, i, 0)
    br = pl.BlockSpec((None, tm, BRANCH_WIDTH), tok)
    return pl.pallas_call(
        _merge_kernel, grid=(t // tm, b),
        in_specs=[pl.BlockSpec((None, tm, d), tok), pl.BlockSpec((None, 6, d), mod_map), br, br, br, br,
                  _const_spec((N_BRANCH, d, d)), _const_spec((N_BRANCH, 1, d)),
                  _const_spec((N_BRANCH, BRANCH_WIDTH, d)), _const_spec((d, d))],
        out_specs=pl.BlockSpec((None, tm, d), tok),
        out_shape=jax.ShapeDtypeStruct((b, t, d), F32),
        compiler_params=_params(56, 2), name="merge",
    )(x, mod, *branches, lw["w_gate"], lw["b_gate"], lw["w_branch"], lw["w_out"])


def _ffn_kernel(x_ref, xp_ref, xn_ref, mod_ref, wu_ref, cw_ref, cb_ref, wd_ref, out_ref, h_scr, acc_scr):
    tm = x_ref.shape[0]
    i = pl.program_id(0)
    nt = pl.num_programs(0)
    sh, sc = mod_ref[3:4, :], mod_ref[4:5, :]

    def modulated(v):
        return _rms(v) * (1.0 + sc) + sh

    x = x_ref[...]
    h_scr[0:HALO, :] = jnp.where(i > 0, modulated(xp_ref[...]), 0.0).astype(BF16)
    h_scr[HALO:HALO + tm, :] = modulated(x).astype(BF16)
    h_scr[HALO + tm:, :] = jnp.where(i < nt - 1, modulated(xn_ref[...]), 0.0).astype(BF16)
    acc_scr[...] = jnp.zeros_like(acc_scr)
    rows = tm + 2 * HALO

    def chunk(j, carry):
        up = _dot(h_scr[...], wu_ref[j])
        cw = cw_ref[j]
        conv = (pltpu.roll(up, 1, 0)[HALO:HALO + tm] * cw[0:1, :] + up[HALO:HALO + tm] * cw[1:2, :]
                + pltpu.roll(up, rows - 1, 0)[HALO:HALO + tm] * cw[2:3, :] + cb_ref[j])
        a, g = conv[:, 0:FFN_CHUNK], conv[:, FFN_CHUNK:]
        act = (g * jax.nn.sigmoid(g) * a).astype(BF16)
        acc_scr[...] += _dot(act, wd_ref[j])
        return carry

    lax.fori_loop(0, N_FFN_CHUNKS, chunk, 0)
    out_ref[...] = x + mod_ref[5:6, :] * acc_scr[...]


def _ffn_call(x, mod, mod_row, lw, tm):
    b, t, d = x.shape
    nt = t // tm
    hb = tm // HALO
    last_halo = t // HALO - 1
    if mod_row is None:
        mod_map = lambda i, j: (j, 0, 0)
    else:
        mod_map = lambda i, j: (mod_row, 0, 0)
    tok = lambda i, j: (j, i, 0)
    return pl.pallas_call(
        _ffn_kernel, grid=(nt, b),
        in_specs=[pl.BlockSpec((None, tm, d), tok),
                  pl.BlockSpec((None, HALO, d), lambda i, j: (j, jnp.maximum(i * hb - 1, 0), 0)),
                  pl.BlockSpec((None, HALO, d), lambda i, j: (j, jnp.minimum((i + 1) * hb, last_halo), 0)),
                  pl.BlockSpec((None, 6, d), mod_map),
                  _const_spec((N_FFN_CHUNKS, d, 2 * FFN_CHUNK)),
                  _const_spec((N_FFN_CHUNKS, 3, 2 * FFN_CHUNK)),
                  _const_spec((N_FFN_CHUNKS, 1, 2 * FFN_CHUNK)),
                  _const_spec((N_FFN_CHUNKS, FFN_CHUNK, d))],
        out_specs=pl.BlockSpec((None, tm, d), tok),
        out_shape=jax.ShapeDtypeStruct((b, t, d), F32),
        scratch_shapes=[pltpu.VMEM((tm + 2 * HALO, d), BF16), pltpu.VMEM((tm, d), F32)],
        compiler_params=_params(56, 2), name="ffn",
    )(x, x, x, mod, lw["w_up"], lw["conv_w"], lw["conv_b"], lw["w_down"])


def _block_diag_ones(width, segs):
    m = np.zeros((width, width), np.float32)
    for lo, hi in segs:
        m[lo:hi, lo:hi] = 1.0
    return jnp.asarray(m, BF16)


def _rope_pattern(pos, width):
    half = width // 2
    inv = ROPE_THETA ** (-jnp.arange(half, dtype=F32) / half)
    ang = pos.astype(F32)[:, None] * inv[None, :]
    cos, sin = jnp.cos(ang), jnp.sin(ang)
    return jnp.concatenate([cos, cos], axis=1), jnp.concatenate([-sin, sin], axis=1)


def _rope_tables(t, identity):
    if identity:
        return {"gc": jnp.ones((t, 256), F32), "gs": jnp.zeros((t, 256), F32),
                "mc": jnp.ones((t, LANES), F32), "ms": jnp.zeros((t, LANES), F32)}
    pos = jnp.arange(t)
    rows, cols = pos // GRID_W, pos % GRID_W
    rc, rs = _rope_pattern(rows, HEAD_DIM // 2)
    cc, cs = _rope_pattern(cols, HEAD_DIM // 2)
    gc = jnp.tile(jnp.concatenate([rc, cc], axis=1), (1, 4))
    gs = jnp.tile(jnp.concatenate([rs, cs], axis=1), (1, 4))
    rc, rs = _rope_pattern(rows, MLA_ROPE // 2)
    cc, cs = _rope_pattern(cols, MLA_ROPE // 2)
    one, zero = jnp.ones((t, MLA_NOPE), F32), jnp.zeros((t, MLA_NOPE), F32)
    pad1, pad0 = jnp.ones((t, LANES - MLA_NOPE - MLA_ROPE), F32), jnp.zeros((t, LANES - MLA_NOPE - MLA_ROPE), F32)
    return {"gc": gc, "gs": gs,
            "mc": jnp.concatenate([one, rc, cc, pad1], axis=1),
            "ms": jnp.concatenate([zero, rs, cs, pad0], axis=1)}


def _arrange_layer(p):
    d = D_MODEL
    w_in = p["w_in"]
    na, mla = w_in[:, :NA_IN], w_in[:, NA_IN:NA_IN + MLA_IN]
    sg = w_in[:, NA_IN + MLA_IN:NA_IN + MLA_IN + SG_IN]
    gqa = w_in[:, NA_IN + MLA_IN + SG_IN:]
    z = lambda n: jnp.zeros((d, n), F32)
    cq, ckv, kr = mla[:, :MLA_Q_LORA], mla[:, MLA_Q_LORA:MLA_Q_LORA + MLA_KV_LORA], mla[:, MLA_Q_LORA + MLA_KV_LORA:]
    gq, gk, gv = gqa[:, :256], gqa[:, 256:384], gqa[:, 384:]
    gk_dup = jnp.concatenate([gk[:, :64], gk[:, :64], gk[:, 64:], gk[:, 64:]], axis=1)
    w_in_arr = jnp.concatenate([na, cq, ckv, z(256 - MLA_KV_LORA), z(MLA_NOPE), kr, z(LANES - MLA_NOPE - MLA_ROPE),
                                sg, gq, gk_dup, gv], axis=1).astype(BF16)

    w_uq = p["mla_w_uq"].reshape(MLA_Q_LORA, MLA_HEADS, MLA_NOPE + MLA_ROPE)
    w_uq = jnp.pad(w_uq, ((0, 0), (0, 0), (0, LANES - MLA_NOPE - MLA_ROPE))).reshape(MLA_Q_LORA, MLA_HEADS * LANES)
    w_ukv = p["mla_w_ukv"].reshape(MLA_KV_LORA, MLA_HEADS, MLA_NOPE + MLA_V)
    k_part = jnp.pad(w_ukv[:, :, :MLA_NOPE], ((0, 0), (0, 0), (0, LANES - MLA_NOPE))).reshape(MLA_KV_LORA, MLA_HEADS * LANES)
    v_part = w_ukv[:, :, MLA_NOPE:].reshape(MLA_KV_LORA, MLA_HEADS * MLA_V)
    w_ukv = jnp.pad(jnp.concatenate([k_part, v_part], axis=1), ((0, 256 - MLA_KV_LORA), (0, 0)))

    qg, kg = p["mla_q_norm"], p["mla_k_norm"]
    z1 = lambda n: jnp.zeros((n,), F32)
    gains = jnp.concatenate([
        jnp.tile(p["na_q_norm"], 4), jnp.tile(p["na_k_norm"], 4),
        p["mla_cq_norm"],
        p["mla_ckv_norm"], z1(256 - MLA_KV_LORA),
        jnp.tile(jnp.concatenate([qg, z1(LANES - MLA_NOPE - MLA_ROPE)]), 4),
        jnp.tile(jnp.concatenate([kg[:MLA_NOPE], z1(LANES - MLA_NOPE)]), 4),
        z1(MLA_NOPE), kg[MLA_NOPE:], z1(LANES - MLA_NOPE - MLA_ROPE),
        p["sg_v_norm"],
        jnp.tile(p["gqa_q_norm"], 4), jnp.tile(p["gqa_k_norm"], 4),
    ]).reshape(1, G_TOT)

    invq = np.tile(np.concatenate([np.full(MLA_NOPE, 1.0 / MLA_NOPE), np.full(MLA_ROPE, 1.0 / MLA_ROPE),
                                   np.ones(LANES - MLA_NOPE - MLA_ROPE)]), 4).astype(np.float32).reshape(1, 512)
    sgb = jnp.repeat(p["sg_b_s"].T, SG_WIDTH // SG_GROUPS, axis=1)

    w_up = p["w_up"]
    a_part = w_up[:, :FFN_DIM].reshape(d, N_FFN_CHUNKS, FFN_CHUNK)
    g_part = w_up[:, FFN_DIM:].reshape(d, N_FFN_CHUNKS, FFN_CHUNK)
    w_up_arr = jnp.concatenate([a_part, g_part], axis=2).transpose(1, 0, 2).astype(BF16)

    def chunked(v):
        lead = v.shape[:-1]
        a = v[..., :FFN_DIM].reshape(*lead, N_FFN_CHUNKS, FFN_CHUNK)
        g = v[..., FFN_DIM:].reshape(*lead, N_FFN_CHUNKS, FFN_CHUNK)
        return jnp.moveaxis(jnp.concatenate([a, g], axis=-1), -2, 0)

    return {
        "w_in": w_in_arr, "w_uq": w_uq.astype(BF16), "w_ukv": w_ukv.astype(BF16), "gains": gains,
        "s64": _block_diag_ones(MXU_DIM, [(64 * i, 64 * i + 64) for i in range(4)]),
        "smq": _block_diag_ones(MXU_DIM, [(0, 64), (64, 96), (128, 192), (192, 224)]),
        "invq": jnp.asarray(invq),
        "sgw": p["sg_w_s"].astype(BF16), "sgb": sgb,
        "w_gate": p["w_gate"].astype(BF16), "b_gate": p["b_gate"].reshape(N_BRANCH, 1, d),
        "w_branch": p["w_branch"].astype(BF16), "w_out": p["w_out"].astype(BF16),
        "w_up": w_up_arr, "conv_w": chunked(p["conv_w"]), "conv_b": chunked(p["conv_b"].reshape(1, -1)),
        "w_down": p["w_down"].reshape(N_FFN_CHUNKS, FFN_CHUNK, d).astype(BF16),
    }


def kernel(x, c, ctx, c_ctx, w_ada, b_ada, w_in, na_q_norm, na_k_norm, na_rpb, mla_cq_norm, mla_ckv_norm,
           mla_w_uq, mla_w_ukv, mla_q_norm, mla_k_norm, sg_v_norm, sg_w_s, sg_b_s, gqa_q_norm, gqa_k_norm,
           w_branch, w_gate, b_gate, w_out, w_up, conv_w, conv_b, w_down):
    b, t, d = x.shape
    n_ctx = ctx.shape[1]
    depth = w_in.shape[0]
    ctx_row = b
    cvec = jnp.zeros((8, d), F32).at[:b].set(c).at[ctx_row].set(c_ctx)
    mod_all = _ada_call(cvec, w_ada, b_ada).reshape(depth, 8, 6, d)

    tm = min(512, t)
    tq = min(256, t)
    tk = min(512, t)
    tabs_lat = _rope_tables(t, identity=False)
    tabs_ctx = _rope_tables(n_ctx, identity=True)
    mla_scale = (MLA_NOPE + MLA_ROPE) ** -0.5
    gqa_scale = HEAD_DIM ** -0.5

    cx = ctx
    for l in range(depth):
        lw = _arrange_layer({
            "w_in": w_in[l], "na_q_norm": na_q_norm[l], "na_k_norm": na_k_norm[l],
            "mla_cq_norm": mla_cq_norm[l], "mla_ckv_norm": mla_ckv_norm[l], "mla_w_uq": mla_w_uq[l],
            "mla_w_ukv": mla_w_ukv[l], "mla_q_norm": mla_q_norm[l], "mla_k_norm": mla_k_norm[l],
            "sg_v_norm": sg_v_norm[l], "sg_w_s": sg_w_s[l], "sg_b_s": sg_b_s[l],
            "gqa_q_norm": gqa_q_norm[l], "gqa_k_norm": gqa_k_norm[l],
            "w_branch": w_branch[l], "w_gate": w_gate[l], "b_gate": b_gate[l], "w_out": w_out[l],
            "w_up": w_up[l], "conv_w": conv_w[l], "conv_b": conv_b[l], "w_down": w_down[l]})
        mod = mod_all[l]
        pc = _in_call(cx, mod, ctx_row, lw, tabs_ctx, n_ctx)
        pz = _in_call(x, mod, None, lw, tabs_lat, tm)

        o_na = _na_call(pz["na_q"], pz["na_k"], pz["na_v"], pc["na_k"], pc["na_v"], _na_bias(na_rpb[l], t // (4 * GRID_W)))
        o_mla = _flash_call(pz["mla_q"], pz["mla_k"], pz["mla_vt"], pc["mla_k"], pc["mla_vt"], mla_scale, tq, tk)
        o_gqa = _flash_call(pz["gqa_q"], pz["gqa_k"], pz["gqa_vt"], pc["gqa_k"], pc["gqa_vt"], gqa_scale, tq, tk)
        x = _merge_call(x, mod, None, (o_na, o_mla, pz["sg_o"], o_gqa), lw, tm)
        x = _ffn_call(x, mod, None, lw, tm)

        if l < depth - 1:
            oc_na = _na_ctx_call(pc["na_q"], pc["na_k"], pc["na_v"])
            oc_mla = _flash_call(pc["mla_q"], pc["mla_k"], pc["mla_vt"], None, None, mla_scale, n_ctx, n_ctx)
            oc_gqa = _flash_call(pc["gqa_q"], pc["gqa_k"], pc["gqa_vt"], None, None, gqa_scale, n_ctx, n_ctx)
            cx = _merge_call(cx, mod, ctx_row, (oc_na, oc_mla, pc["sg_o"], oc_gqa), lw, n_ctx)
            cx = _ffn_call(cx, mod, ctx_row, lw, n_ctx)
    return x
```

```python
import functools
import math

import numpy as np
import jax
import jax.numpy as jnp
from jax import lax
from jax.experimental import pallas as pl
from jax.experimental.pallas import tpu as pltpu

F32 = jnp.float32
BF16 = jnp.bfloat16

D_MODEL = 1024
GRID_W = 64
HEAD_DIM = 64
NA_HEADS = 4
NA_ROWS = 8
NA_COLS = 16
MLA_HEADS = 4
MLA_Q_LORA = 256
MLA_KV_LORA = 192
MLA_NOPE = 64
MLA_ROPE = 32
MLA_V = 64
SG_GROUPS = 4
SG_CHUNK = 128
SG_WIDTH = 256
GQA_HEADS = 4
GQA_KV_HEADS = 2
N_BRANCH = 4
BRANCH_WIDTH = 256
FFN_DIM = 2816
ROPE_THETA = 10000.0
EPS = 1e-6
NEG_INF = -1e30

NA_IN = 3 * NA_HEADS * HEAD_DIM
MLA_IN = MLA_Q_LORA + MLA_KV_LORA + MLA_ROPE
SG_IN = 2 * SG_WIDTH
GQA_IN = (GQA_HEADS + 2 * GQA_KV_HEADS) * HEAD_DIM

LANES = 128
MXU_DIM = 256
BF16_SUBLANES = 16
V_AUG = HEAD_DIM + BF16_SUBLANES
LOG2E = math.log2(math.e)

C_NA = 0
C_MLA = C_NA + NA_IN
C_SG = C_MLA + 640
C_GQA = C_SG + SG_IN
IN_ARR = C_GQA + 640

G_NA, G_CQ, G_CKV, G_MQ, G_MKN, G_MKR, G_SG, G_GQA, G_TOT = 0, 512, 768, 1024, 1536, 2048, 2176, 2432, 2944

FFN_CHUNK = 256
N_FFN_CHUNKS = FFN_DIM // FFN_CHUNK
HALO = BF16_SUBLANES


def _const_spec(shape):
    nd = len(shape)
    return pl.BlockSpec(shape, lambda *_: (0,) * nd, pipeline_mode=pl.Buffered(1))


def _params(vmem_mb, n_grid):
    return pltpu.CompilerParams(dimension_semantics=("arbitrary",) * n_grid,
                                vmem_limit_bytes=vmem_mb * 1024 * 1024)


def _dot(a, b):
    return jnp.dot(a, b, preferred_element_type=F32)


def _dot_nt(a, b):
    return lax.dot_general(a, b, (((1,), (1,)), ((), ())), preferred_element_type=F32)


def _rms(xf):
    return xf * lax.rsqrt(jnp.mean(xf * xf, axis=-1, keepdims=True) + EPS)


def _segsum(x2, s_ref):
    hi = x2.astype(BF16)
    lo = (x2 - hi.astype(F32)).astype(BF16)
    s = s_ref[...]
    parts = []
    for j in range(x2.shape[1] // MXU_DIM):
        sl = slice(MXU_DIM * j, MXU_DIM * (j + 1))
        parts.append(_dot(hi[:, sl], s) + _dot(lo[:, sl], s))
    return parts[0] if len(parts) == 1 else jnp.concatenate(parts, axis=1)


def _rope(x, cos, sin_signed, shift, first_half):
    w = x.shape[1]
    partner = jnp.where(first_half, pltpu.roll(x, w - shift, 1), pltpu.roll(x, shift, 1))
    return x * cos + partner * sin_signed


def _ada_kernel(c_ref, w_ref, b_ref, o_ref):
    c = c_ref[...]
    a = (c * jax.nn.sigmoid(c)).astype(BF16)
    o_ref[...] = _dot(a, w_ref[...].astype(BF16)) + b_ref[...]


def _ada_call(cvec, w_ada, b_ada):
    n_layers, d, n = w_ada.shape
    tn = 1536
    return pl.pallas_call(
        _ada_kernel,
        grid=(n_layers, n // tn),
        in_specs=[pl.BlockSpec((8, d), lambda l, j: (0, 0)),
                  pl.BlockSpec((None, d, tn), lambda l, j: (l, 0, j)),
                  pl.BlockSpec((None, 1, tn), lambda l, j: (l, 0, j))],
        out_specs=pl.BlockSpec((None, 8, tn), lambda l, j: (l, 0, j)),
        out_shape=jax.ShapeDtypeStruct((n_layers, 8, n), F32),
        compiler_params=_params(32, 2),
        name="ada",
    )(cvec, w_ada, b_ada.reshape(n_layers, 1, n))


def _in_kernel(x_ref, mod_ref, w_in_ref, w_uq_ref, w_ukv_ref, g_ref, s64_ref, smq_ref, invq_ref,
               gc_ref, gs_ref, mc_ref, ms_ref, sgw_ref, sgb_ref,
               naq_ref, nak_ref, nav_ref, mq_ref, mk_ref, mvt_ref, sgo_ref, gq_ref, gk_ref, gvt_ref):
    tm = x_ref.shape[0]
    x = x_ref[...]
    h = (_rms(x) * (1.0 + mod_ref[1:2, :]) + mod_ref[0:1, :]).astype(BF16)

    def gain(off, width):
        return g_ref[:, off:off + width]

    inv_head = 1.0 / HEAD_DIM

    z = _dot(h, w_in_ref[:, C_NA:C_NA + NA_IN])
    qk = z[:, 0:512]
    qk = qk * lax.rsqrt(_segsum(qk * qk, s64_ref) * inv_head + EPS) * gain(G_NA, 512)
    naq_ref[...] = qk[:, 0:256].astype(BF16)
    nak_ref[...] = qk[:, 256:512].astype(BF16)
    nav_ref[...] = z[:, 512:768].astype(BF16)

    z = _dot(h, w_in_ref[:, C_MLA:C_MLA + 640])
    cq, ckv, kr = z[:, 0:256], z[:, 256:512], z[:, 512:640]
    cq = cq * lax.rsqrt(jnp.mean(cq * cq, axis=-1, keepdims=True) + EPS) * gain(G_CQ, 256)
    ckv = ckv * lax.rsqrt(jnp.sum(ckv * ckv, axis=-1, keepdims=True) * (1.0 / MLA_KV_LORA) + EPS) * gain(G_CKV, 256)
    q = _dot(cq.astype(BF16), w_uq_ref[...])
    q = q * lax.rsqrt(_segsum(q * q, smq_ref) * invq_ref[...] + EPS) * gain(G_MQ, 512)
    kv = _dot(ckv.astype(BF16), w_ukv_ref[...])
    kn = kv[:, 0:512]
    kn = kn * lax.rsqrt(_segsum(kn * kn, s64_ref) * inv_head + EPS) * gain(G_MKN, 512)
    kr = kr * lax.rsqrt(jnp.sum(kr * kr, axis=-1, keepdims=True) * (1.0 / MLA_ROPE) + EPS) * gain(G_MKR, 128)
    lane = lax.broadcasted_iota(jnp.int32, (tm, LANES), 1)
    first8 = (lane & 15) < 8
    mc, ms = mc_ref[...], ms_ref[...]
    kr = _rope(kr, mc, ms, 8, first8)
    for hh in range(MLA_HEADS):
        sl = slice(LANES * hh, LANES * (hh + 1))
        mq_ref[hh] = _rope(q[:, sl], mc, ms, 8, first8).astype(BF16)
        mk_ref[hh] = (kn[:, sl] + kr).astype(BF16)
    vt = kv[:, 512:768].T
    ones = jnp.ones((BF16_SUBLANES, tm), BF16)
    for hh in range(MLA_HEADS):
        mvt_ref[hh, 0:HEAD_DIM, :] = vt[HEAD_DIM * hh:HEAD_DIM * (hh + 1), :].astype(BF16)
        mvt_ref[hh, HEAD_DIM:V_AUG, :] = ones

    z = _dot(h, w_in_ref[:, C_SG:C_SG + SG_IN])
    uv = jax.nn.gelu(z)
    u, v = uv[:, 0:SG_WIDTH], uv[:, SG_WIDTH:]
    v = v * lax.rsqrt(jnp.mean(v * v, axis=-1, keepdims=True) + EPS) * gain(G_SG, 256)
    grp = lax.broadcasted_iota(jnp.int32, (SG_CHUNK, SG_WIDTH), 1) >> 6
    for c in range(tm // SG_CHUNK):
        rows = slice(SG_CHUNK * c, SG_CHUNK * (c + 1))
        vc = v[rows, :]
        mixed = sgb_ref[...]
        for gi in range(SG_GROUPS):
            mixed = mixed + _dot(sgw_ref[gi], jnp.where(grp == gi, vc, 0.0).astype(BF16))
        sgo_ref[rows, :] = (u[rows, :] * mixed).astype(BF16)

    z = _dot(h, w_in_ref[:, C_GQA:C_GQA + 640])
    qk = z[:, 0:512]
    qk = qk * lax.rsqrt(_segsum(qk * qk, s64_ref) * inv_head + EPS) * gain(G_GQA, 512)
    lane2 = lax.broadcasted_iota(jnp.int32, (tm, 2 * LANES), 1)
    first16 = (lane2 & 31) < 16
    gc, gs = gc_ref[...], gs_ref[...]
    qn = _rope(qk[:, 0:256], gc, gs, 16, first16)
    kn = _rope(qk[:, 256:512], gc, gs, 16, first16)
    half = lane >> 6
    for g in range(GQA_KV_HEADS):
        sl = slice(LANES * g, LANES * (g + 1))
        for r in range(GQA_HEADS // GQA_KV_HEADS):
            gq_ref[2 * g + r] = jnp.where(half == r, qn[:, sl], 0.0).astype(BF16)
        gk_ref[g] = kn[:, sl].astype(BF16)
    vt = z[:, 512:640].T
    for g in range(GQA_KV_HEADS):
        gvt_ref[g, 0:HEAD_DIM, :] = vt[HEAD_DIM * g:HEAD_DIM * (g + 1), :].astype(BF16)
        gvt_ref[g, HEAD_DIM:V_AUG, :] = ones


def _in_call(x, mod, mod_row, lw, tabs, tm):
    b, t, d = x.shape
    nt = t // tm
    if mod_row is None:
        mod_map = lambda i, j: (j, 0, 0)
    else:
        mod_map = lambda i, j: (mod_row, 0, 0)
    tok = lambda i, j: (j, i, 0)
    tab = lambda i, j: (i, 0)
    head_tok = lambda i, j: (j, 0, i, 0)
    head_t = lambda i, j: (j, 0, 0, i)
    in_specs = [
        pl.BlockSpec((None, tm, d), tok),
        pl.BlockSpec((None, 6, d), mod_map),
        _const_spec((d, IN_ARR)),
        _const_spec((MLA_Q_LORA, 512)),
        _const_spec((256, 768)),
        _const_spec((1, G_TOT)),
        _const_spec((MXU_DIM, MXU_DIM)),
        _const_spec((MXU_DIM, MXU_DIM)),
        _const_spec((1, 512)),
        pl.BlockSpec((tm, 256), tab), pl.BlockSpec((tm, 256), tab),
        pl.BlockSpec((tm, LANES), tab), pl.BlockSpec((tm, LANES), tab),
        _const_spec((SG_GROUPS, SG_CHUNK, SG_CHUNK)),
        _const_spec((SG_CHUNK, SG_WIDTH)),
    ]
    out_specs = [
        pl.BlockSpec((None, tm, 256), tok), pl.BlockSpec((None, tm, 256), tok), pl.BlockSpec((None, tm, 256), tok),
        pl.BlockSpec((None, MLA_HEADS, tm, LANES), head_tok),
        pl.BlockSpec((None, MLA_HEADS, tm, LANES), head_tok),
        pl.BlockSpec((None, MLA_HEADS, V_AUG, tm), head_t),
        pl.BlockSpec((None, tm, 256), tok),
        pl.BlockSpec((None, GQA_HEADS, tm, LANES), head_tok),
        pl.BlockSpec((None, GQA_KV_HEADS, tm, LANES), head_tok),
        pl.BlockSpec((None, GQA_KV_HEADS, V_AUG, tm), head_t),
    ]
    sds = jax.ShapeDtypeStruct
    out_shape = [
        sds((b, t, 256), BF16), sds((b, t, 256), BF16), sds((b, t, 256), BF16),
        sds((b, MLA_HEADS, t, LANES), BF16), sds((b, MLA_HEADS, t, LANES), BF16),
        sds((b, MLA_HEADS, V_AUG, t), BF16),
        sds((b, t, 256), BF16),
        sds((b, GQA_HEADS, t, LANES), BF16), sds((b, GQA_KV_HEADS, t, LANES), BF16),
        sds((b, GQA_KV_HEADS, V_AUG, t), BF16),
    ]
    outs = pl.pallas_call(
        _in_kernel, grid=(nt, b), in_specs=in_specs, out_specs=out_specs, out_shape=out_shape,
        compiler_params=_params(48, 2), name="in_proj",
    )(x, mod, lw["w_in"], lw["w_uq"], lw["w_ukv"], lw["gains"], lw["s64"], lw["smq"], lw["invq"],
      tabs["gc"], tabs["gs"], tabs["mc"], tabs["ms"], lw["sgw"], lw["sgb"])
    names = ("na_q", "na_k", "na_v", "mla_q", "mla_k", "mla_vt", "sg_o", "gqa_q", "gqa_k", "gqa_vt")
    return dict(zip(names, outs))


def _na_heads(q, k_blocks, v_blocks, biases):
    tq = q.shape[0]
    head_of_lane = lax.broadcasted_iota(jnp.int32, (tq, 256), 1) >> 6
    qf = q.astype(F32) * (HEAD_DIM ** -0.5)
    out = jnp.zeros((tq, 256), F32)
    for hh in range(NA_HEADS):
        qh = jnp.where(head_of_lane == hh, qf, 0.0).astype(BF16)
        scores = []
        for kb, bias in zip(k_blocks, biases):
            s = _dot_nt(qh, kb)
            if bias is not None:
                s = s + bias[hh]
            scores.append(s)
        m = scores[0].max(axis=-1, keepdims=True)
        for s in scores[1:]:
            m = jnp.maximum(m, s.max(axis=-1, keepdims=True))
        l = jnp.zeros((tq, 1), F32)
        o = jnp.zeros((tq, 256), F32)
        for s, vb in zip(scores, v_blocks):
            p = jnp.exp(s - m)
            l = l + p.sum(axis=-1, keepdims=True)
            o = o + _dot(p.astype(BF16), vb)
        out = out + jnp.where(head_of_lane == hh, o * (1.0 / l), 0.0)
    return out


def _na_kernel(q_ref, kp_ref, kc_ref, kn_ref, vp_ref, vc_ref, vn_ref, kx_ref, vx_ref, bias_ref, o_ref):
    biases = [bias_ref.at[:, :, 0:256], bias_ref.at[:, :, 256:512], bias_ref.at[:, :, 512:768], None]
    out = _na_heads(q_ref[...], [kp_ref[...], kc_ref[...], kn_ref[...], kx_ref[...]],
                    [vp_ref[...], vc_ref[...], vn_ref[...], vx_ref[...]], biases)
    o_ref[...] = out.astype(BF16)


def _na_call(q, k, v, kx, vx, bias):
    b, t, w = q.shape
    c = kx.shape[1]
    tq = 4 * GRID_W
    nt = t // tq
    cur = lambda bi, i: (bi, i, 0)
    prev = lambda bi, i: (bi, jnp.maximum(i - 1, 0), 0)
    nxt = lambda bi, i: (bi, jnp.minimum(i + 1, nt - 1), 0)
    ctx = lambda bi, i: (bi, 0, 0)
    variant = lambda bi, i: (jnp.where(i == 0, 0, jnp.where(i == nt - 1, 2, 1)), 0, 0, 0)
    blk = lambda m: pl.BlockSpec((None, tq, w), m)
    return pl.pallas_call(
        _na_kernel, grid=(b, nt),
        in_specs=[blk(cur), blk(prev), blk(cur), blk(nxt), blk(prev), blk(cur), blk(nxt),
                  pl.BlockSpec((None, c, w), ctx), pl.BlockSpec((None, c, w), ctx),
                  pl.BlockSpec((None, NA_HEADS, tq, 3 * tq), variant)],
        out_specs=blk(cur),
        out_shape=jax.ShapeDtypeStruct((b, t, w), BF16),
        compiler_params=_params(48, 2), name="na_attn",
    )(q, k, k, k, v, v, v, kx, vx, bias)


def _na_ctx_kernel(q_ref, k_ref, v_ref, o_ref):
    o_ref[...] = _na_heads(q_ref[...], [k_ref[...]], [v_ref[...]], [None]).astype(BF16)


def _na_ctx_call(q, k, v):
    b, c, w = q.shape
    spec = pl.BlockSpec((None, c, w), lambda bi: (bi, 0, 0))
    return pl.pallas_call(
        _na_ctx_kernel, grid=(b,), in_specs=[spec, spec, spec], out_specs=spec,
        out_shape=jax.ShapeDtypeStruct((b, c, w), BF16),
        compiler_params=_params(32, 1), name="na_ctx_attn",
    )(q, k, v)


def _na_bias(rpb, nt):
    rows = 4 * nt
    col = np.arange(GRID_W)
    c_start = np.clip(col - NA_COLS // 2, 0, GRID_W - NA_COLS)
    valid_c = (col[None, :] >= c_start[:, None]) & (col[None, :] < c_start[:, None] + NA_COLS)
    dc = np.clip(col[None, :] - col[:, None] + (NA_COLS - 1), 0, 2 * NA_COLS - 2)
    pick_dc = jnp.asarray(dc[:, :, None] == np.arange(2 * NA_COLS - 1), F32)
    by_col = jnp.einsum("hrd,qkd->hrqk", rpb, pick_dc, precision=lax.Precision.HIGHEST)
    variants = []
    for i in (0, min(1, nt - 1), nt - 1):
        rq = 4 * i + np.arange(4)
        start = np.clip(rq - NA_ROWS // 2, 0, rows - NA_ROWS)
        blocks = np.array([i - 1, i, i + 1])
        rk = (4 * blocks[:, None] + np.arange(4)[None, :]).reshape(-1)
        block_ok = np.repeat((blocks >= 0) & (blocks < nt), 4)
        valid_r = (rk[None, :] >= start[:, None]) & (rk[None, :] < start[:, None] + NA_ROWS) & block_ok[None, :]
        dr = np.clip(rk[None, :] - rq[:, None] + (NA_ROWS - 1), 0, 2 * NA_ROWS - 2)
        pick_dr = jnp.asarray(dr[:, :, None] == np.arange(2 * NA_ROWS - 1), F32)
        vals = jnp.einsum("abr,hrqk->haqbk", pick_dr, by_col, precision=lax.Precision.HIGHEST)
        ok = valid_r[:, None, :, None] & valid_c[None, :, None, :]
        variants.append(jnp.where(ok[None], vals, NEG_INF).reshape(NA_HEADS, 4 * GRID_W, 12 * GRID_W))
    return jnp.stack(variants).astype(F32)


STREAM_UNROLL = 4
SCORE_LOOKAHEAD = 2
MAX_LOG2_GAP = 64.0


def _col_max(s):
    keys = s.shape[0]
    if keys % 64 == 0 and keys > 64:
        s = jnp.max(s.reshape(keys // 64, 64, s.shape[1]), axis=0)
    return jnp.max(s, axis=0, keepdims=True)


def _flash_kernel(*refs, n_heads, group, tk, c_scale, has_ctx):
    if has_ctx:
        q_ref, k_ref, vt_ref, kx_ref, vtx_ref, o_ref, acc_scr, ot_scr = refs
    else:
        q_ref, k_ref, vt_ref, o_ref, acc_scr, ot_scr = refs
    tq = q_ref.shape[1]
    n_chunks = k_ref.shape[1] // tk
    n_slabs = n_heads // group
    heads = range(n_heads)

    def load_chunk(j):
        off = pl.multiple_of(j * tk, tk)
        return ([k_ref[sl, pl.ds(off, tk), :] for sl in range(n_slabs)],
                [vt_ref[sl, :, pl.ds(off, tk)] for sl in range(n_slabs)])

    def pv(vt, p):
        return _dot(vt, p.astype(BF16))

    def exact_step(ks, vts, ms):
        new_ms = []
        for hh in heads:
            s = _dot_nt(ks[hh // group], q_ref[hh])
            m_new = jnp.maximum(ms[hh], _col_max(s))
            p = jnp.exp2((s - m_new) * c_scale)
            alpha = jnp.exp2((ms[hh] - m_new) * c_scale)
            acc_scr[hh] = acc_scr[hh] * alpha + pv(vts[hh // group], p)
            new_ms.append(m_new)
        return tuple(new_ms)

    def streamed_steps(j0, n_steps, carry):
        ms, gaps = list(carry[0]), list(carry[1])
        chunks = [load_chunk(j0 + i) for i in range(n_steps)]
        units = [(i, hh) for i in range(n_steps) for hh in heads]
        score = lambda u: _dot_nt(chunks[u[0]][0][u[1] // group], q_ref[u[1]])
        pending = [score(u) for u in units[:SCORE_LOOKAHEAD]]
        for n, (i, hh) in enumerate(units):
            s = pending.pop(0)
            if n + SCORE_LOOKAHEAD < len(units):
                pending.append(score(units[n + SCORE_LOOKAHEAD]))
            p = jnp.exp2((s - ms[hh]) * c_scale)
            m_chunk = _col_max(s)
            m_new = jnp.maximum(ms[hh], m_chunk)
            alpha = jnp.exp2((ms[hh] - m_new) * c_scale)
            acc_scr[hh] = (acc_scr[hh] + pv(chunks[i][1][hh // group], p)) * alpha
            gaps[hh] = jnp.maximum(gaps[hh], (m_chunk - ms[hh]) * c_scale)
            ms[hh] = m_new
        return tuple(ms), tuple(gaps)

    def first_chunk():
        acc_scr[...] = jnp.zeros_like(acc_scr)
        ms = tuple(jnp.full((1, tq), NEG_INF, F32) for _ in heads)
        if has_ctx:
            return exact_step([kx_ref[sl] for sl in range(n_slabs)], [vtx_ref[sl] for sl in range(n_slabs)], ms), 0
        return exact_step(*load_chunk(0), ms), 1

    ms, first = first_chunk()
    gaps = tuple(jnp.zeros((1, tq), F32) for _ in heads)
    n_streamed = n_chunks - first
    per_iter = max(u for u in (STREAM_UNROLL, 2, 1) if n_streamed % u == 0)
    _, gaps = lax.fori_loop(0, n_streamed // per_iter,
                            lambda j, c: streamed_steps(first + j * per_iter, per_iter, c), (ms, gaps))
    worst = gaps[0]
    for g in gaps[1:]:
        worst = jnp.maximum(worst, g)

    @pl.when(jnp.max(worst) > MAX_LOG2_GAP)
    def _():
        ms, first = first_chunk()
        lax.fori_loop(first, n_chunks, lambda j, m: exact_step(*load_chunk(j), m), ms)

    for hh in heads:
        acc = acc_scr[hh]
        ot_scr[HEAD_DIM * hh:HEAD_DIM * (hh + 1), :] = acc[0:HEAD_DIM] * (1.0 / acc[HEAD_DIM:HEAD_DIM + 1])
    o_ref[...] = ot_scr[...].T.astype(BF16)


def _flash_call(q, k, vt, kx, vtx, scale, tq, tk):
    b, n_heads, t_q, w = q.shape
    slabs, t_k = k.shape[1], k.shape[2]
    has_ctx = kx is not None
    full4 = lambda bi, i: (bi, 0, 0, 0)
    in_specs = [pl.BlockSpec((None, n_heads, tq, w), lambda bi, i: (bi, 0, i, 0)),
                pl.BlockSpec((None, slabs, t_k, w), full4),
                pl.BlockSpec((None, slabs, V_AUG, t_k), full4)]
    args = [q, k, vt]
    if has_ctx:
        c = kx.shape[2]
        in_specs += [pl.BlockSpec((None, slabs, c, w), full4), pl.BlockSpec((None, slabs, V_AUG, c), full4)]
        args += [kx, vtx]
    kern = functools.partial(_flash_kernel, n_heads=n_heads, group=n_heads // slabs, tk=tk,
                             c_scale=scale * LOG2E, has_ctx=has_ctx)
    return pl.pallas_call(
        kern, grid=(b, t_q // tq), in_specs=in_specs,
        out_specs=pl.BlockSpec((None, tq, n_heads * HEAD_DIM), lambda bi, i: (bi, i, 0)),
        out_shape=jax.ShapeDtypeStruct((b, t_q, n_heads * HEAD_DIM), BF16),
        scratch_shapes=[pltpu.VMEM((n_heads, V_AUG, tq), F32), pltpu.VMEM((n_heads * HEAD_DIM, tq), F32)],
        compiler_params=_params(56, 2), name="flash",
    )(*args)


def _merge_kernel(x_ref, mod_ref, o0_ref, o1_ref, o2_ref, o3_ref, wg_ref, bg_ref, wb_ref, wo_ref, out_ref):
    x = x_ref[...]
    h = (_rms(x) * (1.0 + mod_ref[1:2, :]) + mod_ref[0:1, :]).astype(BF16)
    y = None
    for i, o_ref in enumerate((o0_ref, o1_ref, o2_ref, o3_ref)):
        gate = jax.nn.sigmoid(_dot(h, wg_ref[i]) + bg_ref[i])
        term = gate * _dot(o_ref[...], wb_ref[i])
        y = term if y is None else y + term
    out_ref[...] = x + mod_ref[2:3, :] * _dot(y.astype(BF16), wo_ref[...])


def _merge_call(x, mod, mod_row, branches, lw, tm):
    b, t, d = x.shape
    if mod_row is None:
        mod_map = lambda i, j: (j, 0, 0)
    else:
        mod_map = lambda i, j: (mod_row, 0, 0)
    tok = lambda i, j: (j, i, 0)
    br = pl.BlockSpec((None, tm, BRANCH_WIDTH), tok)
    return pl.pallas_call(
        _merge_kernel, grid=(t // tm, b),
        in_specs=[pl.BlockSpec((None, tm, d), tok), pl.BlockSpec((None, 6, d), mod_map), br, br, br, br,
                  _const_spec((N_BRANCH, d, d)), _const_spec((N_BRANCH, 1, d)),
                  _const_spec((N_BRANCH, BRANCH_WIDTH, d)), _const_spec((d, d))],
        out_specs=pl.BlockSpec((None, tm, d), tok),
        out_shape=jax.ShapeDtypeStruct((b, t, d), F32),
        compiler_params=_params(56, 2), name="merge",
    )(x, mod, *branches, lw["w_gate"], lw["b_gate"], lw["w_branch"], lw["w_out"])


def _ffn_kernel(x_ref, xp_ref, xn_ref, mod_ref, wu_ref, cw_ref, cb_ref, wd_ref, out_ref, h_scr, acc_scr):
    tm = x_ref.shape[0]
    i = pl.program_id(0)
    nt = pl.num_programs(0)
    sh, sc = mod_ref[3:4, :], mod_ref[4:5, :]

    def modulated(v):
        return _rms(v) * (1.0 + sc) + sh

    x = x_ref[...]
    h_scr[0:HALO, :] = jnp.where(i > 0, modulated(xp_ref[...]), 0.0).astype(BF16)
    h_scr[HALO:HALO + tm, :] = modulated(x).astype(BF16)
    h_scr[HALO + tm:, :] = jnp.where(i < nt - 1, modulated(xn_ref[...]), 0.0).astype(BF16)
    acc_scr[...] = jnp.zeros_like(acc_scr)
    rows = tm + 2 * HALO

    def chunk(j, carry):
        up = _dot(h_scr[...], wu_ref[j])
        cw = cw_ref[j]
        conv = (pltpu.roll(up, 1, 0)[HALO:HALO + tm] * cw[0:1, :] + up[HALO:HALO + tm] * cw[1:2, :]
                + pltpu.roll(up, rows - 1, 0)[HALO:HALO + tm] * cw[2:3, :] + cb_ref[j])
        a, g = conv[:, 0:FFN_CHUNK], conv[:, FFN_CHUNK:]
        act = (g * jax.nn.sigmoid(g) * a).astype(BF16)
        acc_scr[...] += _dot(act, wd_ref[j])
        return carry

    lax.fori_loop(0, N_FFN_CHUNKS, chunk, 0)
    out_ref[...] = x + mod_ref[5:6, :] * acc_scr[...]


def _ffn_call(x, mod, mod_row, lw, tm):
    b, t, d = x.shape
    nt = t // tm
    hb = tm // HALO
    last_halo = t // HALO - 1
    if mod_row is None:
        mod_map = lambda i, j: (j, 0, 0)
    else:
        mod_map = lambda i, j: (mod_row, 0, 0)
    tok = lambda i, j: (j, i, 0)
    return pl.pallas_call(
        _ffn_kernel, grid=(nt, b),
        in_specs=[pl.BlockSpec((None, tm, d), tok),
                  pl.BlockSpec((None, HALO, d), lambda i, j: (j, jnp.maximum(i * hb - 1, 0), 0)),
                  pl.BlockSpec((None, HALO, d), lambda i, j: (j, jnp.minimum((i + 1) * hb, last_halo), 0)),
                  pl.BlockSpec((None, 6, d), mod_map),
                  _const_spec((N_FFN_CHUNKS, d, 2 * FFN_CHUNK)),
                  _const_spec((N_FFN_CHUNKS, 3, 2 * FFN_CHUNK)),
                  _const_spec((N_FFN_CHUNKS, 1, 2 * FFN_CHUNK)),
                  _const_spec((N_FFN_CHUNKS, FFN_CHUNK, d))],
        out_specs=pl.BlockSpec((None, tm, d), tok),
        out_shape=jax.ShapeDtypeStruct((b, t, d), F32),
        scratch_shapes=[pltpu.VMEM((tm + 2 * HALO, d), BF16), pltpu.VMEM((tm, d), F32)],
        compiler_params=_params(56, 2), name="ffn",
    )(x, x, x, mod, lw["w_up"], lw["conv_w"], lw["conv_b"], lw["w_down"])


def _block_diag_ones(width, segs):
    m = np.zeros((width, width), np.float32)
    for lo, hi in segs:
        m[lo:hi, lo:hi] = 1.0
    return jnp.asarray(m, BF16)


def _rope_pattern(pos, width):
    half = width // 2
    inv = ROPE_THETA ** (-jnp.arange(half, dtype=F32) / half)
    ang = pos.astype(F32)[:, None] * inv[None, :]
    cos, sin = jnp.cos(ang), jnp.sin(ang)
    return jnp.concatenate([cos, cos], axis=1), jnp.concatenate([-sin, sin], axis=1)


def _rope_tables(t, identity):
    if identity:
        return {"gc": jnp.ones((t, 256), F32), "gs": jnp.zeros((t, 256), F32),
                "mc": jnp.ones((t, LANES), F32), "ms": jnp.zeros((t, LANES), F32)}
    pos = jnp.arange(t)
    rows, cols = pos // GRID_W, pos % GRID_W
    rc, rs = _rope_pattern(rows, HEAD_DIM // 2)
    cc, cs = _rope_pattern(cols, HEAD_DIM // 2)
    gc = jnp.tile(jnp.concatenate([rc, cc], axis=1), (1, 4))
    gs = jnp.tile(jnp.concatenate([rs, cs], axis=1), (1, 4))
    rc, rs = _rope_pattern(rows, MLA_ROPE // 2)
    cc, cs = _rope_pattern(cols, MLA_ROPE // 2)
    one, zero = jnp.ones((t, MLA_NOPE), F32), jnp.zeros((t, MLA_NOPE), F32)
    pad1, pad0 = jnp.ones((t, LANES - MLA_NOPE - MLA_ROPE), F32), jnp.zeros((t, LANES - MLA_NOPE - MLA_ROPE), F32)
    return {"gc": gc, "gs": gs,
            "mc": jnp.concatenate([one, rc, cc, pad1], axis=1),
            "ms": jnp.concatenate([zero, rs, cs, pad0], axis=1)}


def _arrange_layer(p):
    d = D_MODEL
    w_in = p["w_in"]
    na, mla = w_in[:, :NA_IN], w_in[:, NA_IN:NA_IN + MLA_IN]
    sg = w_in[:, NA_IN + MLA_IN:NA_IN + MLA_IN + SG_IN]
    gqa = w_in[:, NA_IN + MLA_IN + SG_IN:]
    z = lambda n: jnp.zeros((d, n), F32)
    cq, ckv, kr = mla[:, :MLA_Q_LORA], mla[:, MLA_Q_LORA:MLA_Q_LORA + MLA_KV_LORA], mla[:, MLA_Q_LORA + MLA_KV_LORA:]
    gq, gk, gv = gqa[:, :256], gqa[:, 256:384], gqa[:, 384:]
    gk_dup = jnp.concatenate([gk[:, :64], gk[:, :64], gk[:, 64:], gk[:, 64:]], axis=1)
    w_in_arr = jnp.concatenate([na, cq, ckv, z(256 - MLA_KV_LORA), z(MLA_NOPE), kr, z(LANES - MLA_NOPE - MLA_ROPE),
                                sg, gq, gk_dup, gv], axis=1).astype(BF16)

    w_uq = p["mla_w_uq"].reshape(MLA_Q_LORA, MLA_HEADS, MLA_NOPE + MLA_ROPE)
    w_uq = jnp.pad(w_uq, ((0, 0), (0, 0), (0, LANES - MLA_NOPE - MLA_ROPE))).reshape(MLA_Q_LORA, MLA_HEADS * LANES)
    w_ukv = p["mla_w_ukv"].reshape(MLA_KV_LORA, MLA_HEADS, MLA_NOPE + MLA_V)
    k_part = jnp.pad(w_ukv[:, :, :MLA_NOPE], ((0, 0), (0, 0), (0, LANES - MLA_NOPE))).reshape(MLA_KV_LORA, MLA_HEADS * LANES)
    v_part = w_ukv[:, :, MLA_NOPE:].reshape(MLA_KV_LORA, MLA_HEADS * MLA_V)
    w_ukv = jnp.pad(jnp.concatenate([k_part, v_part], axis=1), ((0, 256 - MLA_KV_LORA), (0, 0)))

    qg, kg = p["mla_q_norm"], p["mla_k_norm"]
    z1 = lambda n: jnp.zeros((n,), F32)
    gains = jnp.concatenate([
        jnp.tile(p["na_q_norm"], 4), jnp.tile(p["na_k_norm"], 4),
        p["mla_cq_norm"],
        p["mla_ckv_norm"], z1(256 - MLA_KV_LORA),
        jnp.tile(jnp.concatenate([qg, z1(LANES - MLA_NOPE - MLA_ROPE)]), 4),
        jnp.tile(jnp.concatenate([kg[:MLA_NOPE], z1(LANES - MLA_NOPE)]), 4),
        z1(MLA_NOPE), kg[MLA_NOPE:], z1(LANES - MLA_NOPE - MLA_ROPE),
        p["sg_v_norm"],
        jnp.tile(p["gqa_q_norm"], 4), jnp.tile(p["gqa_k_norm"], 4),
    ]).reshape(1, G_TOT)

    invq = np.tile(np.concatenate([np.full(MLA_NOPE, 1.0 / MLA_NOPE), np.full(MLA_ROPE, 1.0 / MLA_ROPE),
                                   np.ones(LANES - MLA_NOPE - MLA_ROPE)]), 4).astype(np.float32).reshape(1, 512)
    sgb = jnp.repeat(p["sg_b_s"].T, SG_WIDTH // SG_GROUPS, axis=1)

    w_up = p["w_up"]
    a_part = w_up[:, :FFN_DIM].reshape(d, N_FFN_CHUNKS, FFN_CHUNK)
    g_part = w_up[:, FFN_DIM:].reshape(d, N_FFN_CHUNKS, FFN_CHUNK)
    w_up_arr = jnp.concatenate([a_part, g_part], axis=2).transpose(1, 0, 2).astype(BF16)

    def chunked(v):
        lead = v.shape[:-1]
        a = v[..., :FFN_DIM].reshape(*lead, N_FFN_CHUNKS, FFN_CHUNK)
        g = v[..., FFN_DIM:].reshape(*lead, N_FFN_CHUNKS, FFN_CHUNK)
        return jnp.moveaxis(jnp.concatenate([a, g], axis=-1), -2, 0)

    return {
        "w_in": w_in_arr, "w_uq": w_uq.astype(BF16), "w_ukv": w_ukv.astype(BF16), "gains": gains,
        "s64": _block_diag_ones(MXU_DIM, [(64 * i, 64 * i + 64) for i in range(4)]),
        "smq": _block_diag_ones(MXU_DIM, [(0, 64), (64, 96), (128, 192), (192, 224)]),
        "invq": jnp.asarray(invq),
        "sgw": p["sg_w_s"].astype(BF16), "sgb": sgb,
        "w_gate": p["w_gate"].astype(BF16), "b_gate": p["b_gate"].reshape(N_BRANCH, 1, d),
        "w_branch": p["w_branch"].astype(BF16), "w_out": p["w_out"].astype(BF16),
        "w_up": w_up_arr, "conv_w": chunked(p["conv_w"]), "conv_b": chunked(p["conv_b"].reshape(1, -1)),
        "w_down": p["w_down"].reshape(N_FFN_CHUNKS, FFN_CHUNK, d).astype(BF16),
    }


def kernel(x, c, ctx, c_ctx, w_ada, b_ada, w_in, na_q_norm, na_k_norm, na_rpb, mla_cq_norm, mla_ckv_norm,
           mla_w_uq, mla_w_ukv, mla_q_norm, mla_k_norm, sg_v_norm, sg_w_s, sg_b_s, gqa_q_norm, gqa_k_norm,
           w_branch, w_gate, b_gate, w_out, w_up, conv_w, conv_b, w_down):
    b, t, d = x.shape
    n_ctx = ctx.shape[1]
    depth = w_in.shape[0]
    ctx_row = b
    cvec = jnp.zeros((8, d), F32).at[:b].set(c).at[ctx_row].set(c_ctx)
    mod_all = _ada_call(cvec, w_ada, b_ada).reshape(depth, 8, 6, d)

    tm = min(512, t)
    tq = min(256, t)
    tk = min(512, t)
    tabs_lat = _rope_tables(t, identity=False)
    tabs_ctx = _rope_tables(n_ctx, identity=True)
    mla_scale = (MLA_NOPE + MLA_ROPE) ** -0.5
    gqa_scale = HEAD_DIM ** -0.5

    cx = ctx
    for l in range(depth):
        lw = _arrange_layer({
            "w_in": w_in[l], "na_q_norm": na_q_norm[l], "na_k_norm": na_k_norm[l],
            "mla_cq_norm": mla_cq_norm[l], "mla_ckv_norm": mla_ckv_norm[l], "mla_w_uq": mla_w_uq[l],
            "mla_w_ukv": mla_w_ukv[l], "mla_q_norm": mla_q_norm[l], "mla_k_norm": mla_k_norm[l],
            "sg_v_norm": sg_v_norm[l], "sg_w_s": sg_w_s[l], "sg_b_s": sg_b_s[l],
            "gqa_q_norm": gqa_q_norm[l], "gqa_k_norm": gqa_k_norm[l],
            "w_branch": w_branch[l], "w_gate": w_gate[l], "b_gate": b_gate[l], "w_out": w_out[l],
            "w_up": w_up[l], "conv_w": conv_w[l], "conv_b": conv_b[l], "w_down": w_down[l]})
        mod = mod_all[l]
        pc = _in_call(cx, mod, ctx_row, lw, tabs_ctx, n_ctx)
        pz = _in_call(x, mod, None, lw, tabs_lat, tm)

        o_na = _na_call(pz["na_q"], pz["na_k"], pz["na_v"], pc["na_k"], pc["na_v"], _na_bias(na_rpb[l], t // (4 * GRID_W)))
        o_mla = _flash_call(pz["mla_q"], pz["mla_k"], pz["mla_vt"], pc["mla_k"], pc["mla_vt"], mla_scale, tq, tk)
        o_gqa = _flash_call(pz["gqa_q"], pz["gqa_k"], pz["gqa_vt"], pc["gqa_k"], pc["gqa_vt"], gqa_scale, tq, tk)
        x = _merge_call(x, mod, None, (o_na, o_mla, pz["sg_o"], o_gqa), lw, tm)
        x = _ffn_call(x, mod, None, lw, tm)

        if l < depth - 1:
            oc_na = _na_ctx_call(pc["na_q"], pc["na_k"], pc["na_v"])
            oc_mla = _flash_call(pc["mla_q"], pc["mla_k"], pc["mla_vt"], None, None, mla_scale, n_ctx, n_ctx)
            oc_gqa = _flash_call(pc["gqa_q"], pc["gqa_k"], pc["gqa_vt"], None, None, gqa_scale, n_ctx, n_ctx)
            cx = _merge_call(cx, mod, ctx_row, (oc_na, oc_mla, pc["sg_o"], oc_gqa), lw, n_ctx)
            cx = _ffn_call(cx, mod, ctx_row, lw, n_ctx)
    return x
```

```python
import functools
import math

import numpy as np
import jax
import jax.numpy as jnp
from jax import lax
from jax.experimental import pallas as pl
from jax.experimental.pallas import tpu as pltpu

F32 = jnp.float32
BF16 = jnp.bfloat16

D_MODEL = 1024
GRID_W = 64
HEAD_DIM = 64
NA_HEADS = 4
NA_ROWS = 8
NA_COLS = 16
MLA_HEADS = 4
MLA_Q_LORA = 256
MLA_KV_LORA = 192
MLA_NOPE = 64
MLA_ROPE = 32
MLA_V = 64
SG_GROUPS = 4
SG_CHUNK = 128
SG_WIDTH = 256
GQA_HEADS = 4
GQA_KV_HEADS = 2
N_BRANCH = 4
BRANCH_WIDTH = 256
FFN_DIM = 2816
ROPE_THETA = 10000.0
EPS = 1e-6
NEG_INF = -1e30

NA_IN = 3 * NA_HEADS * HEAD_DIM
MLA_IN = MLA_Q_LORA + MLA_KV_LORA + MLA_ROPE
SG_IN = 2 * SG_WIDTH
GQA_IN = (GQA_HEADS + 2 * GQA_KV_HEADS) * HEAD_DIM

LANES = 128
MXU_DIM = 256
BF16_SUBLANES = 16
V_AUG = HEAD_DIM + BF16_SUBLANES
LOG2E = math.log2(math.e)

C_NA = 0
C_MLA = C_NA + NA_IN
C_SG = C_MLA + 640
C_GQA = C_SG + SG_IN
IN_ARR = C_GQA + 640

G_NA, G_CQ, G_CKV, G_MQ, G_MKN, G_MKR, G_SG, G_GQA, G_TOT = 0, 512, 768, 1024, 1536, 2048, 2176, 2432, 2944

FFN_CHUNK = 256
N_FFN_CHUNKS = FFN_DIM // FFN_CHUNK
assert N_FFN_CHUNKS % 2 == 1
HALO = BF16_SUBLANES


def _const_spec(shape):
    nd = len(shape)
    return pl.BlockSpec(shape, lambda *_: (0,) * nd, pipeline_mode=pl.Buffered(1))


def _params(vmem_mb, n_grid):
    return pltpu.CompilerParams(dimension_semantics=("arbitrary",) * n_grid,
                                vmem_limit_bytes=vmem_mb * 1024 * 1024)


def _dot(a, b):
    return jnp.dot(a, b, preferred_element_type=F32)


def _dot_nt(a, b):
    return lax.dot_general(a, b, (((1,), (1,)), ((), ())), preferred_element_type=F32)


def _rms(xf):
    return xf * lax.rsqrt(jnp.mean(xf * xf, axis=-1, keepdims=True) + EPS)


def _segsum(x2, s_ref):
    hi = x2.astype(BF16)
    lo = (x2 - hi.astype(F32)).astype(BF16)
    s = s_ref[...]
    parts = []
    for j in range(x2.shape[1] // MXU_DIM):
        sl = slice(MXU_DIM * j, MXU_DIM * (j + 1))
        parts.append(_dot(hi[:, sl], s) + _dot(lo[:, sl], s))
    return parts[0] if len(parts) == 1 else jnp.concatenate(parts, axis=1)


def _rope(x, cos, sin_signed, shift, first_half):
    w = x.shape[1]
    partner = jnp.where(first_half, pltpu.roll(x, w - shift, 1), pltpu.roll(x, shift, 1))
    return x * cos + partner * sin_signed


def _ada_kernel(c_ref, w_ref, b_ref, o_ref):
    c = c_ref[...]
    a = (c * jax.nn.sigmoid(c)).astype(BF16)
    o_ref[...] = _dot(a, w_ref[...].astype(BF16)) + b_ref[...]


def _ada_call(cvec, w_ada, b_ada):
    n_layers, d, n = w_ada.shape
    tn = 1536
    return pl.pallas_call(
        _ada_kernel,
        grid=(n_layers, n // tn),
        in_specs=[pl.BlockSpec((8, d), lambda l, j: (0, 0)),
                  pl.BlockSpec((None, d, tn), lambda l, j: (l, 0, j)),
                  pl.BlockSpec((None, 1, tn), lambda l, j: (l, 0, j))],
        out_specs=pl.BlockSpec((None, 8, tn), lambda l, j: (l, 0, j)),
        out_shape=jax.ShapeDtypeStruct((n_layers, 8, n), F32),
        compiler_params=_params(32, 2),
        name="ada",
    )(cvec, w_ada, b_ada.reshape(n_layers, 1, n))


def _in_kernel(x_ref, mod_ref, w_in_ref, w_uq_ref, w_ukv_ref, g_ref, s64_ref, smq_ref, invq_ref,
               gc_ref, gs_ref, mc_ref, ms_ref, sgw_ref, sgb_ref,
               naq_ref, nak_ref, nav_ref, mq_ref, mk_ref, mvt_ref, sgo_ref, gq_ref, gk_ref, gvt_ref):
    tm = x_ref.shape[0]
    x = x_ref[...]
    h = (_rms(x) * (1.0 + mod_ref[1:2, :]) + mod_ref[0:1, :]).astype(BF16)

    def gain(off, width):
        return g_ref[:, off:off + width]

    inv_head = 1.0 / HEAD_DIM

    z = _dot(h, w_in_ref[:, C_NA:C_NA + NA_IN])
    qk = z[:, 0:512]
    qk = qk * lax.rsqrt(_segsum(qk * qk, s64_ref) * inv_head + EPS) * gain(G_NA, 512)
    naq_ref[...] = qk[:, 0:256].astype(BF16)
    nak_ref[...] = qk[:, 256:512].astype(BF16)
    nav_ref[...] = z[:, 512:768].astype(BF16)

    z = _dot(h, w_in_ref[:, C_MLA:C_MLA + 640])
    cq, ckv, kr = z[:, 0:256], z[:, 256:512], z[:, 512:640]
    cq = cq * lax.rsqrt(jnp.mean(cq * cq, axis=-1, keepdims=True) + EPS) * gain(G_CQ, 256)
    ckv = ckv * lax.rsqrt(jnp.sum(ckv * ckv, axis=-1, keepdims=True) * (1.0 / MLA_KV_LORA) + EPS) * gain(G_CKV, 256)
    q = _dot(cq.astype(BF16), w_uq_ref[...])
    q = q * lax.rsqrt(_segsum(q * q, smq_ref) * invq_ref[...] + EPS) * gain(G_MQ, 512)
    kv = _dot(ckv.astype(BF16), w_ukv_ref[...])
    kn = kv[:, 0:512]
    kn = kn * lax.rsqrt(_segsum(kn * kn, s64_ref) * inv_head + EPS) * gain(G_MKN, 512)
    kr = kr * lax.rsqrt(jnp.sum(kr * kr, axis=-1, keepdims=True) * (1.0 / MLA_ROPE) + EPS) * gain(G_MKR, 128)
    lane = lax.broadcasted_iota(jnp.int32, (tm, LANES), 1)
    first8 = (lane & 15) < 8
    mc, ms = mc_ref[...], ms_ref[...]
    kr = _rope(kr, mc, ms, 8, first8)
    for hh in range(MLA_HEADS):
        sl = slice(LANES * hh, LANES * (hh + 1))
        mq_ref[hh] = _rope(q[:, sl], mc, ms, 8, first8).astype(BF16)
        mk_ref[hh] = (kn[:, sl] + kr).astype(BF16)
    vt = kv[:, 512:768].T
    ones = jnp.ones((BF16_SUBLANES, tm), BF16)
    for hh in range(MLA_HEADS):
        mvt_ref[hh, 0:HEAD_DIM, :] = vt[HEAD_DIM * hh:HEAD_DIM * (hh + 1), :].astype(BF16)
        mvt_ref[hh, HEAD_DIM:V_AUG, :] = ones

    z = _dot(h, w_in_ref[:, C_SG:C_SG + SG_IN])
    uv = jax.nn.gelu(z)
    u, v = uv[:, 0:SG_WIDTH], uv[:, SG_WIDTH:]
    v = v * lax.rsqrt(jnp.mean(v * v, axis=-1, keepdims=True) + EPS) * gain(G_SG, 256)
    grp = lax.broadcasted_iota(jnp.int32, (SG_CHUNK, SG_WIDTH), 1) >> 6
    for c in range(tm // SG_CHUNK):
        rows = slice(SG_CHUNK * c, SG_CHUNK * (c + 1))
        vc = v[rows, :]
        mixed = sgb_ref[...]
        for gi in range(SG_GROUPS):
            mixed = mixed + _dot(sgw_ref[gi], jnp.where(grp == gi, vc, 0.0).astype(BF16))
        sgo_ref[rows, :] = (u[rows, :] * mixed).astype(BF16)

    z = _dot(h, w_in_ref[:, C_GQA:C_GQA + 640])
    qk = z[:, 0:512]
    qk = qk * lax.rsqrt(_segsum(qk * qk, s64_ref) * inv_head + EPS) * gain(G_GQA, 512)
    lane2 = lax.broadcasted_iota(jnp.int32, (tm, 2 * LANES), 1)
    first16 = (lane2 & 31) < 16
    gc, gs = gc_ref[...], gs_ref[...]
    qn = _rope(qk[:, 0:256], gc, gs, 16, first16)
    kn = _rope(qk[:, 256:512], gc, gs, 16, first16)
    half = lane >> 6
    for g in range(GQA_KV_HEADS):
        sl = slice(LANES * g, LANES * (g + 1))
        for r in range(GQA_HEADS // GQA_KV_HEADS):
            gq_ref[2 * g + r] = jnp.where(half == r, qn[:, sl], 0.0).astype(BF16)
        gk_ref[g] = kn[:, sl].astype(BF16)
    vt = z[:, 512:640].T
    for g in range(GQA_KV_HEADS):
        gvt_ref[g, 0:HEAD_DIM, :] = vt[HEAD_DIM * g:HEAD_DIM * (g + 1), :].astype(BF16)
        gvt_ref[g, HEAD_DIM:V_AUG, :] = ones


def _in_call(x, mod, mod_row, lw, tabs, tm):
    b, t, d = x.shape
    nt = t // tm
    if mod_row is None:
        mod_map = lambda i, j: (j, 0, 0)
    else:
        mod_map = lambda i, j: (mod_row, 0, 0)
    tok = lambda i, j: (j, i, 0)
    tab = lambda i, j: (i, 0)
    head_tok = lambda i, j: (j, 0, i, 0)
    head_t = lambda i, j: (j, 0, 0, i)
    in_specs = [
        pl.BlockSpec((None, tm, d), tok),
        pl.BlockSpec((None, 6, d), mod_map),
        _const_spec((d, IN_ARR)),
        _const_spec((MLA_Q_LORA, 512)),
        _const_spec((256, 768)),
        _const_spec((1, G_TOT)),
        _const_spec((MXU_DIM, MXU_DIM)),
        _const_spec((MXU_DIM, MXU_DIM)),
        _const_spec((1, 512)),
        pl.BlockSpec((tm, 256), tab), pl.BlockSpec((tm, 256), tab),
        pl.BlockSpec((tm, LANES), tab), pl.BlockSpec((tm, LANES), tab),
        _const_spec((SG_GROUPS, SG_CHUNK, SG_CHUNK)),
        _const_spec((SG_CHUNK, SG_WIDTH)),
    ]
    out_specs = [
        pl.BlockSpec((None, tm, 256), tok), pl.BlockSpec((None, tm, 256), tok), pl.BlockSpec((None, tm, 256), tok),
        pl.BlockSpec((None, MLA_HEADS, tm, LANES), head_tok),
        pl.BlockSpec((None, MLA_HEADS, tm, LANES), head_tok),
        pl.BlockSpec((None, MLA_HEADS, V_AUG, tm), head_t),
        pl.BlockSpec((None, tm, 256), tok),
        pl.BlockSpec((None, GQA_HEADS, tm, LANES), head_tok),
        pl.BlockSpec((None, GQA_KV_HEADS, tm, LANES), head_tok),
        pl.BlockSpec((None, GQA_KV_HEADS, V_AUG, tm), head_t),
    ]
    sds = jax.ShapeDtypeStruct
    out_shape = [
        sds((b, t, 256), BF16), sds((b, t, 256), BF16), sds((b, t, 256), BF16),
        sds((b, MLA_HEADS, t, LANES), BF16), sds((b, MLA_HEADS, t, LANES), BF16),
        sds((b, MLA_HEADS, V_AUG, t), BF16),
        sds((b, t, 256), BF16),
        sds((b, GQA_HEADS, t, LANES), BF16), sds((b, GQA_KV_HEADS, t, LANES), BF16),
        sds((b, GQA_KV_HEADS, V_AUG, t), BF16),
    ]
    outs = pl.pallas_call(
        _in_kernel, grid=(nt, b), in_specs=in_specs, out_specs=out_specs, out_shape=out_shape,
        compiler_params=_params(48, 2), name="in_proj",
    )(x, mod, lw["w_in"], lw["w_uq"], lw["w_ukv"], lw["gains"], lw["s64"], lw["smq"], lw["invq"],
      tabs["gc"], tabs["gs"], tabs["mc"], tabs["ms"], lw["sgw"], lw["sgb"])
    names = ("na_q", "na_k", "na_v", "mla_q", "mla_k", "mla_vt", "sg_o", "gqa_q", "gqa_k", "gqa_vt")
    return dict(zip(names, outs))


def _na_heads(q, k_blocks, v_blocks, biases):
    tq = q.shape[0]
    head_of_lane = lax.broadcasted_iota(jnp.int32, (tq, 256), 1) >> 6
    qf = q.astype(F32) * (HEAD_DIM ** -0.5)
    out = jnp.zeros((tq, 256), F32)
    for hh in range(NA_HEADS):
        qh = jnp.where(head_of_lane == hh, qf, 0.0).astype(BF16)
        scores = []
        for kb, bias in zip(k_blocks, biases):
            s = _dot_nt(qh, kb)
            if bias is not None:
                s = s + bias[hh]
            scores.append(s)
        m = scores[0].max(axis=-1, keepdims=True)
        for s in scores[1:]:
            m = jnp.maximum(m, s.max(axis=-1, keepdims=True))
        l = jnp.zeros((tq, 1), F32)
        o = jnp.zeros((tq, 256), F32)
        for s, vb in zip(scores, v_blocks):
            p = jnp.exp(s - m)
            l = l + p.sum(axis=-1, keepdims=True)
            o = o + _dot(p.astype(BF16), vb)
        out = out + jnp.where(head_of_lane == hh, o * (1.0 / l), 0.0)
    return out


def _na_kernel(q_ref, kp_ref, kc_ref, kn_ref, vp_ref, vc_ref, vn_ref, kx_ref, vx_ref, bias_ref, o_ref):
    biases = [bias_ref.at[:, :, 0:256], bias_ref.at[:, :, 256:512], bias_ref.at[:, :, 512:768], None]
    out = _na_heads(q_ref[...], [kp_ref[...], kc_ref[...], kn_ref[...], kx_ref[...]],
                    [vp_ref[...], vc_ref[...], vn_ref[...], vx_ref[...]], biases)
    o_ref[...] = out.astype(BF16)


def _na_call(q, k, v, kx, vx, bias):
    b, t, w = q.shape
    c = kx.shape[1]
    tq = 4 * GRID_W
    nt = t // tq
    cur = lambda bi, i: (bi, i, 0)
    prev = lambda bi, i: (bi, jnp.maximum(i - 1, 0), 0)
    nxt = lambda bi, i: (bi, jnp.minimum(i + 1, nt - 1), 0)
    ctx = lambda bi, i: (bi, 0, 0)
    variant = lambda bi, i: (jnp.where(i == 0, 0, jnp.where(i == nt - 1, 2, 1)), 0, 0, 0)
    blk = lambda m: pl.BlockSpec((None, tq, w), m)
    return pl.pallas_call(
        _na_kernel, grid=(b, nt),
        in_specs=[blk(cur), blk(prev), blk(cur), blk(nxt), blk(prev), blk(cur), blk(nxt),
                  pl.BlockSpec((None, c, w), ctx), pl.BlockSpec((None, c, w), ctx),
                  pl.BlockSpec((None, NA_HEADS, tq, 3 * tq), variant)],
        out_specs=blk(cur),
        out_shape=jax.ShapeDtypeStruct((b, t, w), BF16),
        compiler_params=_params(48, 2), name="na_attn",
    )(q, k, k, k, v, v, v, kx, vx, bias)


def _na_ctx_kernel(q_ref, k_ref, v_ref, o_ref):
    o_ref[...] = _na_heads(q_ref[...], [k_ref[...]], [v_ref[...]], [None]).astype(BF16)


def _na_ctx_call(q, k, v):
    b, c, w = q.shape
    spec = pl.BlockSpec((None, c, w), lambda bi: (bi, 0, 0))
    return pl.pallas_call(
        _na_ctx_kernel, grid=(b,), in_specs=[spec, spec, spec], out_specs=spec,
        out_shape=jax.ShapeDtypeStruct((b, c, w), BF16),
        compiler_params=_params(32, 1), name="na_ctx_attn",
    )(q, k, v)


def _na_bias(rpb, nt):
    rows = 4 * nt
    col = np.arange(GRID_W)
    c_start = np.clip(col - NA_COLS // 2, 0, GRID_W - NA_COLS)
    valid_c = (col[None, :] >= c_start[:, None]) & (col[None, :] < c_start[:, None] + NA_COLS)
    dc = np.clip(col[None, :] - col[:, None] + (NA_COLS - 1), 0, 2 * NA_COLS - 2)
    pick_dc = jnp.asarray(dc[:, :, None] == np.arange(2 * NA_COLS - 1), F32)
    by_col = jnp.einsum("hrd,qkd->hrqk", rpb, pick_dc, precision=lax.Precision.HIGHEST)
    variants = []
    for i in (0, min(1, nt - 1), nt - 1):
        rq = 4 * i + np.arange(4)
        start = np.clip(rq - NA_ROWS // 2, 0, rows - NA_ROWS)
        blocks = np.array([i - 1, i, i + 1])
        rk = (4 * blocks[:, None] + np.arange(4)[None, :]).reshape(-1)
        block_ok = np.repeat((blocks >= 0) & (blocks < nt), 4)
        valid_r = (rk[None, :] >= start[:, None]) & (rk[None, :] < start[:, None] + NA_ROWS) & block_ok[None, :]
        dr = np.clip(rk[None, :] - rq[:, None] + (NA_ROWS - 1), 0, 2 * NA_ROWS - 2)
        pick_dr = jnp.asarray(dr[:, :, None] == np.arange(2 * NA_ROWS - 1), F32)
        vals = jnp.einsum("abr,hrqk->haqbk", pick_dr, by_col, precision=lax.Precision.HIGHEST)
        ok = valid_r[:, None, :, None] & valid_c[None, :, None, :]
        variants.append(jnp.where(ok[None], vals, NEG_INF).reshape(NA_HEADS, 4 * GRID_W, 12 * GRID_W))
    return jnp.stack(variants).astype(F32)


STREAM_UNROLL = 4
SCORE_LOOKAHEAD = 2
MAX_LOG2_GAP = 64.0


def _col_max(s):
    keys = s.shape[0]
    if keys % 64 == 0 and keys > 64:
        s = jnp.max(s.reshape(keys // 64, 64, s.shape[1]), axis=0)
    return jnp.max(s, axis=0, keepdims=True)


def _flash_kernel(*refs, n_heads, group, tk, c_scale, has_ctx):
    if has_ctx:
        q_ref, k_ref, vt_ref, kx_ref, vtx_ref, o_ref, acc_scr, ot_scr = refs
    else:
        q_ref, k_ref, vt_ref, o_ref, acc_scr, ot_scr = refs
    tq = q_ref.shape[1]
    n_chunks = k_ref.shape[1] // tk
    n_slabs = n_heads // group
    heads = range(n_heads)

    def load_chunk(j):
        off = pl.multiple_of(j * tk, tk)
        return ([k_ref[sl, pl.ds(off, tk), :] for sl in range(n_slabs)],
                [vt_ref[sl, :, pl.ds(off, tk)] for sl in range(n_slabs)])

    def pv(vt, p):
        return _dot(vt, p.astype(BF16))

    def exact_step(ks, vts, ms):
        new_ms = []
        for hh in heads:
            s = _dot_nt(ks[hh // group], q_ref[hh])
            m_new = jnp.maximum(ms[hh], _col_max(s))
            p = jnp.exp2((s - m_new) * c_scale)
            alpha = jnp.exp2((ms[hh] - m_new) * c_scale)
            acc_scr[hh] = acc_scr[hh] * alpha + pv(vts[hh // group], p)
            new_ms.append(m_new)
        return tuple(new_ms)

    def streamed_steps(j0, n_steps, carry):
        ms, gaps = list(carry[0]), list(carry[1])
        chunks = [load_chunk(j0 + i) for i in range(n_steps)]
        units = [(i, hh) for i in range(n_steps) for hh in heads]
        score = lambda u: _dot_nt(chunks[u[0]][0][u[1] // group], q_ref[u[1]])
        pending = [score(u) for u in units[:SCORE_LOOKAHEAD]]
        for n, (i, hh) in enumerate(units):
            s = pending.pop(0)
            if n + SCORE_LOOKAHEAD < len(units):
                pending.append(score(units[n + SCORE_LOOKAHEAD]))
            p = jnp.exp2((s - ms[hh]) * c_scale)
            m_chunk = _col_max(s)
            m_new = jnp.maximum(ms[hh], m_chunk)
            alpha = jnp.exp2((ms[hh] - m_new) * c_scale)
            acc_scr[hh] = (acc_scr[hh] + pv(chunks[i][1][hh // group], p)) * alpha
            gaps[hh] = jnp.maximum(gaps[hh], (m_chunk - ms[hh]) * c_scale)
            ms[hh] = m_new
        return tuple(ms), tuple(gaps)

    def first_chunk():
        acc_scr[...] = jnp.zeros_like(acc_scr)
        ms = tuple(jnp.full((1, tq), NEG_INF, F32) for _ in heads)
        if has_ctx:
            return exact_step([kx_ref[sl] for sl in range(n_slabs)], [vtx_ref[sl] for sl in range(n_slabs)], ms), 0
        return exact_step(*load_chunk(0), ms), 1

    ms, first = first_chunk()
    gaps = tuple(jnp.zeros((1, tq), F32) for _ in heads)
    n_streamed = n_chunks - first
    per_iter = max(u for u in (STREAM_UNROLL, 2, 1) if n_streamed % u == 0)
    _, gaps = lax.fori_loop(0, n_streamed // per_iter,
                            lambda j, c: streamed_steps(first + j * per_iter, per_iter, c), (ms, gaps))
    worst = gaps[0]
    for g in gaps[1:]:
        worst = jnp.maximum(worst, g)

    @pl.when(jnp.max(worst) > MAX_LOG2_GAP)
    def _():
        ms, first = first_chunk()
        lax.fori_loop(first, n_chunks, lambda j, m: exact_step(*load_chunk(j), m), ms)

    for hh in heads:
        acc = acc_scr[hh]
        ot_scr[HEAD_DIM * hh:HEAD_DIM * (hh + 1), :] = acc[0:HEAD_DIM] * (1.0 / acc[HEAD_DIM:HEAD_DIM + 1])
    o_ref[...] = ot_scr[...].T.astype(BF16)


def _flash_call(q, k, vt, kx, vtx, scale, tq, tk):
    b, n_heads, t_q, w = q.shape
    slabs, t_k = k.shape[1], k.shape[2]
    has_ctx = kx is not None
    full4 = lambda bi, i: (bi, 0, 0, 0)
    in_specs = [pl.BlockSpec((None, n_heads, tq, w), lambda bi, i: (bi, 0, i, 0)),
                pl.BlockSpec((None, slabs, t_k, w), full4),
                pl.BlockSpec((None, slabs, V_AUG, t_k), full4)]
    args = [q, k, vt]
    if has_ctx:
        c = kx.shape[2]
        in_specs += [pl.BlockSpec((None, slabs, c, w), full4), pl.BlockSpec((None, slabs, V_AUG, c), full4)]
        args += [kx, vtx]
    kern = functools.partial(_flash_kernel, n_heads=n_heads, group=n_heads // slabs, tk=tk,
                             c_scale=scale * LOG2E, has_ctx=has_ctx)
    return pl.pallas_call(
        kern, grid=(b, t_q // tq), in_specs=in_specs,
        out_specs=pl.BlockSpec((None, tq, n_heads * HEAD_DIM), lambda bi, i: (bi, i, 0)),
        out_shape=jax.ShapeDtypeStruct((b, t_q, n_heads * HEAD_DIM), BF16),
        scratch_shapes=[pltpu.VMEM((n_heads, V_AUG, tq), F32), pltpu.VMEM((n_heads * HEAD_DIM, tq), F32)],
        compiler_params=_params(56, 2), name="flash",
    )(*args)


def _merge_kernel(x_ref, mod_ref, o0_ref, o1_ref, o2_ref, o3_ref, wg_ref, bg_ref, wb_ref, wo_ref, out_ref):
    x = x_ref[...]
    h = (_rms(x) * (1.0 + mod_ref[1:2, :]) + mod_ref[0:1, :]).astype(BF16)
    y = None
    for i, o_ref in enumerate((o0_ref, o1_ref, o2_ref, o3_ref)):
        gate = jax.nn.sigmoid(_dot(h, wg_ref[i]) + bg_ref[i])
        term = gate * _dot(o_ref[...], wb_ref[i])
        y = term if y is None else y + term
    out_ref[...] = x + mod_ref[2:3, :] * _dot(y.astype(BF16), wo_ref[...])


def _merge_call(x, mod, mod_row, branches, lw, tm):
    b, t, d = x.shape
    if mod_row is None:
        mod_map = lambda i, j: (j, 0, 0)
    else:
        mod_map = lambda i, j: (mod_row, 0, 0)
    tok = lambda i, j: (j, i, 0)
    br = pl.BlockSpec((None, tm, BRANCH_WIDTH), tok)
    return pl.pallas_call(
        _merge_kernel, grid=(t // tm, b),
        in_specs=[pl.BlockSpec((None, tm, d), tok), pl.BlockSpec((None, 6, d), mod_map), br, br, br, br,
                  _const_spec((N_BRANCH, d, d)), _const_spec((N_BRANCH, 1, d)),
                  _const_spec((N_BRANCH, BRANCH_WIDTH, d)), _const_spec((d, d))],
        out_specs=pl.BlockSpec((None, tm, d), tok),
        out_shape=jax.ShapeDtypeStruct((b, t, d), F32),
        compiler_params=_params(56, 2), name="merge",
    )(x, mod, *branches, lw["w_gate"], lw["b_gate"], lw["w_branch"], lw["w_out"])


def _ffn_kernel(x_ref, xp_ref, xn_ref, mod_ref, wu_ref, cw_ref, cb_ref, wd_ref, out_ref,
                h_scr, acc_scr, up0_scr, up1_scr):
    tm = x_ref.shape[0]
    i = pl.program_id(0)
    nt = pl.num_programs(0)
    sh, sc = mod_ref[3:4, :], mod_ref[4:5, :]

    def modulated(v):
        return _rms(v) * (1.0 + sc) + sh

    x = x_ref[...]
    h_scr[0:HALO, :] = jnp.where(i > 0, modulated(xp_ref[...]), 0.0).astype(BF16)
    h_scr[HALO:HALO + tm, :] = modulated(x).astype(BF16)
    h_scr[HALO + tm:, :] = jnp.where(i < nt - 1, modulated(xn_ref[...]), 0.0).astype(BF16)
    acc_scr[...] = jnp.zeros_like(acc_scr)
    rows = tm + 2 * HALO

    def up_proj(j, up_scr):
        up_scr[...] = _dot(h_scr[...], wu_ref[j])

    def consume(j, up_scr):
        cw = cw_ref[j]
        conv = (up_scr[HALO - 1:HALO - 1 + tm, :] * cw[0:1, :] + up_scr[HALO:HALO + tm, :] * cw[1:2, :]
                + up_scr[HALO + 1:HALO + 1 + tm, :] * cw[2:3, :] + cb_ref[j])
        a, g = conv[:, 0:FFN_CHUNK], conv[:, FFN_CHUNK:]
        act = (g * jax.nn.sigmoid(g) * a).astype(BF16)
        acc_scr[...] += _dot(act, wd_ref[j])

    up_proj(0, up0_scr)

    def chunk_pair(i, carry):
        j = 2 * i
        up_proj(j + 1, up1_scr)
        consume(j, up0_scr)
        up_proj(j + 2, up0_scr)
        consume(j + 1, up1_scr)
        return carry

    lax.fori_loop(0, (N_FFN_CHUNKS - 1) // 2, chunk_pair, 0)
    consume(N_FFN_CHUNKS - 1, up0_scr)
    out_ref[...] = x + mod_ref[5:6, :] * acc_scr[...]


def _ffn_call(x, mod, mod_row, lw, tm):
    b, t, d = x.shape
    nt = t // tm
    hb = tm // HALO
    last_halo = t // HALO - 1
    if mod_row is None:
        mod_map = lambda i, j: (j, 0, 0)
    else:
        mod_map = lambda i, j: (mod_row, 0, 0)
    tok = lambda i, j: (j, i, 0)
    return pl.pallas_call(
        _ffn_kernel, grid=(nt, b),
        in_specs=[pl.BlockSpec((None, tm, d), tok),
                  pl.BlockSpec((None, HALO, d), lambda i, j: (j, jnp.maximum(i * hb - 1, 0), 0)),
                  pl.BlockSpec((None, HALO, d), lambda i, j: (j, jnp.minimum((i + 1) * hb, last_halo), 0)),
                  pl.BlockSpec((None, 6, d), mod_map),
                  _const_spec((N_FFN_CHUNKS, d, 2 * FFN_CHUNK)),
                  _const_spec((N_FFN_CHUNKS, 3, 2 * FFN_CHUNK)),
                  _const_spec((N_FFN_CHUNKS, 1, 2 * FFN_CHUNK)),
                  _const_spec((N_FFN_CHUNKS, FFN_CHUNK, d))],
        out_specs=pl.BlockSpec((None, tm, d), tok),
        out_shape=jax.ShapeDtypeStruct((b, t, d), F32),
        scratch_shapes=[pltpu.VMEM((tm + 2 * HALO, d), BF16), pltpu.VMEM((tm, d), F32),
                        pltpu.VMEM((tm + 2 * HALO, 2 * FFN_CHUNK), F32),
                        pltpu.VMEM((tm + 2 * HALO, 2 * FFN_CHUNK), F32)],
        compiler_params=_params(56, 2), name="ffn",
    )(x, x, x, mod, lw["w_up"], lw["conv_w"], lw["conv_b"], lw["w_down"])


def _block_diag_ones(width, segs):
    m = np.zeros((width, width), np.float32)
    for lo, hi in segs:
        m[lo:hi, lo:hi] = 1.0
    return jnp.asarray(m, BF16)


def _rope_pattern(pos, width):
    half = width // 2
    inv = ROPE_THETA ** (-jnp.arange(half, dtype=F32) / half)
    ang = pos.astype(F32)[:, None] * inv[None, :]
    cos, sin = jnp.cos(ang), jnp.sin(ang)
    return jnp.concatenate([cos, cos], axis=1), jnp.concatenate([-sin, sin], axis=1)


def _rope_tables(t, identity):
    if identity:
        return {"gc": jnp.ones((t, 256), F32), "gs": jnp.zeros((t, 256), F32),
                "mc": jnp.ones((t, LANES), F32), "ms": jnp.zeros((t, LANES), F32)}
    pos = jnp.arange(t)
    rows, cols = pos // GRID_W, pos % GRID_W
    rc, rs = _rope_pattern(rows, HEAD_DIM // 2)
    cc, cs = _rope_pattern(cols, HEAD_DIM // 2)
    gc = jnp.tile(jnp.concatenate([rc, cc], axis=1), (1, 4))
    gs = jnp.tile(jnp.concatenate([rs, cs], axis=1), (1, 4))
    rc, rs = _rope_pattern(rows, MLA_ROPE // 2)
    cc, cs = _rope_pattern(cols, MLA_ROPE // 2)
    one, zero = jnp.ones((t, MLA_NOPE), F32), jnp.zeros((t, MLA_NOPE), F32)
    pad1, pad0 = jnp.ones((t, LANES - MLA_NOPE - MLA_ROPE), F32), jnp.zeros((t, LANES - MLA_NOPE - MLA_ROPE), F32)
    return {"gc": gc, "gs": gs,
            "mc": jnp.concatenate([one, rc, cc, pad1], axis=1),
            "ms": jnp.concatenate([zero, rs, cs, pad0], axis=1)}


def _arrange_layer(p):
    d = D_MODEL
    w_in = p["w_in"]
    na, mla = w_in[:, :NA_IN], w_in[:, NA_IN:NA_IN + MLA_IN]
    sg = w_in[:, NA_IN + MLA_IN:NA_IN + MLA_IN + SG_IN]
    gqa = w_in[:, NA_IN + MLA_IN + SG_IN:]
    z = lambda n: jnp.zeros((d, n), F32)
    cq, ckv, kr = mla[:, :MLA_Q_LORA], mla[:, MLA_Q_LORA:MLA_Q_LORA + MLA_KV_LORA], mla[:, MLA_Q_LORA + MLA_KV_LORA:]
    gq, gk, gv = gqa[:, :256], gqa[:, 256:384], gqa[:, 384:]
    gk_dup = jnp.concatenate([gk[:, :64], gk[:, :64], gk[:, 64:], gk[:, 64:]], axis=1)
    w_in_arr = jnp.concatenate([na, cq, ckv, z(256 - MLA_KV_LORA), z(MLA_NOPE), kr, z(LANES - MLA_NOPE - MLA_ROPE),
                                sg, gq, gk_dup, gv], axis=1).astype(BF16)

    w_uq = p["mla_w_uq"].reshape(MLA_Q_LORA, MLA_HEADS, MLA_NOPE + MLA_ROPE)
    w_uq = jnp.pad(w_uq, ((0, 0), (0, 0), (0, LANES - MLA_NOPE - MLA_ROPE))).reshape(MLA_Q_LORA, MLA_HEADS * LANES)
    w_ukv = p["mla_w_ukv"].reshape(MLA_KV_LORA, MLA_HEADS, MLA_NOPE + MLA_V)
    k_part = jnp.pad(w_ukv[:, :, :MLA_NOPE], ((0, 0), (0, 0), (0, LANES - MLA_NOPE))).reshape(MLA_KV_LORA, MLA_HEADS * LANES)
    v_part = w_ukv[:, :, MLA_NOPE:].reshape(MLA_KV_LORA, MLA_HEADS * MLA_V)
    w_ukv = jnp.pad(jnp.concatenate([k_part, v_part], axis=1), ((0, 256 - MLA_KV_LORA), (0, 0)))

    qg, kg = p["mla_q_norm"], p["mla_k_norm"]
    z1 = lambda n: jnp.zeros((n,), F32)
    gains = jnp.concatenate([
        jnp.tile(p["na_q_norm"], 4), jnp.tile(p["na_k_norm"], 4),
        p["mla_cq_norm"],
        p["mla_ckv_norm"], z1(256 - MLA_KV_LORA),
        jnp.tile(jnp.concatenate([qg, z1(LANES - MLA_NOPE - MLA_ROPE)]), 4),
        jnp.tile(jnp.concatenate([kg[:MLA_NOPE], z1(LANES - MLA_NOPE)]), 4),
        z1(MLA_NOPE), kg[MLA_NOPE:], z1(LANES - MLA_NOPE - MLA_ROPE),
        p["sg_v_norm"],
        jnp.tile(p["gqa_q_norm"], 4), jnp.tile(p["gqa_k_norm"], 4),
    ]).reshape(1, G_TOT)

    invq = np.tile(np.concatenate([np.full(MLA_NOPE, 1.0 / MLA_NOPE), np.full(MLA_ROPE, 1.0 / MLA_ROPE),
                                   np.ones(LANES - MLA_NOPE - MLA_ROPE)]), 4).astype(np.float32).reshape(1, 512)
    sgb = jnp.repeat(p["sg_b_s"].T, SG_WIDTH // SG_GROUPS, axis=1)

    w_up = p["w_up"]
    a_part = w_up[:, :FFN_DIM].reshape(d, N_FFN_CHUNKS, FFN_CHUNK)
    g_part = w_up[:, FFN_DIM:].reshape(d, N_FFN_CHUNKS, FFN_CHUNK)
    w_up_arr = jnp.concatenate([a_part, g_part], axis=2).transpose(1, 0, 2).astype(BF16)

    def chunked(v):
        lead = v.shape[:-1]
        a = v[..., :FFN_DIM].reshape(*lead, N_FFN_CHUNKS, FFN_CHUNK)
        g = v[..., FFN_DIM:].reshape(*lead, N_FFN_CHUNKS, FFN_CHUNK)
        return jnp.moveaxis(jnp.concatenate([a, g], axis=-1), -2, 0)

    return {
        "w_in": w_in_arr, "w_uq": w_uq.astype(BF16), "w_ukv": w_ukv.astype(BF16), "gains": gains,
        "s64": _block_diag_ones(MXU_DIM, [(64 * i, 64 * i + 64) for i in range(4)]),
        "smq": _block_diag_ones(MXU_DIM, [(0, 64), (64, 96), (128, 192), (192, 224)]),
        "invq": jnp.asarray(invq),
        "sgw": p["sg_w_s"].astype(BF16), "sgb": sgb,
        "w_gate": p["w_gate"].astype(BF16), "b_gate": p["b_gate"].reshape(N_BRANCH, 1, d),
        "w_branch": p["w_branch"].astype(BF16), "w_out": p["w_out"].astype(BF16),
        "w_up": w_up_arr, "conv_w": chunked(p["conv_w"]), "conv_b": chunked(p["conv_b"].reshape(1, -1)),
        "w_down": p["w_down"].reshape(N_FFN_CHUNKS, FFN_CHUNK, d).astype(BF16),
    }


def kernel(x, c, ctx, c_ctx, w_ada, b_ada, w_in, na_q_norm, na_k_norm, na_rpb, mla_cq_norm, mla_ckv_norm,
           mla_w_uq, mla_w_ukv, mla_q_norm, mla_k_norm, sg_v_norm, sg_w_s, sg_b_s, gqa_q_norm, gqa_k_norm,
           w_branch, w_gate, b_gate, w_out, w_up, conv_w, conv_b, w_down):
    b, t, d = x.shape
    n_ctx = ctx.shape[1]
    depth = w_in.shape[0]
    ctx_row = b
    cvec = jnp.zeros((8, d), F32).at[:b].set(c).at[ctx_row].set(c_ctx)
    mod_all = _ada_call(cvec, w_ada, b_ada).reshape(depth, 8, 6, d)

    tm = min(512, t)
    tq = min(256, t)
    tk = min(512, t)
    tabs_lat = _rope_tables(t, identity=False)
    tabs_ctx = _rope_tables(n_ctx, identity=True)
    mla_scale = (MLA_NOPE + MLA_ROPE) ** -0.5
    gqa_scale = HEAD_DIM ** -0.5

    cx = ctx
    for l in range(depth):
        lw = _arrange_layer({
            "w_in": w_in[l], "na_q_norm": na_q_norm[l], "na_k_norm": na_k_norm[l],
            "mla_cq_norm": mla_cq_norm[l], "mla_ckv_norm": mla_ckv_norm[l], "mla_w_uq": mla_w_uq[l],
            "mla_w_ukv": mla_w_ukv[l], "mla_q_norm": mla_q_norm[l], "mla_k_norm": mla_k_norm[l],
            "sg_v_norm": sg_v_norm[l], "sg_w_s": sg_w_s[l], "sg_b_s": sg_b_s[l],
            "gqa_q_norm": gqa_q_norm[l], "gqa_k_norm": gqa_k_norm[l],
            "w_branch": w_branch[l], "w_gate": w_gate[l], "b_gate": b_gate[l], "w_out": w_out[l],
            "w_up": w_up[l], "conv_w": conv_w[l], "conv_b": conv_b[l], "w_down": w_down[l]})
        mod = mod_all[l]
        pc = _in_call(cx, mod, ctx_row, lw, tabs_ctx, n_ctx)
        pz = _in_call(x, mod, None, lw, tabs_lat, tm)

        o_na = _na_call(pz["na_q"], pz["na_k"], pz["na_v"], pc["na_k"], pc["na_v"], _na_bias(na_rpb[l], t // (4 * GRID_W)))
        o_mla = _flash_call(pz["mla_q"], pz["mla_k"], pz["mla_vt"], pc["mla_k"], pc["mla_vt"], mla_scale, tq, tk)
        o_gqa = _flash_call(pz["gqa_q"], pz["gqa_k"], pz["gqa_vt"], pc["gqa_k"], pc["gqa_vt"], gqa_scale, tq, tk)
        x = _merge_call(x, mod, None, (o_na, o_mla, pz["sg_o"], o_gqa), lw, tm)
        x = _ffn_call(x, mod, None, lw, tm)

        if l < depth - 1:
            oc_na = _na_ctx_call(pc["na_q"], pc["na_k"], pc["na_v"])
            oc_mla = _flash_call(pc["mla_q"], pc["mla_k"], pc["mla_vt"], None, None, mla_scale, n_ctx, n_ctx)
            oc_gqa = _flash_call(pc["gqa_q"], pc["gqa_k"], pc["gqa_vt"], None, None, gqa_scale, n_ctx, n_ctx)
            cx = _merge_call(cx, mod, ctx_row, (oc_na, oc_mla, pc["sg_o"], oc_gqa), lw, n_ctx)
            cx = _ffn_call(cx, mod, ctx_row, lw, n_ctx)
    return x
```

```python
import functools
import math

import numpy as np
import jax
import jax.numpy as jnp
from jax import lax
from jax.experimental import pallas as pl
from jax.experimental.pallas import tpu as pltpu

F32 = jnp.float32
BF16 = jnp.bfloat16

D_MODEL = 1024
GRID_W = 64
HEAD_DIM = 64
NA_HEADS = 4
NA_ROWS = 8
NA_COLS = 16
MLA_HEADS = 4
MLA_Q_LORA = 256
MLA_KV_LORA = 192
MLA_NOPE = 64
MLA_ROPE = 32
MLA_V = 64
SG_GROUPS = 4
SG_CHUNK = 128
SG_WIDTH = 256
GQA_HEADS = 4
GQA_KV_HEADS = 2
N_BRANCH = 4
BRANCH_WIDTH = 256
FFN_DIM = 2816
ROPE_THETA = 10000.0
EPS = 1e-6
NEG_INF = -1e30

NA_IN = 3 * NA_HEADS * HEAD_DIM
MLA_IN = MLA_Q_LORA + MLA_KV_LORA + MLA_ROPE
SG_IN = 2 * SG_WIDTH
GQA_IN = (GQA_HEADS + 2 * GQA_KV_HEADS) * HEAD_DIM

LANES = 128
MXU_DIM = 256
BF16_SUBLANES = 16
V_AUG = HEAD_DIM + BF16_SUBLANES
LOG2E = math.log2(math.e)
MLA_LOGIT_SCALE = (MLA_NOPE + MLA_ROPE) ** -0.5 * LOG2E
GQA_LOGIT_SCALE = HEAD_DIM ** -0.5 * LOG2E

C_NA = 0
C_MLA = C_NA + NA_IN
C_SG = C_MLA + 640
C_GQA = C_SG + SG_IN
IN_ARR = C_GQA + 640

G_NA, G_CQ, G_CKV, G_MQ, G_MKN, G_MKR, G_SG, G_GQA, G_TOT = 0, 512, 768, 1024, 1536, 2048, 2176, 2432, 2944

FFN_CHUNK = 256
N_FFN_CHUNKS = FFN_DIM // FFN_CHUNK
assert N_FFN_CHUNKS % 2 == 1
HALO = BF16_SUBLANES


def _const_spec(shape):
    nd = len(shape)
    return pl.BlockSpec(shape, lambda *_: (0,) * nd, pipeline_mode=pl.Buffered(1))


def _params(vmem_mb, n_grid):
    return pltpu.CompilerParams(dimension_semantics=("arbitrary",) * n_grid,
                                vmem_limit_bytes=vmem_mb * 1024 * 1024)


def _dot(a, b):
    return jnp.dot(a, b, preferred_element_type=F32)


def _dot_nt(a, b):
    return lax.dot_general(a, b, (((1,), (1,)), ((), ())), preferred_element_type=F32)


def _rms(xf):
    return xf * lax.rsqrt(jnp.mean(xf * xf, axis=-1, keepdims=True) + EPS)


def _segsum(x2, s_ref):
    hi = x2.astype(BF16)
    lo = (x2 - hi.astype(F32)).astype(BF16)
    s = s_ref[...]
    parts = []
    for j in range(x2.shape[1] // MXU_DIM):
        sl = slice(MXU_DIM * j, MXU_DIM * (j + 1))
        parts.append(_dot(hi[:, sl], s) + _dot(lo[:, sl], s))
    return parts[0] if len(parts) == 1 else jnp.concatenate(parts, axis=1)


def _rope(x, cos, sin_signed, shift, first_half):
    w = x.shape[1]
    partner = jnp.where(first_half, pltpu.roll(x, w - shift, 1), pltpu.roll(x, shift, 1))
    return x * cos + partner * sin_signed


def _ada_kernel(c_ref, w_ref, b_ref, o_ref):
    c = c_ref[...]
    a = (c * jax.nn.sigmoid(c)).astype(BF16)
    o_ref[...] = _dot(a, w_ref[...].astype(BF16)) + b_ref[...]


def _ada_call(cvec, w_ada, b_ada):
    n_layers, d, n = w_ada.shape
    tn = 1536
    return pl.pallas_call(
        _ada_kernel,
        grid=(n_layers, n // tn),
        in_specs=[pl.BlockSpec((8, d), lambda l, j: (0, 0)),
                  pl.BlockSpec((None, d, tn), lambda l, j: (l, 0, j)),
                  pl.BlockSpec((None, 1, tn), lambda l, j: (l, 0, j))],
        out_specs=pl.BlockSpec((None, 8, tn), lambda l, j: (l, 0, j)),
        out_shape=jax.ShapeDtypeStruct((n_layers, 8, n), F32),
        compiler_params=_params(32, 2),
        name="ada",
    )(cvec, w_ada, b_ada.reshape(n_layers, 1, n))


def _in_kernel(x_ref, mod_ref, w_in_ref, w_uq_ref, w_ukv_ref, g_ref, s64_ref, smq_ref, invq_ref,
               gc_ref, gs_ref, mc_ref, ms_ref, sgw_ref, sgb_ref,
               naq_ref, nak_ref, nav_ref, mq_ref, mk_ref, mvt_ref, sgo_ref, gq_ref, gk_ref, gvt_ref):
    tm = x_ref.shape[0]
    x = x_ref[...]
    h = (_rms(x) * (1.0 + mod_ref[1:2, :]) + mod_ref[0:1, :]).astype(BF16)

    def gain(off, width):
        return g_ref[:, off:off + width]

    inv_head = 1.0 / HEAD_DIM

    z = _dot(h, w_in_ref[:, C_NA:C_NA + NA_IN])
    qk = z[:, 0:512]
    qk = qk * lax.rsqrt(_segsum(qk * qk, s64_ref) * inv_head + EPS) * gain(G_NA, 512)
    naq_ref[...] = qk[:, 0:256].astype(BF16)
    nak_ref[...] = qk[:, 256:512].astype(BF16)
    nav_ref[...] = z[:, 512:768].astype(BF16)

    z = _dot(h, w_in_ref[:, C_MLA:C_MLA + 640])
    cq, ckv, kr = z[:, 0:256], z[:, 256:512], z[:, 512:640]
    cq = cq * lax.rsqrt(jnp.mean(cq * cq, axis=-1, keepdims=True) + EPS) * gain(G_CQ, 256)
    ckv = ckv * lax.rsqrt(jnp.sum(ckv * ckv, axis=-1, keepdims=True) * (1.0 / MLA_KV_LORA) + EPS) * gain(G_CKV, 256)
    q = _dot(cq.astype(BF16), w_uq_ref[...])
    q = q * lax.rsqrt(_segsum(q * q, smq_ref) * invq_ref[...] + EPS) * gain(G_MQ, 512)
    kv = _dot(ckv.astype(BF16), w_ukv_ref[...])
    kn = kv[:, 0:512]
    kn = kn * lax.rsqrt(_segsum(kn * kn, s64_ref) * inv_head + EPS) * gain(G_MKN, 512)
    kr = kr * lax.rsqrt(jnp.sum(kr * kr, axis=-1, keepdims=True) * (1.0 / MLA_ROPE) + EPS) * gain(G_MKR, 128)
    lane = lax.broadcasted_iota(jnp.int32, (tm, LANES), 1)
    first8 = (lane & 15) < 8
    mc, ms = mc_ref[...], ms_ref[...]
    kr = _rope(kr, mc, ms, 8, first8)
    for hh in range(MLA_HEADS):
        sl = slice(LANES * hh, LANES * (hh + 1))
        mq_ref[hh] = _rope(q[:, sl], mc, ms, 8, first8).astype(BF16)
        mk_ref[hh] = (kn[:, sl] + kr).astype(BF16)
    vt = kv[:, 512:768].T
    ones = jnp.ones((BF16_SUBLANES, tm), BF16)
    for hh in range(MLA_HEADS):
        mvt_ref[hh, 0:HEAD_DIM, :] = vt[HEAD_DIM * hh:HEAD_DIM * (hh + 1), :].astype(BF16)
        mvt_ref[hh, HEAD_DIM:V_AUG, :] = ones

    z = _dot(h, w_in_ref[:, C_SG:C_SG + SG_IN])
    uv = jax.nn.gelu(z)
    u, v = uv[:, 0:SG_WIDTH], uv[:, SG_WIDTH:]
    v = v * lax.rsqrt(jnp.mean(v * v, axis=-1, keepdims=True) + EPS) * gain(G_SG, 256)
    grp = lax.broadcasted_iota(jnp.int32, (SG_CHUNK, SG_WIDTH), 1) >> 6
    for c in range(tm // SG_CHUNK):
        rows = slice(SG_CHUNK * c, SG_CHUNK * (c + 1))
        vc = v[rows, :]
        mixed = sgb_ref[...]
        for gi in range(SG_GROUPS):
            mixed = mixed + _dot(sgw_ref[gi], jnp.where(grp == gi, vc, 0.0).astype(BF16))
        sgo_ref[rows, :] = (u[rows, :] * mixed).astype(BF16)

    z = _dot(h, w_in_ref[:, C_GQA:C_GQA + 640])
    qk = z[:, 0:512]
    qk = qk * lax.rsqrt(_segsum(qk * qk, s64_ref) * inv_head + EPS) * gain(G_GQA, 512)
    lane2 = lax.broadcasted_iota(jnp.int32, (tm, 2 * LANES), 1)
    first16 = (lane2 & 31) < 16
    gc, gs = gc_ref[...], gs_ref[...]
    qn = _rope(qk[:, 0:256], gc, gs, 16, first16)
    kn = _rope(qk[:, 256:512], gc, gs, 16, first16)
    half = lane >> 6
    for g in range(GQA_KV_HEADS):
        sl = slice(LANES * g, LANES * (g + 1))
        for r in range(GQA_HEADS // GQA_KV_HEADS):
            gq_ref[2 * g + r] = jnp.where(half == r, qn[:, sl], 0.0).astype(BF16)
        gk_ref[g] = kn[:, sl].astype(BF16)
    vt = z[:, 512:640].T
    for g in range(GQA_KV_HEADS):
        gvt_ref[g, 0:HEAD_DIM, :] = vt[HEAD_DIM * g:HEAD_DIM * (g + 1), :].astype(BF16)
        gvt_ref[g, HEAD_DIM:V_AUG, :] = ones


def _in_call(x, mod, mod_row, lw, tabs, tm):
    b, t, d = x.shape
    nt = t // tm
    if mod_row is None:
        mod_map = lambda i, j: (j, 0, 0)
    else:
        mod_map = lambda i, j: (mod_row, 0, 0)
    tok = lambda i, j: (j, i, 0)
    tab = lambda i, j: (i, 0)
    head_tok = lambda i, j: (j, 0, i, 0)
    head_t = lambda i, j: (j, 0, 0, i)
    in_specs = [
        pl.BlockSpec((None, tm, d), tok),
        pl.BlockSpec((None, 6, d), mod_map),
        _const_spec((d, IN_ARR)),
        _const_spec((MLA_Q_LORA, 512)),
        _const_spec((256, 768)),
        _const_spec((1, G_TOT)),
        _const_spec((MXU_DIM, MXU_DIM)),
        _const_spec((MXU_DIM, MXU_DIM)),
        _const_spec((1, 512)),
        pl.BlockSpec((tm, 256), tab), pl.BlockSpec((tm, 256), tab),
        pl.BlockSpec((tm, LANES), tab), pl.BlockSpec((tm, LANES), tab),
        _const_spec((SG_GROUPS, SG_CHUNK, SG_CHUNK)),
        _const_spec((SG_CHUNK, SG_WIDTH)),
    ]
    out_specs = [
        pl.BlockSpec((None, tm, 256), tok), pl.BlockSpec((None, tm, 256), tok), pl.BlockSpec((None, tm, 256), tok),
        pl.BlockSpec((None, MLA_HEADS, tm, LANES), head_tok),
        pl.BlockSpec((None, MLA_HEADS, tm, LANES), head_tok),
        pl.BlockSpec((None, MLA_HEADS, V_AUG, tm), head_t),
        pl.BlockSpec((None, tm, 256), tok),
        pl.BlockSpec((None, GQA_HEADS, tm, LANES), head_tok),
        pl.BlockSpec((None, GQA_KV_HEADS, tm, LANES), head_tok),
        pl.BlockSpec((None, GQA_KV_HEADS, V_AUG, tm), head_t),
    ]
    sds = jax.ShapeDtypeStruct
    out_shape = [
        sds((b, t, 256), BF16), sds((b, t, 256), BF16), sds((b, t, 256), BF16),
        sds((b, MLA_HEADS, t, LANES), BF16), sds((b, MLA_HEADS, t, LANES), BF16),
        sds((b, MLA_HEADS, V_AUG, t), BF16),
        sds((b, t, 256), BF16),
        sds((b, GQA_HEADS, t, LANES), BF16), sds((b, GQA_KV_HEADS, t, LANES), BF16),
        sds((b, GQA_KV_HEADS, V_AUG, t), BF16),
    ]
    outs = pl.pallas_call(
        _in_kernel, grid=(nt, b), in_specs=in_specs, out_specs=out_specs, out_shape=out_shape,
        compiler_params=_params(48, 2), name="in_proj",
    )(x, mod, lw["w_in"], lw["w_uq"], lw["w_ukv"], lw["gains"], lw["s64"], lw["smq"], lw["invq"],
      tabs["gc"], tabs["gs"], tabs["mc"], tabs["ms"], lw["sgw"], lw["sgb"])
    names = ("na_q", "na_k", "na_v", "mla_q", "mla_k", "mla_vt", "sg_o", "gqa_q", "gqa_k", "gqa_vt")
    return dict(zip(names, outs))


def _na_heads(q, k_blocks, v_blocks, biases):
    tq = q.shape[0]
    head_of_lane = lax.broadcasted_iota(jnp.int32, (tq, 256), 1) >> 6
    qf = q.astype(F32) * (HEAD_DIM ** -0.5)
    out = jnp.zeros((tq, 256), F32)
    for hh in range(NA_HEADS):
        qh = jnp.where(head_of_lane == hh, qf, 0.0).astype(BF16)
        scores = []
        for kb, bias in zip(k_blocks, biases):
            s = _dot_nt(qh, kb)
            if bias is not None:
                s = s + bias[hh]
            scores.append(s)
        m = scores[0].max(axis=-1, keepdims=True)
        for s in scores[1:]:
            m = jnp.maximum(m, s.max(axis=-1, keepdims=True))
        l = jnp.zeros((tq, 1), F32)
        o = jnp.zeros((tq, 256), F32)
        for s, vb in zip(scores, v_blocks):
            p = jnp.exp(s - m)
            l = l + p.sum(axis=-1, keepdims=True)
            o = o + _dot(p.astype(BF16), vb)
        out = out + jnp.where(head_of_lane == hh, o * (1.0 / l), 0.0)
    return out


def _na_kernel(q_ref, kp_ref, kc_ref, kn_ref, vp_ref, vc_ref, vn_ref, kx_ref, vx_ref, bias_ref, o_ref):
    biases = [bias_ref.at[:, :, 0:256], bias_ref.at[:, :, 256:512], bias_ref.at[:, :, 512:768], None]
    out = _na_heads(q_ref[...], [kp_ref[...], kc_ref[...], kn_ref[...], kx_ref[...]],
                    [vp_ref[...], vc_ref[...], vn_ref[...], vx_ref[...]], biases)
    o_ref[...] = out.astype(BF16)


def _na_call(q, k, v, kx, vx, bias):
    b, t, w = q.shape
    c = kx.shape[1]
    tq = 4 * GRID_W
    nt = t // tq
    cur = lambda bi, i: (bi, i, 0)
    prev = lambda bi, i: (bi, jnp.maximum(i - 1, 0), 0)
    nxt = lambda bi, i: (bi, jnp.minimum(i + 1, nt - 1), 0)
    ctx = lambda bi, i: (bi, 0, 0)
    variant = lambda bi, i: (jnp.where(i == 0, 0, jnp.where(i == nt - 1, 2, 1)), 0, 0, 0)
    blk = lambda m: pl.BlockSpec((None, tq, w), m)
    return pl.pallas_call(
        _na_kernel, grid=(b, nt),
        in_specs=[blk(cur), blk(prev), blk(cur), blk(nxt), blk(prev), blk(cur), blk(nxt),
                  pl.BlockSpec((None, c, w), ctx), pl.BlockSpec((None, c, w), ctx),
                  pl.BlockSpec((None, NA_HEADS, tq, 3 * tq), variant)],
        out_specs=blk(cur),
        out_shape=jax.ShapeDtypeStruct((b, t, w), BF16),
        compiler_params=_params(48, 2), name="na_attn",
    )(q, k, k, k, v, v, v, kx, vx, bias)


def _na_ctx_kernel(q_ref, k_ref, v_ref, o_ref):
    o_ref[...] = _na_heads(q_ref[...], [k_ref[...]], [v_ref[...]], [None]).astype(BF16)


def _na_ctx_call(q, k, v):
    b, c, w = q.shape
    spec = pl.BlockSpec((None, c, w), lambda bi: (bi, 0, 0))
    return pl.pallas_call(
        _na_ctx_kernel, grid=(b,), in_specs=[spec, spec, spec], out_specs=spec,
        out_shape=jax.ShapeDtypeStruct((b, c, w), BF16),
        compiler_params=_params(32, 1), name="na_ctx_attn",
    )(q, k, v)


def _na_bias(rpb, nt):
    rows = 4 * nt
    col = np.arange(GRID_W)
    c_start = np.clip(col - NA_COLS // 2, 0, GRID_W - NA_COLS)
    valid_c = (col[None, :] >= c_start[:, None]) & (col[None, :] < c_start[:, None] + NA_COLS)
    dc = np.clip(col[None, :] - col[:, None] + (NA_COLS - 1), 0, 2 * NA_COLS - 2)
    pick_dc = jnp.asarray(dc[:, :, None] == np.arange(2 * NA_COLS - 1), F32)
    by_col = jnp.einsum("hrd,qkd->hrqk", rpb, pick_dc, precision=lax.Precision.HIGHEST)
    variants = []
    for i in (0, min(1, nt - 1), nt - 1):
        rq = 4 * i + np.arange(4)
        start = np.clip(rq - NA_ROWS // 2, 0, rows - NA_ROWS)
        blocks = np.array([i - 1, i, i + 1])
        rk = (4 * blocks[:, None] + np.arange(4)[None, :]).reshape(-1)
        block_ok = np.repeat((blocks >= 0) & (blocks < nt), 4)
        valid_r = (rk[None, :] >= start[:, None]) & (rk[None, :] < start[:, None] + NA_ROWS) & block_ok[None, :]
        dr = np.clip(rk[None, :] - rq[:, None] + (NA_ROWS - 1), 0, 2 * NA_ROWS - 2)
        pick_dr = jnp.asarray(dr[:, :, None] == np.arange(2 * NA_ROWS - 1), F32)
        vals = jnp.einsum("abr,hrqk->haqbk", pick_dr, by_col, precision=lax.Precision.HIGHEST)
        ok = valid_r[:, None, :, None] & valid_c[None, :, None, :]
        variants.append(jnp.where(ok[None], vals, NEG_INF).reshape(NA_HEADS, 4 * GRID_W, 12 * GRID_W))
    return jnp.stack(variants).astype(F32)


STREAM_UNROLL = 8
SCORE_LOOKAHEAD = 2
SEED_KEYS = 64
F32_SAFE_MAX = 3e38
MIN_DENOMINATOR = 2.0 ** -40


def _col_max(s):
    keys = s.shape[0]
    if keys % 64 == 0 and keys > 64:
        s = jnp.max(s.reshape(keys // 64, 64, s.shape[1]), axis=0)
    return jnp.max(s, axis=0, keepdims=True)


def _flash_kernel(*refs, n_heads, group, tk, has_ctx):
    if has_ctx:
        q_ref, k_ref, vt_ref, kx_ref, vtx_ref, o_ref, acc_scr, ot_scr = refs
    else:
        q_ref, k_ref, vt_ref, o_ref, acc_scr, ot_scr = refs
    tq = q_ref.shape[1]
    n_chunks = k_ref.shape[1] // tk
    n_slabs = n_heads // group
    heads = range(n_heads)

    def load_chunk(j):
        off = pl.multiple_of(j * tk, tk)
        return ([k_ref[sl, pl.ds(off, tk), :] for sl in range(n_slabs)],
                [vt_ref[sl, :, pl.ds(off, tk)] for sl in range(n_slabs)])

    def pv(vt, p):
        return _dot(vt, p.astype(BF16))

    def exact_step(ks, vts, ms):
        new_ms = []
        for hh in heads:
            s = _dot_nt(ks[hh // group], q_ref[hh])
            m_new = jnp.maximum(ms[hh], _col_max(s))
            p = jnp.exp2(s - m_new)
            acc_scr[hh] = acc_scr[hh] * jnp.exp2(ms[hh] - m_new) + pv(vts[hh // group], p)
            new_ms.append(m_new)
        return tuple(new_ms)

    def stream(chunks, ms):
        units = [(i, hh) for i in range(len(chunks)) for hh in heads]
        score = lambda u: _dot_nt(chunks[u[0]][0][u[1] // group], q_ref[u[1]])
        pending = [score(u) for u in units[:SCORE_LOOKAHEAD]]
        for n, (i, hh) in enumerate(units):
            s = pending.pop(0)
            if n + SCORE_LOOKAHEAD < len(units):
                pending.append(score(units[n + SCORE_LOOKAHEAD]))
            acc_scr[hh] += pv(chunks[i][1][hh // group], jnp.exp2(s - ms[hh]))

    ctx_chunk = ([kx_ref[sl] for sl in range(n_slabs)], [vtx_ref[sl] for sl in range(n_slabs)]) if has_ctx else None
    seed_ref = kx_ref if has_ctx else k_ref
    ms = tuple(_col_max(_dot_nt(seed_ref[hh // group, 0:SEED_KEYS, :], q_ref[hh])) for hh in heads)
    acc_scr[...] = jnp.zeros_like(acc_scr)
    if has_ctx:
        stream([ctx_chunk], ms)
    per_iter = max(u for u in (STREAM_UNROLL, 4, 2, 1) if n_chunks % u == 0)

    def stream_body(j, carry):
        stream([load_chunk(j * per_iter + i) for i in range(per_iter)], ms)
        return carry

    lax.fori_loop(0, n_chunks // per_iter, stream_body, 0)
    unusable = jnp.zeros((V_AUG, tq), F32)
    for hh in heads:
        acc = acc_scr[hh]
        unusable = jnp.maximum(unusable, jnp.where(jnp.abs(acc) < F32_SAFE_MAX, 0.0, 1.0))
        unusable = jnp.maximum(unusable, jnp.where(acc[HEAD_DIM:HEAD_DIM + 1] > MIN_DENOMINATOR, 0.0, 1.0))

    @pl.when(jnp.max(unusable) > 0.0)
    def _():
        acc_scr[...] = jnp.zeros_like(acc_scr)
        ms = tuple(jnp.full((1, tq), NEG_INF, F32) for _ in heads)
        if has_ctx:
            ms = exact_step(*ctx_chunk, ms)
        lax.fori_loop(0, n_chunks, lambda j, m: exact_step(*load_chunk(j), m), ms)

    for hh in heads:
        acc = acc_scr[hh]
        ot_scr[HEAD_DIM * hh:HEAD_DIM * (hh + 1), :] = acc[0:HEAD_DIM] * (1.0 / acc[HEAD_DIM:HEAD_DIM + 1])
    o_ref[...] = ot_scr[...].T.astype(BF16)


def _flash_call(q, k, vt, kx, vtx, tq, tk):
    b, n_heads, t_q, w = q.shape
    slabs, t_k = k.shape[1], k.shape[2]
    has_ctx = kx is not None
    full4 = lambda bi, i: (bi, 0, 0, 0)
    in_specs = [pl.BlockSpec((None, n_heads, tq, w), lambda bi, i: (bi, 0, i, 0)),
                pl.BlockSpec((None, slabs, t_k, w), full4),
                pl.BlockSpec((None, slabs, V_AUG, t_k), full4)]
    args = [q, k, vt]
    if has_ctx:
        c = kx.shape[2]
        in_specs += [pl.BlockSpec((None, slabs, c, w), full4), pl.BlockSpec((None, slabs, V_AUG, c), full4)]
        args += [kx, vtx]
    kern = functools.partial(_flash_kernel, n_heads=n_heads, group=n_heads // slabs, tk=tk,
                             has_ctx=has_ctx)
    return pl.pallas_call(
        kern, grid=(b, t_q // tq), in_specs=in_specs,
        out_specs=pl.BlockSpec((None, tq, n_heads * HEAD_DIM), lambda bi, i: (bi, i, 0)),
        out_shape=jax.ShapeDtypeStruct((b, t_q, n_heads * HEAD_DIM), BF16),
        scratch_shapes=[pltpu.VMEM((n_heads, V_AUG, tq), F32), pltpu.VMEM((n_heads * HEAD_DIM, tq), F32)],
        compiler_params=_params(56, 2), name="flash",
    )(*args)


def _merge_kernel(x_ref, mod_ref, o0_ref, o1_ref, o2_ref, o3_ref, wg_ref, bg_ref, wb_ref, wo_ref, out_ref):
    x = x_ref[...]
    h = (_rms(x) * (1.0 + mod_ref[1:2, :]) + mod_ref[0:1, :]).astype(BF16)
    y = None
    for i, o_ref in enumerate((o0_ref, o1_ref, o2_ref, o3_ref)):
        gate = jax.nn.sigmoid(_dot(h, wg_ref[i]) + bg_ref[i])
        term = gate * _dot(o_ref[...], wb_ref[i])
        y = term if y is None else y + term
    out_ref[...] = x + mod_ref[2:3, :] * _dot(y.astype(BF16), wo_ref[...])


def _merge_call(x, mod, mod_row, branches, lw, tm):
    b, t, d = x.shape
    if mod_row is None:
        mod_map = lambda i, j: (j, 0, 0)
    else:
        mod_map = lambda i, j: (mod_row, 0, 0)
    tok = lambda i, j: (j, i, 0)
    br = pl.BlockSpec((None, tm, BRANCH_WIDTH), tok)
    return pl.pallas_call(
        _merge_kernel, grid=(t // tm, b),
        in_specs=[pl.BlockSpec((None, tm, d), tok), pl.BlockSpec((None, 6, d), mod_map), br, br, br, br,
                  _const_spec((N_BRANCH, d, d)), _const_spec((N_BRANCH, 1, d)),
                  _const_spec((N_BRANCH, BRANCH_WIDTH, d)), _const_spec((d, d))],
        out_specs=pl.BlockSpec((None, tm, d), tok),
        out_shape=jax.ShapeDtypeStruct((b, t, d), F32),
        compiler_params=_params(56, 2), name="merge",
    )(x, mod, *branches, lw["w_gate"], lw["b_gate"], lw["w_branch"], lw["w_out"])


def _ffn_kernel(x_ref, xp_ref, xn_ref, mod_ref, wu_ref, cw_ref, cb_ref, wd_ref, out_ref,
                h_scr, acc_scr, up0_scr, up1_scr):
    tm = x_ref.shape[0]
    i = pl.program_id(0)
    nt = pl.num_programs(0)
    sh, sc = mod_ref[3:4, :], mod_ref[4:5, :]

    def modulated(v):
        return _rms(v) * (1.0 + sc) + sh

    x = x_ref[...]
    h_scr[0:HALO, :] = jnp.where(i > 0, modulated(xp_ref[...]), 0.0).astype(BF16)
    h_scr[HALO:HALO + tm, :] = modulated(x).astype(BF16)
    h_scr[HALO + tm:, :] = jnp.where(i < nt - 1, modulated(xn_ref[...]), 0.0).astype(BF16)
    acc_scr[...] = jnp.zeros_like(acc_scr)
    rows = tm + 2 * HALO

    def up_proj(j, up_scr):
        up_scr[...] = _dot(h_scr[...], wu_ref[j])

    def consume(j, up_scr):
        cw = cw_ref[j]
        conv = (up_scr[HALO - 1:HALO - 1 + tm, :] * cw[0:1, :] + up_scr[HALO:HALO + tm, :] * cw[1:2, :]
                + up_scr[HALO + 1:HALO + 1 + tm, :] * cw[2:3, :] + cb_ref[j])
        a, g = conv[:, 0:FFN_CHUNK], conv[:, FFN_CHUNK:]
        act = (g * jax.nn.sigmoid(g) * a).astype(BF16)
        acc_scr[...] += _dot(act, wd_ref[j])

    up_proj(0, up0_scr)

    def chunk_pair(i, carry):
        j = 2 * i
        up_proj(j + 1, up1_scr)
        consume(j, up0_scr)
        up_proj(j + 2, up0_scr)
        consume(j + 1, up1_scr)
        return carry

    lax.fori_loop(0, (N_FFN_CHUNKS - 1) // 2, chunk_pair, 0)
    consume(N_FFN_CHUNKS - 1, up0_scr)
    out_ref[...] = x + mod_ref[5:6, :] * acc_scr[...]


def _ffn_call(x, mod, mod_row, lw, tm):
    b, t, d = x.shape
    nt = t // tm
    hb = tm // HALO
    last_halo = t // HALO - 1
    if mod_row is None:
        mod_map = lambda i, j: (j, 0, 0)
    else:
        mod_map = lambda i, j: (mod_row, 0, 0)
    tok = lambda i, j: (j, i, 0)
    return pl.pallas_call(
        _ffn_kernel, grid=(nt, b),
        in_specs=[pl.BlockSpec((None, tm, d), tok),
                  pl.BlockSpec((None, HALO, d), lambda i, j: (j, jnp.maximum(i * hb - 1, 0), 0)),
                  pl.BlockSpec((None, HALO, d), lambda i, j: (j, jnp.minimum((i + 1) * hb, last_halo), 0)),
                  pl.BlockSpec((None, 6, d), mod_map),
                  _const_spec((N_FFN_CHUNKS, d, 2 * FFN_CHUNK)),
                  _const_spec((N_FFN_CHUNKS, 3, 2 * FFN_CHUNK)),
                  _const_spec((N_FFN_CHUNKS, 1, 2 * FFN_CHUNK)),
                  _const_spec((N_FFN_CHUNKS, FFN_CHUNK, d))],
        out_specs=pl.BlockSpec((None, tm, d), tok),
        out_shape=jax.ShapeDtypeStruct((b, t, d), F32),
        scratch_shapes=[pltpu.VMEM((tm + 2 * HALO, d), BF16), pltpu.VMEM((tm, d), F32),
                        pltpu.VMEM((tm + 2 * HALO, 2 * FFN_CHUNK), F32),
                        pltpu.VMEM((tm + 2 * HALO, 2 * FFN_CHUNK), F32)],
        compiler_params=_params(56, 2), name="ffn",
    )(x, x, x, mod, lw["w_up"], lw["conv_w"], lw["conv_b"], lw["w_down"])


def _block_diag_ones(width, segs):
    m = np.zeros((width, width), np.float32)
    for lo, hi in segs:
        m[lo:hi, lo:hi] = 1.0
    return jnp.asarray(m, BF16)


def _rope_pattern(pos, width):
    half = width // 2
    inv = ROPE_THETA ** (-jnp.arange(half, dtype=F32) / half)
    ang = pos.astype(F32)[:, None] * inv[None, :]
    cos, sin = jnp.cos(ang), jnp.sin(ang)
    return jnp.concatenate([cos, cos], axis=1), jnp.concatenate([-sin, sin], axis=1)


def _rope_tables(t, identity):
    if identity:
        return {"gc": jnp.ones((t, 256), F32), "gs": jnp.zeros((t, 256), F32),
                "mc": jnp.ones((t, LANES), F32), "ms": jnp.zeros((t, LANES), F32)}
    pos = jnp.arange(t)
    rows, cols = pos // GRID_W, pos % GRID_W
    rc, rs = _rope_pattern(rows, HEAD_DIM // 2)
    cc, cs = _rope_pattern(cols, HEAD_DIM // 2)
    gc = jnp.tile(jnp.concatenate([rc, cc], axis=1), (1, 4))
    gs = jnp.tile(jnp.concatenate([rs, cs], axis=1), (1, 4))
    rc, rs = _rope_pattern(rows, MLA_ROPE // 2)
    cc, cs = _rope_pattern(cols, MLA_ROPE // 2)
    one, zero = jnp.ones((t, MLA_NOPE), F32), jnp.zeros((t, MLA_NOPE), F32)
    pad1, pad0 = jnp.ones((t, LANES - MLA_NOPE - MLA_ROPE), F32), jnp.zeros((t, LANES - MLA_NOPE - MLA_ROPE), F32)
    return {"gc": gc, "gs": gs,
            "mc": jnp.concatenate([one, rc, cc, pad1], axis=1),
            "ms": jnp.concatenate([zero, rs, cs, pad0], axis=1)}


def _arrange_layer(p):
    d = D_MODEL
    w_in = p["w_in"]
    na, mla = w_in[:, :NA_IN], w_in[:, NA_IN:NA_IN + MLA_IN]
    sg = w_in[:, NA_IN + MLA_IN:NA_IN + MLA_IN + SG_IN]
    gqa = w_in[:, NA_IN + MLA_IN + SG_IN:]
    z = lambda n: jnp.zeros((d, n), F32)
    cq, ckv, kr = mla[:, :MLA_Q_LORA], mla[:, MLA_Q_LORA:MLA_Q_LORA + MLA_KV_LORA], mla[:, MLA_Q_LORA + MLA_KV_LORA:]
    gq, gk, gv = gqa[:, :256], gqa[:, 256:384], gqa[:, 384:]
    gk_dup = jnp.concatenate([gk[:, :64], gk[:, :64], gk[:, 64:], gk[:, 64:]], axis=1)
    w_in_arr = jnp.concatenate([na, cq, ckv, z(256 - MLA_KV_LORA), z(MLA_NOPE), kr, z(LANES - MLA_NOPE - MLA_ROPE),
                                sg, gq, gk_dup, gv], axis=1).astype(BF16)

    w_uq = p["mla_w_uq"].reshape(MLA_Q_LORA, MLA_HEADS, MLA_NOPE + MLA_ROPE)
    w_uq = jnp.pad(w_uq, ((0, 0), (0, 0), (0, LANES - MLA_NOPE - MLA_ROPE))).reshape(MLA_Q_LORA, MLA_HEADS * LANES)
    w_ukv = p["mla_w_ukv"].reshape(MLA_KV_LORA, MLA_HEADS, MLA_NOPE + MLA_V)
    k_part = jnp.pad(w_ukv[:, :, :MLA_NOPE], ((0, 0), (0, 0), (0, LANES - MLA_NOPE))).reshape(MLA_KV_LORA, MLA_HEADS * LANES)
    v_part = w_ukv[:, :, MLA_NOPE:].reshape(MLA_KV_LORA, MLA_HEADS * MLA_V)
    w_ukv = jnp.pad(jnp.concatenate([k_part, v_part], axis=1), ((0, 256 - MLA_KV_LORA), (0, 0)))

    qg, kg = p["mla_q_norm"], p["mla_k_norm"]
    z1 = lambda n: jnp.zeros((n,), F32)
    gains = jnp.concatenate([
        jnp.tile(p["na_q_norm"], 4), jnp.tile(p["na_k_norm"], 4),
        p["mla_cq_norm"],
        p["mla_ckv_norm"], z1(256 - MLA_KV_LORA),
        jnp.tile(jnp.concatenate([qg * MLA_LOGIT_SCALE, z1(LANES - MLA_NOPE - MLA_ROPE)]), 4),
        jnp.tile(jnp.concatenate([kg[:MLA_NOPE], z1(LANES - MLA_NOPE)]), 4),
        z1(MLA_NOPE), kg[MLA_NOPE:], z1(LANES - MLA_NOPE - MLA_ROPE),
        p["sg_v_norm"],
        jnp.tile(p["gqa_q_norm"] * GQA_LOGIT_SCALE, 4), jnp.tile(p["gqa_k_norm"], 4),
    ]).reshape(1, G_TOT)

    invq = np.tile(np.concatenate([np.full(MLA_NOPE, 1.0 / MLA_NOPE), np.full(MLA_ROPE, 1.0 / MLA_ROPE),
                                   np.ones(LANES - MLA_NOPE - MLA_ROPE)]), 4).astype(np.float32).reshape(1, 512)
    sgb = jnp.repeat(p["sg_b_s"].T, SG_WIDTH // SG_GROUPS, axis=1)

    w_up = p["w_up"]
    a_part = w_up[:, :FFN_DIM].reshape(d, N_FFN_CHUNKS, FFN_CHUNK)
    g_part = w_up[:, FFN_DIM:].reshape(d, N_FFN_CHUNKS, FFN_CHUNK)
    w_up_arr = jnp.concatenate([a_part, g_part], axis=2).transpose(1, 0, 2).astype(BF16)

    def chunked(v):
        lead = v.shape[:-1]
        a = v[..., :FFN_DIM].reshape(*lead, N_FFN_CHUNKS, FFN_CHUNK)
        g = v[..., FFN_DIM:].reshape(*lead, N_FFN_CHUNKS, FFN_CHUNK)
        return jnp.moveaxis(jnp.concatenate([a, g], axis=-1), -2, 0)

    return {
        "w_in": w_in_arr, "w_uq": w_uq.astype(BF16), "w_ukv": w_ukv.astype(BF16), "gains": gains,
        "s64": _block_diag_ones(MXU_DIM, [(64 * i, 64 * i + 64) for i in range(4)]),
        "smq": _block_diag_ones(MXU_DIM, [(0, 64), (64, 96), (128, 192), (192, 224)]),
        "invq": jnp.asarray(invq),
        "sgw": p["sg_w_s"].astype(BF16), "sgb": sgb,
        "w_gate": p["w_gate"].astype(BF16), "b_gate": p["b_gate"].reshape(N_BRANCH, 1, d),
        "w_branch": p["w_branch"].astype(BF16), "w_out": p["w_out"].astype(BF16),
        "w_up": w_up_arr, "conv_w": chunked(p["conv_w"]), "conv_b": chunked(p["conv_b"].reshape(1, -1)),
        "w_down": p["w_down"].reshape(N_FFN_CHUNKS, FFN_CHUNK, d).astype(BF16),
    }


def kernel(x, c, ctx, c_ctx, w_ada, b_ada, w_in, na_q_norm, na_k_norm, na_rpb, mla_cq_norm, mla_ckv_norm,
           mla_w_uq, mla_w_ukv, mla_q_norm, mla_k_norm, sg_v_norm, sg_w_s, sg_b_s, gqa_q_norm, gqa_k_norm,
           w_branch, w_gate, b_gate, w_out, w_up, conv_w, conv_b, w_down):
    b, t, d = x.shape
    n_ctx = ctx.shape[1]
    depth = w_in.shape[0]
    ctx_row = b
    cvec = jnp.zeros((8, d), F32).at[:b].set(c).at[ctx_row].set(c_ctx)
    mod_all = _ada_call(cvec, w_ada, b_ada).reshape(depth, 8, 6, d)

    tm = min(512, t)
    tq = min(256, t)
    tk = min(512, t)
    tabs_lat = _rope_tables(t, identity=False)
    tabs_ctx = _rope_tables(n_ctx, identity=True)

    cx = ctx
    for l in range(depth):
        lw = _arrange_layer({
            "w_in": w_in[l], "na_q_norm": na_q_norm[l], "na_k_norm": na_k_norm[l],
            "mla_cq_norm": mla_cq_norm[l], "mla_ckv_norm": mla_ckv_norm[l], "mla_w_uq": mla_w_uq[l],
            "mla_w_ukv": mla_w_ukv[l], "mla_q_norm": mla_q_norm[l], "mla_k_norm": mla_k_norm[l],
            "sg_v_norm": sg_v_norm[l], "sg_w_s": sg_w_s[l], "sg_b_s": sg_b_s[l],
            "gqa_q_norm": gqa_q_norm[l], "gqa_k_norm": gqa_k_norm[l],
            "w_branch": w_branch[l], "w_gate": w_gate[l], "b_gate": b_gate[l], "w_out": w_out[l],
            "w_up": w_up[l], "conv_w": conv_w[l], "conv_b": conv_b[l], "w_down": w_down[l]})
        mod = mod_all[l]
        pc = _in_call(cx, mod, ctx_row, lw, tabs_ctx, n_ctx)
        pz = _in_call(x, mod, None, lw, tabs_lat, tm)

        o_na = _na_call(pz["na_q"], pz["na_k"], pz["na_v"], pc["na_k"], pc["na_v"], _na_bias(na_rpb[l], t // (4 * GRID_W)))
        o_mla = _flash_call(pz["mla_q"], pz["mla_k"], pz["mla_vt"], pc["mla_k"], pc["mla_vt"], tq, tk)
        o_gqa = _flash_call(pz["gqa_q"], pz["gqa_k"], pz["gqa_vt"], pc["gqa_k"], pc["gqa_vt"], tq, tk)
        x = _merge_call(x, mod, None, (o_na, o_mla, pz["sg_o"], o_gqa), lw, tm)
        x = _ffn_call(x, mod, None, lw, tm)

        if l < depth - 1:
            oc_na = _na_ctx_call(pc["na_q"], pc["na_k"], pc["na_v"])
            oc_mla = _flash_call(pc["mla_q"], pc["mla_k"], pc["mla_vt"], None, None, n_ctx, n_ctx)
            oc_gqa = _flash_call(pc["gqa_q"], pc["gqa_k"], pc["gqa_vt"], None, None, n_ctx, n_ctx)
            cx = _merge_call(cx, mod, ctx_row, (oc_na, oc_mla, pc["sg_o"], oc_gqa), lw, n_ctx)
            cx = _ffn_call(cx, mod, ctx_row, lw, n_ctx)
    return x
```

```python
import functools
import math

import numpy as np
import jax
import jax.numpy as jnp
from jax import lax
from jax.experimental import pallas as pl
from jax.experimental.pallas import tpu as pltpu

F32 = jnp.float32
BF16 = jnp.bfloat16

D_MODEL = 1024
GRID_W = 64
HEAD_DIM = 64
NA_HEADS = 4
NA_ROWS = 8
NA_COLS = 16
MLA_HEADS = 4
MLA_Q_LORA = 256
MLA_KV_LORA = 192
MLA_NOPE = 64
MLA_ROPE = 32
MLA_V = 64
SG_GROUPS = 4
SG_CHUNK = 128
SG_WIDTH = 256
GQA_HEADS = 4
GQA_KV_HEADS = 2
N_BRANCH = 4
BRANCH_WIDTH = 256
FFN_DIM = 2816
ROPE_THETA = 10000.0
EPS = 1e-6
NEG_INF = -1e30

NA_IN = 3 * NA_HEADS * HEAD_DIM
MLA_IN = MLA_Q_LORA + MLA_KV_LORA + MLA_ROPE
SG_IN = 2 * SG_WIDTH
GQA_IN = (GQA_HEADS + 2 * GQA_KV_HEADS) * HEAD_DIM

LANES = 128
MXU_DIM = 256
BF16_SUBLANES = 16
V_AUG = HEAD_DIM + BF16_SUBLANES
LOG2E = math.log2(math.e)
MLA_LOGIT_SCALE = (MLA_NOPE + MLA_ROPE) ** -0.5 * LOG2E
GQA_LOGIT_SCALE = HEAD_DIM ** -0.5 * LOG2E

C_NA = 0
C_MLA = C_NA + NA_IN
C_SG = C_MLA + 640
C_GQA = C_SG + SG_IN
IN_ARR = C_GQA + 640

G_NA, G_CQ, G_CKV, G_MQ, G_MKN, G_MKR, G_SG, G_GQA, G_TOT = 0, 512, 768, 1024, 1536, 2048, 2176, 2432, 2944

FFN_CHUNK = 256
N_FFN_CHUNKS = FFN_DIM // FFN_CHUNK
assert N_FFN_CHUNKS % 2 == 1
HALO = BF16_SUBLANES


def _const_spec(shape):
    nd = len(shape)
    return pl.BlockSpec(shape, lambda *_: (0,) * nd, pipeline_mode=pl.Buffered(1))


def _params(vmem_mb, n_grid):
    return pltpu.CompilerParams(dimension_semantics=("arbitrary",) * n_grid,
                                vmem_limit_bytes=vmem_mb * 1024 * 1024)


def _dot(a, b):
    return jnp.dot(a, b, preferred_element_type=F32)


def _dot_nt(a, b):
    return lax.dot_general(a, b, (((1,), (1,)), ((), ())), preferred_element_type=F32)


def _rms(xf):
    return xf * lax.rsqrt(jnp.mean(xf * xf, axis=-1, keepdims=True) + EPS)


def _segsum(x2, s_ref):
    hi = x2.astype(BF16)
    lo = (x2 - hi.astype(F32)).astype(BF16)
    s = s_ref[...]
    parts = []
    for j in range(x2.shape[1] // MXU_DIM):
        sl = slice(MXU_DIM * j, MXU_DIM * (j + 1))
        parts.append(_dot(hi[:, sl], s) + _dot(lo[:, sl], s))
    return parts[0] if len(parts) == 1 else jnp.concatenate(parts, axis=1)


def _rope(x, cos, sin_signed, shift, first_half):
    w = x.shape[1]
    partner = jnp.where(first_half, pltpu.roll(x, w - shift, 1), pltpu.roll(x, shift, 1))
    return x * cos + partner * sin_signed


def _ada_kernel(c_ref, w_ref, b_ref, o_ref):
    c = c_ref[...]
    a = (c * jax.nn.sigmoid(c)).astype(BF16)
    o_ref[...] = _dot(a, w_ref[...].astype(BF16)) + b_ref[...]


def _ada_call(cvec, w_ada, b_ada):
    n_layers, d, n = w_ada.shape
    tn = 1536
    return pl.pallas_call(
        _ada_kernel,
        grid=(n_layers, n // tn),
        in_specs=[pl.BlockSpec((8, d), lambda l, j: (0, 0)),
                  pl.BlockSpec((None, d, tn), lambda l, j: (l, 0, j)),
                  pl.BlockSpec((None, 1, tn), lambda l, j: (l, 0, j))],
        out_specs=pl.BlockSpec((None, 8, tn), lambda l, j: (l, 0, j)),
        out_shape=jax.ShapeDtypeStruct((n_layers, 8, n), F32),
        compiler_params=_params(32, 2),
        name="ada",
    )(cvec, w_ada, b_ada.reshape(n_layers, 1, n))


def _in_kernel(x_ref, mod_ref, w_in_ref, w_uq_ref, w_ukv_ref, g_ref, s64_ref, smq_ref, invq_ref,
               gc_ref, gs_ref, mc_ref, ms_ref, sgw_ref, sgb_ref,
               naq_ref, nak_ref, nav_ref, mq_ref, mk_ref, mvt_ref, sgo_ref, gq_ref, gk_ref, gvt_ref):
    tm = x_ref.shape[0]
    x = x_ref[...]
    h = (_rms(x) * (1.0 + mod_ref[1:2, :]) + mod_ref[0:1, :]).astype(BF16)

    def gain(off, width):
        return g_ref[:, off:off + width]

    inv_head = 1.0 / HEAD_DIM

    z = _dot(h, w_in_ref[:, C_NA:C_NA + NA_IN])
    qk = z[:, 0:512]
    qk = qk * lax.rsqrt(_segsum(qk * qk, s64_ref) * inv_head + EPS) * gain(G_NA, 512)
    naq_ref[...] = qk[:, 0:256].astype(BF16)
    nak_ref[...] = qk[:, 256:512].astype(BF16)
    nav_ref[...] = z[:, 512:768].astype(BF16)

    z = _dot(h, w_in_ref[:, C_MLA:C_MLA + 640])
    cq, ckv, kr = z[:, 0:256], z[:, 256:512], z[:, 512:640]
    cq = cq * lax.rsqrt(jnp.mean(cq * cq, axis=-1, keepdims=True) + EPS) * gain(G_CQ, 256)
    ckv = ckv * lax.rsqrt(jnp.sum(ckv * ckv, axis=-1, keepdims=True) * (1.0 / MLA_KV_LORA) + EPS) * gain(G_CKV, 256)
    q = _dot(cq.astype(BF16), w_uq_ref[...])
    q = q * lax.rsqrt(_segsum(q * q, smq_ref) * invq_ref[...] + EPS) * gain(G_MQ, 512)
    kv = _dot(ckv.astype(BF16), w_ukv_ref[...])
    kn = kv[:, 0:512]
    kn = kn * lax.rsqrt(_segsum(kn * kn, s64_ref) * inv_head + EPS) * gain(G_MKN, 512)
    kr = kr * lax.rsqrt(jnp.sum(kr * kr, axis=-1, keepdims=True) * (1.0 / MLA_ROPE) + EPS) * gain(G_MKR, 128)
    lane = lax.broadcasted_iota(jnp.int32, (tm, LANES), 1)
    first8 = (lane & 15) < 8
    mc, ms = mc_ref[...], ms_ref[...]
    kr = _rope(kr, mc, ms, 8, first8)
    for hh in range(MLA_HEADS):
        sl = slice(LANES * hh, LANES * (hh + 1))
        mq_ref[hh] = _rope(q[:, sl], mc, ms, 8, first8).astype(BF16)
        mk_ref[hh] = (kn[:, sl] + kr).astype(BF16)
    vt = kv[:, 512:768].T
    ones = jnp.ones((BF16_SUBLANES, tm), BF16)
    for hh in range(MLA_HEADS):
        mvt_ref[hh, 0:HEAD_DIM, :] = vt[HEAD_DIM * hh:HEAD_DIM * (hh + 1), :].astype(BF16)
        mvt_ref[hh, HEAD_DIM:V_AUG, :] = ones

    z = _dot(h, w_in_ref[:, C_SG:C_SG + SG_IN])
    uv = jax.nn.gelu(z)
    u, v = uv[:, 0:SG_WIDTH], uv[:, SG_WIDTH:]
    v = v * lax.rsqrt(jnp.mean(v * v, axis=-1, keepdims=True) + EPS) * gain(G_SG, 256)
    grp = lax.broadcasted_iota(jnp.int32, (SG_CHUNK, SG_WIDTH), 1) >> 6
    for c in range(tm // SG_CHUNK):
        rows = slice(SG_CHUNK * c, SG_CHUNK * (c + 1))
        vc = v[rows, :]
        mixed = sgb_ref[...]
        for gi in range(SG_GROUPS):
            mixed = mixed + _dot(sgw_ref[gi], jnp.where(grp == gi, vc, 0.0).astype(BF16))
        sgo_ref[rows, :] = (u[rows, :] * mixed).astype(BF16)

    z = _dot(h, w_in_ref[:, C_GQA:C_GQA + 640])
    qk = z[:, 0:512]
    qk = qk * lax.rsqrt(_segsum(qk * qk, s64_ref) * inv_head + EPS) * gain(G_GQA, 512)
    lane2 = lax.broadcasted_iota(jnp.int32, (tm, 2 * LANES), 1)
    first16 = (lane2 & 31) < 16
    gc, gs = gc_ref[...], gs_ref[...]
    qn = _rope(qk[:, 0:256], gc, gs, 16, first16)
    kn = _rope(qk[:, 256:512], gc, gs, 16, first16)
    half = lane >> 6
    for g in range(GQA_KV_HEADS):
        sl = slice(LANES * g, LANES * (g + 1))
        for r in range(GQA_HEADS // GQA_KV_HEADS):
            gq_ref[2 * g + r] = jnp.where(half == r, qn[:, sl], 0.0).astype(BF16)
        gk_ref[g] = kn[:, sl].astype(BF16)
    vt = z[:, 512:640].T
    for g in range(GQA_KV_HEADS):
        gvt_ref[g, 0:HEAD_DIM, :] = vt[HEAD_DIM * g:HEAD_DIM * (g + 1), :].astype(BF16)
        gvt_ref[g, HEAD_DIM:V_AUG, :] = ones


def _in_call(x, mod, mod_row, lw, tabs, tm):
    b, t, d = x.shape
    nt = t // tm
    if mod_row is None:
        mod_map = lambda i, j: (j, 0, 0)
    else:
        mod_map = lambda i, j: (mod_row, 0, 0)
    tok = lambda i, j: (j, i, 0)
    tab = lambda i, j: (i, 0)
    head_tok = lambda i, j: (j, 0, i, 0)
    head_t = lambda i, j: (j, 0, 0, i)
    in_specs = [
        pl.BlockSpec((None, tm, d), tok),
        pl.BlockSpec((None, 6, d), mod_map),
        _const_spec((d, IN_ARR)),
        _const_spec((MLA_Q_LORA, 512)),
        _const_spec((256, 768)),
        _const_spec((1, G_TOT)),
        _const_spec((MXU_DIM, MXU_DIM)),
        _const_spec((MXU_DIM, MXU_DIM)),
        _const_spec((1, 512)),
        pl.BlockSpec((tm, 256), tab), pl.BlockSpec((tm, 256), tab),
        pl.BlockSpec((tm, LANES), tab), pl.BlockSpec((tm, LANES), tab),
        _const_spec((SG_GROUPS, SG_CHUNK, SG_CHUNK)),
        _const_spec((SG_CHUNK, SG_WIDTH)),
    ]
    out_specs = [
        pl.BlockSpec((None, tm, 256), tok), pl.BlockSpec((None, tm, 256), tok), pl.BlockSpec((None, tm, 256), tok),
        pl.BlockSpec((None, MLA_HEADS, tm, LANES), head_tok),
        pl.BlockSpec((None, MLA_HEADS, tm, LANES), head_tok),
        pl.BlockSpec((None, MLA_HEADS, V_AUG, tm), head_t),
        pl.BlockSpec((None, tm, 256), tok),
        pl.BlockSpec((None, GQA_HEADS, tm, LANES), head_tok),
        pl.BlockSpec((None, GQA_KV_HEADS, tm, LANES), head_tok),
        pl.BlockSpec((None, GQA_KV_HEADS, V_AUG, tm), head_t),
    ]
    sds = jax.ShapeDtypeStruct
    out_shape = [
        sds((b, t, 256), BF16), sds((b, t, 256), BF16), sds((b, t, 256), BF16),
        sds((b, MLA_HEADS, t, LANES), BF16), sds((b, MLA_HEADS, t, LANES), BF16),
        sds((b, MLA_HEADS, V_AUG, t), BF16),
        sds((b, t, 256), BF16),
        sds((b, GQA_HEADS, t, LANES), BF16), sds((b, GQA_KV_HEADS, t, LANES), BF16),
        sds((b, GQA_KV_HEADS, V_AUG, t), BF16),
    ]
    outs = pl.pallas_call(
        _in_kernel, grid=(nt, b), in_specs=in_specs, out_specs=out_specs, out_shape=out_shape,
        compiler_params=_params(48, 2), name="in_proj",
    )(x, mod, lw["w_in"], lw["w_uq"], lw["w_ukv"], lw["gains"], lw["s64"], lw["smq"], lw["invq"],
      tabs["gc"], tabs["gs"], tabs["mc"], tabs["ms"], lw["sgw"], lw["sgb"])
    names = ("na_q", "na_k", "na_v", "mla_q", "mla_k", "mla_vt", "sg_o", "gqa_q", "gqa_k", "gqa_vt")
    return dict(zip(names, outs))


def _na_heads(q, k_blocks, v_blocks, biases):
    tq = q.shape[0]
    head_of_lane = lax.broadcasted_iota(jnp.int32, (tq, 256), 1) >> 6
    qf = q.astype(F32) * (HEAD_DIM ** -0.5)
    out = jnp.zeros((tq, 256), F32)
    for hh in range(NA_HEADS):
        qh = jnp.where(head_of_lane == hh, qf, 0.0).astype(BF16)
        scores = []
        for kb, bias in zip(k_blocks, biases):
            s = _dot_nt(qh, kb)
            if bias is not None:
                s = s + bias[hh]
            scores.append(s)
        m = scores[0].max(axis=-1, keepdims=True)
        for s in scores[1:]:
            m = jnp.maximum(m, s.max(axis=-1, keepdims=True))
        l = jnp.zeros((tq, 1), F32)
        o = jnp.zeros((tq, 256), F32)
        for s, vb in zip(scores, v_blocks):
            p = jnp.exp(s - m)
            l = l + p.sum(axis=-1, keepdims=True)
            o = o + _dot(p.astype(BF16), vb)
        out = out + jnp.where(head_of_lane == hh, o * (1.0 / l), 0.0)
    return out


def _na_kernel(q_ref, kp_ref, kc_ref, kn_ref, vp_ref, vc_ref, vn_ref, kx_ref, vx_ref, bias_ref, o_ref):
    biases = [bias_ref.at[:, :, 0:256], bias_ref.at[:, :, 256:512], bias_ref.at[:, :, 512:768], None]
    out = _na_heads(q_ref[...], [kp_ref[...], kc_ref[...], kn_ref[...], kx_ref[...]],
                    [vp_ref[...], vc_ref[...], vn_ref[...], vx_ref[...]], biases)
    o_ref[...] = out.astype(BF16)


def _na_call(q, k, v, kx, vx, bias):
    b, t, w = q.shape
    c = kx.shape[1]
    tq = 4 * GRID_W
    nt = t // tq
    cur = lambda bi, i: (bi, i, 0)
    prev = lambda bi, i: (bi, jnp.maximum(i - 1, 0), 0)
    nxt = lambda bi, i: (bi, jnp.minimum(i + 1, nt - 1), 0)
    ctx = lambda bi, i: (bi, 0, 0)
    variant = lambda bi, i: (jnp.where(i == 0, 0, jnp.where(i == nt - 1, 2, 1)), 0, 0, 0)
    blk = lambda m: pl.BlockSpec((None, tq, w), m)
    return pl.pallas_call(
        _na_kernel, grid=(b, nt),
        in_specs=[blk(cur), blk(prev), blk(cur), blk(nxt), blk(prev), blk(cur), blk(nxt),
                  pl.BlockSpec((None, c, w), ctx), pl.BlockSpec((None, c, w), ctx),
                  pl.BlockSpec((None, NA_HEADS, tq, 3 * tq), variant)],
        out_specs=blk(cur),
        out_shape=jax.ShapeDtypeStruct((b, t, w), BF16),
        compiler_params=_params(48, 2), name="na_attn",
    )(q, k, k, k, v, v, v, kx, vx, bias)


def _na_ctx_kernel(q_ref, k_ref, v_ref, o_ref):
    o_ref[...] = _na_heads(q_ref[...], [k_ref[...]], [v_ref[...]], [None]).astype(BF16)


def _na_ctx_call(q, k, v):
    b, c, w = q.shape
    spec = pl.BlockSpec((None, c, w), lambda bi: (bi, 0, 0))
    return pl.pallas_call(
        _na_ctx_kernel, grid=(b,), in_specs=[spec, spec, spec], out_specs=spec,
        out_shape=jax.ShapeDtypeStruct((b, c, w), BF16),
        compiler_params=_params(32, 1), name="na_ctx_attn",
    )(q, k, v)


def _na_bias(rpb, nt):
    rows = 4 * nt
    col = np.arange(GRID_W)
    c_start = np.clip(col - NA_COLS // 2, 0, GRID_W - NA_COLS)
    valid_c = (col[None, :] >= c_start[:, None]) & (col[None, :] < c_start[:, None] + NA_COLS)
    dc = np.clip(col[None, :] - col[:, None] + (NA_COLS - 1), 0, 2 * NA_COLS - 2)
    pick_dc = jnp.asarray(dc[:, :, None] == np.arange(2 * NA_COLS - 1), F32)
    by_col = jnp.einsum("hrd,qkd->hrqk", rpb, pick_dc, precision=lax.Precision.HIGHEST)
    variants = []
    for i in (0, min(1, nt - 1), nt - 1):
        rq = 4 * i + np.arange(4)
        start = np.clip(rq - NA_ROWS // 2, 0, rows - NA_ROWS)
        blocks = np.array([i - 1, i, i + 1])
        rk = (4 * blocks[:, None] + np.arange(4)[None, :]).reshape(-1)
        block_ok = np.repeat((blocks >= 0) & (blocks < nt), 4)
        valid_r = (rk[None, :] >= start[:, None]) & (rk[None, :] < start[:, None] + NA_ROWS) & block_ok[None, :]
        dr = np.clip(rk[None, :] - rq[:, None] + (NA_ROWS - 1), 0, 2 * NA_ROWS - 2)
        pick_dr = jnp.asarray(dr[:, :, None] == np.arange(2 * NA_ROWS - 1), F32)
        vals = jnp.einsum("abr,hrqk->haqbk", pick_dr, by_col, precision=lax.Precision.HIGHEST)
        ok = valid_r[:, None, :, None] & valid_c[None, :, None, :]
        variants.append(jnp.where(ok[None], vals, NEG_INF).reshape(NA_HEADS, 4 * GRID_W, 12 * GRID_W))
    return jnp.stack(variants).astype(F32)


STREAM_UNROLL = 16
SCORE_LOOKAHEAD = 2
SEED_KEYS = 64
F32_SAFE_MAX = 3e38
MIN_DENOMINATOR = 2.0 ** -40


def _col_max(s):
    keys = s.shape[0]
    if keys % 64 == 0 and keys > 64:
        s = jnp.max(s.reshape(keys // 64, 64, s.shape[1]), axis=0)
    return jnp.max(s, axis=0, keepdims=True)


def _flash_kernel(*refs, n_heads, group, tk, has_ctx):
    if has_ctx:
        q_ref, k_ref, vt_ref, kx_ref, vtx_ref, o_ref, acc_scr, ot_scr = refs
    else:
        q_ref, k_ref, vt_ref, o_ref, acc_scr, ot_scr = refs
    tq = q_ref.shape[1]
    n_chunks = k_ref.shape[1] // tk
    n_slabs = n_heads // group
    heads = range(n_heads)

    def load_chunk(j):
        off = pl.multiple_of(j * tk, tk)
        return ([k_ref[sl, pl.ds(off, tk), :] for sl in range(n_slabs)],
                [vt_ref[sl, :, pl.ds(off, tk)] for sl in range(n_slabs)])

    def pv(vt, p):
        return _dot(vt, p.astype(BF16))

    def exact_step(ks, vts, ms):
        new_ms = []
        for hh in heads:
            s = _dot_nt(ks[hh // group], q_ref[hh])
            m_new = jnp.maximum(ms[hh], _col_max(s))
            p = jnp.exp2(s - m_new)
            acc_scr[hh] = acc_scr[hh] * jnp.exp2(ms[hh] - m_new) + pv(vts[hh // group], p)
            new_ms.append(m_new)
        return tuple(new_ms)

    def stream(chunks, ms):
        units = [(i, hh) for i in range(len(chunks)) for hh in heads]
        score = lambda u: _dot_nt(chunks[u[0]][0][u[1] // group], q_ref[u[1]])
        pending = [score(u) for u in units[:SCORE_LOOKAHEAD]]
        for n, (i, hh) in enumerate(units):
            s = pending.pop(0)
            if n + SCORE_LOOKAHEAD < len(units):
                pending.append(score(units[n + SCORE_LOOKAHEAD]))
            acc_scr[hh] += pv(chunks[i][1][hh // group], jnp.exp2(s - ms[hh]))

    ctx_chunk = ([kx_ref[sl] for sl in range(n_slabs)], [vtx_ref[sl] for sl in range(n_slabs)]) if has_ctx else None
    seed_ref = kx_ref if has_ctx else k_ref
    ms = tuple(_col_max(_dot_nt(seed_ref[hh // group, 0:SEED_KEYS, :], q_ref[hh])) for hh in heads)
    acc_scr[...] = jnp.zeros_like(acc_scr)
    if has_ctx:
        stream([ctx_chunk], ms)
    per_iter = max(u for u in (STREAM_UNROLL, 8, 4, 2, 1) if n_chunks % u == 0)

    def stream_body(j, carry):
        stream([load_chunk(j * per_iter + i) for i in range(per_iter)], ms)
        return carry

    lax.fori_loop(0, n_chunks // per_iter, stream_body, 0)
    unusable = jnp.zeros((V_AUG, tq), F32)
    for hh in heads:
        acc = acc_scr[hh]
        unusable = jnp.maximum(unusable, jnp.where(jnp.abs(acc) < F32_SAFE_MAX, 0.0, 1.0))
        unusable = jnp.maximum(unusable, jnp.where(acc[HEAD_DIM:HEAD_DIM + 1] > MIN_DENOMINATOR, 0.0, 1.0))

    @pl.when(jnp.max(unusable) > 0.0)
    def _():
        acc_scr[...] = jnp.zeros_like(acc_scr)
        ms = tuple(jnp.full((1, tq), NEG_INF, F32) for _ in heads)
        if has_ctx:
            ms = exact_step(*ctx_chunk, ms)
        lax.fori_loop(0, n_chunks, lambda j, m: exact_step(*load_chunk(j), m), ms)

    for hh in heads:
        acc = acc_scr[hh]
        ot_scr[HEAD_DIM * hh:HEAD_DIM * (hh + 1), :] = acc[0:HEAD_DIM] * (1.0 / acc[HEAD_DIM:HEAD_DIM + 1])
    o_ref[...] = ot_scr[...].T.astype(BF16)


def _flash_call(q, k, vt, kx, vtx, tq, tk):
    b, n_heads, t_q, w = q.shape
    slabs, t_k = k.shape[1], k.shape[2]
    has_ctx = kx is not None
    full4 = lambda bi, i: (bi, 0, 0, 0)
    in_specs = [pl.BlockSpec((None, n_heads, tq, w), lambda bi, i: (bi, 0, i, 0)),
                pl.BlockSpec((None, slabs, t_k, w), full4),
                pl.BlockSpec((None, slabs, V_AUG, t_k), full4)]
    args = [q, k, vt]
    if has_ctx:
        c = kx.shape[2]
        in_specs += [pl.BlockSpec((None, slabs, c, w), full4), pl.BlockSpec((None, slabs, V_AUG, c), full4)]
        args += [kx, vtx]
    kern = functools.partial(_flash_kernel, n_heads=n_heads, group=n_heads // slabs, tk=tk,
                             has_ctx=has_ctx)
    return pl.pallas_call(
        kern, grid=(b, t_q // tq), in_specs=in_specs,
        out_specs=pl.BlockSpec((None, tq, n_heads * HEAD_DIM), lambda bi, i: (bi, i, 0)),
        out_shape=jax.ShapeDtypeStruct((b, t_q, n_heads * HEAD_DIM), BF16),
        scratch_shapes=[pltpu.VMEM((n_heads, V_AUG, tq), F32), pltpu.VMEM((n_heads * HEAD_DIM, tq), F32)],
        compiler_params=_params(56, 2), name="flash",
    )(*args)


def _merge_kernel(x_ref, mod_ref, o0_ref, o1_ref, o2_ref, o3_ref, wg_ref, bg_ref, wb_ref, wo_ref, out_ref):
    x = x_ref[...]
    h = (_rms(x) * (1.0 + mod_ref[1:2, :]) + mod_ref[0:1, :]).astype(BF16)
    y = None
    for i, o_ref in enumerate((o0_ref, o1_ref, o2_ref, o3_ref)):
        gate = jax.nn.sigmoid(_dot(h, wg_ref[i]) + bg_ref[i])
        term = gate * _dot(o_ref[...], wb_ref[i])
        y = term if y is None else y + term
    out_ref[...] = x + mod_ref[2:3, :] * _dot(y.astype(BF16), wo_ref[...])


def _merge_call(x, mod, mod_row, branches, lw, tm):
    b, t, d = x.shape
    if mod_row is None:
        mod_map = lambda i, j: (j, 0, 0)
    else:
        mod_map = lambda i, j: (mod_row, 0, 0)
    tok = lambda i, j: (j, i, 0)
    br = pl.BlockSpec((None, tm, BRANCH_WIDTH), tok)
    return pl.pallas_call(
        _merge_kernel, grid=(t // tm, b),
        in_specs=[pl.BlockSpec((None, tm, d), tok), pl.BlockSpec((None, 6, d), mod_map), br, br, br, br,
                  _const_spec((N_BRANCH, d, d)), _const_spec((N_BRANCH, 1, d)),
                  _const_spec((N_BRANCH, BRANCH_WIDTH, d)), _const_spec((d, d))],
        out_specs=pl.BlockSpec((None, tm, d), tok),
        out_shape=jax.ShapeDtypeStruct((b, t, d), F32),
        compiler_params=_params(56, 2), name="merge",
    )(x, mod, *branches, lw["w_gate"], lw["b_gate"], lw["w_branch"], lw["w_out"])


def _ffn_kernel(x_ref, xp_ref, xn_ref, mod_ref, wu_ref, cw_ref, cb_ref, wd_ref, out_ref,
                h_scr, acc_scr, up0_scr, up1_scr):
    tm = x_ref.shape[0]
    i = pl.program_id(0)
    nt = pl.num_programs(0)
    sh, sc = mod_ref[3:4, :], mod_ref[4:5, :]

    def modulated(v):
        return _rms(v) * (1.0 + sc) + sh

    x = x_ref[...]
    h_scr[0:HALO, :] = jnp.where(i > 0, modulated(xp_ref[...]), 0.0).astype(BF16)
    h_scr[HALO:HALO + tm, :] = modulated(x).astype(BF16)
    h_scr[HALO + tm:, :] = jnp.where(i < nt - 1, modulated(xn_ref[...]), 0.0).astype(BF16)
    acc_scr[...] = jnp.zeros_like(acc_scr)
    rows = tm + 2 * HALO

    def up_proj(j, up_scr):
        up_scr[...] = _dot(h_scr[...], wu_ref[j])

    def consume(j, up_scr):
        cw = cw_ref[j]
        conv = (up_scr[HALO - 1:HALO - 1 + tm, :] * cw[0:1, :] + up_scr[HALO:HALO + tm, :] * cw[1:2, :]
                + up_scr[HALO + 1:HALO + 1 + tm, :] * cw[2:3, :] + cb_ref[j])
        a, g = conv[:, 0:FFN_CHUNK], conv[:, FFN_CHUNK:]
        act = (g * jax.nn.sigmoid(g) * a).astype(BF16)
        acc_scr[...] += _dot(act, wd_ref[j])

    up_proj(0, up0_scr)

    def chunk_pair(i, carry):
        j = 2 * i
        up_proj(j + 1, up1_scr)
        consume(j, up0_scr)
        up_proj(j + 2, up0_scr)
        consume(j + 1, up1_scr)
        return carry

    lax.fori_loop(0, (N_FFN_CHUNKS - 1) // 2, chunk_pair, 0)
    consume(N_FFN_CHUNKS - 1, up0_scr)
    out_ref[...] = x + mod_ref[5:6, :] * acc_scr[...]


def _ffn_call(x, mod, mod_row, lw, tm):
    b, t, d = x.shape
    nt = t // tm
    hb = tm // HALO
    last_halo = t // HALO - 1
    if mod_row is None:
        mod_map = lambda i, j: (j, 0, 0)
    else:
        mod_map = lambda i, j: (mod_row, 0, 0)
    tok = lambda i, j: (j, i, 0)
    return pl.pallas_call(
        _ffn_kernel, grid=(nt, b),
        in_specs=[pl.BlockSpec((None, tm, d), tok),
                  pl.BlockSpec((None, HALO, d), lambda i, j: (j, jnp.maximum(i * hb - 1, 0), 0)),
                  pl.BlockSpec((None, HALO, d), lambda i, j: (j, jnp.minimum((i + 1) * hb, last_halo), 0)),
                  pl.BlockSpec((None, 6, d), mod_map),
                  _const_spec((N_FFN_CHUNKS, d, 2 * FFN_CHUNK)),
                  _const_spec((N_FFN_CHUNKS, 3, 2 * FFN_CHUNK)),
                  _const_spec((N_FFN_CHUNKS, 1, 2 * FFN_CHUNK)),
                  _const_spec((N_FFN_CHUNKS, FFN_CHUNK, d))],
        out_specs=pl.BlockSpec((None, tm, d), tok),
        out_shape=jax.ShapeDtypeStruct((b, t, d), F32),
        scratch_shapes=[pltpu.VMEM((tm + 2 * HALO, d), BF16), pltpu.VMEM((tm, d), F32),
                        pltpu.VMEM((tm + 2 * HALO, 2 * FFN_CHUNK), F32),
                        pltpu.VMEM((tm + 2 * HALO, 2 * FFN_CHUNK), F32)],
        compiler_params=_params(56, 2), name="ffn",
    )(x, x, x, mod, lw["w_up"], lw["conv_w"], lw["conv_b"], lw["w_down"])


def _block_diag_ones(width, segs):
    m = np.zeros((width, width), np.float32)
    for lo, hi in segs:
        m[lo:hi, lo:hi] = 1.0
    return jnp.asarray(m, BF16)


def _rope_pattern(pos, width):
    half = width // 2
    inv = ROPE_THETA ** (-jnp.arange(half, dtype=F32) / half)
    ang = pos.astype(F32)[:, None] * inv[None, :]
    cos, sin = jnp.cos(ang), jnp.sin(ang)
    return jnp.concatenate([cos, cos], axis=1), jnp.concatenate([-sin, sin], axis=1)


def _rope_tables(t, identity):
    if identity:
        return {"gc": jnp.ones((t, 256), F32), "gs": jnp.zeros((t, 256), F32),
                "mc": jnp.ones((t, LANES), F32), "ms": jnp.zeros((t, LANES), F32)}
    pos = jnp.arange(t)
    rows, cols = pos // GRID_W, pos % GRID_W
    rc, rs = _rope_pattern(rows, HEAD_DIM // 2)
    cc, cs = _rope_pattern(cols, HEAD_DIM // 2)
    gc = jnp.tile(jnp.concatenate([rc, cc], axis=1), (1, 4))
    gs = jnp.tile(jnp.concatenate([rs, cs], axis=1), (1, 4))
    rc, rs = _rope_pattern(rows, MLA_ROPE // 2)
    cc, cs = _rope_pattern(cols, MLA_ROPE // 2)
    one, zero = jnp.ones((t, MLA_NOPE), F32), jnp.zeros((t, MLA_NOPE), F32)
    pad1, pad0 = jnp.ones((t, LANES - MLA_NOPE - MLA_ROPE), F32), jnp.zeros((t, LANES - MLA_NOPE - MLA_ROPE), F32)
    return {"gc": gc, "gs": gs,
            "mc": jnp.concatenate([one, rc, cc, pad1], axis=1),
            "ms": jnp.concatenate([zero, rs, cs, pad0], axis=1)}


def _arrange_layer(p):
    d = D_MODEL
    w_in = p["w_in"]
    na, mla = w_in[:, :NA_IN], w_in[:, NA_IN:NA_IN + MLA_IN]
    sg = w_in[:, NA_IN + MLA_IN:NA_IN + MLA_IN + SG_IN]
    gqa = w_in[:, NA_IN + MLA_IN + SG_IN:]
    z = lambda n: jnp.zeros((d, n), F32)
    cq, ckv, kr = mla[:, :MLA_Q_LORA], mla[:, MLA_Q_LORA:MLA_Q_LORA + MLA_KV_LORA], mla[:, MLA_Q_LORA + MLA_KV_LORA:]
    gq, gk, gv = gqa[:, :256], gqa[:, 256:384], gqa[:, 384:]
    gk_dup = jnp.concatenate([gk[:, :64], gk[:, :64], gk[:, 64:], gk[:, 64:]], axis=1)
    w_in_arr = jnp.concatenate([na, cq, ckv, z(256 - MLA_KV_LORA), z(MLA_NOPE), kr, z(LANES - MLA_NOPE - MLA_ROPE),
                                sg, gq, gk_dup, gv], axis=1).astype(BF16)

    w_uq = p["mla_w_uq"].reshape(MLA_Q_LORA, MLA_HEADS, MLA_NOPE + MLA_ROPE)
    w_uq = jnp.pad(w_uq, ((0, 0), (0, 0), (0, LANES - MLA_NOPE - MLA_ROPE))).reshape(MLA_Q_LORA, MLA_HEADS * LANES)
    w_ukv = p["mla_w_ukv"].reshape(MLA_KV_LORA, MLA_HEADS, MLA_NOPE + MLA_V)
    k_part = jnp.pad(w_ukv[:, :, :MLA_NOPE], ((0, 0), (0, 0), (0, LANES - MLA_NOPE))).reshape(MLA_KV_LORA, MLA_HEADS * LANES)
    v_part = w_ukv[:, :, MLA_NOPE:].reshape(MLA_KV_LORA, MLA_HEADS * MLA_V)
    w_ukv = jnp.pad(jnp.concatenate([k_part, v_part], axis=1), ((0, 256 - MLA_KV_LORA), (0, 0)))

    qg, kg = p["mla_q_norm"], p["mla_k_norm"]
    z1 = lambda n: jnp.zeros((n,), F32)
    gains = jnp.concatenate([
        jnp.tile(p["na_q_norm"], 4), jnp.tile(p["na_k_norm"], 4),
        p["mla_cq_norm"],
        p["mla_ckv_norm"], z1(256 - MLA_KV_LORA),
        jnp.tile(jnp.concatenate([qg * MLA_LOGIT_SCALE, z1(LANES - MLA_NOPE - MLA_ROPE)]), 4),
        jnp.tile(jnp.concatenate([kg[:MLA_NOPE], z1(LANES - MLA_NOPE)]), 4),
        z1(MLA_NOPE), kg[MLA_NOPE:], z1(LANES - MLA_NOPE - MLA_ROPE),
        p["sg_v_norm"],
        jnp.tile(p["gqa_q_norm"] * GQA_LOGIT_SCALE, 4), jnp.tile(p["gqa_k_norm"], 4),
    ]).reshape(1, G_TOT)

    invq = np.tile(np.concatenate([np.full(MLA_NOPE, 1.0 / MLA_NOPE), np.full(MLA_ROPE, 1.0 / MLA_ROPE),
                                   np.ones(LANES - MLA_NOPE - MLA_ROPE)]), 4).astype(np.float32).reshape(1, 512)
    sgb = jnp.repeat(p["sg_b_s"].T, SG_WIDTH // SG_GROUPS, axis=1)

    w_up = p["w_up"]
    a_part = w_up[:, :FFN_DIM].reshape(d, N_FFN_CHUNKS, FFN_CHUNK)
    g_part = w_up[:, FFN_DIM:].reshape(d, N_FFN_CHUNKS, FFN_CHUNK)
    w_up_arr = jnp.concatenate([a_part, g_part], axis=2).transpose(1, 0, 2).astype(BF16)

    def chunked(v):
        lead = v.shape[:-1]
        a = v[..., :FFN_DIM].reshape(*lead, N_FFN_CHUNKS, FFN_CHUNK)
        g = v[..., FFN_DIM:].reshape(*lead, N_FFN_CHUNKS, FFN_CHUNK)
        return jnp.moveaxis(jnp.concatenate([a, g], axis=-1), -2, 0)

    return {
        "w_in": w_in_arr, "w_uq": w_uq.astype(BF16), "w_ukv": w_ukv.astype(BF16), "gains": gains,
        "s64": _block_diag_ones(MXU_DIM, [(64 * i, 64 * i + 64) for i in range(4)]),
        "smq": _block_diag_ones(MXU_DIM, [(0, 64), (64, 96), (128, 192), (192, 224)]),
        "invq": jnp.asarray(invq),
        "sgw": p["sg_w_s"].astype(BF16), "sgb": sgb,
        "w_gate": p["w_gate"].astype(BF16), "b_gate": p["b_gate"].reshape(N_BRANCH, 1, d),
        "w_branch": p["w_branch"].astype(BF16), "w_out": p["w_out"].astype(BF16),
        "w_up": w_up_arr, "conv_w": chunked(p["conv_w"]), "conv_b": chunked(p["conv_b"].reshape(1, -1)),
        "w_down": p["w_down"].reshape(N_FFN_CHUNKS, FFN_CHUNK, d).astype(BF16),
    }


def kernel(x, c, ctx, c_ctx, w_ada, b_ada, w_in, na_q_norm, na_k_norm, na_rpb, mla_cq_norm, mla_ckv_norm,
           mla_w_uq, mla_w_ukv, mla_q_norm, mla_k_norm, sg_v_norm, sg_w_s, sg_b_s, gqa_q_norm, gqa_k_norm,
           w_branch, w_gate, b_gate, w_out, w_up, conv_w, conv_b, w_down):
    b, t, d = x.shape
    n_ctx = ctx.shape[1]
    depth = w_in.shape[0]
    ctx_row = b
    cvec = jnp.zeros((8, d), F32).at[:b].set(c).at[ctx_row].set(c_ctx)
    mod_all = _ada_call(cvec, w_ada, b_ada).reshape(depth, 8, 6, d)

    tm = min(512, t)
    tq = min(256, t)
    tk = min(512, t)
    tabs_lat = _rope_tables(t, identity=False)
    tabs_ctx = _rope_tables(n_ctx, identity=True)

    cx = ctx
    for l in range(depth):
        lw = _arrange_layer({
            "w_in": w_in[l], "na_q_norm": na_q_norm[l], "na_k_norm": na_k_norm[l],
            "mla_cq_norm": mla_cq_norm[l], "mla_ckv_norm": mla_ckv_norm[l], "mla_w_uq": mla_w_uq[l],
            "mla_w_ukv": mla_w_ukv[l], "mla_q_norm": mla_q_norm[l], "mla_k_norm": mla_k_norm[l],
            "sg_v_norm": sg_v_norm[l], "sg_w_s": sg_w_s[l], "sg_b_s": sg_b_s[l],
            "gqa_q_norm": gqa_q_norm[l], "gqa_k_norm": gqa_k_norm[l],
            "w_branch": w_branch[l], "w_gate": w_gate[l], "b_gate": b_gate[l], "w_out": w_out[l],
            "w_up": w_up[l], "conv_w": conv_w[l], "conv_b": conv_b[l], "w_down": w_down[l]})
        mod = mod_all[l]
        pc = _in_call(cx, mod, ctx_row, lw, tabs_ctx, n_ctx)
        pz = _in_call(x, mod, None, lw, tabs_lat, tm)

        o_na = _na_call(pz["na_q"], pz["na_k"], pz["na_v"], pc["na_k"], pc["na_v"], _na_bias(na_rpb[l], t // (4 * GRID_W)))
        o_mla = _flash_call(pz["mla_q"], pz["mla_k"], pz["mla_vt"], pc["mla_k"], pc["mla_vt"], tq, tk)
        o_gqa = _flash_call(pz["gqa_q"], pz["gqa_k"], pz["gqa_vt"], pc["gqa_k"], pc["gqa_vt"], tq, tk)
        x = _merge_call(x, mod, None, (o_na, o_mla, pz["sg_o"], o_gqa), lw, tm)
        x = _ffn_call(x, mod, None, lw, min(1024, t))

        if l < depth - 1:
            oc_na = _na_ctx_call(pc["na_q"], pc["na_k"], pc["na_v"])
            oc_mla = _flash_call(pc["mla_q"], pc["mla_k"], pc["mla_vt"], None, None, n_ctx, n_ctx)
            oc_gqa = _flash_call(pc["gqa_q"], pc["gqa_k"], pc["gqa_vt"], None, None, n_ctx, n_ctx)
            cx = _merge_call(cx, mod, ctx_row, (oc_na, oc_mla, pc["sg_o"], oc_gqa), lw, n_ctx)
            cx = _ffn_call(cx, mod, ctx_row, lw, n_ctx)
    return x
```

```python
import functools
import math

import numpy as np
import jax
import jax.numpy as jnp
from jax import lax
from jax.experimental import pallas as pl
from jax.experimental.pallas import tpu as pltpu

F32 = jnp.float32
BF16 = jnp.bfloat16

D_MODEL = 1024
GRID_W = 64
HEAD_DIM = 64
NA_HEADS = 4
NA_ROWS = 8
NA_COLS = 16
MLA_HEADS = 4
MLA_Q_LORA = 256
MLA_KV_LORA = 192
MLA_NOPE = 64
MLA_ROPE = 32
MLA_V = 64
SG_GROUPS = 4
SG_CHUNK = 128
SG_WIDTH = 256
GQA_HEADS = 4
GQA_KV_HEADS = 2
N_BRANCH = 4
BRANCH_WIDTH = 256
FFN_DIM = 2816
ROPE_THETA = 10000.0
EPS = 1e-6
NEG_INF = -1e30

NA_IN = 3 * NA_HEADS * HEAD_DIM
MLA_IN = MLA_Q_LORA + MLA_KV_LORA + MLA_ROPE
SG_IN = 2 * SG_WIDTH
GQA_IN = (GQA_HEADS + 2 * GQA_KV_HEADS) * HEAD_DIM

LANES = 128
MXU_DIM = 256
BF16_SUBLANES = 16
V_AUG = HEAD_DIM + BF16_SUBLANES
LOG2E = math.log2(math.e)
MLA_LOGIT_SCALE = (MLA_NOPE + MLA_ROPE) ** -0.5 * LOG2E
GQA_LOGIT_SCALE = HEAD_DIM ** -0.5 * LOG2E

C_NA = 0
C_MLA = C_NA + NA_IN
C_SG = C_MLA + 640
C_GQA = C_SG + SG_IN
IN_ARR = C_GQA + 640

G_NA, G_CQ, G_CKV, G_MQ, G_MKN, G_MKR, G_SG, G_GQA, G_TOT = 0, 512, 768, 1024, 1536, 2048, 2176, 2432, 2944

FFN_CHUNK = 256
N_FFN_CHUNKS = FFN_DIM // FFN_CHUNK
FFN_LOOKAHEAD = 2
HALO = BF16_SUBLANES


def _const_spec(shape):
    nd = len(shape)
    return pl.BlockSpec(shape, lambda *_: (0,) * nd, pipeline_mode=pl.Buffered(1))


def _params(vmem_mb, n_grid):
    return pltpu.CompilerParams(dimension_semantics=("arbitrary",) * n_grid,
                                vmem_limit_bytes=vmem_mb * 1024 * 1024)


def _dot(a, b):
    return jnp.dot(a, b, preferred_element_type=F32)


def _dot_nt(a, b):
    return lax.dot_general(a, b, (((1,), (1,)), ((), ())), preferred_element_type=F32)


def _rms(xf):
    return xf * lax.rsqrt(jnp.mean(xf * xf, axis=-1, keepdims=True) + EPS)


def _segsum(x2, s_ref):
    hi = x2.astype(BF16)
    lo = (x2 - hi.astype(F32)).astype(BF16)
    s = s_ref[...]
    parts = []
    for j in range(x2.shape[1] // MXU_DIM):
        sl = slice(MXU_DIM * j, MXU_DIM * (j + 1))
        parts.append(_dot(hi[:, sl], s) + _dot(lo[:, sl], s))
    return parts[0] if len(parts) == 1 else jnp.concatenate(parts, axis=1)


def _rope(x, cos, sin_signed, shift, first_half):
    w = x.shape[1]
    partner = jnp.where(first_half, pltpu.roll(x, w - shift, 1), pltpu.roll(x, shift, 1))
    return x * cos + partner * sin_signed


def _ada_kernel(c_ref, w_ref, b_ref, o_ref):
    c = c_ref[...]
    a = (c * jax.nn.sigmoid(c)).astype(BF16)
    o_ref[...] = _dot(a, w_ref[...].astype(BF16)) + b_ref[...]


def _ada_call(cvec, w_ada, b_ada):
    n_layers, d, n = w_ada.shape
    tn = 1536
    return pl.pallas_call(
        _ada_kernel,
        grid=(n_layers, n // tn),
        in_specs=[pl.BlockSpec((8, d), lambda l, j: (0, 0)),
                  pl.BlockSpec((None, d, tn), lambda l, j: (l, 0, j)),
                  pl.BlockSpec((None, 1, tn), lambda l, j: (l, 0, j))],
        out_specs=pl.BlockSpec((None, 8, tn), lambda l, j: (l, 0, j)),
        out_shape=jax.ShapeDtypeStruct((n_layers, 8, n), F32),
        compiler_params=_params(32, 2),
        name="ada",
    )(cvec, w_ada, b_ada.reshape(n_layers, 1, n))


def _in_kernel(x_ref, mod_ref, w_in_ref, w_uq_ref, w_ukv_ref, g_ref, s64_ref, smq_ref, invq_ref,
               gc_ref, gs_ref, mc_ref, ms_ref, sgw_ref, sgb_ref,
               naq_ref, nak_ref, nav_ref, mq_ref, mk_ref, mvt_ref, sgo_ref, gq_ref, gk_ref, gvt_ref):
    tm = x_ref.shape[0]
    x = x_ref[...]
    h = (_rms(x) * (1.0 + mod_ref[1:2, :]) + mod_ref[0:1, :]).astype(BF16)

    def gain(off, width):
        return g_ref[:, off:off + width]

    inv_head = 1.0 / HEAD_DIM

    z = _dot(h, w_in_ref[:, C_NA:C_NA + NA_IN])
    qk = z[:, 0:512]
    qk = qk * lax.rsqrt(_segsum(qk * qk, s64_ref) * inv_head + EPS) * gain(G_NA, 512)
    naq_ref[...] = qk[:, 0:256].astype(BF16)
    nak_ref[...] = qk[:, 256:512].astype(BF16)
    nav_ref[...] = z[:, 512:768].astype(BF16)

    z = _dot(h, w_in_ref[:, C_MLA:C_MLA + 640])
    cq, ckv, kr = z[:, 0:256], z[:, 256:512], z[:, 512:640]
    cq = cq * lax.rsqrt(jnp.mean(cq * cq, axis=-1, keepdims=True) + EPS) * gain(G_CQ, 256)
    ckv = ckv * lax.rsqrt(jnp.sum(ckv * ckv, axis=-1, keepdims=True) * (1.0 / MLA_KV_LORA) + EPS) * gain(G_CKV, 256)
    q = _dot(cq.astype(BF16), w_uq_ref[...])
    q = q * lax.rsqrt(_segsum(q * q, smq_ref) * invq_ref[...] + EPS) * gain(G_MQ, 512)
    kv = _dot(ckv.astype(BF16), w_ukv_ref[...])
    kn = kv[:, 0:512]
    kn = kn * lax.rsqrt(_segsum(kn * kn, s64_ref) * inv_head + EPS) * gain(G_MKN, 512)
    kr = kr * lax.rsqrt(jnp.sum(kr * kr, axis=-1, keepdims=True) * (1.0 / MLA_ROPE) + EPS) * gain(G_MKR, 128)
    lane = lax.broadcasted_iota(jnp.int32, (tm, LANES), 1)
    first8 = (lane & 15) < 8
    mc, ms = mc_ref[...], ms_ref[...]
    kr = _rope(kr, mc, ms, 8, first8)
    for hh in range(MLA_HEADS):
        sl = slice(LANES * hh, LANES * (hh + 1))
        mq_ref[hh] = _rope(q[:, sl], mc, ms, 8, first8).astype(BF16)
        mk_ref[hh] = (kn[:, sl] + kr).astype(BF16)
    vt = kv[:, 512:768].T
    ones = jnp.ones((BF16_SUBLANES, tm), BF16)
    for hh in range(MLA_HEADS):
        mvt_ref[hh, 0:HEAD_DIM, :] = vt[HEAD_DIM * hh:HEAD_DIM * (hh + 1), :].astype(BF16)
        mvt_ref[hh, HEAD_DIM:V_AUG, :] = ones

    z = _dot(h, w_in_ref[:, C_SG:C_SG + SG_IN])
    uv = jax.nn.gelu(z)
    u, v = uv[:, 0:SG_WIDTH], uv[:, SG_WIDTH:]
    v = v * lax.rsqrt(jnp.mean(v * v, axis=-1, keepdims=True) + EPS) * gain(G_SG, 256)
    grp = lax.broadcasted_iota(jnp.int32, (SG_CHUNK, SG_WIDTH), 1) >> 6
    for c in range(tm // SG_CHUNK):
        rows = slice(SG_CHUNK * c, SG_CHUNK * (c + 1))
        vc = v[rows, :]
        mixed = sgb_ref[...]
        for gi in range(SG_GROUPS):
            mixed = mixed + _dot(sgw_ref[gi], jnp.where(grp == gi, vc, 0.0).astype(BF16))
        sgo_ref[rows, :] = (u[rows, :] * mixed).astype(BF16)

    z = _dot(h, w_in_ref[:, C_GQA:C_GQA + 640])
    qk = z[:, 0:512]
    qk = qk * lax.rsqrt(_segsum(qk * qk, s64_ref) * inv_head + EPS) * gain(G_GQA, 512)
    lane2 = lax.broadcasted_iota(jnp.int32, (tm, 2 * LANES), 1)
    first16 = (lane2 & 31) < 16
    gc, gs = gc_ref[...], gs_ref[...]
    qn = _rope(qk[:, 0:256], gc, gs, 16, first16)
    kn = _rope(qk[:, 256:512], gc, gs, 16, first16)
    half = lane >> 6
    for g in range(GQA_KV_HEADS):
        sl = slice(LANES * g, LANES * (g + 1))
        for r in range(GQA_HEADS // GQA_KV_HEADS):
            gq_ref[2 * g + r] = jnp.where(half == r, qn[:, sl], 0.0).astype(BF16)
        gk_ref[g] = kn[:, sl].astype(BF16)
    vt = z[:, 512:640].T
    for g in range(GQA_KV_HEADS):
        gvt_ref[g, 0:HEAD_DIM, :] = vt[HEAD_DIM * g:HEAD_DIM * (g + 1), :].astype(BF16)
        gvt_ref[g, HEAD_DIM:V_AUG, :] = ones


def _in_call(x, mod, mod_row, lw, tabs, tm):
    b, t, d = x.shape
    nt = t // tm
    if mod_row is None:
        mod_map = lambda i, j: (j, 0, 0)
    else:
        mod_map = lambda i, j: (mod_row, 0, 0)
    tok = lambda i, j: (j, i, 0)
    tab = lambda i, j: (i, 0)
    head_tok = lambda i, j: (j, 0, i, 0)
    head_t = lambda i, j: (j, 0, 0, i)
    in_specs = [
        pl.BlockSpec((None, tm, d), tok),
        pl.BlockSpec((None, 6, d), mod_map),
        _const_spec((d, IN_ARR)),
        _const_spec((MLA_Q_LORA, 512)),
        _const_spec((256, 768)),
        _const_spec((1, G_TOT)),
        _const_spec((MXU_DIM, MXU_DIM)),
        _const_spec((MXU_DIM, MXU_DIM)),
        _const_spec((1, 512)),
        pl.BlockSpec((tm, 256), tab), pl.BlockSpec((tm, 256), tab),
        pl.BlockSpec((tm, LANES), tab), pl.BlockSpec((tm, LANES), tab),
        _const_spec((SG_GROUPS, SG_CHUNK, SG_CHUNK)),
        _const_spec((SG_CHUNK, SG_WIDTH)),
    ]
    out_specs = [
        pl.BlockSpec((None, tm, 256), tok), pl.BlockSpec((None, tm, 256), tok), pl.BlockSpec((None, tm, 256), tok),
        pl.BlockSpec((None, MLA_HEADS, tm, LANES), head_tok),
        pl.BlockSpec((None, MLA_HEADS, tm, LANES), head_tok),
        pl.BlockSpec((None, MLA_HEADS, V_AUG, tm), head_t),
        pl.BlockSpec((None, tm, 256), tok),
        pl.BlockSpec((None, GQA_HEADS, tm, LANES), head_tok),
        pl.BlockSpec((None, GQA_KV_HEADS, tm, LANES), head_tok),
        pl.BlockSpec((None, GQA_KV_HEADS, V_AUG, tm), head_t),
    ]
    sds = jax.ShapeDtypeStruct
    out_shape = [
        sds((b, t, 256), BF16), sds((b, t, 256), BF16), sds((b, t, 256), BF16),
        sds((b, MLA_HEADS, t, LANES), BF16), sds((b, MLA_HEADS, t, LANES), BF16),
        sds((b, MLA_HEADS, V_AUG, t), BF16),
        sds((b, t, 256), BF16),
        sds((b, GQA_HEADS, t, LANES), BF16), sds((b, GQA_KV_HEADS, t, LANES), BF16),
        sds((b, GQA_KV_HEADS, V_AUG, t), BF16),
    ]
    outs = pl.pallas_call(
        _in_kernel, grid=(nt, b), in_specs=in_specs, out_specs=out_specs, out_shape=out_shape,
        compiler_params=_params(48, 2), name="in_proj",
    )(x, mod, lw["w_in"], lw["w_uq"], lw["w_ukv"], lw["gains"], lw["s64"], lw["smq"], lw["invq"],
      tabs["gc"], tabs["gs"], tabs["mc"], tabs["ms"], lw["sgw"], lw["sgb"])
    names = ("na_q", "na_k", "na_v", "mla_q", "mla_k", "mla_vt", "sg_o", "gqa_q", "gqa_k", "gqa_vt")
    return dict(zip(names, outs))


def _na_heads(q, k_blocks, v_blocks, biases):
    tq = q.shape[0]
    head_of_lane = lax.broadcasted_iota(jnp.int32, (tq, 256), 1) >> 6
    qf = q.astype(F32) * (HEAD_DIM ** -0.5)
    out = jnp.zeros((tq, 256), F32)
    for hh in range(NA_HEADS):
        qh = jnp.where(head_of_lane == hh, qf, 0.0).astype(BF16)
        scores = []
        for kb, bias in zip(k_blocks, biases):
            s = _dot_nt(qh, kb)
            if bias is not None:
                s = s + bias[hh]
            scores.append(s)
        m = scores[0].max(axis=-1, keepdims=True)
        for s in scores[1:]:
            m = jnp.maximum(m, s.max(axis=-1, keepdims=True))
        l = jnp.zeros((tq, 1), F32)
        o = jnp.zeros((tq, 256), F32)
        for s, vb in zip(scores, v_blocks):
            p = jnp.exp(s - m)
            l = l + p.sum(axis=-1, keepdims=True)
            o = o + _dot(p.astype(BF16), vb)
        out = out + jnp.where(head_of_lane == hh, o * (1.0 / l), 0.0)
    return out


def _na_kernel(q_ref, kp_ref, kc_ref, kn_ref, vp_ref, vc_ref, vn_ref, kx_ref, vx_ref, bias_ref, o_ref):
    biases = [bias_ref.at[:, :, 0:256], bias_ref.at[:, :, 256:512], bias_ref.at[:, :, 512:768], None]
    out = _na_heads(q_ref[...], [kp_ref[...], kc_ref[...], kn_ref[...], kx_ref[...]],
                    [vp_ref[...], vc_ref[...], vn_ref[...], vx_ref[...]], biases)
    o_ref[...] = out.astype(BF16)


def _na_call(q, k, v, kx, vx, bias):
    b, t, w = q.shape
    c = kx.shape[1]
    tq = 4 * GRID_W
    nt = t // tq
    cur = lambda bi, i: (bi, i, 0)
    prev = lambda bi, i: (bi, jnp.maximum(i - 1, 0), 0)
    nxt = lambda bi, i: (bi, jnp.minimum(i + 1, nt - 1), 0)
    ctx = lambda bi, i: (bi, 0, 0)
    variant = lambda bi, i: (jnp.where(i == 0, 0, jnp.where(i == nt - 1, 2, 1)), 0, 0, 0)
    blk = lambda m: pl.BlockSpec((None, tq, w), m)
    return pl.pallas_call(
        _na_kernel, grid=(b, nt),
        in_specs=[blk(cur), blk(prev), blk(cur), blk(nxt), blk(prev), blk(cur), blk(nxt),
                  pl.BlockSpec((None, c, w), ctx), pl.BlockSpec((None, c, w), ctx),
                  pl.BlockSpec((None, NA_HEADS, tq, 3 * tq), variant)],
        out_specs=blk(cur),
        out_shape=jax.ShapeDtypeStruct((b, t, w), BF16),
        compiler_params=_params(48, 2), name="na_attn",
    )(q, k, k, k, v, v, v, kx, vx, bias)


def _na_ctx_kernel(q_ref, k_ref, v_ref, o_ref):
    o_ref[...] = _na_heads(q_ref[...], [k_ref[...]], [v_ref[...]], [None]).astype(BF16)


def _na_ctx_call(q, k, v):
    b, c, w = q.shape
    spec = pl.BlockSpec((None, c, w), lambda bi: (bi, 0, 0))
    return pl.pallas_call(
        _na_ctx_kernel, grid=(b,), in_specs=[spec, spec, spec], out_specs=spec,
        out_shape=jax.ShapeDtypeStruct((b, c, w), BF16),
        compiler_params=_params(32, 1), name="na_ctx_attn",
    )(q, k, v)


def _na_bias(rpb, nt):
    rows = 4 * nt
    col = np.arange(GRID_W)
    c_start = np.clip(col - NA_COLS // 2, 0, GRID_W - NA_COLS)
    valid_c = (col[None, :] >= c_start[:, None]) & (col[None, :] < c_start[:, None] + NA_COLS)
    dc = np.clip(col[None, :] - col[:, None] + (NA_COLS - 1), 0, 2 * NA_COLS - 2)
    pick_dc = jnp.asarray(dc[:, :, None] == np.arange(2 * NA_COLS - 1), F32)
    by_col = jnp.einsum("hrd,qkd->hrqk", rpb, pick_dc, precision=lax.Precision.HIGHEST)
    variants = []
    for i in (0, min(1, nt - 1), nt - 1):
        rq = 4 * i + np.arange(4)
        start = np.clip(rq - NA_ROWS // 2, 0, rows - NA_ROWS)
        blocks = np.array([i - 1, i, i + 1])
        rk = (4 * blocks[:, None] + np.arange(4)[None, :]).reshape(-1)
        block_ok = np.repeat((blocks >= 0) & (blocks < nt), 4)
        valid_r = (rk[None, :] >= start[:, None]) & (rk[None, :] < start[:, None] + NA_ROWS) & block_ok[None, :]
        dr = np.clip(rk[None, :] - rq[:, None] + (NA_ROWS - 1), 0, 2 * NA_ROWS - 2)
        pick_dr = jnp.asarray(dr[:, :, None] == np.arange(2 * NA_ROWS - 1), F32)
        vals = jnp.einsum("abr,hrqk->haqbk", pick_dr, by_col, precision=lax.Precision.HIGHEST)
        ok = valid_r[:, None, :, None] & valid_c[None, :, None, :]
        variants.append(jnp.where(ok[None], vals, NEG_INF).reshape(NA_HEADS, 4 * GRID_W, 12 * GRID_W))
    return jnp.stack(variants).astype(F32)


STREAM_UNROLL = 16
SCORE_LOOKAHEAD = 2
SEED_KEYS = 64
F32_SAFE_MAX = 3e38
MIN_DENOMINATOR = 2.0 ** -40


def _col_max(s):
    keys = s.shape[0]
    if keys % 64 == 0 and keys > 64:
        s = jnp.max(s.reshape(keys // 64, 64, s.shape[1]), axis=0)
    return jnp.max(s, axis=0, keepdims=True)


def _flash_kernel(*refs, n_heads, group, tk, has_ctx):
    if has_ctx:
        q_ref, k_ref, vt_ref, kx_ref, vtx_ref, o_ref, acc_scr, ot_scr = refs
    else:
        q_ref, k_ref, vt_ref, o_ref, acc_scr, ot_scr = refs
    tq = q_ref.shape[1]
    n_chunks = k_ref.shape[1] // tk
    n_slabs = n_heads // group
    heads = range(n_heads)

    def load_chunk(j):
        off = pl.multiple_of(j * tk, tk)
        return ([k_ref[sl, pl.ds(off, tk), :] for sl in range(n_slabs)],
                [vt_ref[sl, :, pl.ds(off, tk)] for sl in range(n_slabs)])

    def pv(vt, p):
        return _dot(vt, p.astype(BF16))

    def exact_step(ks, vts, ms):
        new_ms = []
        for hh in heads:
            s = _dot_nt(ks[hh // group], q_ref[hh])
            m_new = jnp.maximum(ms[hh], _col_max(s))
            p = jnp.exp2(s - m_new)
            acc_scr[hh] = acc_scr[hh] * jnp.exp2(ms[hh] - m_new) + pv(vts[hh // group], p)
            new_ms.append(m_new)
        return tuple(new_ms)

    def stream(chunks, ms):
        units = [(i, hh) for i in range(len(chunks)) for hh in heads]
        score = lambda u: _dot_nt(chunks[u[0]][0][u[1] // group], q_ref[u[1]])
        pending = [score(u) for u in units[:SCORE_LOOKAHEAD]]
        for n, (i, hh) in enumerate(units):
            s = pending.pop(0)
            if n + SCORE_LOOKAHEAD < len(units):
                pending.append(score(units[n + SCORE_LOOKAHEAD]))
            acc_scr[hh] += pv(chunks[i][1][hh // group], jnp.exp2(s - ms[hh]))

    ctx_chunk = ([kx_ref[sl] for sl in range(n_slabs)], [vtx_ref[sl] for sl in range(n_slabs)]) if has_ctx else None
    seed_ref = kx_ref if has_ctx else k_ref
    ms = tuple(_col_max(_dot_nt(seed_ref[hh // group, 0:SEED_KEYS, :], q_ref[hh])) for hh in heads)
    acc_scr[...] = jnp.zeros_like(acc_scr)
    if has_ctx:
        stream([ctx_chunk], ms)
    per_iter = max(u for u in (STREAM_UNROLL, 8, 4, 2, 1) if n_chunks % u == 0)

    def stream_body(j, carry):
        stream([load_chunk(j * per_iter + i) for i in range(per_iter)], ms)
        return carry

    lax.fori_loop(0, n_chunks // per_iter, stream_body, 0)
    unusable = jnp.zeros((V_AUG, tq), F32)
    for hh in heads:
        acc = acc_scr[hh]
        unusable = jnp.maximum(unusable, jnp.where(jnp.abs(acc) < F32_SAFE_MAX, 0.0, 1.0))
        unusable = jnp.maximum(unusable, jnp.where(acc[HEAD_DIM:HEAD_DIM + 1] > MIN_DENOMINATOR, 0.0, 1.0))

    @pl.when(jnp.max(unusable) > 0.0)
    def _():
        acc_scr[...] = jnp.zeros_like(acc_scr)
        ms = tuple(jnp.full((1, tq), NEG_INF, F32) for _ in heads)
        if has_ctx:
            ms = exact_step(*ctx_chunk, ms)
        lax.fori_loop(0, n_chunks, lambda j, m: exact_step(*load_chunk(j), m), ms)

    for hh in heads:
        acc = acc_scr[hh]
        ot_scr[HEAD_DIM * hh:HEAD_DIM * (hh + 1), :] = acc[0:HEAD_DIM] * (1.0 / acc[HEAD_DIM:HEAD_DIM + 1])
    o_ref[...] = ot_scr[...].T.astype(BF16)


def _flash_call(q, k, vt, kx, vtx, tq, tk):
    b, n_heads, t_q, w = q.shape
    slabs, t_k = k.shape[1], k.shape[2]
    has_ctx = kx is not None
    full4 = lambda bi, i: (bi, 0, 0, 0)
    in_specs = [pl.BlockSpec((None, n_heads, tq, w), lambda bi, i: (bi, 0, i, 0)),
                pl.BlockSpec((None, slabs, t_k, w), full4),
                pl.BlockSpec((None, slabs, V_AUG, t_k), full4)]
    args = [q, k, vt]
    if has_ctx:
        c = kx.shape[2]
        in_specs += [pl.BlockSpec((None, slabs, c, w), full4), pl.BlockSpec((None, slabs, V_AUG, c), full4)]
        args += [kx, vtx]
    kern = functools.partial(_flash_kernel, n_heads=n_heads, group=n_heads // slabs, tk=tk,
                             has_ctx=has_ctx)
    return pl.pallas_call(
        kern, grid=(b, t_q // tq), in_specs=in_specs,
        out_specs=pl.BlockSpec((None, tq, n_heads * HEAD_DIM), lambda bi, i: (bi, i, 0)),
        out_shape=jax.ShapeDtypeStruct((b, t_q, n_heads * HEAD_DIM), BF16),
        scratch_shapes=[pltpu.VMEM((n_heads, V_AUG, tq), F32), pltpu.VMEM((n_heads * HEAD_DIM, tq), F32)],
        compiler_params=_params(56, 2), name="flash",
    )(*args)


def _merge_kernel(x_ref, mod_ref, o0_ref, o1_ref, o2_ref, o3_ref, wg_ref, bg_ref, wb_ref, wo_ref, out_ref):
    x = x_ref[...]
    h = (_rms(x) * (1.0 + mod_ref[1:2, :]) + mod_ref[0:1, :]).astype(BF16)
    y = None
    for i, o_ref in enumerate((o0_ref, o1_ref, o2_ref, o3_ref)):
        gate = jax.nn.sigmoid(_dot(h, wg_ref[i]) + bg_ref[i])
        term = gate * _dot(o_ref[...], wb_ref[i])
        y = term if y is None else y + term
    out_ref[...] = x + mod_ref[2:3, :] * _dot(y.astype(BF16), wo_ref[...])


def _merge_call(x, mod, mod_row, branches, lw, tm):
    b, t, d = x.shape
    if mod_row is None:
        mod_map = lambda i, j: (j, 0, 0)
    else:
        mod_map = lambda i, j: (mod_row, 0, 0)
    tok = lambda i, j: (j, i, 0)
    br = pl.BlockSpec((None, tm, BRANCH_WIDTH), tok)
    return pl.pallas_call(
        _merge_kernel, grid=(t // tm, b),
        in_specs=[pl.BlockSpec((None, tm, d), tok), pl.BlockSpec((None, 6, d), mod_map), br, br, br, br,
                  _const_spec((N_BRANCH, d, d)), _const_spec((N_BRANCH, 1, d)),
                  _const_spec((N_BRANCH, BRANCH_WIDTH, d)), _const_spec((d, d))],
        out_specs=pl.BlockSpec((None, tm, d), tok),
        out_shape=jax.ShapeDtypeStruct((b, t, d), F32),
        compiler_params=_params(56, 2), name="merge",
    )(x, mod, *branches, lw["w_gate"], lw["b_gate"], lw["w_branch"], lw["w_out"])


def _ffn_kernel(x_ref, xp_ref, xn_ref, mod_ref, wu_ref, cw_ref, cb_ref, wd_ref, out_ref, h_scr, act_scr):
    tm = x_ref.shape[0]
    i = pl.program_id(0)
    nt = pl.num_programs(0)
    sh, sc = mod_ref[3:4, :], mod_ref[4:5, :]

    def modulated(v):
        return _rms(v) * (1.0 + sc) + sh

    x = x_ref[...]
    h_scr[0:HALO, :] = jnp.where(i > 0, modulated(xp_ref[...]), 0.0).astype(BF16)
    h_scr[HALO:HALO + tm, :] = modulated(x).astype(BF16)
    h_scr[HALO + tm:, :] = jnp.where(i < nt - 1, modulated(xn_ref[...]), 0.0).astype(BF16)
    rows = tm + 2 * HALO
    main = slice(HALO, HALO + tm)

    def up_proj(j):
        return _dot(h_scr[...], wu_ref[j])

    ups = [up_proj(j) for j in range(FFN_LOOKAHEAD)]
    for j in range(N_FFN_CHUNKS):
        up = ups.pop(0)
        if j + FFN_LOOKAHEAD < N_FFN_CHUNKS:
            ups.append(up_proj(j + FFN_LOOKAHEAD))
        cw = cw_ref[j]
        conv = (pltpu.roll(up, 1, 0)[main] * cw[0:1, :] + up[main] * cw[1:2, :]
                + pltpu.roll(up, rows - 1, 0)[main] * cw[2:3, :] + cb_ref[j])
        a, g = conv[:, 0:FFN_CHUNK], conv[:, FFN_CHUNK:]
        act_scr[:, FFN_CHUNK * j:FFN_CHUNK * (j + 1)] = (g * jax.nn.sigmoid(g) * a).astype(BF16)
    out_ref[...] = x + mod_ref[5:6, :] * _dot(act_scr[...], wd_ref[...])


def _ffn_call(x, mod, mod_row, lw, tm):
    b, t, d = x.shape
    nt = t // tm
    hb = tm // HALO
    last_halo = t // HALO - 1
    if mod_row is None:
        mod_map = lambda i, j: (j, 0, 0)
    else:
        mod_map = lambda i, j: (mod_row, 0, 0)
    tok = lambda i, j: (j, i, 0)
    return pl.pallas_call(
        _ffn_kernel, grid=(nt, b),
        in_specs=[pl.BlockSpec((None, tm, d), tok),
                  pl.BlockSpec((None, HALO, d), lambda i, j: (j, jnp.maximum(i * hb - 1, 0), 0)),
                  pl.BlockSpec((None, HALO, d), lambda i, j: (j, jnp.minimum((i + 1) * hb, last_halo), 0)),
                  pl.BlockSpec((None, 6, d), mod_map),
                  _const_spec((N_FFN_CHUNKS, d, 2 * FFN_CHUNK)),
                  _const_spec((N_FFN_CHUNKS, 3, 2 * FFN_CHUNK)),
                  _const_spec((N_FFN_CHUNKS, 1, 2 * FFN_CHUNK)),
                  _const_spec((FFN_DIM, d))],
        out_specs=pl.BlockSpec((None, tm, d), tok),
        out_shape=jax.ShapeDtypeStruct((b, t, d), F32),
        scratch_shapes=[pltpu.VMEM((tm + 2 * HALO, d), BF16), pltpu.VMEM((tm, FFN_DIM), BF16)],
        compiler_params=_params(56, 2), name="ffn",
    )(x, x, x, mod, lw["w_up"], lw["conv_w"], lw["conv_b"], lw["w_down"])


def _block_diag_ones(width, segs):
    m = np.zeros((width, width), np.float32)
    for lo, hi in segs:
        m[lo:hi, lo:hi] = 1.0
    return jnp.asarray(m, BF16)


def _rope_pattern(pos, width):
    half = width // 2
    inv = ROPE_THETA ** (-jnp.arange(half, dtype=F32) / half)
    ang = pos.astype(F32)[:, None] * inv[None, :]
    cos, sin = jnp.cos(ang), jnp.sin(ang)
    return jnp.concatenate([cos, cos], axis=1), jnp.concatenate([-sin, sin], axis=1)


def _rope_tables(t, identity):
    if identity:
        return {"gc": jnp.ones((t, 256), F32), "gs": jnp.zeros((t, 256), F32),
                "mc": jnp.ones((t, LANES), F32), "ms": jnp.zeros((t, LANES), F32)}
    pos = jnp.arange(t)
    rows, cols = pos // GRID_W, pos % GRID_W
    rc, rs = _rope_pattern(rows, HEAD_DIM // 2)
    cc, cs = _rope_pattern(cols, HEAD_DIM // 2)
    gc = jnp.tile(jnp.concatenate([rc, cc], axis=1), (1, 4))
    gs = jnp.tile(jnp.concatenate([rs, cs], axis=1), (1, 4))
    rc, rs = _rope_pattern(rows, MLA_ROPE // 2)
    cc, cs = _rope_pattern(cols, MLA_ROPE // 2)
    one, zero = jnp.ones((t, MLA_NOPE), F32), jnp.zeros((t, MLA_NOPE), F32)
    pad1, pad0 = jnp.ones((t, LANES - MLA_NOPE - MLA_ROPE), F32), jnp.zeros((t, LANES - MLA_NOPE - MLA_ROPE), F32)
    return {"gc": gc, "gs": gs,
            "mc": jnp.concatenate([one, rc, cc, pad1], axis=1),
            "ms": jnp.concatenate([zero, rs, cs, pad0], axis=1)}


def _arrange_layer(p):
    d = D_MODEL
    w_in = p["w_in"]
    na, mla = w_in[:, :NA_IN], w_in[:, NA_IN:NA_IN + MLA_IN]
    sg = w_in[:, NA_IN + MLA_IN:NA_IN + MLA_IN + SG_IN]
    gqa = w_in[:, NA_IN + MLA_IN + SG_IN:]
    z = lambda n: jnp.zeros((d, n), F32)
    cq, ckv, kr = mla[:, :MLA_Q_LORA], mla[:, MLA_Q_LORA:MLA_Q_LORA + MLA_KV_LORA], mla[:, MLA_Q_LORA + MLA_KV_LORA:]
    gq, gk, gv = gqa[:, :256], gqa[:, 256:384], gqa[:, 384:]
    gk_dup = jnp.concatenate([gk[:, :64], gk[:, :64], gk[:, 64:], gk[:, 64:]], axis=1)
    w_in_arr = jnp.concatenate([na, cq, ckv, z(256 - MLA_KV_LORA), z(MLA_NOPE), kr, z(LANES - MLA_NOPE - MLA_ROPE),
                                sg, gq, gk_dup, gv], axis=1).astype(BF16)

    w_uq = p["mla_w_uq"].reshape(MLA_Q_LORA, MLA_HEADS, MLA_NOPE + MLA_ROPE)
    w_uq = jnp.pad(w_uq, ((0, 0), (0, 0), (0, LANES - MLA_NOPE - MLA_ROPE))).reshape(MLA_Q_LORA, MLA_HEADS * LANES)
    w_ukv = p["mla_w_ukv"].reshape(MLA_KV_LORA, MLA_HEADS, MLA_NOPE + MLA_V)
    k_part = jnp.pad(w_ukv[:, :, :MLA_NOPE], ((0, 0), (0, 0), (0, LANES - MLA_NOPE))).reshape(MLA_KV_LORA, MLA_HEADS * LANES)
    v_part = w_ukv[:, :, MLA_NOPE:].reshape(MLA_KV_LORA, MLA_HEADS * MLA_V)
    w_ukv = jnp.pad(jnp.concatenate([k_part, v_part], axis=1), ((0, 256 - MLA_KV_LORA), (0, 0)))

    qg, kg = p["mla_q_norm"], p["mla_k_norm"]
    z1 = lambda n: jnp.zeros((n,), F32)
    gains = jnp.concatenate([
        jnp.tile(p["na_q_norm"], 4), jnp.tile(p["na_k_norm"], 4),
        p["mla_cq_norm"],
        p["mla_ckv_norm"], z1(256 - MLA_KV_LORA),
        jnp.tile(jnp.concatenate([qg * MLA_LOGIT_SCALE, z1(LANES - MLA_NOPE - MLA_ROPE)]), 4),
        jnp.tile(jnp.concatenate([kg[:MLA_NOPE], z1(LANES - MLA_NOPE)]), 4),
        z1(MLA_NOPE), kg[MLA_NOPE:], z1(LANES - MLA_NOPE - MLA_ROPE),
        p["sg_v_norm"],
        jnp.tile(p["gqa_q_norm"] * GQA_LOGIT_SCALE, 4), jnp.tile(p["gqa_k_norm"], 4),
    ]).reshape(1, G_TOT)

    invq = np.tile(np.concatenate([np.full(MLA_NOPE, 1.0 / MLA_NOPE), np.full(MLA_ROPE, 1.0 / MLA_ROPE),
                                   np.ones(LANES - MLA_NOPE - MLA_ROPE)]), 4).astype(np.float32).reshape(1, 512)
    sgb = jnp.repeat(p["sg_b_s"].T, SG_WIDTH // SG_GROUPS, axis=1)

    w_up = p["w_up"]
    a_part = w_up[:, :FFN_DIM].reshape(d, N_FFN_CHUNKS, FFN_CHUNK)
    g_part = w_up[:, FFN_DIM:].reshape(d, N_FFN_CHUNKS, FFN_CHUNK)
    w_up_arr = jnp.concatenate([a_part, g_part], axis=2).transpose(1, 0, 2).astype(BF16)

    def chunked(v):
        lead = v.shape[:-1]
        a = v[..., :FFN_DIM].reshape(*lead, N_FFN_CHUNKS, FFN_CHUNK)
        g = v[..., FFN_DIM:].reshape(*lead, N_FFN_CHUNKS, FFN_CHUNK)
        return jnp.moveaxis(jnp.concatenate([a, g], axis=-1), -2, 0)

    return {
        "w_in": w_in_arr, "w_uq": w_uq.astype(BF16), "w_ukv": w_ukv.astype(BF16), "gains": gains,
        "s64": _block_diag_ones(MXU_DIM, [(64 * i, 64 * i + 64) for i in range(4)]),
        "smq": _block_diag_ones(MXU_DIM, [(0, 64), (64, 96), (128, 192), (192, 224)]),
        "invq": jnp.asarray(invq),
        "sgw": p["sg_w_s"].astype(BF16), "sgb": sgb,
        "w_gate": p["w_gate"].astype(BF16), "b_gate": p["b_gate"].reshape(N_BRANCH, 1, d),
        "w_branch": p["w_branch"].astype(BF16), "w_out": p["w_out"].astype(BF16),
        "w_up": w_up_arr, "conv_w": chunked(p["conv_w"]), "conv_b": chunked(p["conv_b"].reshape(1, -1)),
        "w_down": p["w_down"].astype(BF16),
    }


def kernel(x, c, ctx, c_ctx, w_ada, b_ada, w_in, na_q_norm, na_k_norm, na_rpb, mla_cq_norm, mla_ckv_norm,
           mla_w_uq, mla_w_ukv, mla_q_norm, mla_k_norm, sg_v_norm, sg_w_s, sg_b_s, gqa_q_norm, gqa_k_norm,
           w_branch, w_gate, b_gate, w_out, w_up, conv_w, conv_b, w_down):
    b, t, d = x.shape
    n_ctx = ctx.shape[1]
    depth = w_in.shape[0]
    ctx_row = b
    cvec = jnp.zeros((8, d), F32).at[:b].set(c).at[ctx_row].set(c_ctx)
    mod_all = _ada_call(cvec, w_ada, b_ada).reshape(depth, 8, 6, d)

    tm = min(512, t)
    tq = min(256, t)
    tk = min(512, t)
    tabs_lat = _rope_tables(t, identity=False)
    tabs_ctx = _rope_tables(n_ctx, identity=True)

    cx = ctx
    for l in range(depth):
        lw = _arrange_layer({
            "w_in": w_in[l], "na_q_norm": na_q_norm[l], "na_k_norm": na_k_norm[l],
            "mla_cq_norm": mla_cq_norm[l], "mla_ckv_norm": mla_ckv_norm[l], "mla_w_uq": mla_w_uq[l],
            "mla_w_ukv": mla_w_ukv[l], "mla_q_norm": mla_q_norm[l], "mla_k_norm": mla_k_norm[l],
            "sg_v_norm": sg_v_norm[l], "sg_w_s": sg_w_s[l], "sg_b_s": sg_b_s[l],
            "gqa_q_norm": gqa_q_norm[l], "gqa_k_norm": gqa_k_norm[l],
            "w_branch": w_branch[l], "w_gate": w_gate[l], "b_gate": b_gate[l], "w_out": w_out[l],
            "w_up": w_up[l], "conv_w": conv_w[l], "conv_b": conv_b[l], "w_down": w_down[l]})
        mod = mod_all[l]
        pc = _in_call(cx, mod, ctx_row, lw, tabs_ctx, n_ctx)
        pz = _in_call(x, mod, None, lw, tabs_lat, tm)

        o_na = _na_call(pz["na_q"], pz["na_k"], pz["na_v"], pc["na_k"], pc["na_v"], _na_bias(na_rpb[l], t // (4 * GRID_W)))
        o_mla = _flash_call(pz["mla_q"], pz["mla_k"], pz["mla_vt"], pc["mla_k"], pc["mla_vt"], tq, tk)
        o_gqa = _flash_call(pz["gqa_q"], pz["gqa_k"], pz["gqa_vt"], pc["gqa_k"], pc["gqa_vt"], tq, tk)
        x = _merge_call(x, mod, None, (o_na, o_mla, pz["sg_o"], o_gqa), lw, tm)
        x = _ffn_call(x, mod, None, lw, tm)

        if l < depth - 1:
            oc_na = _na_ctx_call(pc["na_q"], pc["na_k"], pc["na_v"])
            oc_mla = _flash_call(pc["mla_q"], pc["mla_k"], pc["mla_vt"], None, None, n_ctx, n_ctx)
            oc_gqa = _flash_call(pc["gqa_q"], pc["gqa_k"], pc["gqa_vt"], None, None, n_ctx, n_ctx)
            cx = _merge_call(cx, mod, ctx_row, (oc_na, oc_mla, pc["sg_o"], oc_gqa), lw, n_ctx)
            cx = _ffn_call(cx, mod, ctx_row, lw, n_ctx)
    return x
```

```python
import functools
import math

import numpy as np
import jax
import jax.numpy as jnp
from jax import lax
from jax.experimental import pallas as pl
from jax.experimental.pallas import tpu as pltpu

F32 = jnp.float32
BF16 = jnp.bfloat16

D_MODEL = 1024
GRID_W = 64
HEAD_DIM = 64
NA_HEADS = 4
NA_ROWS = 8
NA_COLS = 16
MLA_HEADS = 4
MLA_Q_LORA = 256
MLA_KV_LORA = 192
MLA_NOPE = 64
MLA_ROPE = 32
MLA_V = 64
SG_GROUPS = 4
SG_CHUNK = 128
SG_WIDTH = 256
GQA_HEADS = 4
GQA_KV_HEADS = 2
N_BRANCH = 4
BRANCH_WIDTH = 256
FFN_DIM = 2816
ROPE_THETA = 10000.0
EPS = 1e-6
NEG_INF = -1e30

NA_IN = 3 * NA_HEADS * HEAD_DIM
MLA_IN = MLA_Q_LORA + MLA_KV_LORA + MLA_ROPE
SG_IN = 2 * SG_WIDTH
GQA_IN = (GQA_HEADS + 2 * GQA_KV_HEADS) * HEAD_DIM

LANES = 128
MXU_DIM = 256
BF16_SUBLANES = 16
V_AUG = HEAD_DIM + BF16_SUBLANES
LOG2E = math.log2(math.e)
MLA_LOGIT_SCALE = (MLA_NOPE + MLA_ROPE) ** -0.5 * LOG2E
GQA_LOGIT_SCALE = HEAD_DIM ** -0.5 * LOG2E

C_NA = 0
C_MLA = C_NA + NA_IN
C_SG = C_MLA + 640
C_GQA = C_SG + SG_IN
IN_ARR = C_GQA + 640

G_NA, G_CQ, G_CKV, G_MQ, G_MKN, G_MKR, G_SG, G_GQA, G_TOT = 0, 512, 768, 1024, 1536, 2048, 2176, 2432, 2944

FFN_CHUNK = 256
N_FFN_CHUNKS = FFN_DIM // FFN_CHUNK
FFN_LOOKAHEAD = 2
HALO = BF16_SUBLANES


def _const_spec(shape):
    nd = len(shape)
    return pl.BlockSpec(shape, lambda *_: (0,) * nd, pipeline_mode=pl.Buffered(1))


def _params(vmem_mb, n_grid):
    return pltpu.CompilerParams(dimension_semantics=("arbitrary",) * n_grid,
                                vmem_limit_bytes=vmem_mb * 1024 * 1024)


def _dot(a, b):
    return jnp.dot(a, b, preferred_element_type=F32)


def _dot_nt(a, b):
    return lax.dot_general(a, b, (((1,), (1,)), ((), ())), preferred_element_type=F32)


def _rms(xf):
    return xf * lax.rsqrt(jnp.mean(xf * xf, axis=-1, keepdims=True) + EPS)


def _segsum(x2, s_ref):
    hi = x2.astype(BF16)
    lo = (x2 - hi.astype(F32)).astype(BF16)
    s = s_ref[...]
    parts = []
    for j in range(x2.shape[1] // MXU_DIM):
        sl = slice(MXU_DIM * j, MXU_DIM * (j + 1))
        parts.append(_dot(hi[:, sl], s) + _dot(lo[:, sl], s))
    return parts[0] if len(parts) == 1 else jnp.concatenate(parts, axis=1)


def _rope(x, cos, sin_signed, shift, first_half):
    w = x.shape[1]
    partner = jnp.where(first_half, pltpu.roll(x, w - shift, 1), pltpu.roll(x, shift, 1))
    return x * cos + partner * sin_signed


def _ada_kernel(c_ref, w_ref, b_ref, o_ref):
    c = c_ref[...]
    a = (c * jax.nn.sigmoid(c)).astype(BF16)
    o_ref[...] = _dot(a, w_ref[...].astype(BF16)) + b_ref[...]


def _ada_call(cvec, w_ada, b_ada):
    n_layers, d, n = w_ada.shape
    tn = 1536
    return pl.pallas_call(
        _ada_kernel,
        grid=(n_layers, n // tn),
        in_specs=[pl.BlockSpec((8, d), lambda l, j: (0, 0)),
                  pl.BlockSpec((None, d, tn), lambda l, j: (l, 0, j)),
                  pl.BlockSpec((None, 1, tn), lambda l, j: (l, 0, j))],
        out_specs=pl.BlockSpec((None, 8, tn), lambda l, j: (l, 0, j)),
        out_shape=jax.ShapeDtypeStruct((n_layers, 8, n), F32),
        compiler_params=_params(32, 2),
        name="ada",
    )(cvec, w_ada, b_ada.reshape(n_layers, 1, n))


def _in_kernel(x_ref, mod_ref, w_in_ref, w_uq_ref, w_ukv_ref, g_ref, s64_ref, smq_ref, invq_ref,
               gc_ref, gs_ref, mc_ref, ms_ref, sgw_ref, sgb_ref,
               naq_ref, nak_ref, nav_ref, mq_ref, mk_ref, mvt_ref, sgo_ref, gq_ref, gk_ref, gvt_ref):
    tm = x_ref.shape[0]
    x = x_ref[...]
    h = (_rms(x) * (1.0 + mod_ref[1:2, :]) + mod_ref[0:1, :]).astype(BF16)

    def gain(off, width):
        return g_ref[:, off:off + width]

    inv_head = 1.0 / HEAD_DIM

    z = _dot(h, w_in_ref[:, C_NA:C_NA + NA_IN])
    qk = z[:, 0:512]
    qk = qk * lax.rsqrt(_segsum(qk * qk, s64_ref) * inv_head + EPS) * gain(G_NA, 512)
    naq_ref[...] = qk[:, 0:256].astype(BF16)
    nak_ref[...] = qk[:, 256:512].astype(BF16)
    nav_ref[...] = z[:, 512:768].astype(BF16)

    z = _dot(h, w_in_ref[:, C_MLA:C_MLA + 640])
    cq, ckv, kr = z[:, 0:256], z[:, 256:512], z[:, 512:640]
    cq = cq * lax.rsqrt(jnp.mean(cq * cq, axis=-1, keepdims=True) + EPS) * gain(G_CQ, 256)
    ckv = ckv * lax.rsqrt(jnp.sum(ckv * ckv, axis=-1, keepdims=True) * (1.0 / MLA_KV_LORA) + EPS) * gain(G_CKV, 256)
    q = _dot(cq.astype(BF16), w_uq_ref[...])
    q = q * lax.rsqrt(_segsum(q * q, smq_ref) * invq_ref[...] + EPS) * gain(G_MQ, 512)
    kv = _dot(ckv.astype(BF16), w_ukv_ref[...])
    kn = kv[:, 0:512]
    kn = kn * lax.rsqrt(_segsum(kn * kn, s64_ref) * inv_head + EPS) * gain(G_MKN, 512)
    kr = kr * lax.rsqrt(jnp.sum(kr * kr, axis=-1, keepdims=True) * (1.0 / MLA_ROPE) + EPS) * gain(G_MKR, 128)
    lane = lax.broadcasted_iota(jnp.int32, (tm, LANES), 1)
    first8 = (lane & 15) < 8
    mc, ms = mc_ref[...], ms_ref[...]
    kr = _rope(kr, mc, ms, 8, first8)
    for hh in range(MLA_HEADS):
        sl = slice(LANES * hh, LANES * (hh + 1))
        mq_ref[hh] = _rope(q[:, sl], mc, ms, 8, first8).astype(BF16)
        mk_ref[hh] = (kn[:, sl] + kr).astype(BF16)
    vt = kv[:, 512:768].T
    ones = jnp.ones((BF16_SUBLANES, tm), BF16)
    for hh in range(MLA_HEADS):
        mvt_ref[hh, 0:HEAD_DIM, :] = vt[HEAD_DIM * hh:HEAD_DIM * (hh + 1), :].astype(BF16)
        mvt_ref[hh, HEAD_DIM:V_AUG, :] = ones

    z = _dot(h, w_in_ref[:, C_SG:C_SG + SG_IN])
    uv = jax.nn.gelu(z)
    u, v = uv[:, 0:SG_WIDTH], uv[:, SG_WIDTH:]
    v = v * lax.rsqrt(jnp.mean(v * v, axis=-1, keepdims=True) + EPS) * gain(G_SG, 256)
    grp = lax.broadcasted_iota(jnp.int32, (SG_CHUNK, SG_WIDTH), 1) >> 6
    for c in range(tm // SG_CHUNK):
        rows = slice(SG_CHUNK * c, SG_CHUNK * (c + 1))
        vc = v[rows, :]
        mixed = sgb_ref[...]
        for gi in range(SG_GROUPS):
            mixed = mixed + _dot(sgw_ref[gi], jnp.where(grp == gi, vc, 0.0).astype(BF16))
        sgo_ref[rows, :] = (u[rows, :] * mixed).astype(BF16)

    z = _dot(h, w_in_ref[:, C_GQA:C_GQA + 640])
    qk = z[:, 0:512]
    qk = qk * lax.rsqrt(_segsum(qk * qk, s64_ref) * inv_head + EPS) * gain(G_GQA, 512)
    lane2 = lax.broadcasted_iota(jnp.int32, (tm, 2 * LANES), 1)
    first16 = (lane2 & 31) < 16
    gc, gs = gc_ref[...], gs_ref[...]
    qn = _rope(qk[:, 0:256], gc, gs, 16, first16)
    kn = _rope(qk[:, 256:512], gc, gs, 16, first16)
    half = lane >> 6
    for g in range(GQA_KV_HEADS):
        sl = slice(LANES * g, LANES * (g + 1))
        for r in range(GQA_HEADS // GQA_KV_HEADS):
            gq_ref[2 * g + r] = jnp.where(half == r, qn[:, sl], 0.0).astype(BF16)
        gk_ref[g] = kn[:, sl].astype(BF16)
    vt = z[:, 512:640].T
    for g in range(GQA_KV_HEADS):
        gvt_ref[g, 0:HEAD_DIM, :] = vt[HEAD_DIM * g:HEAD_DIM * (g + 1), :].astype(BF16)
        gvt_ref[g, HEAD_DIM:V_AUG, :] = ones


def _in_call(x, mod, mod_row, lw, tabs, tm):
    b, t, d = x.shape
    nt = t // tm
    if mod_row is None:
        mod_map = lambda i, j: (j, 0, 0)
    else:
        mod_map = lambda i, j: (mod_row, 0, 0)
    tok = lambda i, j: (j, i, 0)
    tab = lambda i, j: (i, 0)
    head_tok = lambda i, j: (j, 0, i, 0)
    head_t = lambda i, j: (j, 0, 0, i)
    in_specs = [
        pl.BlockSpec((None, tm, d), tok),
        pl.BlockSpec((None, 6, d), mod_map),
        _const_spec((d, IN_ARR)),
        _const_spec((MLA_Q_LORA, 512)),
        _const_spec((256, 768)),
        _const_spec((1, G_TOT)),
        _const_spec((MXU_DIM, MXU_DIM)),
        _const_spec((MXU_DIM, MXU_DIM)),
        _const_spec((1, 512)),
        pl.BlockSpec((tm, 256), tab), pl.BlockSpec((tm, 256), tab),
        pl.BlockSpec((tm, LANES), tab), pl.BlockSpec((tm, LANES), tab),
        _const_spec((SG_GROUPS, SG_CHUNK, SG_CHUNK)),
        _const_spec((SG_CHUNK, SG_WIDTH)),
    ]
    out_specs = [
        pl.BlockSpec((None, tm, 256), tok), pl.BlockSpec((None, tm, 256), tok), pl.BlockSpec((None, tm, 256), tok),
        pl.BlockSpec((None, MLA_HEADS, tm, LANES), head_tok),
        pl.BlockSpec((None, MLA_HEADS, tm, LANES), head_tok),
        pl.BlockSpec((None, MLA_HEADS, V_AUG, tm), head_t),
        pl.BlockSpec((None, tm, 256), tok),
        pl.BlockSpec((None, GQA_HEADS, tm, LANES), head_tok),
        pl.BlockSpec((None, GQA_KV_HEADS, tm, LANES), head_tok),
        pl.BlockSpec((None, GQA_KV_HEADS, V_AUG, tm), head_t),
    ]
    sds = jax.ShapeDtypeStruct
    out_shape = [
        sds((b, t, 256), BF16), sds((b, t, 256), BF16), sds((b, t, 256), BF16),
        sds((b, MLA_HEADS, t, LANES), BF16), sds((b, MLA_HEADS, t, LANES), BF16),
        sds((b, MLA_HEADS, V_AUG, t), BF16),
        sds((b, t, 256), BF16),
        sds((b, GQA_HEADS, t, LANES), BF16), sds((b, GQA_KV_HEADS, t, LANES), BF16),
        sds((b, GQA_KV_HEADS, V_AUG, t), BF16),
    ]
    outs = pl.pallas_call(
        _in_kernel, grid=(nt, b), in_specs=in_specs, out_specs=out_specs, out_shape=out_shape,
        compiler_params=_params(48, 2), name="in_proj",
    )(x, mod, lw["w_in"], lw["w_uq"], lw["w_ukv"], lw["gains"], lw["s64"], lw["smq"], lw["invq"],
      tabs["gc"], tabs["gs"], tabs["mc"], tabs["ms"], lw["sgw"], lw["sgb"])
    names = ("na_q", "na_k", "na_v", "mla_q", "mla_k", "mla_vt", "sg_o", "gqa_q", "gqa_k", "gqa_vt")
    return dict(zip(names, outs))


def _na_heads(q, k_blocks, v_blocks, biases):
    tq = q.shape[0]
    head_of_lane = lax.broadcasted_iota(jnp.int32, (tq, 256), 1) >> 6
    qf = q.astype(F32) * (HEAD_DIM ** -0.5)
    out = jnp.zeros((tq, 256), F32)
    for hh in range(NA_HEADS):
        qh = jnp.where(head_of_lane == hh, qf, 0.0).astype(BF16)
        scores = []
        for kb, bias in zip(k_blocks, biases):
            s = _dot_nt(qh, kb)
            if bias is not None:
                s = s + bias[hh]
            scores.append(s)
        m = scores[0].max(axis=-1, keepdims=True)
        for s in scores[1:]:
            m = jnp.maximum(m, s.max(axis=-1, keepdims=True))
        l = jnp.zeros((tq, 1), F32)
        o = jnp.zeros((tq, 256), F32)
        for s, vb in zip(scores, v_blocks):
            p = jnp.exp(s - m)
            l = l + p.sum(axis=-1, keepdims=True)
            o = o + _dot(p.astype(BF16), vb)
        out = out + jnp.where(head_of_lane == hh, o * (1.0 / l), 0.0)
    return out


def _na_kernel(q_ref, kp_ref, kc_ref, kn_ref, vp_ref, vc_ref, vn_ref, kx_ref, vx_ref, bias_ref, o_ref):
    biases = [bias_ref.at[:, :, 0:256], bias_ref.at[:, :, 256:512], bias_ref.at[:, :, 512:768], None]
    out = _na_heads(q_ref[...], [kp_ref[...], kc_ref[...], kn_ref[...], kx_ref[...]],
                    [vp_ref[...], vc_ref[...], vn_ref[...], vx_ref[...]], biases)
    o_ref[...] = out.astype(BF16)


def _na_call(q, k, v, kx, vx, bias):
    b, t, w = q.shape
    c = kx.shape[1]
    tq = 4 * GRID_W
    nt = t // tq
    cur = lambda bi, i: (bi, i, 0)
    prev = lambda bi, i: (bi, jnp.maximum(i - 1, 0), 0)
    nxt = lambda bi, i: (bi, jnp.minimum(i + 1, nt - 1), 0)
    ctx = lambda bi, i: (bi, 0, 0)
    variant = lambda bi, i: (jnp.where(i == 0, 0, jnp.where(i == nt - 1, 2, 1)), 0, 0, 0)
    blk = lambda m: pl.BlockSpec((None, tq, w), m)
    return pl.pallas_call(
        _na_kernel, grid=(b, nt),
        in_specs=[blk(cur), blk(prev), blk(cur), blk(nxt), blk(prev), blk(cur), blk(nxt),
                  pl.BlockSpec((None, c, w), ctx), pl.BlockSpec((None, c, w), ctx),
                  pl.BlockSpec((None, NA_HEADS, tq, 3 * tq), variant)],
        out_specs=blk(cur),
        out_shape=jax.ShapeDtypeStruct((b, t, w), BF16),
        compiler_params=_params(48, 2), name="na_attn",
    )(q, k, k, k, v, v, v, kx, vx, bias)


def _na_ctx_kernel(q_ref, k_ref, v_ref, o_ref):
    o_ref[...] = _na_heads(q_ref[...], [k_ref[...]], [v_ref[...]], [None]).astype(BF16)


def _na_ctx_call(q, k, v):
    b, c, w = q.shape
    spec = pl.BlockSpec((None, c, w), lambda bi: (bi, 0, 0))
    return pl.pallas_call(
        _na_ctx_kernel, grid=(b,), in_specs=[spec, spec, spec], out_specs=spec,
        out_shape=jax.ShapeDtypeStruct((b, c, w), BF16),
        compiler_params=_params(32, 1), name="na_ctx_attn",
    )(q, k, v)


def _na_bias(rpb, nt):
    rows = 4 * nt
    col = np.arange(GRID_W)
    c_start = np.clip(col - NA_COLS // 2, 0, GRID_W - NA_COLS)
    valid_c = (col[None, :] >= c_start[:, None]) & (col[None, :] < c_start[:, None] + NA_COLS)
    dc = np.clip(col[None, :] - col[:, None] + (NA_COLS - 1), 0, 2 * NA_COLS - 2)
    pick_dc = jnp.asarray(dc[:, :, None] == np.arange(2 * NA_COLS - 1), F32)
    by_col = jnp.einsum("hrd,qkd->hrqk", rpb, pick_dc, precision=lax.Precision.HIGHEST)
    variants = []
    for i in (0, min(1, nt - 1), nt - 1):
        rq = 4 * i + np.arange(4)
        start = np.clip(rq - NA_ROWS // 2, 0, rows - NA_ROWS)
        blocks = np.array([i - 1, i, i + 1])
        rk = (4 * blocks[:, None] + np.arange(4)[None, :]).reshape(-1)
        block_ok = np.repeat((blocks >= 0) & (blocks < nt), 4)
        valid_r = (rk[None, :] >= start[:, None]) & (rk[None, :] < start[:, None] + NA_ROWS) & block_ok[None, :]
        dr = np.clip(rk[None, :] - rq[:, None] + (NA_ROWS - 1), 0, 2 * NA_ROWS - 2)
        pick_dr = jnp.asarray(dr[:, :, None] == np.arange(2 * NA_ROWS - 1), F32)
        vals = jnp.einsum("abr,hrqk->haqbk", pick_dr, by_col, precision=lax.Precision.HIGHEST)
        ok = valid_r[:, None, :, None] & valid_c[None, :, None, :]
        variants.append(jnp.where(ok[None], vals, NEG_INF).reshape(NA_HEADS, 4 * GRID_W, 12 * GRID_W))
    return jnp.stack(variants).astype(F32)


FLASH_COL = MXU_DIM
STREAM_UNROLL = 16
SCORE_LOOKAHEAD = 2
SEED_KEYS = 64
F32_SAFE_MAX = 3e38
MIN_DENOMINATOR = 2.0 ** -40


def _col_max(s):
    keys = s.shape[0]
    if keys % 64 == 0 and keys > 64:
        s = jnp.max(s.reshape(keys // 64, 64, s.shape[1]), axis=0)
    return jnp.max(s, axis=0, keepdims=True)


def _flash_kernel(*refs, n_heads, group, tq, tk, has_ctx):
    if has_ctx:
        q_ref, k_ref, vt_ref, kx_ref, vtx_ref, o_ref, acc_scr, ot_scr = refs
    else:
        q_ref, k_ref, vt_ref, o_ref, acc_scr, ot_scr = refs
    n_sub = q_ref.shape[1] // tq
    n_chunks = k_ref.shape[1] // tk
    n_slabs = n_heads // group
    cols = [(hh, qi) for hh in range(n_heads) for qi in range(n_sub)]
    heads = range(len(cols))
    slab_of = [hh // group for hh, _ in cols]

    def q_col(c):
        hh, qi = cols[c]
        return q_ref[hh, qi * tq:(qi + 1) * tq, :]

    def load_chunk(j):
        off = pl.multiple_of(j * tk, tk)
        return ([k_ref[sl, pl.ds(off, tk), :] for sl in range(n_slabs)],
                [vt_ref[sl, :, pl.ds(off, tk)] for sl in range(n_slabs)])

    def pv(vt, p):
        return _dot(vt, p.astype(BF16))

    def exact_step(ks, vts, ms):
        new_ms = []
        for hh in heads:
            s = _dot_nt(ks[slab_of[hh]], q_col(hh))
            m_new = jnp.maximum(ms[hh], _col_max(s))
            p = jnp.exp2(s - m_new)
            acc_scr[hh] = acc_scr[hh] * jnp.exp2(ms[hh] - m_new) + pv(vts[slab_of[hh]], p)
            new_ms.append(m_new)
        return tuple(new_ms)

    def stream(chunks, ms):
        units = [(i, hh) for i in range(len(chunks)) for hh in heads]
        score = lambda u: _dot_nt(chunks[u[0]][0][slab_of[u[1]]], q_col(u[1]))
        pending = [score(u) for u in units[:SCORE_LOOKAHEAD]]
        for n, (i, hh) in enumerate(units):
            s = pending.pop(0)
            if n + SCORE_LOOKAHEAD < len(units):
                pending.append(score(units[n + SCORE_LOOKAHEAD]))
            acc_scr[hh] += pv(chunks[i][1][slab_of[hh]], jnp.exp2(s - ms[hh]))

    ctx_chunk = ([kx_ref[sl] for sl in range(n_slabs)], [vtx_ref[sl] for sl in range(n_slabs)]) if has_ctx else None
    seed_ref = kx_ref if has_ctx else k_ref
    ms = tuple(_col_max(_dot_nt(seed_ref[slab_of[hh], 0:SEED_KEYS, :], q_col(hh))) for hh in heads)
    acc_scr[...] = jnp.zeros_like(acc_scr)
    if has_ctx:
        stream([ctx_chunk], ms)
    per_iter = max(u for u in (STREAM_UNROLL, 8, 4, 2, 1) if n_chunks % u == 0)

    def stream_body(j, carry):
        stream([load_chunk(j * per_iter + i) for i in range(per_iter)], ms)
        return carry

    lax.fori_loop(0, n_chunks // per_iter, stream_body, 0)
    unusable = jnp.zeros((V_AUG, tq), F32)
    for hh in heads:
        acc = acc_scr[hh]
        unusable = jnp.maximum(unusable, jnp.where(jnp.abs(acc) < F32_SAFE_MAX, 0.0, 1.0))
        unusable = jnp.maximum(unusable, jnp.where(acc[HEAD_DIM:HEAD_DIM + 1] > MIN_DENOMINATOR, 0.0, 1.0))

    @pl.when(jnp.max(unusable) > 0.0)
    def _():
        acc_scr[...] = jnp.zeros_like(acc_scr)
        ms = tuple(jnp.full((1, tq), NEG_INF, F32) for _ in heads)
        if has_ctx:
            ms = exact_step(*ctx_chunk, ms)
        lax.fori_loop(0, n_chunks, lambda j, m: exact_step(*load_chunk(j), m), ms)

    for c, (hh, qi) in enumerate(cols):
        acc = acc_scr[c]
        ot_scr[HEAD_DIM * hh:HEAD_DIM * (hh + 1), qi * tq:(qi + 1) * tq] = (
            acc[0:HEAD_DIM] * (1.0 / acc[HEAD_DIM:HEAD_DIM + 1]))
    o_ref[...] = ot_scr[...].T.astype(BF16)


def _flash_call(q, k, vt, kx, vtx, tq, tk):
    b, n_heads, t_q, w = q.shape
    slabs, t_k = k.shape[1], k.shape[2]
    has_ctx = kx is not None
    full4 = lambda bi, i: (bi, 0, 0, 0)
    in_specs = [pl.BlockSpec((None, n_heads, tq, w), lambda bi, i: (bi, 0, i, 0)),
                pl.BlockSpec((None, slabs, t_k, w), full4),
                pl.BlockSpec((None, slabs, V_AUG, t_k), full4)]
    args = [q, k, vt]
    if has_ctx:
        c = kx.shape[2]
        in_specs += [pl.BlockSpec((None, slabs, c, w), full4), pl.BlockSpec((None, slabs, V_AUG, c), full4)]
        args += [kx, vtx]
    col = min(FLASH_COL, tq)
    kern = functools.partial(_flash_kernel, n_heads=n_heads, group=n_heads // slabs, tq=col, tk=tk,
                             has_ctx=has_ctx)
    return pl.pallas_call(
        kern, grid=(b, t_q // tq), in_specs=in_specs,
        out_specs=pl.BlockSpec((None, tq, n_heads * HEAD_DIM), lambda bi, i: (bi, i, 0)),
        out_shape=jax.ShapeDtypeStruct((b, t_q, n_heads * HEAD_DIM), BF16),
        scratch_shapes=[pltpu.VMEM((n_heads * (tq // col), V_AUG, col), F32),
                        pltpu.VMEM((n_heads * HEAD_DIM, tq), F32)],
        compiler_params=_params(56, 2), name="flash",
    )(*args)


def _merge_kernel(x_ref, mod_ref, o0_ref, o1_ref, o2_ref, o3_ref, wg_ref, bg_ref, wb_ref, wo_ref, out_ref):
    x = x_ref[...]
    h = (_rms(x) * (1.0 + mod_ref[1:2, :]) + mod_ref[0:1, :]).astype(BF16)
    y = None
    for i, o_ref in enumerate((o0_ref, o1_ref, o2_ref, o3_ref)):
        gate = jax.nn.sigmoid(_dot(h, wg_ref[i]) + bg_ref[i])
        term = gate * _dot(o_ref[...], wb_ref[i])
        y = term if y is None else y + term
    out_ref[...] = x + mod_ref[2:3, :] * _dot(y.astype(BF16), wo_ref[...])


def _merge_call(x, mod, mod_row, branches, lw, tm):
    b, t, d = x.shape
    if mod_row is None:
        mod_map = lambda i, j: (j, 0, 0)
    else:
        mod_map = lambda i, j: (mod_row, 0, 0)
    tok = lambda i, j: (j, i, 0)
    br = pl.BlockSpec((None, tm, BRANCH_WIDTH), tok)
    return pl.pallas_call(
        _merge_kernel, grid=(t // tm, b),
        in_specs=[pl.BlockSpec((None, tm, d), tok), pl.BlockSpec((None, 6, d), mod_map), br, br, br, br,
                  _const_spec((N_BRANCH, d, d)), _const_spec((N_BRANCH, 1, d)),
                  _const_spec((N_BRANCH, BRANCH_WIDTH, d)), _const_spec((d, d))],
        out_specs=pl.BlockSpec((None, tm, d), tok),
        out_shape=jax.ShapeDtypeStruct((b, t, d), F32),
        compiler_params=_params(56, 2), name="merge",
    )(x, mod, *branches, lw["w_gate"], lw["b_gate"], lw["w_branch"], lw["w_out"])


def _ffn_kernel(x_ref, xp_ref, xn_ref, mod_ref, wu_ref, cw_ref, cb_ref, wd_ref, out_ref, h_scr, act_scr):
    tm = x_ref.shape[0]
    i = pl.program_id(0)
    nt = pl.num_programs(0)
    sh, sc = mod_ref[3:4, :], mod_ref[4:5, :]

    def modulated(v):
        return _rms(v) * (1.0 + sc) + sh

    x = x_ref[...]
    h_scr[0:HALO, :] = jnp.where(i > 0, modulated(xp_ref[...]), 0.0).astype(BF16)
    h_scr[HALO:HALO + tm, :] = modulated(x).astype(BF16)
    h_scr[HALO + tm:, :] = jnp.where(i < nt - 1, modulated(xn_ref[...]), 0.0).astype(BF16)
    rows = tm + 2 * HALO
    main = slice(HALO, HALO + tm)

    def up_proj(j):
        return _dot(h_scr[...], wu_ref[j])

    ups = [up_proj(j) for j in range(FFN_LOOKAHEAD)]
    for j in range(N_FFN_CHUNKS):
        up = ups.pop(0)
        if j + FFN_LOOKAHEAD < N_FFN_CHUNKS:
            ups.append(up_proj(j + FFN_LOOKAHEAD))
        cw = cw_ref[j]
        conv = (pltpu.roll(up, 1, 0)[main] * cw[0:1, :] + up[main] * cw[1:2, :]
                + pltpu.roll(up, rows - 1, 0)[main] * cw[2:3, :] + cb_ref[j])
        a, g = conv[:, 0:FFN_CHUNK], conv[:, FFN_CHUNK:]
        act_scr[:, FFN_CHUNK * j:FFN_CHUNK * (j + 1)] = (g * jax.nn.sigmoid(g) * a).astype(BF16)
    out_ref[...] = x + mod_ref[5:6, :] * _dot(act_scr[...], wd_ref[...])


def _ffn_call(x, mod, mod_row, lw, tm):
    b, t, d = x.shape
    nt = t // tm
    hb = tm // HALO
    last_halo = t // HALO - 1
    if mod_row is None:
        mod_map = lambda i, j: (j, 0, 0)
    else:
        mod_map = lambda i, j: (mod_row, 0, 0)
    tok = lambda i, j: (j, i, 0)
    return pl.pallas_call(
        _ffn_kernel, grid=(nt, b),
        in_specs=[pl.BlockSpec((None, tm, d), tok),
                  pl.BlockSpec((None, HALO, d), lambda i, j: (j, jnp.maximum(i * hb - 1, 0), 0)),
                  pl.BlockSpec((None, HALO, d), lambda i, j: (j, jnp.minimum((i + 1) * hb, last_halo), 0)),
                  pl.BlockSpec((None, 6, d), mod_map),
                  _const_spec((N_FFN_CHUNKS, d, 2 * FFN_CHUNK)),
                  _const_spec((N_FFN_CHUNKS, 3, 2 * FFN_CHUNK)),
                  _const_spec((N_FFN_CHUNKS, 1, 2 * FFN_CHUNK)),
                  _const_spec((FFN_DIM, d))],
        out_specs=pl.BlockSpec((None, tm, d), tok),
        out_shape=jax.ShapeDtypeStruct((b, t, d), F32),
        scratch_shapes=[pltpu.VMEM((tm + 2 * HALO, d), BF16), pltpu.VMEM((tm, FFN_DIM), BF16)],
        compiler_params=_params(56, 2), name="ffn",
    )(x, x, x, mod, lw["w_up"], lw["conv_w"], lw["conv_b"], lw["w_down"])


def _block_diag_ones(width, segs):
    m = np.zeros((width, width), np.float32)
    for lo, hi in segs:
        m[lo:hi, lo:hi] = 1.0
    return jnp.asarray(m, BF16)


def _rope_pattern(pos, width):
    half = width // 2
    inv = ROPE_THETA ** (-jnp.arange(half, dtype=F32) / half)
    ang = pos.astype(F32)[:, None] * inv[None, :]
    cos, sin = jnp.cos(ang), jnp.sin(ang)
    return jnp.concatenate([cos, cos], axis=1), jnp.concatenate([-sin, sin], axis=1)


def _rope_tables(t, identity):
    if identity:
        return {"gc": jnp.ones((t, 256), F32), "gs": jnp.zeros((t, 256), F32),
                "mc": jnp.ones((t, LANES), F32), "ms": jnp.zeros((t, LANES), F32)}
    pos = jnp.arange(t)
    rows, cols = pos // GRID_W, pos % GRID_W
    rc, rs = _rope_pattern(rows, HEAD_DIM // 2)
    cc, cs = _rope_pattern(cols, HEAD_DIM // 2)
    gc = jnp.tile(jnp.concatenate([rc, cc], axis=1), (1, 4))
    gs = jnp.tile(jnp.concatenate([rs, cs], axis=1), (1, 4))
    rc, rs = _rope_pattern(rows, MLA_ROPE // 2)
    cc, cs = _rope_pattern(cols, MLA_ROPE // 2)
    one, zero = jnp.ones((t, MLA_NOPE), F32), jnp.zeros((t, MLA_NOPE), F32)
    pad1, pad0 = jnp.ones((t, LANES - MLA_NOPE - MLA_ROPE), F32), jnp.zeros((t, LANES - MLA_NOPE - MLA_ROPE), F32)
    return {"gc": gc, "gs": gs,
            "mc": jnp.concatenate([one, rc, cc, pad1], axis=1),
            "ms": jnp.concatenate([zero, rs, cs, pad0], axis=1)}


def _arrange_layer(p):
    d = D_MODEL
    w_in = p["w_in"]
    na, mla = w_in[:, :NA_IN], w_in[:, NA_IN:NA_IN + MLA_IN]
    sg = w_in[:, NA_IN + MLA_IN:NA_IN + MLA_IN + SG_IN]
    gqa = w_in[:, NA_IN + MLA_IN + SG_IN:]
    z = lambda n: jnp.zeros((d, n), F32)
    cq, ckv, kr = mla[:, :MLA_Q_LORA], mla[:, MLA_Q_LORA:MLA_Q_LORA + MLA_KV_LORA], mla[:, MLA_Q_LORA + MLA_KV_LORA:]
    gq, gk, gv = gqa[:, :256], gqa[:, 256:384], gqa[:, 384:]
    gk_dup = jnp.concatenate([gk[:, :64], gk[:, :64], gk[:, 64:], gk[:, 64:]], axis=1)
    w_in_arr = jnp.concatenate([na, cq, ckv, z(256 - MLA_KV_LORA), z(MLA_NOPE), kr, z(LANES - MLA_NOPE - MLA_ROPE),
                                sg, gq, gk_dup, gv], axis=1).astype(BF16)

    w_uq = p["mla_w_uq"].reshape(MLA_Q_LORA, MLA_HEADS, MLA_NOPE + MLA_ROPE)
    w_uq = jnp.pad(w_uq, ((0, 0), (0, 0), (0, LANES - MLA_NOPE - MLA_ROPE))).reshape(MLA_Q_LORA, MLA_HEADS * LANES)
    w_ukv = p["mla_w_ukv"].reshape(MLA_KV_LORA, MLA_HEADS, MLA_NOPE + MLA_V)
    k_part = jnp.pad(w_ukv[:, :, :MLA_NOPE], ((0, 0), (0, 0), (0, LANES - MLA_NOPE))).reshape(MLA_KV_LORA, MLA_HEADS * LANES)
    v_part = w_ukv[:, :, MLA_NOPE:].reshape(MLA_KV_LORA, MLA_HEADS * MLA_V)
    w_ukv = jnp.pad(jnp.concatenate([k_part, v_part], axis=1), ((0, 256 - MLA_KV_LORA), (0, 0)))

    qg, kg = p["mla_q_norm"], p["mla_k_norm"]
    z1 = lambda n: jnp.zeros((n,), F32)
    gains = jnp.concatenate([
        jnp.tile(p["na_q_norm"], 4), jnp.tile(p["na_k_norm"], 4),
        p["mla_cq_norm"],
        p["mla_ckv_norm"], z1(256 - MLA_KV_LORA),
        jnp.tile(jnp.concatenate([qg * MLA_LOGIT_SCALE, z1(LANES - MLA_NOPE - MLA_ROPE)]), 4),
        jnp.tile(jnp.concatenate([kg[:MLA_NOPE], z1(LANES - MLA_NOPE)]), 4),
        z1(MLA_NOPE), kg[MLA_NOPE:], z1(LANES - MLA_NOPE - MLA_ROPE),
        p["sg_v_norm"],
        jnp.tile(p["gqa_q_norm"] * GQA_LOGIT_SCALE, 4), jnp.tile(p["gqa_k_norm"], 4),
    ]).reshape(1, G_TOT)

    invq = np.tile(np.concatenate([np.full(MLA_NOPE, 1.0 / MLA_NOPE), np.full(MLA_ROPE, 1.0 / MLA_ROPE),
                                   np.ones(LANES - MLA_NOPE - MLA_ROPE)]), 4).astype(np.float32).reshape(1, 512)
    sgb = jnp.repeat(p["sg_b_s"].T, SG_WIDTH // SG_GROUPS, axis=1)

    w_up = p["w_up"]
    a_part = w_up[:, :FFN_DIM].reshape(d, N_FFN_CHUNKS, FFN_CHUNK)
    g_part = w_up[:, FFN_DIM:].reshape(d, N_FFN_CHUNKS, FFN_CHUNK)
    w_up_arr = jnp.concatenate([a_part, g_part], axis=2).transpose(1, 0, 2).astype(BF16)

    def chunked(v):
        lead = v.shape[:-1]
        a = v[..., :FFN_DIM].reshape(*lead, N_FFN_CHUNKS, FFN_CHUNK)
        g = v[..., FFN_DIM:].reshape(*lead, N_FFN_CHUNKS, FFN_CHUNK)
        return jnp.moveaxis(jnp.concatenate([a, g], axis=-1), -2, 0)

    return {
        "w_in": w_in_arr, "w_uq": w_uq.astype(BF16), "w_ukv": w_ukv.astype(BF16), "gains": gains,
        "s64": _block_diag_ones(MXU_DIM, [(64 * i, 64 * i + 64) for i in range(4)]),
        "smq": _block_diag_ones(MXU_DIM, [(0, 64), (64, 96), (128, 192), (192, 224)]),
        "invq": jnp.asarray(invq),
        "sgw": p["sg_w_s"].astype(BF16), "sgb": sgb,
        "w_gate": p["w_gate"].astype(BF16), "b_gate": p["b_gate"].reshape(N_BRANCH, 1, d),
        "w_branch": p["w_branch"].astype(BF16), "w_out": p["w_out"].astype(BF16),
        "w_up": w_up_arr, "conv_w": chunked(p["conv_w"]), "conv_b": chunked(p["conv_b"].reshape(1, -1)),
        "w_down": p["w_down"].astype(BF16),
    }


def kernel(x, c, ctx, c_ctx, w_ada, b_ada, w_in, na_q_norm, na_k_norm, na_rpb, mla_cq_norm, mla_ckv_norm,
           mla_w_uq, mla_w_ukv, mla_q_norm, mla_k_norm, sg_v_norm, sg_w_s, sg_b_s, gqa_q_norm, gqa_k_norm,
           w_branch, w_gate, b_gate, w_out, w_up, conv_w, conv_b, w_down):
    b, t, d = x.shape
    n_ctx = ctx.shape[1]
    depth = w_in.shape[0]
    ctx_row = b
    cvec = jnp.zeros((8, d), F32).at[:b].set(c).at[ctx_row].set(c_ctx)
    mod_all = _ada_call(cvec, w_ada, b_ada).reshape(depth, 8, 6, d)

    tm = min(512, t)
    tq = min(512, t)
    tk = min(512, t)
    tabs_lat = _rope_tables(t, identity=False)
    tabs_ctx = _rope_tables(n_ctx, identity=True)

    cx = ctx
    for l in range(depth):
        lw = _arrange_layer({
            "w_in": w_in[l], "na_q_norm": na_q_norm[l], "na_k_norm": na_k_norm[l],
            "mla_cq_norm": mla_cq_norm[l], "mla_ckv_norm": mla_ckv_norm[l], "mla_w_uq": mla_w_uq[l],
            "mla_w_ukv": mla_w_ukv[l], "mla_q_norm": mla_q_norm[l], "mla_k_norm": mla_k_norm[l],
            "sg_v_norm": sg_v_norm[l], "sg_w_s": sg_w_s[l], "sg_b_s": sg_b_s[l],
            "gqa_q_norm": gqa_q_norm[l], "gqa_k_norm": gqa_k_norm[l],
            "w_branch": w_branch[l], "w_gate": w_gate[l], "b_gate": b_gate[l], "w_out": w_out[l],
            "w_up": w_up[l], "conv_w": conv_w[l], "conv_b": conv_b[l], "w_down": w_down[l]})
        mod = mod_all[l]
        pc = _in_call(cx, mod, ctx_row, lw, tabs_ctx, n_ctx)
        pz = _in_call(x, mod, None, lw, tabs_lat, tm)

        o_na = _na_call(pz["na_q"], pz["na_k"], pz["na_v"], pc["na_k"], pc["na_v"], _na_bias(na_rpb[l], t // (4 * GRID_W)))
        o_mla = _flash_call(pz["mla_q"], pz["mla_k"], pz["mla_vt"], pc["mla_k"], pc["mla_vt"], tq, tk)
        o_gqa = _flash_call(pz["gqa_q"], pz["gqa_k"], pz["gqa_vt"], pc["gqa_k"], pc["gqa_vt"], tq, tk)
        x = _merge_call(x, mod, None, (o_na, o_mla, pz["sg_o"], o_gqa), lw, tm)
        x = _ffn_call(x, mod, None, lw, tm)

        if l < depth - 1:
            oc_na = _na_ctx_call(pc["na_q"], pc["na_k"], pc["na_v"])
            oc_mla = _flash_call(pc["mla_q"], pc["mla_k"], pc["mla_vt"], None, None, n_ctx, n_ctx)
            oc_gqa = _flash_call(pc["gqa_q"], pc["gqa_k"], pc["gqa_vt"], None, None, n_ctx, n_ctx)
            cx = _merge_call(cx, mod, ctx_row, (oc_na, oc_mla, pc["sg_o"], oc_gqa), lw, n_ctx)
            cx = _ffn_call(cx, mod, ctx_row, lw, n_ctx)
    return x
```

```python
import functools
import math

import numpy as np
import jax
import jax.numpy as jnp
from jax import lax
from jax.experimental import pallas as pl
from jax.experimental.pallas import tpu as pltpu

F32 = jnp.float32
BF16 = jnp.bfloat16

D_MODEL = 1024
GRID_W = 64
HEAD_DIM = 64
NA_HEADS = 4
NA_ROWS = 8
NA_COLS = 16
MLA_HEADS = 4
MLA_Q_LORA = 256
MLA_KV_LORA = 192
MLA_NOPE = 64
MLA_ROPE = 32
MLA_V = 64
SG_GROUPS = 4
SG_CHUNK = 128
SG_WIDTH = 256
GQA_HEADS = 4
GQA_KV_HEADS = 2
N_BRANCH = 4
BRANCH_WIDTH = 256
FFN_DIM = 2816
ROPE_THETA = 10000.0
EPS = 1e-6
NEG_INF = -1e30

NA_IN = 3 * NA_HEADS * HEAD_DIM
MLA_IN = MLA_Q_LORA + MLA_KV_LORA + MLA_ROPE
SG_IN = 2 * SG_WIDTH
GQA_IN = (GQA_HEADS + 2 * GQA_KV_HEADS) * HEAD_DIM

LANES = 128
MXU_DIM = 256
BF16_SUBLANES = 16
V_AUG = HEAD_DIM + BF16_SUBLANES
LOG2E = math.log2(math.e)
MLA_LOGIT_SCALE = (MLA_NOPE + MLA_ROPE) ** -0.5 * LOG2E
GQA_LOGIT_SCALE = HEAD_DIM ** -0.5 * LOG2E

C_NA = 0
C_MLA = C_NA + NA_IN
C_SG = C_MLA + 640
C_GQA = C_SG + SG_IN
IN_ARR = C_GQA + 640

G_NA, G_CQ, G_CKV, G_MQ, G_MKN, G_MKR, G_SG, G_GQA, G_TOT = 0, 512, 768, 1024, 1536, 2048, 2176, 2432, 2944

FFN_CHUNK = 256
N_FFN_CHUNKS = FFN_DIM // FFN_CHUNK
FFN_LOOKAHEAD = 2
HALO = BF16_SUBLANES


def _const_spec(shape):
    nd = len(shape)
    return pl.BlockSpec(shape, lambda *_: (0,) * nd, pipeline_mode=pl.Buffered(1))


def _layer_spec(shape, layer):
    nd = len(shape)
    return pl.BlockSpec((None,) + tuple(shape), lambda *_: (layer,) + (0,) * nd, pipeline_mode=pl.Buffered(1))


def _params(vmem_mb, n_grid):
    return pltpu.CompilerParams(dimension_semantics=("arbitrary",) * n_grid,
                                vmem_limit_bytes=vmem_mb * 1024 * 1024)


def _dot(a, b):
    return jnp.dot(a, b, preferred_element_type=F32)


def _dot_nt(a, b):
    return lax.dot_general(a, b, (((1,), (1,)), ((), ())), preferred_element_type=F32)


def _rms(xf):
    return xf * lax.rsqrt(jnp.mean(xf * xf, axis=-1, keepdims=True) + EPS)


def _segsum(x2, s_ref):
    hi = x2.astype(BF16)
    lo = (x2 - hi.astype(F32)).astype(BF16)
    s = s_ref[...]
    parts = []
    for j in range(x2.shape[1] // MXU_DIM):
        sl = slice(MXU_DIM * j, MXU_DIM * (j + 1))
        parts.append(_dot(hi[:, sl], s) + _dot(lo[:, sl], s))
    return parts[0] if len(parts) == 1 else jnp.concatenate(parts, axis=1)


def _rope(x, cos, sin_signed, shift, first_half):
    w = x.shape[1]
    partner = jnp.where(first_half, pltpu.roll(x, w - shift, 1), pltpu.roll(x, shift, 1))
    return x * cos + partner * sin_signed


def _ada_kernel(c_ref, w_ref, b_ref, o_ref):
    c = c_ref[...]
    a = (c * jax.nn.sigmoid(c)).astype(BF16)
    o_ref[...] = _dot(a, w_ref[...].astype(BF16)) + b_ref[...]


def _ada_call(cvec, w_ada, b_ada):
    n_layers, d, n = w_ada.shape
    tn = 1536
    return pl.pallas_call(
        _ada_kernel,
        grid=(n_layers, n // tn),
        in_specs=[pl.BlockSpec((8, d), lambda l, j: (0, 0)),
                  pl.BlockSpec((None, d, tn), lambda l, j: (l, 0, j)),
                  pl.BlockSpec((None, 1, tn), lambda l, j: (l, 0, j))],
        out_specs=pl.BlockSpec((None, 8, tn), lambda l, j: (l, 0, j)),
        out_shape=jax.ShapeDtypeStruct((n_layers, 8, n), F32),
        compiler_params=_params(32, 2),
        name="ada",
    )(cvec, w_ada, b_ada.reshape(n_layers, 1, n))


def _in_kernel(x_ref, mod_ref, w_in_ref, w_uq_ref, w_ukv_ref, g_ref, s64_ref, smq_ref, invq_ref,
               gc_ref, gs_ref, mc_ref, ms_ref, sgw_ref, sgb_ref,
               naq_ref, nak_ref, nav_ref, mq_ref, mk_ref, mvt_ref, sgo_ref, gq_ref, gk_ref, gvt_ref):
    tm = x_ref.shape[0]
    x = x_ref[...]
    h = (_rms(x) * (1.0 + mod_ref[1:2, :]) + mod_ref[0:1, :]).astype(BF16)

    def gain(off, width):
        return g_ref[:, off:off + width]

    inv_head = 1.0 / HEAD_DIM

    z = _dot(h, w_in_ref[:, C_NA:C_NA + NA_IN])
    qk = z[:, 0:512]
    qk = qk * lax.rsqrt(_segsum(qk * qk, s64_ref) * inv_head + EPS) * gain(G_NA, 512)
    naq_ref[...] = qk[:, 0:256].astype(BF16)
    nak_ref[...] = qk[:, 256:512].astype(BF16)
    nav_ref[...] = z[:, 512:768].astype(BF16)

    z = _dot(h, w_in_ref[:, C_MLA:C_MLA + 640])
    cq, ckv, kr = z[:, 0:256], z[:, 256:512], z[:, 512:640]
    cq = cq * lax.rsqrt(jnp.mean(cq * cq, axis=-1, keepdims=True) + EPS) * gain(G_CQ, 256)
    ckv = ckv * lax.rsqrt(jnp.sum(ckv * ckv, axis=-1, keepdims=True) * (1.0 / MLA_KV_LORA) + EPS) * gain(G_CKV, 256)
    q = _dot(cq.astype(BF16), w_uq_ref[...])
    q = q * lax.rsqrt(_segsum(q * q, smq_ref) * invq_ref[...] + EPS) * gain(G_MQ, 512)
    kv = _dot(ckv.astype(BF16), w_ukv_ref[...])
    kn = kv[:, 0:512]
    kn = kn * lax.rsqrt(_segsum(kn * kn, s64_ref) * inv_head + EPS) * gain(G_MKN, 512)
    kr = kr * lax.rsqrt(jnp.sum(kr * kr, axis=-1, keepdims=True) * (1.0 / MLA_ROPE) + EPS) * gain(G_MKR, 128)
    lane = lax.broadcasted_iota(jnp.int32, (tm, LANES), 1)
    first8 = (lane & 15) < 8
    mc, ms = mc_ref[...], ms_ref[...]
    kr = _rope(kr, mc, ms, 8, first8)
    for hh in range(MLA_HEADS):
        sl = slice(LANES * hh, LANES * (hh + 1))
        mq_ref[hh] = _rope(q[:, sl], mc, ms, 8, first8).astype(BF16)
        mk_ref[hh] = (kn[:, sl] + kr).astype(BF16)
    vt = kv[:, 512:768].T
    ones = jnp.ones((BF16_SUBLANES, tm), BF16)
    for hh in range(MLA_HEADS):
        mvt_ref[hh, 0:HEAD_DIM, :] = vt[HEAD_DIM * hh:HEAD_DIM * (hh + 1), :].astype(BF16)
        mvt_ref[hh, HEAD_DIM:V_AUG, :] = ones

    z = _dot(h, w_in_ref[:, C_SG:C_SG + SG_IN])
    uv = jax.nn.gelu(z)
    u, v = uv[:, 0:SG_WIDTH], uv[:, SG_WIDTH:]
    v = v * lax.rsqrt(jnp.mean(v * v, axis=-1, keepdims=True) + EPS) * gain(G_SG, 256)
    grp = lax.broadcasted_iota(jnp.int32, (SG_CHUNK, SG_WIDTH), 1) >> 6
    for c in range(tm // SG_CHUNK):
        rows = slice(SG_CHUNK * c, SG_CHUNK * (c + 1))
        vc = v[rows, :]
        mixed = sgb_ref[...]
        for gi in range(SG_GROUPS):
            mixed = mixed + _dot(sgw_ref[gi], jnp.where(grp == gi, vc, 0.0).astype(BF16))
        sgo_ref[rows, :] = (u[rows, :] * mixed).astype(BF16)

    z = _dot(h, w_in_ref[:, C_GQA:C_GQA + 640])
    qk = z[:, 0:512]
    qk = qk * lax.rsqrt(_segsum(qk * qk, s64_ref) * inv_head + EPS) * gain(G_GQA, 512)
    lane2 = lax.broadcasted_iota(jnp.int32, (tm, 2 * LANES), 1)
    first16 = (lane2 & 31) < 16
    gc, gs = gc_ref[...], gs_ref[...]
    qn = _rope(qk[:, 0:256], gc, gs, 16, first16)
    kn = _rope(qk[:, 256:512], gc, gs, 16, first16)
    half = lane >> 6
    for g in range(GQA_KV_HEADS):
        sl = slice(LANES * g, LANES * (g + 1))
        for r in range(GQA_HEADS // GQA_KV_HEADS):
            gq_ref[2 * g + r] = jnp.where(half == r, qn[:, sl], 0.0).astype(BF16)
        gk_ref[g] = kn[:, sl].astype(BF16)
    vt = z[:, 512:640].T
    for g in range(GQA_KV_HEADS):
        gvt_ref[g, 0:HEAD_DIM, :] = vt[HEAD_DIM * g:HEAD_DIM * (g + 1), :].astype(BF16)
        gvt_ref[g, HEAD_DIM:V_AUG, :] = ones


def _in_call(x, mod, mod_row, lw, tabs, tm):
    b, t, d = x.shape
    nt = t // tm
    if mod_row is None:
        mod_map = lambda i, j: (j, 0, 0)
    else:
        mod_map = lambda i, j: (mod_row, 0, 0)
    tok = lambda i, j: (j, i, 0)
    tab = lambda i, j: (i, 0)
    head_tok = lambda i, j: (j, 0, i, 0)
    head_t = lambda i, j: (j, 0, 0, i)
    in_specs = [
        pl.BlockSpec((None, tm, d), tok),
        pl.BlockSpec((None, 6, d), mod_map),
        _layer_spec((d, IN_ARR), lw["layer"]),
        _layer_spec((MLA_Q_LORA, 512), lw["layer"]),
        _layer_spec((256, 768), lw["layer"]),
        _layer_spec((1, G_TOT), lw["layer"]),
        _const_spec((MXU_DIM, MXU_DIM)),
        _const_spec((MXU_DIM, MXU_DIM)),
        _const_spec((1, 512)),
        pl.BlockSpec((tm, 256), tab), pl.BlockSpec((tm, 256), tab),
        pl.BlockSpec((tm, LANES), tab), pl.BlockSpec((tm, LANES), tab),
        _layer_spec((SG_GROUPS, SG_CHUNK, SG_CHUNK), lw["layer"]),
        _layer_spec((SG_CHUNK, SG_WIDTH), lw["layer"]),
    ]
    out_specs = [
        pl.BlockSpec((None, tm, 256), tok), pl.BlockSpec((None, tm, 256), tok), pl.BlockSpec((None, tm, 256), tok),
        pl.BlockSpec((None, MLA_HEADS, tm, LANES), head_tok),
        pl.BlockSpec((None, MLA_HEADS, tm, LANES), head_tok),
        pl.BlockSpec((None, MLA_HEADS, V_AUG, tm), head_t),
        pl.BlockSpec((None, tm, 256), tok),
        pl.BlockSpec((None, GQA_HEADS, tm, LANES), head_tok),
        pl.BlockSpec((None, GQA_KV_HEADS, tm, LANES), head_tok),
        pl.BlockSpec((None, GQA_KV_HEADS, V_AUG, tm), head_t),
    ]
    sds = jax.ShapeDtypeStruct
    out_shape = [
        sds((b, t, 256), BF16), sds((b, t, 256), BF16), sds((b, t, 256), BF16),
        sds((b, MLA_HEADS, t, LANES), BF16), sds((b, MLA_HEADS, t, LANES), BF16),
        sds((b, MLA_HEADS, V_AUG, t), BF16),
        sds((b, t, 256), BF16),
        sds((b, GQA_HEADS, t, LANES), BF16), sds((b, GQA_KV_HEADS, t, LANES), BF16),
        sds((b, GQA_KV_HEADS, V_AUG, t), BF16),
    ]
    outs = pl.pallas_call(
        _in_kernel, grid=(nt, b), in_specs=in_specs, out_specs=out_specs, out_shape=out_shape,
        compiler_params=_params(48, 2), name="in_proj",
    )(x, mod, lw["w_in"], lw["w_uq"], lw["w_ukv"], lw["gains"], lw["s64"], lw["smq"], lw["invq"],
      tabs["gc"], tabs["gs"], tabs["mc"], tabs["ms"], lw["sgw"], lw["sgb"])
    names = ("na_q", "na_k", "na_v", "mla_q", "mla_k", "mla_vt", "sg_o", "gqa_q", "gqa_k", "gqa_vt")
    return dict(zip(names, outs))


def _na_heads(q, k_blocks, v_blocks, biases):
    tq = q.shape[0]
    head_of_lane = lax.broadcasted_iota(jnp.int32, (tq, 256), 1) >> 6
    qf = q.astype(F32) * (HEAD_DIM ** -0.5)
    out = jnp.zeros((tq, 256), F32)
    for hh in range(NA_HEADS):
        qh = jnp.where(head_of_lane == hh, qf, 0.0).astype(BF16)
        scores = []
        for kb, bias in zip(k_blocks, biases):
            s = _dot_nt(qh, kb)
            if bias is not None:
                s = s + bias[hh]
            scores.append(s)
        m = scores[0].max(axis=-1, keepdims=True)
        for s in scores[1:]:
            m = jnp.maximum(m, s.max(axis=-1, keepdims=True))
        l = jnp.zeros((tq, 1), F32)
        o = jnp.zeros((tq, 256), F32)
        for s, vb in zip(scores, v_blocks):
            p = jnp.exp(s - m)
            l = l + p.sum(axis=-1, keepdims=True)
            o = o + _dot(p.astype(BF16), vb)
        out = out + jnp.where(head_of_lane == hh, o * (1.0 / l), 0.0)
    return out


def _na_kernel(q_ref, kp_ref, kc_ref, kn_ref, vp_ref, vc_ref, vn_ref, kx_ref, vx_ref, bias_ref, o_ref):
    biases = [bias_ref.at[:, :, 0:256], bias_ref.at[:, :, 256:512], bias_ref.at[:, :, 512:768], None]
    out = _na_heads(q_ref[...], [kp_ref[...], kc_ref[...], kn_ref[...], kx_ref[...]],
                    [vp_ref[...], vc_ref[...], vn_ref[...], vx_ref[...]], biases)
    o_ref[...] = out.astype(BF16)


def _na_call(q, k, v, kx, vx, bias, layer):
    b, t, w = q.shape
    c = kx.shape[1]
    tq = 4 * GRID_W
    nt = t // tq
    cur = lambda bi, i: (bi, i, 0)
    prev = lambda bi, i: (bi, jnp.maximum(i - 1, 0), 0)
    nxt = lambda bi, i: (bi, jnp.minimum(i + 1, nt - 1), 0)
    ctx = lambda bi, i: (bi, 0, 0)
    variant = lambda bi, i: (layer, jnp.where(i == 0, 0, jnp.where(i == nt - 1, 2, 1)), 0, 0, 0)
    blk = lambda m: pl.BlockSpec((None, tq, w), m)
    return pl.pallas_call(
        _na_kernel, grid=(b, nt),
        in_specs=[blk(cur), blk(prev), blk(cur), blk(nxt), blk(prev), blk(cur), blk(nxt),
                  pl.BlockSpec((None, c, w), ctx), pl.BlockSpec((None, c, w), ctx),
                  pl.BlockSpec((None, None, NA_HEADS, tq, 3 * tq), variant)],
        out_specs=blk(cur),
        out_shape=jax.ShapeDtypeStruct((b, t, w), BF16),
        compiler_params=_params(48, 2), name="na_attn",
    )(q, k, k, k, v, v, v, kx, vx, bias)


def _na_ctx_kernel(q_ref, k_ref, v_ref, o_ref):
    o_ref[...] = _na_heads(q_ref[...], [k_ref[...]], [v_ref[...]], [None]).astype(BF16)


def _na_ctx_call(q, k, v):
    b, c, w = q.shape
    spec = pl.BlockSpec((None, c, w), lambda bi: (bi, 0, 0))
    return pl.pallas_call(
        _na_ctx_kernel, grid=(b,), in_specs=[spec, spec, spec], out_specs=spec,
        out_shape=jax.ShapeDtypeStruct((b, c, w), BF16),
        compiler_params=_params(32, 1), name="na_ctx_attn",
    )(q, k, v)


def _na_bias(rpb, nt):
    rows = 4 * nt
    col = np.arange(GRID_W)
    c_start = np.clip(col - NA_COLS // 2, 0, GRID_W - NA_COLS)
    valid_c = (col[None, :] >= c_start[:, None]) & (col[None, :] < c_start[:, None] + NA_COLS)
    dc = np.clip(col[None, :] - col[:, None] + (NA_COLS - 1), 0, 2 * NA_COLS - 2)
    pick_dr, ok = [], []
    for i in (0, min(1, nt - 1), nt - 1):
        rq = 4 * i + np.arange(4)
        start = np.clip(rq - NA_ROWS // 2, 0, rows - NA_ROWS)
        blocks = np.array([i - 1, i, i + 1])
        rk = (4 * blocks[:, None] + np.arange(4)[None, :]).reshape(-1)
        block_ok = np.repeat((blocks >= 0) & (blocks < nt), 4)
        valid_r = (rk[None, :] >= start[:, None]) & (rk[None, :] < start[:, None] + NA_ROWS) & block_ok[None, :]
        dr = np.clip(rk[None, :] - rq[:, None] + (NA_ROWS - 1), 0, 2 * NA_ROWS - 2)
        pick_dr.append(dr[:, :, None] == np.arange(2 * NA_ROWS - 1))
        ok.append(valid_r[:, None, :, None] & valid_c[None, :, None, :])
    pick_dc = jnp.asarray(dc[:, :, None] == np.arange(2 * NA_COLS - 1), F32)
    by_col = jnp.einsum("lhrd,qkd->lhrqk", rpb, pick_dc, precision=lax.Precision.HIGHEST)
    vals = jnp.einsum("vabr,lhrqk->lvhaqbk", jnp.asarray(np.stack(pick_dr), F32), by_col,
                      precision=lax.Precision.HIGHEST)
    mask = jnp.asarray(np.stack(ok))[None, :, None]
    return jnp.where(mask, vals, NEG_INF).reshape(rpb.shape[0], 3, NA_HEADS, 4 * GRID_W, 12 * GRID_W)


FLASH_COL = MXU_DIM
STREAM_UNROLL = 16
SCORE_LOOKAHEAD = 2
SEED_KEYS = 64
F32_SAFE_MAX = 3e38
MIN_DENOMINATOR = 2.0 ** -40


def _col_max(s):
    keys = s.shape[0]
    if keys % 64 == 0 and keys > 64:
        s = jnp.max(s.reshape(keys // 64, 64, s.shape[1]), axis=0)
    return jnp.max(s, axis=0, keepdims=True)


def _flash_kernel(*refs, n_heads, group, tq, tk, has_ctx):
    if has_ctx:
        q_ref, k_ref, vt_ref, kx_ref, vtx_ref, o_ref, acc_scr, ot_scr = refs
    else:
        q_ref, k_ref, vt_ref, o_ref, acc_scr, ot_scr = refs
    n_sub = q_ref.shape[1] // tq
    n_chunks = k_ref.shape[1] // tk
    n_slabs = n_heads // group
    cols = [(hh, qi) for hh in range(n_heads) for qi in range(n_sub)]
    heads = range(len(cols))
    slab_of = [hh // group for hh, _ in cols]

    def q_col(c):
        hh, qi = cols[c]
        return q_ref[hh, qi * tq:(qi + 1) * tq, :]

    def load_chunk(j):
        off = pl.multiple_of(j * tk, tk)
        return ([k_ref[sl, pl.ds(off, tk), :] for sl in range(n_slabs)],
                [vt_ref[sl, :, pl.ds(off, tk)] for sl in range(n_slabs)])

    def pv(vt, p):
        return _dot(vt, p.astype(BF16))

    def exact_step(ks, vts, ms):
        new_ms = []
        for hh in heads:
            s = _dot_nt(ks[slab_of[hh]], q_col(hh))
            m_new = jnp.maximum(ms[hh], _col_max(s))
            p = jnp.exp2(s - m_new)
            acc_scr[hh] = acc_scr[hh] * jnp.exp2(ms[hh] - m_new) + pv(vts[slab_of[hh]], p)
            new_ms.append(m_new)
        return tuple(new_ms)

    def stream(chunks, ms):
        units = [(i, hh) for i in range(len(chunks)) for hh in heads]
        score = lambda u: _dot_nt(chunks[u[0]][0][slab_of[u[1]]], q_col(u[1]))
        pending = [score(u) for u in units[:SCORE_LOOKAHEAD]]
        for n, (i, hh) in enumerate(units):
            s = pending.pop(0)
            if n + SCORE_LOOKAHEAD < len(units):
                pending.append(score(units[n + SCORE_LOOKAHEAD]))
            acc_scr[hh] += pv(chunks[i][1][slab_of[hh]], jnp.exp2(s - ms[hh]))

    ctx_chunk = ([kx_ref[sl] for sl in range(n_slabs)], [vtx_ref[sl] for sl in range(n_slabs)]) if has_ctx else None
    seed_ref = kx_ref if has_ctx else k_ref
    ms = tuple(_col_max(_dot_nt(seed_ref[slab_of[hh], 0:SEED_KEYS, :], q_col(hh))) for hh in heads)
    acc_scr[...] = jnp.zeros_like(acc_scr)
    if has_ctx:
        stream([ctx_chunk], ms)
    per_iter = max(u for u in (STREAM_UNROLL, 8, 4, 2, 1) if n_chunks % u == 0)

    def stream_body(j, carry):
        stream([load_chunk(j * per_iter + i) for i in range(per_iter)], ms)
        return carry

    lax.fori_loop(0, n_chunks // per_iter, stream_body, 0)
    unusable = jnp.zeros((V_AUG, tq), F32)
    for hh in heads:
        acc = acc_scr[hh]
        unusable = jnp.maximum(unusable, jnp.where(jnp.abs(acc) < F32_SAFE_MAX, 0.0, 1.0))
        unusable = jnp.maximum(unusable, jnp.where(acc[HEAD_DIM:HEAD_DIM + 1] > MIN_DENOMINATOR, 0.0, 1.0))

    @pl.when(jnp.max(unusable) > 0.0)
    def _():
        acc_scr[...] = jnp.zeros_like(acc_scr)
        ms = tuple(jnp.full((1, tq), NEG_INF, F32) for _ in heads)
        if has_ctx:
            ms = exact_step(*ctx_chunk, ms)
        lax.fori_loop(0, n_chunks, lambda j, m: exact_step(*load_chunk(j), m), ms)

    for c, (hh, qi) in enumerate(cols):
        acc = acc_scr[c]
        ot_scr[HEAD_DIM * hh:HEAD_DIM * (hh + 1), qi * tq:(qi + 1) * tq] = (
            acc[0:HEAD_DIM] * (1.0 / acc[HEAD_DIM:HEAD_DIM + 1]))
    o_ref[...] = ot_scr[...].T.astype(BF16)


def _flash_call(q, k, vt, kx, vtx, tq, tk):
    b, n_heads, t_q, w = q.shape
    slabs, t_k = k.shape[1], k.shape[2]
    has_ctx = kx is not None
    full4 = lambda bi, i: (bi, 0, 0, 0)
    in_specs = [pl.BlockSpec((None, n_heads, tq, w), lambda bi, i: (bi, 0, i, 0)),
                pl.BlockSpec((None, slabs, t_k, w), full4),
                pl.BlockSpec((None, slabs, V_AUG, t_k), full4)]
    args = [q, k, vt]
    if has_ctx:
        c = kx.shape[2]
        in_specs += [pl.BlockSpec((None, slabs, c, w), full4), pl.BlockSpec((None, slabs, V_AUG, c), full4)]
        args += [kx, vtx]
    col = min(FLASH_COL, tq)
    kern = functools.partial(_flash_kernel, n_heads=n_heads, group=n_heads // slabs, tq=col, tk=tk,
                             has_ctx=has_ctx)
    return pl.pallas_call(
        kern, grid=(b, t_q // tq), in_specs=in_specs,
        out_specs=pl.BlockSpec((None, tq, n_heads * HEAD_DIM), lambda bi, i: (bi, i, 0)),
        out_shape=jax.ShapeDtypeStruct((b, t_q, n_heads * HEAD_DIM), BF16),
        scratch_shapes=[pltpu.VMEM((n_heads * (tq // col), V_AUG, col), F32),
                        pltpu.VMEM((n_heads * HEAD_DIM, tq), F32)],
        compiler_params=_params(56, 2), name="flash",
    )(*args)


def _merge_kernel(x_ref, mod_ref, o0_ref, o1_ref, o2_ref, o3_ref, wg_ref, bg_ref, wb_ref, wo_ref, out_ref):
    x = x_ref[...]
    h = (_rms(x) * (1.0 + mod_ref[1:2, :]) + mod_ref[0:1, :]).astype(BF16)
    y = None
    for i, o_ref in enumerate((o0_ref, o1_ref, o2_ref, o3_ref)):
        gate = jax.nn.sigmoid(_dot(h, wg_ref[i]) + bg_ref[i])
        term = gate * _dot(o_ref[...], wb_ref[i])
        y = term if y is None else y + term
    out_ref[...] = x + mod_ref[2:3, :] * _dot(y.astype(BF16), wo_ref[...])


def _merge_call(x, mod, mod_row, branches, lw, tm):
    b, t, d = x.shape
    if mod_row is None:
        mod_map = lambda i, j: (j, 0, 0)
    else:
        mod_map = lambda i, j: (mod_row, 0, 0)
    tok = lambda i, j: (j, i, 0)
    br = pl.BlockSpec((None, tm, BRANCH_WIDTH), tok)
    return pl.pallas_call(
        _merge_kernel, grid=(t // tm, b),
        in_specs=[pl.BlockSpec((None, tm, d), tok), pl.BlockSpec((None, 6, d), mod_map), br, br, br, br,
                  _layer_spec((N_BRANCH, d, d), lw["layer"]), _layer_spec((N_BRANCH, 1, d), lw["layer"]),
                  _layer_spec((N_BRANCH, BRANCH_WIDTH, d), lw["layer"]), _layer_spec((d, d), lw["layer"])],
        out_specs=pl.BlockSpec((None, tm, d), tok),
        out_shape=jax.ShapeDtypeStruct((b, t, d), F32),
        compiler_params=_params(56, 2), name="merge",
    )(x, mod, *branches, lw["w_gate"], lw["b_gate"], lw["w_branch"], lw["w_out"])


def _ffn_kernel(x_ref, xp_ref, xn_ref, mod_ref, wu_ref, cw_ref, cb_ref, wd_ref, out_ref, h_scr, act_scr):
    tm = x_ref.shape[0]
    i = pl.program_id(0)
    nt = pl.num_programs(0)
    sh, sc = mod_ref[3:4, :], mod_ref[4:5, :]

    def modulated(v):
        return _rms(v) * (1.0 + sc) + sh

    x = x_ref[...]
    h_scr[0:HALO, :] = jnp.where(i > 0, modulated(xp_ref[...]), 0.0).astype(BF16)
    h_scr[HALO:HALO + tm, :] = modulated(x).astype(BF16)
    h_scr[HALO + tm:, :] = jnp.where(i < nt - 1, modulated(xn_ref[...]), 0.0).astype(BF16)
    rows = tm + 2 * HALO
    main = slice(HALO, HALO + tm)

    def cols(j, gate):
        lo = gate * FFN_DIM + FFN_CHUNK * j
        return slice(lo, lo + FFN_CHUNK)

    def up_proj(j):
        return tuple(_dot(h_scr[...], wu_ref[:, cols(j, gate)]) for gate in (0, 1))

    def conv(up, sl):
        return (pltpu.roll(up, 1, 0)[main] * cw_ref[0:1, sl] + up[main] * cw_ref[1:2, sl]
                + pltpu.roll(up, rows - 1, 0)[main] * cw_ref[2:3, sl] + cb_ref[:, sl])

    ups = [up_proj(j) for j in range(FFN_LOOKAHEAD)]
    for j in range(N_FFN_CHUNKS):
        up_a, up_g = ups.pop(0)
        if j + FFN_LOOKAHEAD < N_FFN_CHUNKS:
            ups.append(up_proj(j + FFN_LOOKAHEAD))
        a, g = conv(up_a, cols(j, 0)), conv(up_g, cols(j, 1))
        act_scr[:, cols(j, 0)] = (g * jax.nn.sigmoid(g) * a).astype(BF16)
    out_ref[...] = x + mod_ref[5:6, :] * _dot(act_scr[...], wd_ref[...])


def _ffn_call(x, mod, mod_row, lw, tm):
    b, t, d = x.shape
    nt = t // tm
    hb = tm // HALO
    last_halo = t // HALO - 1
    if mod_row is None:
        mod_map = lambda i, j: (j, 0, 0)
    else:
        mod_map = lambda i, j: (mod_row, 0, 0)
    tok = lambda i, j: (j, i, 0)
    return pl.pallas_call(
        _ffn_kernel, grid=(nt, b),
        in_specs=[pl.BlockSpec((None, tm, d), tok),
                  pl.BlockSpec((None, HALO, d), lambda i, j: (j, jnp.maximum(i * hb - 1, 0), 0)),
                  pl.BlockSpec((None, HALO, d), lambda i, j: (j, jnp.minimum((i + 1) * hb, last_halo), 0)),
                  pl.BlockSpec((None, 6, d), mod_map),
                  _layer_spec((d, 2 * FFN_DIM), lw["layer"]),
                  _layer_spec((3, 2 * FFN_DIM), lw["layer"]),
                  _layer_spec((1, 2 * FFN_DIM), lw["layer"]),
                  _layer_spec((FFN_DIM, d), lw["layer"])],
        out_specs=pl.BlockSpec((None, tm, d), tok),
        out_shape=jax.ShapeDtypeStruct((b, t, d), F32),
        scratch_shapes=[pltpu.VMEM((tm + 2 * HALO, d), BF16), pltpu.VMEM((tm, FFN_DIM), BF16)],
        compiler_params=_params(56, 2), name="ffn",
    )(x, x, x, mod, lw["w_up"], lw["conv_w"], lw["conv_b"], lw["w_down"])


def _block_diag_ones(width, segs):
    m = np.zeros((width, width), np.float32)
    for lo, hi in segs:
        m[lo:hi, lo:hi] = 1.0
    return jnp.asarray(m, BF16)


def _rope_pattern(pos, width):
    half = width // 2
    inv = ROPE_THETA ** (-jnp.arange(half, dtype=F32) / half)
    ang = pos.astype(F32)[:, None] * inv[None, :]
    cos, sin = jnp.cos(ang), jnp.sin(ang)
    return jnp.concatenate([cos, cos], axis=1), jnp.concatenate([-sin, sin], axis=1)


def _rope_tables(t, identity):
    if identity:
        return {"gc": jnp.ones((t, 256), F32), "gs": jnp.zeros((t, 256), F32),
                "mc": jnp.ones((t, LANES), F32), "ms": jnp.zeros((t, LANES), F32)}
    n_rows = t // GRID_W

    def grid_tables(width):
        rc, rs = _rope_pattern(jnp.arange(n_rows), width)
        cc, cs = _rope_pattern(jnp.arange(GRID_W), width)
        expand = lambda r, c: jnp.concatenate([jnp.repeat(r, GRID_W, axis=0), jnp.tile(c, (n_rows, 1))], axis=1)
        return expand(rc, cc), expand(rs, cs)

    gcos, gsin = grid_tables(HEAD_DIM // 2)
    mcos, msin = grid_tables(MLA_ROPE // 2)
    one, zero = jnp.ones((t, MLA_NOPE), F32), jnp.zeros((t, MLA_NOPE), F32)
    pad1, pad0 = jnp.ones((t, LANES - MLA_NOPE - MLA_ROPE), F32), jnp.zeros((t, LANES - MLA_NOPE - MLA_ROPE), F32)
    return {"gc": jnp.tile(gcos, (1, 4)), "gs": jnp.tile(gsin, (1, 4)),
            "mc": jnp.concatenate([one, mcos, pad1], axis=1),
            "ms": jnp.concatenate([zero, msin, pad0], axis=1)}


def _arrange(p):
    d = D_MODEL
    n_layers = p["w_in"].shape[0]
    cat = lambda parts: jnp.concatenate(parts, axis=-1)
    w_in = p["w_in"]
    na, mla = w_in[..., :NA_IN], w_in[..., NA_IN:NA_IN + MLA_IN]
    sg = w_in[..., NA_IN + MLA_IN:NA_IN + MLA_IN + SG_IN]
    gqa = w_in[..., NA_IN + MLA_IN + SG_IN:]
    z = lambda n: jnp.zeros((n_layers, d, n), F32)
    cq, ckv, kr = (mla[..., :MLA_Q_LORA], mla[..., MLA_Q_LORA:MLA_Q_LORA + MLA_KV_LORA],
                   mla[..., MLA_Q_LORA + MLA_KV_LORA:])
    gq, gk, gv = gqa[..., :256], gqa[..., 256:384], gqa[..., 384:]
    gk_dup = cat([gk[..., :64], gk[..., :64], gk[..., 64:], gk[..., 64:]])
    w_in_arr = cat([na, cq, ckv, z(256 - MLA_KV_LORA), z(MLA_NOPE), kr, z(LANES - MLA_NOPE - MLA_ROPE),
                    sg, gq, gk_dup, gv]).astype(BF16)

    lane_pad = lambda v, n: jnp.pad(v, ((0, 0),) * (v.ndim - 1) + ((0, n),))
    w_uq = p["mla_w_uq"].reshape(n_layers, MLA_Q_LORA, MLA_HEADS, MLA_NOPE + MLA_ROPE)
    w_uq = lane_pad(w_uq, LANES - MLA_NOPE - MLA_ROPE).reshape(n_layers, MLA_Q_LORA, MLA_HEADS * LANES)
    w_ukv = p["mla_w_ukv"].reshape(n_layers, MLA_KV_LORA, MLA_HEADS, MLA_NOPE + MLA_V)
    k_part = lane_pad(w_ukv[..., :MLA_NOPE], LANES - MLA_NOPE).reshape(n_layers, MLA_KV_LORA, MLA_HEADS * LANES)
    v_part = w_ukv[..., MLA_NOPE:].reshape(n_layers, MLA_KV_LORA, MLA_HEADS * MLA_V)
    w_ukv = jnp.pad(cat([k_part, v_part]), ((0, 0), (0, 256 - MLA_KV_LORA), (0, 0)))

    qg, kg = p["mla_q_norm"], p["mla_k_norm"]
    z1 = lambda n: jnp.zeros((n_layers, n), F32)
    tile4 = lambda v: jnp.tile(v, (1, 4))
    gains = cat([
        tile4(p["na_q_norm"]), tile4(p["na_k_norm"]),
        p["mla_cq_norm"],
        p["mla_ckv_norm"], z1(256 - MLA_KV_LORA),
        tile4(cat([qg * MLA_LOGIT_SCALE, z1(LANES - MLA_NOPE - MLA_ROPE)])),
        tile4(cat([kg[:, :MLA_NOPE], z1(LANES - MLA_NOPE)])),
        z1(MLA_NOPE), kg[:, MLA_NOPE:], z1(LANES - MLA_NOPE - MLA_ROPE),
        p["sg_v_norm"],
        tile4(p["gqa_q_norm"] * GQA_LOGIT_SCALE), tile4(p["gqa_k_norm"]),
    ]).reshape(n_layers, 1, G_TOT)

    invq = np.tile(np.concatenate([np.full(MLA_NOPE, 1.0 / MLA_NOPE), np.full(MLA_ROPE, 1.0 / MLA_ROPE),
                                   np.ones(LANES - MLA_NOPE - MLA_ROPE)]), 4).astype(np.float32).reshape(1, 512)
    sgb = jnp.repeat(jnp.swapaxes(p["sg_b_s"], -1, -2), SG_WIDTH // SG_GROUPS, axis=-1)

    return {
        "w_in": w_in_arr, "w_uq": w_uq.astype(BF16), "w_ukv": w_ukv.astype(BF16), "gains": gains,
        "s64": _block_diag_ones(MXU_DIM, [(64 * i, 64 * i + 64) for i in range(4)]),
        "smq": _block_diag_ones(MXU_DIM, [(0, 64), (64, 96), (128, 192), (192, 224)]),
        "invq": jnp.asarray(invq),
        "sgw": p["sg_w_s"].astype(BF16), "sgb": sgb,
        "w_gate": p["w_gate"].astype(BF16), "b_gate": p["b_gate"].reshape(n_layers, N_BRANCH, 1, d),
        "w_branch": p["w_branch"].astype(BF16), "w_out": p["w_out"].astype(BF16),
        "w_up": p["w_up"].astype(BF16), "conv_w": p["conv_w"], "conv_b": p["conv_b"].reshape(n_layers, 1, -1),
        "w_down": p["w_down"].astype(BF16),
    }


def kernel(x, c, ctx, c_ctx, w_ada, b_ada, w_in, na_q_norm, na_k_norm, na_rpb, mla_cq_norm, mla_ckv_norm,
           mla_w_uq, mla_w_ukv, mla_q_norm, mla_k_norm, sg_v_norm, sg_w_s, sg_b_s, gqa_q_norm, gqa_k_norm,
           w_branch, w_gate, b_gate, w_out, w_up, conv_w, conv_b, w_down):
    b, t, d = x.shape
    n_ctx = ctx.shape[1]
    depth = w_in.shape[0]
    ctx_row = b
    cvec = jnp.zeros((8, d), F32).at[:b].set(c).at[ctx_row].set(c_ctx)
    mod_all = _ada_call(cvec, w_ada, b_ada).reshape(depth, 8, 6, d)

    tm = min(512, t)
    tq = min(512, t)
    tk = min(512, t)
    tabs_lat = _rope_tables(t, identity=False)
    tabs_ctx = _rope_tables(n_ctx, identity=True)
    na_bias = _na_bias(na_rpb, t // (4 * GRID_W))

    lw = _arrange({
        "w_in": w_in, "na_q_norm": na_q_norm, "na_k_norm": na_k_norm,
        "mla_cq_norm": mla_cq_norm, "mla_ckv_norm": mla_ckv_norm, "mla_w_uq": mla_w_uq,
        "mla_w_ukv": mla_w_ukv, "mla_q_norm": mla_q_norm, "mla_k_norm": mla_k_norm,
        "sg_v_norm": sg_v_norm, "sg_w_s": sg_w_s, "sg_b_s": sg_b_s,
        "gqa_q_norm": gqa_q_norm, "gqa_k_norm": gqa_k_norm,
        "w_branch": w_branch, "w_gate": w_gate, "b_gate": b_gate, "w_out": w_out,
        "w_up": w_up, "conv_w": conv_w, "conv_b": conv_b, "w_down": w_down})
    cx = ctx
    for l in range(depth):
        lw = dict(lw, layer=l)
        mod = mod_all[l]
        pc = _in_call(cx, mod, ctx_row, lw, tabs_ctx, n_ctx)
        pz = _in_call(x, mod, None, lw, tabs_lat, tm)

        o_na = _na_call(pz["na_q"], pz["na_k"], pz["na_v"], pc["na_k"], pc["na_v"], na_bias, l)
        o_mla = _flash_call(pz["mla_q"], pz["mla_k"], pz["mla_vt"], pc["mla_k"], pc["mla_vt"], tq, tk)
        o_gqa = _flash_call(pz["gqa_q"], pz["gqa_k"], pz["gqa_vt"], pc["gqa_k"], pc["gqa_vt"], tq, tk)
        x = _merge_call(x, mod, None, (o_na, o_mla, pz["sg_o"], o_gqa), lw, tm)
        x = _ffn_call(x, mod, None, lw, tm)

        if l < depth - 1:
            oc_na = _na_ctx_call(pc["na_q"], pc["na_k"], pc["na_v"])
            oc_mla = _flash_call(pc["mla_q"], pc["mla_k"], pc["mla_vt"], None, None, n_ctx, n_ctx)
            oc_gqa = _flash_call(pc["gqa_q"], pc["gqa_k"], pc["gqa_vt"], None, None, n_ctx, n_ctx)
            cx = _merge_call(cx, mod, ctx_row, (oc_na, oc_mla, pc["sg_o"], oc_gqa), lw, n_ctx)
            cx = _ffn_call(cx, mod, ctx_row, lw, n_ctx)
    return x
```

```python
import functools
import math

import numpy as np
import jax
import jax.numpy as jnp
from jax import lax
from jax.experimental import pallas as pl
from jax.experimental.pallas import tpu as pltpu

F32 = jnp.float32
BF16 = jnp.bfloat16

D_MODEL = 1024
GRID_W = 64
HEAD_DIM = 64
NA_HEADS = 4
NA_ROWS = 8
NA_COLS = 16
MLA_HEADS = 4
MLA_Q_LORA = 256
MLA_KV_LORA = 192
MLA_NOPE = 64
MLA_ROPE = 32
MLA_V = 64
SG_GROUPS = 4
SG_CHUNK = 128
SG_WIDTH = 256
GQA_HEADS = 4
GQA_KV_HEADS = 2
N_BRANCH = 4
BRANCH_WIDTH = 256
FFN_DIM = 2816
ROPE_THETA = 10000.0
EPS = 1e-6
NEG_INF = -1e30

NA_IN = 3 * NA_HEADS * HEAD_DIM
MLA_IN = MLA_Q_LORA + MLA_KV_LORA + MLA_ROPE
SG_IN = 2 * SG_WIDTH
GQA_IN = (GQA_HEADS + 2 * GQA_KV_HEADS) * HEAD_DIM

LANES = 128
MXU_DIM = 256
BF16_SUBLANES = 16
V_AUG = HEAD_DIM + BF16_SUBLANES
LOG2E = math.log2(math.e)
MLA_LOGIT_SCALE = (MLA_NOPE + MLA_ROPE) ** -0.5 * LOG2E
GQA_LOGIT_SCALE = HEAD_DIM ** -0.5 * LOG2E

C_NA = 0
C_MLA = C_NA + NA_IN
C_SG = C_MLA + 640
C_GQA = C_SG + SG_IN
IN_ARR = C_GQA + 640

G_NA, G_CQ, G_CKV, G_MQ, G_MKN, G_MKR, G_SG, G_GQA, G_TOT = 0, 512, 768, 1024, 1536, 2048, 2176, 2432, 2944

FFN_CHUNK = 256
N_FFN_CHUNKS = FFN_DIM // FFN_CHUNK
FFN_LOOKAHEAD = 2
HALO = BF16_SUBLANES


def _const_spec(shape):
    nd = len(shape)
    return pl.BlockSpec(shape, lambda *_: (0,) * nd, pipeline_mode=pl.Buffered(1))


def _layer_spec(shape, layer):
    nd = len(shape)
    return pl.BlockSpec((None,) + tuple(shape), lambda *_: (layer,) + (0,) * nd, pipeline_mode=pl.Buffered(1))


def _params(vmem_mb, n_grid):
    return pltpu.CompilerParams(dimension_semantics=("arbitrary",) * n_grid,
                                vmem_limit_bytes=vmem_mb * 1024 * 1024)


def _dot(a, b):
    return jnp.dot(a, b, preferred_element_type=F32)


def _dot_nt(a, b):
    return lax.dot_general(a, b, (((1,), (1,)), ((), ())), preferred_element_type=F32)


def _rms(xf):
    return xf * lax.rsqrt(jnp.mean(xf * xf, axis=-1, keepdims=True) + EPS)


def _segsum(x2, s_ref):
    hi = x2.astype(BF16)
    lo = (x2 - hi.astype(F32)).astype(BF16)
    s = s_ref[...]
    parts = []
    for j in range(x2.shape[1] // MXU_DIM):
        sl = slice(MXU_DIM * j, MXU_DIM * (j + 1))
        parts.append(_dot(hi[:, sl], s) + _dot(lo[:, sl], s))
    return parts[0] if len(parts) == 1 else jnp.concatenate(parts, axis=1)


def _rope(x, cos, sin_signed, shift, first_half):
    w = x.shape[1]
    partner = jnp.where(first_half, pltpu.roll(x, w - shift, 1), pltpu.roll(x, shift, 1))
    return x * cos + partner * sin_signed


def _ada_kernel(c_ref, w_ref, b_ref, o_ref):
    c = c_ref[...]
    a = (c * jax.nn.sigmoid(c)).astype(BF16)
    o_ref[...] = _dot(a, w_ref[...].astype(BF16)) + b_ref[...]


def _ada_call(cvec, w_ada, b_ada):
    n_layers, d, n = w_ada.shape
    tn = 1536
    return pl.pallas_call(
        _ada_kernel,
        grid=(n_layers, n // tn),
        in_specs=[pl.BlockSpec((8, d), lambda l, j: (0, 0)),
                  pl.BlockSpec((None, d, tn), lambda l, j: (l, 0, j)),
                  pl.BlockSpec((None, 1, tn), lambda l, j: (l, 0, j))],
        out_specs=pl.BlockSpec((None, 8, tn), lambda l, j: (l, 0, j)),
        out_shape=jax.ShapeDtypeStruct((n_layers, 8, n), F32),
        compiler_params=_params(32, 2),
        name="ada",
    )(cvec, w_ada, b_ada.reshape(n_layers, 1, n))


def _in_kernel(x_ref, mod_ref, w_in_ref, w_uq_ref, w_ukv_ref, g_ref, s64_ref, smq_ref, invq_ref,
               gc_ref, gs_ref, mc_ref, ms_ref, sgw_ref, sgb_ref,
               naq_ref, nak_ref, nav_ref, mq_ref, mk_ref, mvt_ref, sgo_ref, gq_ref, gk_ref, gvt_ref):
    tm = x_ref.shape[0]
    x = x_ref[...]
    h = (_rms(x) * (1.0 + mod_ref[1:2, :]) + mod_ref[0:1, :]).astype(BF16)

    def gain(off, width):
        return g_ref[:, off:off + width]

    inv_head = 1.0 / HEAD_DIM

    z = _dot(h, w_in_ref[:, C_NA:C_NA + NA_IN])
    qk = z[:, 0:512]
    qk = qk * lax.rsqrt(_segsum(qk * qk, s64_ref) * inv_head + EPS) * gain(G_NA, 512)
    naq_ref[...] = qk[:, 0:256].astype(BF16)
    nak_ref[...] = qk[:, 256:512].astype(BF16)
    nav_ref[...] = z[:, 512:768].astype(BF16)

    z = _dot(h, w_in_ref[:, C_MLA:C_MLA + 640])
    cq, ckv, kr = z[:, 0:256], z[:, 256:512], z[:, 512:640]
    cq = cq * lax.rsqrt(jnp.mean(cq * cq, axis=-1, keepdims=True) + EPS) * gain(G_CQ, 256)
    ckv = ckv * lax.rsqrt(jnp.sum(ckv * ckv, axis=-1, keepdims=True) * (1.0 / MLA_KV_LORA) + EPS) * gain(G_CKV, 256)
    q = _dot(cq.astype(BF16), w_uq_ref[...])
    q = q * lax.rsqrt(_segsum(q * q, smq_ref) * invq_ref[...] + EPS) * gain(G_MQ, 512)
    kv = _dot(ckv.astype(BF16), w_ukv_ref[...])
    kn = kv[:, 0:512]
    kn = kn * lax.rsqrt(_segsum(kn * kn, s64_ref) * inv_head + EPS) * gain(G_MKN, 512)
    kr = kr * lax.rsqrt(jnp.sum(kr * kr, axis=-1, keepdims=True) * (1.0 / MLA_ROPE) + EPS) * gain(G_MKR, 128)
    lane = lax.broadcasted_iota(jnp.int32, (tm, LANES), 1)
    first8 = (lane & 15) < 8
    mc, ms = mc_ref[...], ms_ref[...]
    kr = _rope(kr, mc, ms, 8, first8)
    for hh in range(MLA_HEADS):
        sl = slice(LANES * hh, LANES * (hh + 1))
        mq_ref[hh] = _rope(q[:, sl], mc, ms, 8, first8).astype(BF16)
        mk_ref[hh] = (kn[:, sl] + kr).astype(BF16)
    vt = kv[:, 512:768].T
    ones = jnp.ones((BF16_SUBLANES, tm), BF16)
    for hh in range(MLA_HEADS):
        mvt_ref[hh, 0:HEAD_DIM, :] = vt[HEAD_DIM * hh:HEAD_DIM * (hh + 1), :].astype(BF16)
        mvt_ref[hh, HEAD_DIM:V_AUG, :] = ones

    z = _dot(h, w_in_ref[:, C_SG:C_SG + SG_IN])
    uv = jax.nn.gelu(z)
    u, v = uv[:, 0:SG_WIDTH], uv[:, SG_WIDTH:]
    v = v * lax.rsqrt(jnp.mean(v * v, axis=-1, keepdims=True) + EPS) * gain(G_SG, 256)
    grp = lax.broadcasted_iota(jnp.int32, (SG_CHUNK, SG_WIDTH), 1) >> 6
    for c in range(tm // SG_CHUNK):
        rows = slice(SG_CHUNK * c, SG_CHUNK * (c + 1))
        vc = v[rows, :]
        mixed = sgb_ref[...]
        for gi in range(SG_GROUPS):
            mixed = mixed + _dot(sgw_ref[gi], jnp.where(grp == gi, vc, 0.0).astype(BF16))
        sgo_ref[rows, :] = (u[rows, :] * mixed).astype(BF16)

    z = _dot(h, w_in_ref[:, C_GQA:C_GQA + 640])
    qk = z[:, 0:512]
    qk = qk * lax.rsqrt(_segsum(qk * qk, s64_ref) * inv_head + EPS) * gain(G_GQA, 512)
    lane2 = lax.broadcasted_iota(jnp.int32, (tm, 2 * LANES), 1)
    first16 = (lane2 & 31) < 16
    gc, gs = gc_ref[...], gs_ref[...]
    qn = _rope(qk[:, 0:256], gc, gs, 16, first16)
    kn = _rope(qk[:, 256:512], gc, gs, 16, first16)
    half = lane >> 6
    for g in range(GQA_KV_HEADS):
        sl = slice(LANES * g, LANES * (g + 1))
        for r in range(GQA_HEADS // GQA_KV_HEADS):
            gq_ref[2 * g + r] = jnp.where(half == r, qn[:, sl], 0.0).astype(BF16)
        gk_ref[g] = kn[:, sl].astype(BF16)
    vt = z[:, 512:640].T
    for g in range(GQA_KV_HEADS):
        gvt_ref[g, 0:HEAD_DIM, :] = vt[HEAD_DIM * g:HEAD_DIM * (g + 1), :].astype(BF16)
        gvt_ref[g, HEAD_DIM:V_AUG, :] = ones


def _in_call(x, mod, mod_row, lw, tabs, tm):
    b, t, d = x.shape
    nt = t // tm
    if mod_row is None:
        mod_map = lambda i, j: (j, 0, 0)
    else:
        mod_map = lambda i, j: (mod_row, 0, 0)
    tok = lambda i, j: (j, i, 0)
    tab = lambda i, j: (i, 0)
    head_tok = lambda i, j: (j, 0, i, 0)
    head_t = lambda i, j: (j, 0, 0, i)
    in_specs = [
        pl.BlockSpec((None, tm, d), tok),
        pl.BlockSpec((None, 6, d), mod_map),
        _layer_spec((d, IN_ARR), lw["layer"]),
        _layer_spec((MLA_Q_LORA, 512), lw["layer"]),
        _layer_spec((256, 768), lw["layer"]),
        _layer_spec((1, G_TOT), lw["layer"]),
        _const_spec((MXU_DIM, MXU_DIM)),
        _const_spec((MXU_DIM, MXU_DIM)),
        _const_spec((1, 512)),
        pl.BlockSpec((tm, 256), tab), pl.BlockSpec((tm, 256), tab),
        pl.BlockSpec((tm, LANES), tab), pl.BlockSpec((tm, LANES), tab),
        _layer_spec((SG_GROUPS, SG_CHUNK, SG_CHUNK), lw["layer"]),
        _layer_spec((SG_CHUNK, SG_WIDTH), lw["layer"]),
    ]
    out_specs = [
        pl.BlockSpec((None, tm, 256), tok), pl.BlockSpec((None, tm, 256), tok), pl.BlockSpec((None, tm, 256), tok),
        pl.BlockSpec((None, MLA_HEADS, tm, LANES), head_tok),
        pl.BlockSpec((None, MLA_HEADS, tm, LANES), head_tok),
        pl.BlockSpec((None, MLA_HEADS, V_AUG, tm), head_t),
        pl.BlockSpec((None, tm, 256), tok),
        pl.BlockSpec((None, GQA_HEADS, tm, LANES), head_tok),
        pl.BlockSpec((None, GQA_KV_HEADS, tm, LANES), head_tok),
        pl.BlockSpec((None, GQA_KV_HEADS, V_AUG, tm), head_t),
    ]
    sds = jax.ShapeDtypeStruct
    out_shape = [
        sds((b, t, 256), BF16), sds((b, t, 256), BF16), sds((b, t, 256), BF16),
        sds((b, MLA_HEADS, t, LANES), BF16), sds((b, MLA_HEADS, t, LANES), BF16),
        sds((b, MLA_HEADS, V_AUG, t), BF16),
        sds((b, t, 256), BF16),
        sds((b, GQA_HEADS, t, LANES), BF16), sds((b, GQA_KV_HEADS, t, LANES), BF16),
        sds((b, GQA_KV_HEADS, V_AUG, t), BF16),
    ]
    outs = pl.pallas_call(
        _in_kernel, grid=(nt, b), in_specs=in_specs, out_specs=out_specs, out_shape=out_shape,
        compiler_params=_params(48, 2), name="in_proj",
    )(x, mod, lw["w_in"], lw["w_uq"], lw["w_ukv"], lw["gains"], lw["s64"], lw["smq"], lw["invq"],
      tabs["gc"], tabs["gs"], tabs["mc"], tabs["ms"], lw["sgw"], lw["sgb"])
    names = ("na_q", "na_k", "na_v", "mla_q", "mla_k", "mla_vt", "sg_o", "gqa_q", "gqa_k", "gqa_vt")
    return dict(zip(names, outs))


def _na_heads(q, k_blocks, v_blocks, biases):
    tq = q.shape[0]
    head_of_lane = lax.broadcasted_iota(jnp.int32, (tq, 256), 1) >> 6
    qf = q.astype(F32) * (HEAD_DIM ** -0.5)
    out = jnp.zeros((tq, 256), F32)
    for hh in range(NA_HEADS):
        qh = jnp.where(head_of_lane == hh, qf, 0.0).astype(BF16)
        scores = []
        for kb, bias in zip(k_blocks, biases):
            s = _dot_nt(qh, kb)
            if bias is not None:
                s = s + bias[hh]
            scores.append(s)
        m = scores[0].max(axis=-1, keepdims=True)
        for s in scores[1:]:
            m = jnp.maximum(m, s.max(axis=-1, keepdims=True))
        l = jnp.zeros((tq, 1), F32)
        o = jnp.zeros((tq, 256), F32)
        for s, vb in zip(scores, v_blocks):
            p = jnp.exp(s - m)
            l = l + p.sum(axis=-1, keepdims=True)
            o = o + _dot(p.astype(BF16), vb)
        out = out + jnp.where(head_of_lane == hh, o * (1.0 / l), 0.0)
    return out


def _na_kernel(q_ref, kp_ref, kc_ref, kn_ref, vp_ref, vc_ref, vn_ref, kx_ref, vx_ref, bias_ref, o_ref):
    biases = [bias_ref.at[:, :, 0:256], bias_ref.at[:, :, 256:512], bias_ref.at[:, :, 512:768], None]
    out = _na_heads(q_ref[...], [kp_ref[...], kc_ref[...], kn_ref[...], kx_ref[...]],
                    [vp_ref[...], vc_ref[...], vn_ref[...], vx_ref[...]], biases)
    o_ref[...] = out.astype(BF16)


def _na_call(q, k, v, kx, vx, bias, layer):
    b, t, w = q.shape
    c = kx.shape[1]
    tq = 4 * GRID_W
    nt = t // tq
    cur = lambda bi, i: (bi, i, 0)
    prev = lambda bi, i: (bi, jnp.maximum(i - 1, 0), 0)
    nxt = lambda bi, i: (bi, jnp.minimum(i + 1, nt - 1), 0)
    ctx = lambda bi, i: (bi, 0, 0)
    variant = lambda bi, i: (layer, jnp.where(i == 0, 0, jnp.where(i == nt - 1, 2, 1)), 0, 0, 0)
    blk = lambda m: pl.BlockSpec((None, tq, w), m)
    return pl.pallas_call(
        _na_kernel, grid=(b, nt),
        in_specs=[blk(cur), blk(prev), blk(cur), blk(nxt), blk(prev), blk(cur), blk(nxt),
                  pl.BlockSpec((None, c, w), ctx), pl.BlockSpec((None, c, w), ctx),
                  pl.BlockSpec((None, None, NA_HEADS, tq, 3 * tq), variant)],
        out_specs=blk(cur),
        out_shape=jax.ShapeDtypeStruct((b, t, w), BF16),
        compiler_params=_params(48, 2), name="na_attn",
    )(q, k, k, k, v, v, v, kx, vx, bias)


def _na_ctx_kernel(q_ref, k_ref, v_ref, o_ref):
    o_ref[...] = _na_heads(q_ref[...], [k_ref[...]], [v_ref[...]], [None]).astype(BF16)


def _na_ctx_call(q, k, v):
    b, c, w = q.shape
    spec = pl.BlockSpec((None, c, w), lambda bi: (bi, 0, 0))
    return pl.pallas_call(
        _na_ctx_kernel, grid=(b,), in_specs=[spec, spec, spec], out_specs=spec,
        out_shape=jax.ShapeDtypeStruct((b, c, w), BF16),
        compiler_params=_params(32, 1), name="na_ctx_attn",
    )(q, k, v)


def _na_bias(rpb, nt):
    rows = 4 * nt
    col = np.arange(GRID_W)
    c_start = np.clip(col - NA_COLS // 2, 0, GRID_W - NA_COLS)
    valid_c = (col[None, :] >= c_start[:, None]) & (col[None, :] < c_start[:, None] + NA_COLS)
    dc = np.clip(col[None, :] - col[:, None] + (NA_COLS - 1), 0, 2 * NA_COLS - 2)
    pick_dr, ok = [], []
    for i in (0, min(1, nt - 1), nt - 1):
        rq = 4 * i + np.arange(4)
        start = np.clip(rq - NA_ROWS // 2, 0, rows - NA_ROWS)
        blocks = np.array([i - 1, i, i + 1])
        rk = (4 * blocks[:, None] + np.arange(4)[None, :]).reshape(-1)
        block_ok = np.repeat((blocks >= 0) & (blocks < nt), 4)
        valid_r = (rk[None, :] >= start[:, None]) & (rk[None, :] < start[:, None] + NA_ROWS) & block_ok[None, :]
        dr = np.clip(rk[None, :] - rq[:, None] + (NA_ROWS - 1), 0, 2 * NA_ROWS - 2)
        pick_dr.append(dr[:, :, None] == np.arange(2 * NA_ROWS - 1))
        ok.append(valid_r[:, None, :, None] & valid_c[None, :, None, :])
    pick_dc = jnp.asarray(dc[:, :, None] == np.arange(2 * NA_COLS - 1), F32)
    by_col = jnp.einsum("lhrd,qkd->lhrqk", rpb, pick_dc, precision=lax.Precision.HIGHEST)
    vals = jnp.einsum("vabr,lhrqk->lvhaqbk", jnp.asarray(np.stack(pick_dr), F32), by_col,
                      precision=lax.Precision.HIGHEST)
    tile_q, tile_k = 4 * GRID_W, 12 * GRID_W
    mask = jnp.asarray(np.stack(ok).reshape(1, 3, 1, tile_q, tile_k))
    return jnp.where(mask, vals.reshape(rpb.shape[0], 3, NA_HEADS, tile_q, tile_k), NEG_INF)


TOKEN_TILE = 512
QUERY_BLOCK = 512
KEY_CHUNK = 512
FLASH_COL = MXU_DIM
STREAM_UNROLL = 16
SCORE_LOOKAHEAD = 2
SEED_KEYS = 64
F32_SAFE_MAX = 3e38
MIN_DENOMINATOR = 2.0 ** -40


def _col_max(s):
    keys = s.shape[0]
    if keys % 64 == 0 and keys > 64:
        s = jnp.max(s.reshape(keys // 64, 64, s.shape[1]), axis=0)
    return jnp.max(s, axis=0, keepdims=True)


def _flash_kernel(*refs, n_heads, group, tq, tk, has_ctx):
    if has_ctx:
        q_ref, k_ref, vt_ref, kx_ref, vtx_ref, o_ref, acc_scr, ot_scr = refs
    else:
        q_ref, k_ref, vt_ref, o_ref, acc_scr, ot_scr = refs
    n_sub = q_ref.shape[1] // tq
    n_chunks = k_ref.shape[1] // tk
    n_slabs = n_heads // group
    cols = [(hh, qi) for hh in range(n_heads) for qi in range(n_sub)]
    heads = range(len(cols))
    slab_of = [hh // group for hh, _ in cols]

    def q_col(c):
        hh, qi = cols[c]
        return q_ref[hh, qi * tq:(qi + 1) * tq, :]

    def load_chunk(j):
        off = pl.multiple_of(j * tk, tk)
        return ([k_ref[sl, pl.ds(off, tk), :] for sl in range(n_slabs)],
                [vt_ref[sl, :, pl.ds(off, tk)] for sl in range(n_slabs)])

    def pv(vt, p):
        return _dot(vt, p.astype(BF16))

    def exact_step(ks, vts, ms):
        new_ms = []
        for hh in heads:
            s = _dot_nt(ks[slab_of[hh]], q_col(hh))
            m_new = jnp.maximum(ms[hh], _col_max(s))
            p = jnp.exp2(s - m_new)
            acc_scr[hh] = acc_scr[hh] * jnp.exp2(ms[hh] - m_new) + pv(vts[slab_of[hh]], p)
            new_ms.append(m_new)
        return tuple(new_ms)

    def stream(chunks, ms):
        units = [(i, hh) for i in range(len(chunks)) for hh in heads]
        score = lambda u: _dot_nt(chunks[u[0]][0][slab_of[u[1]]], q_col(u[1]))
        pending = [score(u) for u in units[:SCORE_LOOKAHEAD]]
        for n, (i, hh) in enumerate(units):
            s = pending.pop(0)
            if n + SCORE_LOOKAHEAD < len(units):
                pending.append(score(units[n + SCORE_LOOKAHEAD]))
            acc_scr[hh] += pv(chunks[i][1][slab_of[hh]], jnp.exp2(s - ms[hh]))

    ctx_chunk = ([kx_ref[sl] for sl in range(n_slabs)], [vtx_ref[sl] for sl in range(n_slabs)]) if has_ctx else None
    seed_ref = kx_ref if has_ctx else k_ref
    ms = tuple(_col_max(_dot_nt(seed_ref[slab_of[hh], 0:SEED_KEYS, :], q_col(hh))) for hh in heads)
    acc_scr[...] = jnp.zeros_like(acc_scr)
    if has_ctx:
        stream([ctx_chunk], ms)
    per_iter = max(u for u in (STREAM_UNROLL, 8, 4, 2, 1) if n_chunks % u == 0)

    def stream_body(j, carry):
        stream([load_chunk(j * per_iter + i) for i in range(per_iter)], ms)
        return carry

    lax.fori_loop(0, n_chunks // per_iter, stream_body, 0)
    unusable = jnp.zeros((V_AUG, tq), F32)
    for hh in heads:
        acc = acc_scr[hh]
        unusable = jnp.maximum(unusable, jnp.where(jnp.abs(acc) < F32_SAFE_MAX, 0.0, 1.0))
        unusable = jnp.maximum(unusable, jnp.where(acc[HEAD_DIM:HEAD_DIM + 1] > MIN_DENOMINATOR, 0.0, 1.0))

    @pl.when(jnp.max(unusable) > 0.0)
    def _():
        acc_scr[...] = jnp.zeros_like(acc_scr)
        ms = tuple(jnp.full((1, tq), NEG_INF, F32) for _ in heads)
        if has_ctx:
            ms = exact_step(*ctx_chunk, ms)
        lax.fori_loop(0, n_chunks, lambda j, m: exact_step(*load_chunk(j), m), ms)

    for c, (hh, qi) in enumerate(cols):
        acc = acc_scr[c]
        ot_scr[HEAD_DIM * hh:HEAD_DIM * (hh + 1), qi * tq:(qi + 1) * tq] = (
            acc[0:HEAD_DIM] * (1.0 / acc[HEAD_DIM:HEAD_DIM + 1]))
    o_ref[...] = ot_scr[...].T.astype(BF16)


def _flash_call(q, k, vt, kx, vtx, tq, tk):
    b, n_heads, t_q, w = q.shape
    slabs, t_k = k.shape[1], k.shape[2]
    has_ctx = kx is not None
    full4 = lambda bi, i: (bi, 0, 0, 0)
    in_specs = [pl.BlockSpec((None, n_heads, tq, w), lambda bi, i: (bi, 0, i, 0)),
                pl.BlockSpec((None, slabs, t_k, w), full4),
                pl.BlockSpec((None, slabs, V_AUG, t_k), full4)]
    args = [q, k, vt]
    if has_ctx:
        c = kx.shape[2]
        in_specs += [pl.BlockSpec((None, slabs, c, w), full4), pl.BlockSpec((None, slabs, V_AUG, c), full4)]
        args += [kx, vtx]
    col = min(FLASH_COL, tq)
    kern = functools.partial(_flash_kernel, n_heads=n_heads, group=n_heads // slabs, tq=col, tk=tk,
                             has_ctx=has_ctx)
    return pl.pallas_call(
        kern, grid=(b, t_q // tq), in_specs=in_specs,
        out_specs=pl.BlockSpec((None, tq, n_heads * HEAD_DIM), lambda bi, i: (bi, i, 0)),
        out_shape=jax.ShapeDtypeStruct((b, t_q, n_heads * HEAD_DIM), BF16),
        scratch_shapes=[pltpu.VMEM((n_heads * (tq // col), V_AUG, col), F32),
                        pltpu.VMEM((n_heads * HEAD_DIM, tq), F32)],
        compiler_params=_params(56, 2), name="flash",
    )(*args)


def _merge_kernel(x_ref, mod_ref, o0_ref, o1_ref, o2_ref, o3_ref, wg_ref, bg_ref, wb_ref, wo_ref, out_ref):
    x = x_ref[...]
    h = (_rms(x) * (1.0 + mod_ref[1:2, :]) + mod_ref[0:1, :]).astype(BF16)
    y = None
    for i, o_ref in enumerate((o0_ref, o1_ref, o2_ref, o3_ref)):
        gate = jax.nn.sigmoid(_dot(h, wg_ref[i]) + bg_ref[i])
        term = gate * _dot(o_ref[...], wb_ref[i])
        y = term if y is None else y + term
    out_ref[...] = x + mod_ref[2:3, :] * _dot(y.astype(BF16), wo_ref[...])


def _merge_call(x, mod, mod_row, branches, lw, tm):
    b, t, d = x.shape
    if mod_row is None:
        mod_map = lambda i, j: (j, 0, 0)
    else:
        mod_map = lambda i, j: (mod_row, 0, 0)
    tok = lambda i, j: (j, i, 0)
    br = pl.BlockSpec((None, tm, BRANCH_WIDTH), tok)
    return pl.pallas_call(
        _merge_kernel, grid=(t // tm, b),
        in_specs=[pl.BlockSpec((None, tm, d), tok), pl.BlockSpec((None, 6, d), mod_map), br, br, br, br,
                  _layer_spec((N_BRANCH, d, d), lw["layer"]), _layer_spec((N_BRANCH, 1, d), lw["layer"]),
                  _layer_spec((N_BRANCH, BRANCH_WIDTH, d), lw["layer"]), _layer_spec((d, d), lw["layer"])],
        out_specs=pl.BlockSpec((None, tm, d), tok),
        out_shape=jax.ShapeDtypeStruct((b, t, d), F32),
        compiler_params=_params(56, 2), name="merge",
    )(x, mod, *branches, lw["w_gate"], lw["b_gate"], lw["w_branch"], lw["w_out"])


def _ffn_kernel(x_ref, xp_ref, xn_ref, mod_ref, wu_ref, cw_ref, cb_ref, wd_ref, out_ref, h_scr, act_scr):
    tm = x_ref.shape[0]
    i = pl.program_id(0)
    nt = pl.num_programs(0)
    sh, sc = mod_ref[3:4, :], mod_ref[4:5, :]

    def modulated(v):
        return _rms(v) * (1.0 + sc) + sh

    x = x_ref[...]
    h_scr[0:HALO, :] = jnp.where(i > 0, modulated(xp_ref[...]), 0.0).astype(BF16)
    h_scr[HALO:HALO + tm, :] = modulated(x).astype(BF16)
    h_scr[HALO + tm:, :] = jnp.where(i < nt - 1, modulated(xn_ref[...]), 0.0).astype(BF16)
    rows = tm + 2 * HALO
    main = slice(HALO, HALO + tm)

    def cols(j, gate):
        lo = gate * FFN_DIM + FFN_CHUNK * j
        return slice(lo, lo + FFN_CHUNK)

    def up_proj(j):
        return tuple(_dot(h_scr[...], wu_ref[:, cols(j, gate)]) for gate in (0, 1))

    def conv(up, sl):
        return (pltpu.roll(up, 1, 0)[main] * cw_ref[0:1, sl] + up[main] * cw_ref[1:2, sl]
                + pltpu.roll(up, rows - 1, 0)[main] * cw_ref[2:3, sl] + cb_ref[:, sl])

    ups = [up_proj(j) for j in range(FFN_LOOKAHEAD)]
    for j in range(N_FFN_CHUNKS):
        up_a, up_g = ups.pop(0)
        if j + FFN_LOOKAHEAD < N_FFN_CHUNKS:
            ups.append(up_proj(j + FFN_LOOKAHEAD))
        a, g = conv(up_a, cols(j, 0)), conv(up_g, cols(j, 1))
        act_scr[:, cols(j, 0)] = (g * jax.nn.sigmoid(g) * a).astype(BF16)
    out_ref[...] = x + mod_ref[5:6, :] * _dot(act_scr[...], wd_ref[...])


def _ffn_call(x, mod, mod_row, lw, tm):
    b, t, d = x.shape
    nt = t // tm
    hb = tm // HALO
    last_halo = t // HALO - 1
    if mod_row is None:
        mod_map = lambda i, j: (j, 0, 0)
    else:
        mod_map = lambda i, j: (mod_row, 0, 0)
    tok = lambda i, j: (j, i, 0)
    return pl.pallas_call(
        _ffn_kernel, grid=(nt, b),
        in_specs=[pl.BlockSpec((None, tm, d), tok),
                  pl.BlockSpec((None, HALO, d), lambda i, j: (j, jnp.maximum(i * hb - 1, 0), 0)),
                  pl.BlockSpec((None, HALO, d), lambda i, j: (j, jnp.minimum((i + 1) * hb, last_halo), 0)),
                  pl.BlockSpec((None, 6, d), mod_map),
                  _layer_spec((d, 2 * FFN_DIM), lw["layer"]),
                  _layer_spec((3, 2 * FFN_DIM), lw["layer"]),
                  _layer_spec((1, 2 * FFN_DIM), lw["layer"]),
                  _layer_spec((FFN_DIM, d), lw["layer"])],
        out_specs=pl.BlockSpec((None, tm, d), tok),
        out_shape=jax.ShapeDtypeStruct((b, t, d), F32),
        scratch_shapes=[pltpu.VMEM((tm + 2 * HALO, d), BF16), pltpu.VMEM((tm, FFN_DIM), BF16)],
        compiler_params=_params(56, 2), name="ffn",
    )(x, x, x, mod, lw["w_up"], lw["conv_w"], lw["conv_b"], lw["w_down"])


def _block_diag_ones(width, segs):
    m = np.zeros((width, width), np.float32)
    for lo, hi in segs:
        m[lo:hi, lo:hi] = 1.0
    return jnp.asarray(m, BF16)


def _rope_pattern(pos, width):
    half = width // 2
    inv = ROPE_THETA ** (-jnp.arange(half, dtype=F32) / half)
    ang = pos.astype(F32)[:, None] * inv[None, :]
    cos, sin = jnp.cos(ang), jnp.sin(ang)
    return jnp.concatenate([cos, cos], axis=1), jnp.concatenate([-sin, sin], axis=1)


def _rope_tables(t, identity):
    if identity:
        return {"gc": jnp.ones((t, 256), F32), "gs": jnp.zeros((t, 256), F32),
                "mc": jnp.ones((t, LANES), F32), "ms": jnp.zeros((t, LANES), F32)}
    n_rows = t // GRID_W

    def grid_tables(width):
        rc, rs = _rope_pattern(jnp.arange(n_rows), width)
        cc, cs = _rope_pattern(jnp.arange(GRID_W), width)
        expand = lambda r, c: jnp.concatenate([jnp.repeat(r, GRID_W, axis=0), jnp.tile(c, (n_rows, 1))], axis=1)
        return expand(rc, cc), expand(rs, cs)

    gcos, gsin = grid_tables(HEAD_DIM // 2)
    mcos, msin = grid_tables(MLA_ROPE // 2)
    one, zero = jnp.ones((t, MLA_NOPE), F32), jnp.zeros((t, MLA_NOPE), F32)
    pad1, pad0 = jnp.ones((t, LANES - MLA_NOPE - MLA_ROPE), F32), jnp.zeros((t, LANES - MLA_NOPE - MLA_ROPE), F32)
    return {"gc": jnp.tile(gcos, (1, 4)), "gs": jnp.tile(gsin, (1, 4)),
            "mc": jnp.concatenate([one, mcos, pad1], axis=1),
            "ms": jnp.concatenate([zero, msin, pad0], axis=1)}


def _arrange(p):
    d = D_MODEL
    n_layers = p["w_in"].shape[0]
    cat = lambda parts: jnp.concatenate(parts, axis=-1)
    w_in = p["w_in"]
    na, mla = w_in[..., :NA_IN], w_in[..., NA_IN:NA_IN + MLA_IN]
    sg = w_in[..., NA_IN + MLA_IN:NA_IN + MLA_IN + SG_IN]
    gqa = w_in[..., NA_IN + MLA_IN + SG_IN:]
    z = lambda n: jnp.zeros((n_layers, d, n), F32)
    cq, ckv, kr = (mla[..., :MLA_Q_LORA], mla[..., MLA_Q_LORA:MLA_Q_LORA + MLA_KV_LORA],
                   mla[..., MLA_Q_LORA + MLA_KV_LORA:])
    gq, gk, gv = gqa[..., :256], gqa[..., 256:384], gqa[..., 384:]
    gk_dup = cat([gk[..., :64], gk[..., :64], gk[..., 64:], gk[..., 64:]])
    w_in_arr = cat([na, cq, ckv, z(256 - MLA_KV_LORA), z(MLA_NOPE), kr, z(LANES - MLA_NOPE - MLA_ROPE),
                    sg, gq, gk_dup, gv]).astype(BF16)

    lane_pad = lambda v, n: jnp.pad(v, ((0, 0),) * (v.ndim - 1) + ((0, n),))
    w_uq = p["mla_w_uq"].reshape(n_layers, MLA_Q_LORA, MLA_HEADS, MLA_NOPE + MLA_ROPE)
    w_uq = lane_pad(w_uq, LANES - MLA_NOPE - MLA_ROPE).reshape(n_layers, MLA_Q_LORA, MLA_HEADS * LANES)
    w_ukv = p["mla_w_ukv"].reshape(n_layers, MLA_KV_LORA, MLA_HEADS, MLA_NOPE + MLA_V)
    k_part = lane_pad(w_ukv[..., :MLA_NOPE], LANES - MLA_NOPE).reshape(n_layers, MLA_KV_LORA, MLA_HEADS * LANES)
    v_part = w_ukv[..., MLA_NOPE:].reshape(n_layers, MLA_KV_LORA, MLA_HEADS * MLA_V)
    w_ukv = jnp.pad(cat([k_part, v_part]), ((0, 0), (0, 256 - MLA_KV_LORA), (0, 0)))

    qg, kg = p["mla_q_norm"], p["mla_k_norm"]
    z1 = lambda n: jnp.zeros((n_layers, n), F32)
    tile4 = lambda v: jnp.tile(v, (1, 4))
    gains = cat([
        tile4(p["na_q_norm"]), tile4(p["na_k_norm"]),
        p["mla_cq_norm"],
        p["mla_ckv_norm"], z1(256 - MLA_KV_LORA),
        tile4(cat([qg * MLA_LOGIT_SCALE, z1(LANES - MLA_NOPE - MLA_ROPE)])),
        tile4(cat([kg[:, :MLA_NOPE], z1(LANES - MLA_NOPE)])),
        z1(MLA_NOPE), kg[:, MLA_NOPE:], z1(LANES - MLA_NOPE - MLA_ROPE),
        p["sg_v_norm"],
        tile4(p["gqa_q_norm"] * GQA_LOGIT_SCALE), tile4(p["gqa_k_norm"]),
    ]).reshape(n_layers, 1, G_TOT)

    invq = np.tile(np.concatenate([np.full(MLA_NOPE, 1.0 / MLA_NOPE), np.full(MLA_ROPE, 1.0 / MLA_ROPE),
                                   np.ones(LANES - MLA_NOPE - MLA_ROPE)]), 4).astype(np.float32).reshape(1, 512)
    sgb = jnp.repeat(jnp.swapaxes(p["sg_b_s"], -1, -2), SG_WIDTH // SG_GROUPS, axis=-1)

    return {
        "w_in": w_in_arr, "w_uq": w_uq.astype(BF16), "w_ukv": w_ukv.astype(BF16), "gains": gains,
        "s64": _block_diag_ones(MXU_DIM, [(64 * i, 64 * i + 64) for i in range(4)]),
        "smq": _block_diag_ones(MXU_DIM, [(0, 64), (64, 96), (128, 192), (192, 224)]),
        "invq": jnp.asarray(invq),
        "sgw": p["sg_w_s"].astype(BF16), "sgb": sgb,
        "w_gate": p["w_gate"].astype(BF16), "b_gate": p["b_gate"].reshape(n_layers, N_BRANCH, 1, d),
        "w_branch": p["w_branch"].astype(BF16), "w_out": p["w_out"].astype(BF16),
        "w_up": p["w_up"].astype(BF16), "conv_w": p["conv_w"], "conv_b": p["conv_b"].reshape(n_layers, 1, -1),
        "w_down": p["w_down"].astype(BF16),
    }


def kernel(x, c, ctx, c_ctx, w_ada, b_ada, w_in, na_q_norm, na_k_norm, na_rpb, mla_cq_norm, mla_ckv_norm,
           mla_w_uq, mla_w_ukv, mla_q_norm, mla_k_norm, sg_v_norm, sg_w_s, sg_b_s, gqa_q_norm, gqa_k_norm,
           w_branch, w_gate, b_gate, w_out, w_up, conv_w, conv_b, w_down):
    b, t, d = x.shape
    n_ctx = ctx.shape[1]
    depth = w_in.shape[0]
    ctx_row = b
    cvec = jnp.zeros((8, d), F32).at[:b].set(c).at[ctx_row].set(c_ctx)
    mod_all = _ada_call(cvec, w_ada, b_ada).reshape(depth, 8, 6, d)

    assert b < 8 and t % (4 * GRID_W) == 0 and n_ctx % SG_CHUNK == 0 and n_ctx >= SEED_KEYS
    tm = min(TOKEN_TILE, t)
    tq = min(QUERY_BLOCK, t)
    tk = min(KEY_CHUNK, t)
    assert t % tm == 0 and t % tq == 0 and t % tk == 0
    tabs_lat = _rope_tables(t, identity=False)
    tabs_ctx = _rope_tables(n_ctx, identity=True)
    na_bias = _na_bias(na_rpb, t // (4 * GRID_W))

    lw = _arrange({
        "w_in": w_in, "na_q_norm": na_q_norm, "na_k_norm": na_k_norm,
        "mla_cq_norm": mla_cq_norm, "mla_ckv_norm": mla_ckv_norm, "mla_w_uq": mla_w_uq,
        "mla_w_ukv": mla_w_ukv, "mla_q_norm": mla_q_norm, "mla_k_norm": mla_k_norm,
        "sg_v_norm": sg_v_norm, "sg_w_s": sg_w_s, "sg_b_s": sg_b_s,
        "gqa_q_norm": gqa_q_norm, "gqa_k_norm": gqa_k_norm,
        "w_branch": w_branch, "w_gate": w_gate, "b_gate": b_gate, "w_out": w_out,
        "w_up": w_up, "conv_w": conv_w, "conv_b": conv_b, "w_down": w_down})
    cx = ctx
    for l in range(depth):
        lw = dict(lw, layer=l)
        mod = mod_all[l]
        pc = _in_call(cx, mod, ctx_row, lw, tabs_ctx, n_ctx)
        pz = _in_call(x, mod, None, lw, tabs_lat, tm)

        o_na = _na_call(pz["na_q"], pz["na_k"], pz["na_v"], pc["na_k"], pc["na_v"], na_bias, l)
        o_mla = _flash_call(pz["mla_q"], pz["mla_k"], pz["mla_vt"], pc["mla_k"], pc["mla_vt"], tq, tk)
        o_gqa = _flash_call(pz["gqa_q"], pz["gqa_k"], pz["gqa_vt"], pc["gqa_k"], pc["gqa_vt"], tq, tk)
        x = _merge_call(x, mod, None, (o_na, o_mla, pz["sg_o"], o_gqa), lw, tm)
        x = _ffn_call(x, mod, None, lw, tm)

        if l < depth - 1:
            oc_na = _na_ctx_call(pc["na_q"], pc["na_k"], pc["na_v"])
            oc_mla = _flash_call(pc["mla_q"], pc["mla_k"], pc["mla_vt"], None, None, n_ctx, n_ctx)
            oc_gqa = _flash_call(pc["gqa_q"], pc["gqa_k"], pc["gqa_vt"], None, None, n_ctx, n_ctx)
            cx = _merge_call(cx, mod, ctx_row, (oc_na, oc_mla, pc["sg_o"], oc_gqa), lw, n_ctx)
            cx = _ffn_call(cx, mod, ctx_row, lw, n_ctx)
    return x
```

```python
import functools
import math

import numpy as np
import jax
import jax.numpy as jnp
from jax import lax
from jax.experimental import pallas as pl
from jax.experimental.pallas import tpu as pltpu

F32 = jnp.float32
BF16 = jnp.bfloat16

D_MODEL = 1024
GRID_W = 64
HEAD_DIM = 64
NA_HEADS = 4
NA_ROWS = 8
NA_COLS = 16
MLA_HEADS = 4
MLA_Q_LORA = 256
MLA_KV_LORA = 192
MLA_NOPE = 64
MLA_ROPE = 32
MLA_V = 64
SG_GROUPS = 4
SG_CHUNK = 128
SG_WIDTH = 256
GQA_HEADS = 4
GQA_KV_HEADS = 2
N_BRANCH = 4
BRANCH_WIDTH = 256
FFN_DIM = 2816
ROPE_THETA = 10000.0
EPS = 1e-6
NEG_INF = -1e30

NA_IN = 3 * NA_HEADS * HEAD_DIM
MLA_IN = MLA_Q_LORA + MLA_KV_LORA + MLA_ROPE
SG_IN = 2 * SG_WIDTH
GQA_IN = (GQA_HEADS + 2 * GQA_KV_HEADS) * HEAD_DIM

LANES = 128
MXU_DIM = 256
BF16_SUBLANES = 16
V_AUG = HEAD_DIM + BF16_SUBLANES
LOG2E = math.log2(math.e)
MLA_LOGIT_SCALE = (MLA_NOPE + MLA_ROPE) ** -0.5 * LOG2E
GQA_LOGIT_SCALE = HEAD_DIM ** -0.5 * LOG2E

C_NA = 0
C_MLA = C_NA + NA_IN
C_SG = C_MLA + 640
C_GQA = C_SG + SG_IN
IN_ARR = C_GQA + 640

G_NA, G_CQ, G_CKV, G_MQ, G_MKN, G_MKR, G_SG, G_GQA, G_TOT = 0, 512, 768, 1024, 1536, 2048, 2176, 2432, 2944

FFN_CHUNK = 256
N_FFN_CHUNKS = FFN_DIM // FFN_CHUNK
FFN_LOOKAHEAD = 2
HALO = BF16_SUBLANES


def _const_spec(shape):
    nd = len(shape)
    return pl.BlockSpec(shape, lambda *_: (0,) * nd, pipeline_mode=pl.Buffered(1))


def _layer_spec(shape, layer):
    nd = len(shape)
    return pl.BlockSpec((None,) + tuple(shape), lambda *_: (layer,) + (0,) * nd, pipeline_mode=pl.Buffered(1))


def _params(vmem_mb, n_grid):
    return pltpu.CompilerParams(dimension_semantics=("arbitrary",) * n_grid,
                                vmem_limit_bytes=vmem_mb * 1024 * 1024)


def _dot(a, b):
    return jnp.dot(a, b, preferred_element_type=F32)


def _dot_nt(a, b):
    return lax.dot_general(a, b, (((1,), (1,)), ((), ())), preferred_element_type=F32)


def _rms(xf):
    return xf * lax.rsqrt(jnp.mean(xf * xf, axis=-1, keepdims=True) + EPS)


def _segsum(x2, s_ref):
    hi = x2.astype(BF16)
    lo = (x2 - hi.astype(F32)).astype(BF16)
    s = s_ref[...]
    parts = []
    for j in range(x2.shape[1] // MXU_DIM):
        sl = slice(MXU_DIM * j, MXU_DIM * (j + 1))
        parts.append(_dot(hi[:, sl], s) + _dot(lo[:, sl], s))
    return parts[0] if len(parts) == 1 else jnp.concatenate(parts, axis=1)


def _rope(x, cos, sin_signed, shift, first_half):
    w = x.shape[1]
    partner = jnp.where(first_half, pltpu.roll(x, w - shift, 1), pltpu.roll(x, shift, 1))
    return x * cos + partner * sin_signed


def _ada_kernel(c_ref, w_ref, b_ref, o_ref):
    c = c_ref[...]
    a = (c * jax.nn.sigmoid(c)).astype(BF16)
    o_ref[...] = _dot(a, w_ref[...].astype(BF16)) + b_ref[...]


def _ada_call(cvec, w_ada, b_ada):
    n_layers, d, n = w_ada.shape
    tn = 1536
    return pl.pallas_call(
        _ada_kernel,
        grid=(n_layers, n // tn),
        in_specs=[pl.BlockSpec((8, d), lambda l, j: (0, 0)),
                  pl.BlockSpec((None, d, tn), lambda l, j: (l, 0, j)),
                  pl.BlockSpec((None, 1, tn), lambda l, j: (l, 0, j))],
        out_specs=pl.BlockSpec((None, 8, tn), lambda l, j: (l, 0, j)),
        out_shape=jax.ShapeDtypeStruct((n_layers, 8, n), F32),
        compiler_params=_params(32, 2),
        name="ada",
    )(cvec, w_ada, b_ada.reshape(n_layers, 1, n))


def _in_kernel(x_ref, mod_ref, w_in_ref, w_uq_ref, w_ukv_ref, g_ref, s64_ref, smq_ref, invq_ref,
               gc_ref, gs_ref, mc_ref, ms_ref, sgw_ref, sgb_ref,
               naq_ref, nak_ref, nav_ref, mq_ref, mk_ref, mvt_ref, sgo_ref, gq_ref, gk_ref, gvt_ref):
    tm = x_ref.shape[0]
    x = x_ref[...]
    h = (_rms(x) * (1.0 + mod_ref[1:2, :]) + mod_ref[0:1, :]).astype(BF16)

    def gain(off, width):
        return g_ref[:, off:off + width]

    inv_head = 1.0 / HEAD_DIM

    z = _dot(h, w_in_ref[:, C_NA:C_NA + NA_IN])
    qk = z[:, 0:512]
    qk = qk * lax.rsqrt(_segsum(qk * qk, s64_ref) * inv_head + EPS) * gain(G_NA, 512)
    naq_ref[...] = qk[:, 0:256].astype(BF16)
    nak_ref[...] = qk[:, 256:512].astype(BF16)
    nav_ref[...] = z[:, 512:768].astype(BF16)

    z = _dot(h, w_in_ref[:, C_MLA:C_MLA + 640])
    cq, ckv, kr = z[:, 0:256], z[:, 256:512], z[:, 512:640]
    cq = cq * lax.rsqrt(jnp.mean(cq * cq, axis=-1, keepdims=True) + EPS) * gain(G_CQ, 256)
    ckv = ckv * lax.rsqrt(jnp.sum(ckv * ckv, axis=-1, keepdims=True) * (1.0 / MLA_KV_LORA) + EPS) * gain(G_CKV, 256)
    q = _dot(cq.astype(BF16), w_uq_ref[...])
    q = q * lax.rsqrt(_segsum(q * q, smq_ref) * invq_ref[...] + EPS) * gain(G_MQ, 512)
    kv = _dot(ckv.astype(BF16), w_ukv_ref[...])
    kn = kv[:, 0:512]
    kn = kn * lax.rsqrt(_segsum(kn * kn, s64_ref) * inv_head + EPS) * gain(G_MKN, 512)
    kr = kr * lax.rsqrt(jnp.sum(kr * kr, axis=-1, keepdims=True) * (1.0 / MLA_ROPE) + EPS) * gain(G_MKR, 128)
    lane = lax.broadcasted_iota(jnp.int32, (tm, LANES), 1)
    first8 = (lane & 15) < 8
    mc, ms = mc_ref[...], ms_ref[...]
    kr = _rope(kr, mc, ms, 8, first8)
    for hh in range(MLA_HEADS):
        sl = slice(LANES * hh, LANES * (hh + 1))
        mq_ref[hh] = _rope(q[:, sl], mc, ms, 8, first8).astype(BF16)
        mk_ref[hh] = (kn[:, sl] + kr).astype(BF16)
    vt = kv[:, 512:768].T
    ones = jnp.ones((BF16_SUBLANES, tm), BF16)
    for hh in range(MLA_HEADS):
        mvt_ref[hh, 0:HEAD_DIM, :] = vt[HEAD_DIM * hh:HEAD_DIM * (hh + 1), :].astype(BF16)
        mvt_ref[hh, HEAD_DIM:V_AUG, :] = ones

    z = _dot(h, w_in_ref[:, C_SG:C_SG + SG_IN])
    uv = jax.nn.gelu(z)
    u, v = uv[:, 0:SG_WIDTH], uv[:, SG_WIDTH:]
    v = v * lax.rsqrt(jnp.mean(v * v, axis=-1, keepdims=True) + EPS) * gain(G_SG, 256)
    grp = lax.broadcasted_iota(jnp.int32, (SG_CHUNK, SG_WIDTH), 1) >> 6
    for c in range(tm // SG_CHUNK):
        rows = slice(SG_CHUNK * c, SG_CHUNK * (c + 1))
        vc = v[rows, :]
        mixed = sgb_ref[...]
        for gi in range(SG_GROUPS):
            mixed = mixed + _dot(sgw_ref[gi], jnp.where(grp == gi, vc, 0.0).astype(BF16))
        sgo_ref[rows, :] = (u[rows, :] * mixed).astype(BF16)

    z = _dot(h, w_in_ref[:, C_GQA:C_GQA + 640])
    qk = z[:, 0:512]
    qk = qk * lax.rsqrt(_segsum(qk * qk, s64_ref) * inv_head + EPS) * gain(G_GQA, 512)
    lane2 = lax.broadcasted_iota(jnp.int32, (tm, 2 * LANES), 1)
    first16 = (lane2 & 31) < 16
    gc, gs = gc_ref[...], gs_ref[...]
    qn = _rope(qk[:, 0:256], gc, gs, 16, first16)
    kn = _rope(qk[:, 256:512], gc, gs, 16, first16)
    half = lane >> 6
    for g in range(GQA_KV_HEADS):
        sl = slice(LANES * g, LANES * (g + 1))
        for r in range(GQA_HEADS // GQA_KV_HEADS):
            gq_ref[2 * g + r] = jnp.where(half == r, qn[:, sl], 0.0).astype(BF16)
        gk_ref[g] = kn[:, sl].astype(BF16)
    vt = z[:, 512:640].T
    for g in range(GQA_KV_HEADS):
        gvt_ref[g, 0:HEAD_DIM, :] = vt[HEAD_DIM * g:HEAD_DIM * (g + 1), :].astype(BF16)
        gvt_ref[g, HEAD_DIM:V_AUG, :] = ones


def _in_call(x, mod, mod_row, lw, tabs, tm):
    b, t, d = x.shape
    nt = t // tm
    if mod_row is None:
        mod_map = lambda i, j: (j, 0, 0)
    else:
        mod_map = lambda i, j: (mod_row, 0, 0)
    tok = lambda i, j: (j, i, 0)
    tab = lambda i, j: (i, 0)
    head_tok = lambda i, j: (j, 0, i, 0)
    head_t = lambda i, j: (j, 0, 0, i)
    in_specs = [
        pl.BlockSpec((None, tm, d), tok),
        pl.BlockSpec((None, 6, d), mod_map),
        _layer_spec((d, IN_ARR), lw["layer"]),
        _layer_spec((MLA_Q_LORA, 512), lw["layer"]),
        _layer_spec((256, 768), lw["layer"]),
        _layer_spec((1, G_TOT), lw["layer"]),
        _const_spec((MXU_DIM, MXU_DIM)),
        _const_spec((MXU_DIM, MXU_DIM)),
        _const_spec((1, 512)),
        pl.BlockSpec((tm, 256), tab), pl.BlockSpec((tm, 256), tab),
        pl.BlockSpec((tm, LANES), tab), pl.BlockSpec((tm, LANES), tab),
        _layer_spec((SG_GROUPS, SG_CHUNK, SG_CHUNK), lw["layer"]),
        _layer_spec((SG_CHUNK, SG_WIDTH), lw["layer"]),
    ]
    out_specs = [
        pl.BlockSpec((None, tm, 256), tok), pl.BlockSpec((None, tm, 256), tok), pl.BlockSpec((None, tm, 256), tok),
        pl.BlockSpec((None, MLA_HEADS, tm, LANES), head_tok),
        pl.BlockSpec((None, MLA_HEADS, tm, LANES), head_tok),
        pl.BlockSpec((None, MLA_HEADS, V_AUG, tm), head_t),
        pl.BlockSpec((None, tm, 256), tok),
        pl.BlockSpec((None, GQA_HEADS, tm, LANES), head_tok),
        pl.BlockSpec((None, GQA_KV_HEADS, tm, LANES), head_tok),
        pl.BlockSpec((None, GQA_KV_HEADS, V_AUG, tm), head_t),
    ]
    sds = jax.ShapeDtypeStruct
    out_shape = [
        sds((b, t, 256), BF16), sds((b, t, 256), BF16), sds((b, t, 256), BF16),
        sds((b, MLA_HEADS, t, LANES), BF16), sds((b, MLA_HEADS, t, LANES), BF16),
        sds((b, MLA_HEADS, V_AUG, t), BF16),
        sds((b, t, 256), BF16),
        sds((b, GQA_HEADS, t, LANES), BF16), sds((b, GQA_KV_HEADS, t, LANES), BF16),
        sds((b, GQA_KV_HEADS, V_AUG, t), BF16),
    ]
    outs = pl.pallas_call(
        _in_kernel, grid=(nt, b), in_specs=in_specs, out_specs=out_specs, out_shape=out_shape,
        compiler_params=_params(48, 2), name="in_proj",
    )(x, mod, lw["w_in"], lw["w_uq"], lw["w_ukv"], lw["gains"], lw["s64"], lw["smq"], lw["invq"],
      tabs["gc"], tabs["gs"], tabs["mc"], tabs["ms"], lw["sgw"], lw["sgb"])
    names = ("na_q", "na_k", "na_v", "mla_q", "mla_k", "mla_vt", "sg_o", "gqa_q", "gqa_k", "gqa_vt")
    return dict(zip(names, outs))


def _na_heads(q, k_blocks, v_blocks, biases, seeded):
    tq = q.shape[0]
    head_of_lane = lax.broadcasted_iota(jnp.int32, (tq, 256), 1) >> 6
    qf = q.astype(F32) * (HEAD_DIM ** -0.5)
    qhs = [jnp.where(head_of_lane == hh, qf, 0.0).astype(BF16) for hh in range(NA_HEADS)]
    vts = [vb.astype(F32).T.astype(BF16) for vb in v_blocks]
    units = [(hh, j) for hh in range(NA_HEADS) for j in range(len(k_blocks))]

    def score(u):
        hh, j = u
        s = _dot_nt(k_blocks[j], qhs[hh])
        return s if biases[j] is None else s + biases[j][hh]

    def value_rows(hh, j):
        ones = jnp.ones((BF16_SUBLANES, vts[j].shape[1]), BF16)
        return jnp.concatenate([vts[j][HEAD_DIM * hh:HEAD_DIM * (hh + 1), :], ones], axis=0)

    lookahead = SCORE_LOOKAHEAD if seeded else len(k_blocks)
    pending = [score(u) for u in units[:lookahead]]
    accs, m = [], None
    for n, (hh, j) in enumerate(units):
        if j == 0:
            acc = jnp.zeros((V_AUG, tq), F32)
            head = [pending.pop(0) for _ in range(1 if seeded else len(k_blocks))]
            m = _col_max(head[0])
            for s in head[1:]:
                m = jnp.maximum(m, _col_max(s))
        else:
            head = head[1:] if not seeded else [pending.pop(0)]
        if n + lookahead < len(units):
            pending.append(score(units[n + lookahead]))
        acc = acc + _dot(value_rows(hh, j), jnp.exp(head[0] - m).astype(BF16))
        if j == len(k_blocks) - 1:
            accs.append(acc)
    out = jnp.concatenate([a[0:HEAD_DIM] * (1.0 / a[HEAD_DIM:HEAD_DIM + 1]) for a in accs], axis=0).T
    return out, accs


def _na_kernel(q_ref, kp_ref, kc_ref, kn_ref, vp_ref, vc_ref, vn_ref, kx_ref, vx_ref, bias_ref, o_ref):
    band = lambda refs: jnp.concatenate([r[...] for r in refs], axis=0)
    operands = lambda: (q_ref[...], [kx_ref[...], band((kp_ref, kc_ref, kn_ref))],
                        [vx_ref[...], band((vp_ref, vc_ref, vn_ref))], [None, bias_ref])
    out, accs = _na_heads(*operands(), seeded=True)
    o_ref[...] = out.astype(BF16)
    unusable = jnp.zeros((V_AUG, q_ref.shape[0]), F32)
    for acc in accs:
        unusable = jnp.maximum(unusable, jnp.where(jnp.abs(acc) < F32_SAFE_MAX, 0.0, 1.0))

    @pl.when(jnp.max(unusable) > 0.0)
    def _():
        o_ref[...] = _na_heads(*operands(), seeded=False)[0].astype(BF16)


def _na_call(q, k, v, kx, vx, bias, layer):
    b, t, w = q.shape
    c = kx.shape[1]
    tq = 4 * GRID_W
    nt = t // tq
    cur = lambda bi, i: (bi, i, 0)
    prev = lambda bi, i: (bi, jnp.maximum(i - 1, 0), 0)
    nxt = lambda bi, i: (bi, jnp.minimum(i + 1, nt - 1), 0)
    ctx = lambda bi, i: (bi, 0, 0)
    variant = lambda bi, i: (layer, jnp.where(i == 0, 0, jnp.where(i == nt - 1, 2, 1)), 0, 0, 0)
    blk = lambda m: pl.BlockSpec((None, tq, w), m)
    return pl.pallas_call(
        _na_kernel, grid=(b, nt),
        in_specs=[blk(cur), blk(prev), blk(cur), blk(nxt), blk(prev), blk(cur), blk(nxt),
                  pl.BlockSpec((None, c, w), ctx), pl.BlockSpec((None, c, w), ctx),
                  pl.BlockSpec((None, None, NA_HEADS, 3 * tq, tq), variant)],
        out_specs=blk(cur),
        out_shape=jax.ShapeDtypeStruct((b, t, w), BF16),
        compiler_params=_params(48, 2), name="na_attn",
    )(q, k, k, k, v, v, v, kx, vx, bias)


def _na_ctx_kernel(q_ref, k_ref, v_ref, o_ref):
    o_ref[...] = _na_heads(q_ref[...], [k_ref[...]], [v_ref[...]], [None], seeded=False)[0].astype(BF16)


def _na_ctx_call(q, k, v):
    b, c, w = q.shape
    spec = pl.BlockSpec((None, c, w), lambda bi: (bi, 0, 0))
    return pl.pallas_call(
        _na_ctx_kernel, grid=(b,), in_specs=[spec, spec, spec], out_specs=spec,
        out_shape=jax.ShapeDtypeStruct((b, c, w), BF16),
        compiler_params=_params(32, 1), name="na_ctx_attn",
    )(q, k, v)


def _na_bias(rpb, nt):
    rows = 4 * nt
    col = np.arange(GRID_W)
    c_start = np.clip(col - NA_COLS // 2, 0, GRID_W - NA_COLS)
    valid_c = (col[None, :] >= c_start[:, None]) & (col[None, :] < c_start[:, None] + NA_COLS)
    dc = np.clip(col[None, :] - col[:, None] + (NA_COLS - 1), 0, 2 * NA_COLS - 2)
    pick_dr, ok = [], []
    for i in (0, min(1, nt - 1), nt - 1):
        rq = 4 * i + np.arange(4)
        start = np.clip(rq - NA_ROWS // 2, 0, rows - NA_ROWS)
        blocks = np.array([i - 1, i, i + 1])
        rk = (4 * blocks[:, None] + np.arange(4)[None, :]).reshape(-1)
        block_ok = np.repeat((blocks >= 0) & (blocks < nt), 4)
        valid_r = (rk[None, :] >= start[:, None]) & (rk[None, :] < start[:, None] + NA_ROWS) & block_ok[None, :]
        dr = np.clip(rk[None, :] - rq[:, None] + (NA_ROWS - 1), 0, 2 * NA_ROWS - 2)
        pick_dr.append(dr[:, :, None] == np.arange(2 * NA_ROWS - 1))
        ok.append(valid_r[:, None, :, None] & valid_c[None, :, None, :])
    pick_dc = jnp.asarray(dc[:, :, None] == np.arange(2 * NA_COLS - 1), F32)
    by_col = jnp.einsum("lhrd,qkd->lhrqk", rpb, pick_dc, precision=lax.Precision.HIGHEST)
    vals = jnp.einsum("vabr,lhrqk->lvhbkaq", jnp.asarray(np.stack(pick_dr), F32), by_col,
                      precision=lax.Precision.HIGHEST)
    tile_q, tile_k = 4 * GRID_W, 12 * GRID_W
    mask = jnp.asarray(np.stack(ok).transpose(0, 3, 4, 1, 2).reshape(1, 3, 1, tile_k, tile_q))
    return jnp.where(mask, vals.reshape(rpb.shape[0], 3, NA_HEADS, tile_k, tile_q), NEG_INF)


TOKEN_TILE = 512
QUERY_BLOCK = 512
KEY_CHUNK = 512
FLASH_COL = MXU_DIM
STREAM_UNROLL = 16
SCORE_LOOKAHEAD = 2
SEED_KEYS = 64
F32_SAFE_MAX = 3e38
MIN_DENOMINATOR = 2.0 ** -40


def _col_max(s):
    keys = s.shape[0]
    if keys % 64 == 0 and keys > 64:
        s = jnp.max(s.reshape(keys // 64, 64, s.shape[1]), axis=0)
    return jnp.max(s, axis=0, keepdims=True)


def _flash_kernel(*refs, n_heads, group, tq, tk, has_ctx):
    if has_ctx:
        q_ref, k_ref, vt_ref, kx_ref, vtx_ref, o_ref, acc_scr, ot_scr = refs
    else:
        q_ref, k_ref, vt_ref, o_ref, acc_scr, ot_scr = refs
    n_sub = q_ref.shape[1] // tq
    n_chunks = k_ref.shape[1] // tk
    n_slabs = n_heads // group
    cols = [(hh, qi) for hh in range(n_heads) for qi in range(n_sub)]
    heads = range(len(cols))
    slab_of = [hh // group for hh, _ in cols]

    def q_col(c):
        hh, qi = cols[c]
        return q_ref[hh, qi * tq:(qi + 1) * tq, :]

    def load_chunk(j):
        off = pl.multiple_of(j * tk, tk)
        return ([k_ref[sl, pl.ds(off, tk), :] for sl in range(n_slabs)],
                [vt_ref[sl, :, pl.ds(off, tk)] for sl in range(n_slabs)])

    def pv(vt, p):
        return _dot(vt, p.astype(BF16))

    def exact_step(ks, vts, ms):
        new_ms = []
        for hh in heads:
            s = _dot_nt(ks[slab_of[hh]], q_col(hh))
            m_new = jnp.maximum(ms[hh], _col_max(s))
            p = jnp.exp2(s - m_new)
            acc_scr[hh] = acc_scr[hh] * jnp.exp2(ms[hh] - m_new) + pv(vts[slab_of[hh]], p)
            new_ms.append(m_new)
        return tuple(new_ms)

    def stream(chunks, ms):
        units = [(i, hh) for i in range(len(chunks)) for hh in heads]
        score = lambda u: _dot_nt(chunks[u[0]][0][slab_of[u[1]]], q_col(u[1]))
        pending = [score(u) for u in units[:SCORE_LOOKAHEAD]]
        for n, (i, hh) in enumerate(units):
            s = pending.pop(0)
            if n + SCORE_LOOKAHEAD < len(units):
                pending.append(score(units[n + SCORE_LOOKAHEAD]))
            acc_scr[hh] += pv(chunks[i][1][slab_of[hh]], jnp.exp2(s - ms[hh]))

    ctx_chunk = ([kx_ref[sl] for sl in range(n_slabs)], [vtx_ref[sl] for sl in range(n_slabs)]) if has_ctx else None
    seed_ref = kx_ref if has_ctx else k_ref
    ms = tuple(_col_max(_dot_nt(seed_ref[slab_of[hh], 0:SEED_KEYS, :], q_col(hh))) for hh in heads)
    acc_scr[...] = jnp.zeros_like(acc_scr)
    if has_ctx:
        stream([ctx_chunk], ms)
    per_iter = max(u for u in (STREAM_UNROLL, 8, 4, 2, 1) if n_chunks % u == 0)

    def stream_body(j, carry):
        stream([load_chunk(j * per_iter + i) for i in range(per_iter)], ms)
        return carry

    lax.fori_loop(0, n_chunks // per_iter, stream_body, 0)
    unusable = jnp.zeros((V_AUG, tq), F32)
    for hh in heads:
        acc = acc_scr[hh]
        unusable = jnp.maximum(unusable, jnp.where(jnp.abs(acc) < F32_SAFE_MAX, 0.0, 1.0))
        unusable = jnp.maximum(unusable, jnp.where(acc[HEAD_DIM:HEAD_DIM + 1] > MIN_DENOMINATOR, 0.0, 1.0))

    @pl.when(jnp.max(unusable) > 0.0)
    def _():
        acc_scr[...] = jnp.zeros_like(acc_scr)
        ms = tuple(jnp.full((1, tq), NEG_INF, F32) for _ in heads)
        if has_ctx:
            ms = exact_step(*ctx_chunk, ms)
        lax.fori_loop(0, n_chunks, lambda j, m: exact_step(*load_chunk(j), m), ms)

    for c, (hh, qi) in enumerate(cols):
        acc = acc_scr[c]
        ot_scr[HEAD_DIM * hh:HEAD_DIM * (hh + 1), qi * tq:(qi + 1) * tq] = (
            acc[0:HEAD_DIM] * (1.0 / acc[HEAD_DIM:HEAD_DIM + 1]))
    o_ref[...] = ot_scr[...].T.astype(BF16)


def _flash_call(q, k, vt, kx, vtx, tq, tk):
    b, n_heads, t_q, w = q.shape
    slabs, t_k = k.shape[1], k.shape[2]
    has_ctx = kx is not None
    full4 = lambda bi, i: (bi, 0, 0, 0)
    in_specs = [pl.BlockSpec((None, n_heads, tq, w), lambda bi, i: (bi, 0, i, 0)),
                pl.BlockSpec((None, slabs, t_k, w), full4),
                pl.BlockSpec((None, slabs, V_AUG, t_k), full4)]
    args = [q, k, vt]
    if has_ctx:
        c = kx.shape[2]
        in_specs += [pl.BlockSpec((None, slabs, c, w), full4), pl.BlockSpec((None, slabs, V_AUG, c), full4)]
        args += [kx, vtx]
    col = min(FLASH_COL, tq)
    kern = functools.partial(_flash_kernel, n_heads=n_heads, group=n_heads // slabs, tq=col, tk=tk,
                             has_ctx=has_ctx)
    return pl.pallas_call(
        kern, grid=(b, t_q // tq), in_specs=in_specs,
        out_specs=pl.BlockSpec((None, tq, n_heads * HEAD_DIM), lambda bi, i: (bi, i, 0)),
        out_shape=jax.ShapeDtypeStruct((b, t_q, n_heads * HEAD_DIM), BF16),
        scratch_shapes=[pltpu.VMEM((n_heads * (tq // col), V_AUG, col), F32),
                        pltpu.VMEM((n_heads * HEAD_DIM, tq), F32)],
        compiler_params=_params(56, 2), name="flash",
    )(*args)


def _merge_kernel(x_ref, mod_ref, o0_ref, o1_ref, o2_ref, o3_ref, wg_ref, bg_ref, wb_ref, wo_ref, out_ref):
    x = x_ref[...]
    h = (_rms(x) * (1.0 + mod_ref[1:2, :]) + mod_ref[0:1, :]).astype(BF16)
    y = None
    for i, o_ref in enumerate((o0_ref, o1_ref, o2_ref, o3_ref)):
        gate = jax.nn.sigmoid(_dot(h, wg_ref[i]) + bg_ref[i])
        term = gate * _dot(o_ref[...], wb_ref[i])
        y = term if y is None else y + term
    out_ref[...] = x + mod_ref[2:3, :] * _dot(y.astype(BF16), wo_ref[...])


def _merge_call(x, mod, mod_row, branches, lw, tm):
    b, t, d = x.shape
    if mod_row is None:
        mod_map = lambda i, j: (j, 0, 0)
    else:
        mod_map = lambda i, j: (mod_row, 0, 0)
    tok = lambda i, j: (j, i, 0)
    br = pl.BlockSpec((None, tm, BRANCH_WIDTH), tok)
    return pl.pallas_call(
        _merge_kernel, grid=(t // tm, b),
        in_specs=[pl.BlockSpec((None, tm, d), tok), pl.BlockSpec((None, 6, d), mod_map), br, br, br, br,
                  _layer_spec((N_BRANCH, d, d), lw["layer"]), _layer_spec((N_BRANCH, 1, d), lw["layer"]),
                  _layer_spec((N_BRANCH, BRANCH_WIDTH, d), lw["layer"]), _layer_spec((d, d), lw["layer"])],
        out_specs=pl.BlockSpec((None, tm, d), tok),
        out_shape=jax.ShapeDtypeStruct((b, t, d), F32),
        compiler_params=_params(56, 2), name="merge",
    )(x, mod, *branches, lw["w_gate"], lw["b_gate"], lw["w_branch"], lw["w_out"])


def _ffn_kernel(x_ref, xp_ref, xn_ref, mod_ref, wu_ref, cw_ref, cb_ref, wd_ref, out_ref, h_scr, act_scr):
    tm = x_ref.shape[0]
    i = pl.program_id(0)
    nt = pl.num_programs(0)
    sh, sc = mod_ref[3:4, :], mod_ref[4:5, :]

    def modulated(v):
        return _rms(v) * (1.0 + sc) + sh

    x = x_ref[...]
    h_scr[0:HALO, :] = jnp.where(i > 0, modulated(xp_ref[...]), 0.0).astype(BF16)
    h_scr[HALO:HALO + tm, :] = modulated(x).astype(BF16)
    h_scr[HALO + tm:, :] = jnp.where(i < nt - 1, modulated(xn_ref[...]), 0.0).astype(BF16)
    rows = tm + 2 * HALO
    main = slice(HALO, HALO + tm)

    def cols(j, gate):
        lo = gate * FFN_DIM + FFN_CHUNK * j
        return slice(lo, lo + FFN_CHUNK)

    def up_proj(j):
        return tuple(_dot(h_scr[...], wu_ref[:, cols(j, gate)]) for gate in (0, 1))

    def conv(up, sl):
        return (pltpu.roll(up, 1, 0)[main] * cw_ref[0:1, sl] + up[main] * cw_ref[1:2, sl]
                + pltpu.roll(up, rows - 1, 0)[main] * cw_ref[2:3, sl] + cb_ref[:, sl])

    ups = [up_proj(j) for j in range(FFN_LOOKAHEAD)]
    for j in range(N_FFN_CHUNKS):
        up_a, up_g = ups.pop(0)
        if j + FFN_LOOKAHEAD < N_FFN_CHUNKS:
            ups.append(up_proj(j + FFN_LOOKAHEAD))
        a, g = conv(up_a, cols(j, 0)), conv(up_g, cols(j, 1))
        act_scr[:, cols(j, 0)] = (g * jax.nn.sigmoid(g) * a).astype(BF16)
    out_ref[...] = x + mod_ref[5:6, :] * _dot(act_scr[...], wd_ref[...])


def _ffn_call(x, mod, mod_row, lw, tm):
    b, t, d = x.shape
    nt = t // tm
    hb = tm // HALO
    last_halo = t // HALO - 1
    if mod_row is None:
        mod_map = lambda i, j: (j, 0, 0)
    else:
        mod_map = lambda i, j: (mod_row, 0, 0)
    tok = lambda i, j: (j, i, 0)
    return pl.pallas_call(
        _ffn_kernel, grid=(nt, b),
        in_specs=[pl.BlockSpec((None, tm, d), tok),
                  pl.BlockSpec((None, HALO, d), lambda i, j: (j, jnp.maximum(i * hb - 1, 0), 0)),
                  pl.BlockSpec((None, HALO, d), lambda i, j: (j, jnp.minimum((i + 1) * hb, last_halo), 0)),
                  pl.BlockSpec((None, 6, d), mod_map),
                  _layer_spec((d, 2 * FFN_DIM), lw["layer"]),
                  _layer_spec((3, 2 * FFN_DIM), lw["layer"]),
                  _layer_spec((1, 2 * FFN_DIM), lw["layer"]),
                  _layer_spec((FFN_DIM, d), lw["layer"])],
        out_specs=pl.BlockSpec((None, tm, d), tok),
        out_shape=jax.ShapeDtypeStruct((b, t, d), F32),
        scratch_shapes=[pltpu.VMEM((tm + 2 * HALO, d), BF16), pltpu.VMEM((tm, FFN_DIM), BF16)],
        compiler_params=_params(56, 2), name="ffn",
    )(x, x, x, mod, lw["w_up"], lw["conv_w"], lw["conv_b"], lw["w_down"])


def _block_diag_ones(width, segs):
    m = np.zeros((width, width), np.float32)
    for lo, hi in segs:
        m[lo:hi, lo:hi] = 1.0
    return jnp.asarray(m, BF16)


def _rope_pattern(pos, width):
    half = width // 2
    inv = ROPE_THETA ** (-jnp.arange(half, dtype=F32) / half)
    ang = pos.astype(F32)[:, None] * inv[None, :]
    cos, sin = jnp.cos(ang), jnp.sin(ang)
    return jnp.concatenate([cos, cos], axis=1), jnp.concatenate([-sin, sin], axis=1)


def _rope_tables(t, identity):
    if identity:
        return {"gc": jnp.ones((t, 256), F32), "gs": jnp.zeros((t, 256), F32),
                "mc": jnp.ones((t, LANES), F32), "ms": jnp.zeros((t, LANES), F32)}
    n_rows = t // GRID_W

    def grid_tables(width):
        rc, rs = _rope_pattern(jnp.arange(n_rows), width)
        cc, cs = _rope_pattern(jnp.arange(GRID_W), width)
        expand = lambda r, c: jnp.concatenate([jnp.repeat(r, GRID_W, axis=0), jnp.tile(c, (n_rows, 1))], axis=1)
        return expand(rc, cc), expand(rs, cs)

    gcos, gsin = grid_tables(HEAD_DIM // 2)
    mcos, msin = grid_tables(MLA_ROPE // 2)
    one, zero = jnp.ones((t, MLA_NOPE), F32), jnp.zeros((t, MLA_NOPE), F32)
    pad1, pad0 = jnp.ones((t, LANES - MLA_NOPE - MLA_ROPE), F32), jnp.zeros((t, LANES - MLA_NOPE - MLA_ROPE), F32)
    return {"gc": jnp.tile(gcos, (1, 4)), "gs": jnp.tile(gsin, (1, 4)),
            "mc": jnp.concatenate([one, mcos, pad1], axis=1),
            "ms": jnp.concatenate([zero, msin, pad0], axis=1)}


def _arrange(p):
    d = D_MODEL
    n_layers = p["w_in"].shape[0]
    cat = lambda parts: jnp.concatenate(parts, axis=-1)
    w_in = p["w_in"]
    na, mla = w_in[..., :NA_IN], w_in[..., NA_IN:NA_IN + MLA_IN]
    sg = w_in[..., NA_IN + MLA_IN:NA_IN + MLA_IN + SG_IN]
    gqa = w_in[..., NA_IN + MLA_IN + SG_IN:]
    z = lambda n: jnp.zeros((n_layers, d, n), F32)
    cq, ckv, kr = (mla[..., :MLA_Q_LORA], mla[..., MLA_Q_LORA:MLA_Q_LORA + MLA_KV_LORA],
                   mla[..., MLA_Q_LORA + MLA_KV_LORA:])
    gq, gk, gv = gqa[..., :256], gqa[..., 256:384], gqa[..., 384:]
    gk_dup = cat([gk[..., :64], gk[..., :64], gk[..., 64:], gk[..., 64:]])
    w_in_arr = cat([na, cq, ckv, z(256 - MLA_KV_LORA), z(MLA_NOPE), kr, z(LANES - MLA_NOPE - MLA_ROPE),
                    sg, gq, gk_dup, gv]).astype(BF16)

    lane_pad = lambda v, n: jnp.pad(v, ((0, 0),) * (v.ndim - 1) + ((0, n),))
    w_uq = p["mla_w_uq"].reshape(n_layers, MLA_Q_LORA, MLA_HEADS, MLA_NOPE + MLA_ROPE)
    w_uq = lane_pad(w_uq, LANES - MLA_NOPE - MLA_ROPE).reshape(n_layers, MLA_Q_LORA, MLA_HEADS * LANES)
    w_ukv = p["mla_w_ukv"].reshape(n_layers, MLA_KV_LORA, MLA_HEADS, MLA_NOPE + MLA_V)
    k_part = lane_pad(w_ukv[..., :MLA_NOPE], LANES - MLA_NOPE).reshape(n_layers, MLA_KV_LORA, MLA_HEADS * LANES)
    v_part = w_ukv[..., MLA_NOPE:].reshape(n_layers, MLA_KV_LORA, MLA_HEADS * MLA_V)
    w_ukv = jnp.pad(cat([k_part, v_part]), ((0, 0), (0, 256 - MLA_KV_LORA), (0, 0)))

    qg, kg = p["mla_q_norm"], p["mla_k_norm"]
    z1 = lambda n: jnp.zeros((n_layers, n), F32)
    tile4 = lambda v: jnp.tile(v, (1, 4))
    gains = cat([
        tile4(p["na_q_norm"]), tile4(p["na_k_norm"]),
        p["mla_cq_norm"],
        p["mla_ckv_norm"], z1(256 - MLA_KV_LORA),
        tile4(cat([qg * MLA_LOGIT_SCALE, z1(LANES - MLA_NOPE - MLA_ROPE)])),
        tile4(cat([kg[:, :MLA_NOPE], z1(LANES - MLA_NOPE)])),
        z1(MLA_NOPE), kg[:, MLA_NOPE:], z1(LANES - MLA_NOPE - MLA_ROPE),
        p["sg_v_norm"],
        tile4(p["gqa_q_norm"] * GQA_LOGIT_SCALE), tile4(p["gqa_k_norm"]),
    ]).reshape(n_layers, 1, G_TOT)

    invq = np.tile(np.concatenate([np.full(MLA_NOPE, 1.0 / MLA_NOPE), np.full(MLA_ROPE, 1.0 / MLA_ROPE),
                                   np.ones(LANES - MLA_NOPE - MLA_ROPE)]), 4).astype(np.float32).reshape(1, 512)
    sgb = jnp.repeat(jnp.swapaxes(p["sg_b_s"], -1, -2), SG_WIDTH // SG_GROUPS, axis=-1)

    return {
        "w_in": w_in_arr, "w_uq": w_uq.astype(BF16), "w_ukv": w_ukv.astype(BF16), "gains": gains,
        "s64": _block_diag_ones(MXU_DIM, [(64 * i, 64 * i + 64) for i in range(4)]),
        "smq": _block_diag_ones(MXU_DIM, [(0, 64), (64, 96), (128, 192), (192, 224)]),
        "invq": jnp.asarray(invq),
        "sgw": p["sg_w_s"].astype(BF16), "sgb": sgb,
        "w_gate": p["w_gate"].astype(BF16), "b_gate": p["b_gate"].reshape(n_layers, N_BRANCH, 1, d),
        "w_branch": p["w_branch"].astype(BF16), "w_out": p["w_out"].astype(BF16),
        "w_up": p["w_up"].astype(BF16), "conv_w": p["conv_w"], "conv_b": p["conv_b"].reshape(n_layers, 1, -1),
        "w_down": p["w_down"].astype(BF16),
    }


def kernel(x, c, ctx, c_ctx, w_ada, b_ada, w_in, na_q_norm, na_k_norm, na_rpb, mla_cq_norm, mla_ckv_norm,
           mla_w_uq, mla_w_ukv, mla_q_norm, mla_k_norm, sg_v_norm, sg_w_s, sg_b_s, gqa_q_norm, gqa_k_norm,
           w_branch, w_gate, b_gate, w_out, w_up, conv_w, conv_b, w_down):
    b, t, d = x.shape
    n_ctx = ctx.shape[1]
    depth = w_in.shape[0]
    ctx_row = b
    cvec = jnp.zeros((8, d), F32).at[:b].set(c).at[ctx_row].set(c_ctx)
    mod_all = _ada_call(cvec, w_ada, b_ada).reshape(depth, 8, 6, d)

    assert b < 8 and t % (4 * GRID_W) == 0 and n_ctx % SG_CHUNK == 0 and n_ctx >= SEED_KEYS
    tm = min(TOKEN_TILE, t)
    tq = min(QUERY_BLOCK, t)
    tk = min(KEY_CHUNK, t)
    assert t % tm == 0 and t % tq == 0 and t % tk == 0
    tabs_lat = _rope_tables(t, identity=False)
    tabs_ctx = _rope_tables(n_ctx, identity=True)
    na_bias = _na_bias(na_rpb, t // (4 * GRID_W))

    lw = _arrange({
        "w_in": w_in, "na_q_norm": na_q_norm, "na_k_norm": na_k_norm,
        "mla_cq_norm": mla_cq_norm, "mla_ckv_norm": mla_ckv_norm, "mla_w_uq": mla_w_uq,
        "mla_w_ukv": mla_w_ukv, "mla_q_norm": mla_q_norm, "mla_k_norm": mla_k_norm,
        "sg_v_norm": sg_v_norm, "sg_w_s": sg_w_s, "sg_b_s": sg_b_s,
        "gqa_q_norm": gqa_q_norm, "gqa_k_norm": gqa_k_norm,
        "w_branch": w_branch, "w_gate": w_gate, "b_gate": b_gate, "w_out": w_out,
        "w_up": w_up, "conv_w": conv_w, "conv_b": conv_b, "w_down": w_down})
    cx = ctx
    for l in range(depth):
        lw = dict(lw, layer=l)
        mod = mod_all[l]
        pc = _in_call(cx, mod, ctx_row, lw, tabs_ctx, n_ctx)
        pz = _in_call(x, mod, None, lw, tabs_lat, tm)

        o_na = _na_call(pz["na_q"], pz["na_k"], pz["na_v"], pc["na_k"], pc["na_v"], na_bias, l)
        o_mla = _flash_call(pz["mla_q"], pz["mla_k"], pz["mla_vt"], pc["mla_k"], pc["mla_vt"], tq, tk)
        o_gqa = _flash_call(pz["gqa_q"], pz["gqa_k"], pz["gqa_vt"], pc["gqa_k"], pc["gqa_vt"], tq, tk)
        x = _merge_call(x, mod, None, (o_na, o_mla, pz["sg_o"], o_gqa), lw, tm)
        x = _ffn_call(x, mod, None, lw, tm)

        if l < depth - 1:
            oc_na = _na_ctx_call(pc["na_q"], pc["na_k"], pc["na_v"])
            oc_mla = _flash_call(pc["mla_q"], pc["mla_k"], pc["mla_vt"], None, None, n_ctx, n_ctx)
            oc_gqa = _flash_call(pc["gqa_q"], pc["gqa_k"], pc["gqa_vt"], None, None, n_ctx, n_ctx)
            cx = _merge_call(cx, mod, ctx_row, (oc_na, oc_mla, pc["sg_o"], oc_gqa), lw, n_ctx)
            cx = _ffn_call(cx, mod, ctx_row, lw, n_ctx)
    return x
```

```python
import functools
import math

import numpy as np
import jax
import jax.numpy as jnp
from jax import lax
from jax.experimental import pallas as pl
from jax.experimental.pallas import tpu as pltpu

F32 = jnp.float32
BF16 = jnp.bfloat16

D_MODEL = 1024
GRID_W = 64
HEAD_DIM = 64
NA_HEADS = 4
NA_ROWS = 8
NA_COLS = 16
MLA_HEADS = 4
MLA_Q_LORA = 256
MLA_KV_LORA = 192
MLA_NOPE = 64
MLA_ROPE = 32
MLA_V = 64
SG_GROUPS = 4
SG_CHUNK = 128
SG_WIDTH = 256
GQA_HEADS = 4
GQA_KV_HEADS = 2
N_BRANCH = 4
BRANCH_WIDTH = 256
FFN_DIM = 2816
ROPE_THETA = 10000.0
EPS = 1e-6
NEG_INF = -1e30

NA_IN = 3 * NA_HEADS * HEAD_DIM
MLA_IN = MLA_Q_LORA + MLA_KV_LORA + MLA_ROPE
SG_IN = 2 * SG_WIDTH
GQA_IN = (GQA_HEADS + 2 * GQA_KV_HEADS) * HEAD_DIM

LANES = 128
MXU_DIM = 256
BF16_SUBLANES = 16
V_AUG = HEAD_DIM + BF16_SUBLANES
LOG2E = math.log2(math.e)
MLA_LOGIT_SCALE = (MLA_NOPE + MLA_ROPE) ** -0.5 * LOG2E
GQA_LOGIT_SCALE = HEAD_DIM ** -0.5 * LOG2E

C_NA = 0
C_MLA = C_NA + NA_IN
C_SG = C_MLA + 640
C_GQA = C_SG + SG_IN
IN_ARR = C_GQA + 640

G_NA, G_CQ, G_CKV, G_MQ, G_MKN, G_MKR, G_SG, G_GQA, G_TOT = 0, 512, 768, 1024, 1536, 2048, 2176, 2432, 2944

FFN_CHUNK = 256
N_FFN_CHUNKS = FFN_DIM // FFN_CHUNK
FFN_LOOKAHEAD = 2
HALO = BF16_SUBLANES


def _const_spec(shape):
    nd = len(shape)
    return pl.BlockSpec(shape, lambda *_: (0,) * nd, pipeline_mode=pl.Buffered(1))


def _layer_spec(shape, layer):
    nd = len(shape)
    return pl.BlockSpec((None,) + tuple(shape), lambda *_: (layer,) + (0,) * nd, pipeline_mode=pl.Buffered(1))


def _params(vmem_mb, n_grid):
    return pltpu.CompilerParams(dimension_semantics=("arbitrary",) * n_grid,
                                vmem_limit_bytes=vmem_mb * 1024 * 1024)


def _dot(a, b):
    return jnp.dot(a, b, preferred_element_type=F32)


def _dot_nt(a, b):
    return lax.dot_general(a, b, (((1,), (1,)), ((), ())), preferred_element_type=F32)


def _rms(xf):
    return xf * lax.rsqrt(jnp.mean(xf * xf, axis=-1, keepdims=True) + EPS)


def _segsum(x2, s_ref):
    hi = x2.astype(BF16)
    lo = (x2 - hi.astype(F32)).astype(BF16)
    s = s_ref[...]
    parts = []
    for j in range(x2.shape[1] // MXU_DIM):
        sl = slice(MXU_DIM * j, MXU_DIM * (j + 1))
        parts.append(_dot(hi[:, sl], s) + _dot(lo[:, sl], s))
    return parts[0] if len(parts) == 1 else jnp.concatenate(parts, axis=1)


def _rope(x, cos, sin_signed, shift, first_half):
    w = x.shape[1]
    partner = jnp.where(first_half, pltpu.roll(x, w - shift, 1), pltpu.roll(x, shift, 1))
    return x * cos + partner * sin_signed


def _ada_kernel(c_ref, w_ref, b_ref, o_ref):
    c = c_ref[...]
    a = (c * jax.nn.sigmoid(c)).astype(BF16)
    o_ref[...] = _dot(a, w_ref[...].astype(BF16)) + b_ref[...]


def _ada_call(cvec, w_ada, b_ada):
    n_layers, d, n = w_ada.shape
    tn = 1536
    return pl.pallas_call(
        _ada_kernel,
        grid=(n_layers, n // tn),
        in_specs=[pl.BlockSpec((8, d), lambda l, j: (0, 0)),
                  pl.BlockSpec((None, d, tn), lambda l, j: (l, 0, j)),
                  pl.BlockSpec((None, 1, tn), lambda l, j: (l, 0, j))],
        out_specs=pl.BlockSpec((None, 8, tn), lambda l, j: (l, 0, j)),
        out_shape=jax.ShapeDtypeStruct((n_layers, 8, n), F32),
        compiler_params=_params(32, 2),
        name="ada",
    )(cvec, w_ada, b_ada.reshape(n_layers, 1, n))


def _in_kernel(x_ref, mod_ref, w_in_ref, w_uq_ref, w_ukv_ref, g_ref, s64_ref, smq_ref, invq_ref,
               gc_ref, gs_ref, mc_ref, ms_ref, sgw_ref, sgb_ref,
               naq_ref, nak_ref, nav_ref, mq_ref, mk_ref, mvt_ref, sgo_ref, gq_ref, gk_ref, gvt_ref):
    tm = x_ref.shape[0]
    x = x_ref[...]
    h = (_rms(x) * (1.0 + mod_ref[1:2, :]) + mod_ref[0:1, :]).astype(BF16)

    def gain(off, width):
        return g_ref[:, off:off + width]

    inv_head = 1.0 / HEAD_DIM

    z = _dot(h, w_in_ref[:, C_NA:C_NA + NA_IN])
    qk = z[:, 0:512]
    qk = qk * lax.rsqrt(_segsum(qk * qk, s64_ref) * inv_head + EPS) * gain(G_NA, 512)
    naq_ref[...] = qk[:, 0:256].astype(BF16)
    nak_ref[...] = qk[:, 256:512].astype(BF16)
    nav_ref[...] = z[:, 512:768].astype(BF16)

    z = _dot(h, w_in_ref[:, C_MLA:C_MLA + 640])
    cq, ckv, kr = z[:, 0:256], z[:, 256:512], z[:, 512:640]
    cq = cq * lax.rsqrt(jnp.mean(cq * cq, axis=-1, keepdims=True) + EPS) * gain(G_CQ, 256)
    ckv = ckv * lax.rsqrt(jnp.sum(ckv * ckv, axis=-1, keepdims=True) * (1.0 / MLA_KV_LORA) + EPS) * gain(G_CKV, 256)
    q = _dot(cq.astype(BF16), w_uq_ref[...])
    q = q * lax.rsqrt(_segsum(q * q, smq_ref) * invq_ref[...] + EPS) * gain(G_MQ, 512)
    kv = _dot(ckv.astype(BF16), w_ukv_ref[...])
    kn = kv[:, 0:512]
    kn = kn * lax.rsqrt(_segsum(kn * kn, s64_ref) * inv_head + EPS) * gain(G_MKN, 512)
    kr = kr * lax.rsqrt(jnp.sum(kr * kr, axis=-1, keepdims=True) * (1.0 / MLA_ROPE) + EPS) * gain(G_MKR, 128)
    lane = lax.broadcasted_iota(jnp.int32, (tm, LANES), 1)
    first8 = (lane & 15) < 8
    mc, ms = mc_ref[...], ms_ref[...]
    kr = _rope(kr, mc, ms, 8, first8)
    for hh in range(MLA_HEADS):
        sl = slice(LANES * hh, LANES * (hh + 1))
        mq_ref[hh] = _rope(q[:, sl], mc, ms, 8, first8).T.astype(BF16)
        mk_ref[hh] = (kn[:, sl] + kr).astype(BF16)
    vt = kv[:, 512:768].T
    ones = jnp.ones((BF16_SUBLANES, tm), BF16)
    for hh in range(MLA_HEADS):
        mvt_ref[hh, 0:HEAD_DIM, :] = vt[HEAD_DIM * hh:HEAD_DIM * (hh + 1), :].astype(BF16)
        mvt_ref[hh, HEAD_DIM:V_AUG, :] = ones

    z = _dot(h, w_in_ref[:, C_SG:C_SG + SG_IN])
    uv = jax.nn.gelu(z)
    u, v = uv[:, 0:SG_WIDTH], uv[:, SG_WIDTH:]
    v = v * lax.rsqrt(jnp.mean(v * v, axis=-1, keepdims=True) + EPS) * gain(G_SG, 256)
    grp = lax.broadcasted_iota(jnp.int32, (SG_CHUNK, SG_WIDTH), 1) >> 6
    for c in range(tm // SG_CHUNK):
        rows = slice(SG_CHUNK * c, SG_CHUNK * (c + 1))
        vc = v[rows, :]
        mixed = sgb_ref[...]
        for gi in range(SG_GROUPS):
            mixed = mixed + _dot(sgw_ref[gi], jnp.where(grp == gi, vc, 0.0).astype(BF16))
        sgo_ref[rows, :] = (u[rows, :] * mixed).astype(BF16)

    z = _dot(h, w_in_ref[:, C_GQA:C_GQA + 640])
    qk = z[:, 0:512]
    qk = qk * lax.rsqrt(_segsum(qk * qk, s64_ref) * inv_head + EPS) * gain(G_GQA, 512)
    lane2 = lax.broadcasted_iota(jnp.int32, (tm, 2 * LANES), 1)
    first16 = (lane2 & 31) < 16
    gc, gs = gc_ref[...], gs_ref[...]
    qn = _rope(qk[:, 0:256], gc, gs, 16, first16)
    kn = _rope(qk[:, 256:512], gc, gs, 16, first16)
    half = lane >> 6
    for g in range(GQA_KV_HEADS):
        sl = slice(LANES * g, LANES * (g + 1))
        for r in range(GQA_HEADS // GQA_KV_HEADS):
            gq_ref[2 * g + r] = jnp.where(half == r, qn[:, sl], 0.0).T.astype(BF16)
        gk_ref[g] = kn[:, sl].astype(BF16)
    vt = z[:, 512:640].T
    for g in range(GQA_KV_HEADS):
        gvt_ref[g, 0:HEAD_DIM, :] = vt[HEAD_DIM * g:HEAD_DIM * (g + 1), :].astype(BF16)
        gvt_ref[g, HEAD_DIM:V_AUG, :] = ones


def _in_call(x, mod, mod_row, lw, tabs, tm):
    b, t, d = x.shape
    nt = t // tm
    if mod_row is None:
        mod_map = lambda i, j: (j, 0, 0)
    else:
        mod_map = lambda i, j: (mod_row, 0, 0)
    tok = lambda i, j: (j, i, 0)
    tab = lambda i, j: (i, 0)
    head_tok = lambda i, j: (j, 0, i, 0)
    head_t = lambda i, j: (j, 0, 0, i)
    in_specs = [
        pl.BlockSpec((None, tm, d), tok),
        pl.BlockSpec((None, 6, d), mod_map),
        _layer_spec((d, IN_ARR), lw["layer"]),
        _layer_spec((MLA_Q_LORA, 512), lw["layer"]),
        _layer_spec((256, 768), lw["layer"]),
        _layer_spec((1, G_TOT), lw["layer"]),
        _const_spec((MXU_DIM, MXU_DIM)),
        _const_spec((MXU_DIM, MXU_DIM)),
        _const_spec((1, 512)),
        pl.BlockSpec((tm, 256), tab), pl.BlockSpec((tm, 256), tab),
        pl.BlockSpec((tm, LANES), tab), pl.BlockSpec((tm, LANES), tab),
        _layer_spec((SG_GROUPS, SG_CHUNK, SG_CHUNK), lw["layer"]),
        _layer_spec((SG_CHUNK, SG_WIDTH), lw["layer"]),
    ]
    out_specs = [
        pl.BlockSpec((None, tm, 256), tok), pl.BlockSpec((None, tm, 256), tok), pl.BlockSpec((None, tm, 256), tok),
        pl.BlockSpec((None, MLA_HEADS, LANES, tm), head_t),
        pl.BlockSpec((None, MLA_HEADS, tm, LANES), head_tok),
        pl.BlockSpec((None, MLA_HEADS, V_AUG, tm), head_t),
        pl.BlockSpec((None, tm, 256), tok),
        pl.BlockSpec((None, GQA_HEADS, LANES, tm), head_t),
        pl.BlockSpec((None, GQA_KV_HEADS, tm, LANES), head_tok),
        pl.BlockSpec((None, GQA_KV_HEADS, V_AUG, tm), head_t),
    ]
    sds = jax.ShapeDtypeStruct
    out_shape = [
        sds((b, t, 256), BF16), sds((b, t, 256), BF16), sds((b, t, 256), BF16),
        sds((b, MLA_HEADS, LANES, t), BF16), sds((b, MLA_HEADS, t, LANES), BF16),
        sds((b, MLA_HEADS, V_AUG, t), BF16),
        sds((b, t, 256), BF16),
        sds((b, GQA_HEADS, LANES, t), BF16), sds((b, GQA_KV_HEADS, t, LANES), BF16),
        sds((b, GQA_KV_HEADS, V_AUG, t), BF16),
    ]
    outs = pl.pallas_call(
        _in_kernel, grid=(nt, b), in_specs=in_specs, out_specs=out_specs, out_shape=out_shape,
        compiler_params=_params(48, 2), name="in_proj",
    )(x, mod, lw["w_in"], lw["w_uq"], lw["w_ukv"], lw["gains"], lw["s64"], lw["smq"], lw["invq"],
      tabs["gc"], tabs["gs"], tabs["mc"], tabs["ms"], lw["sgw"], lw["sgb"])
    names = ("na_q", "na_k", "na_v", "mla_q", "mla_k", "mla_vt", "sg_o", "gqa_q", "gqa_k", "gqa_vt")
    return dict(zip(names, outs))


def _na_heads(q, k_blocks, v_blocks, biases, seeded):
    tq = q.shape[0]
    head_of_lane = lax.broadcasted_iota(jnp.int32, (tq, 256), 1) >> 6
    qf = q.astype(F32) * (HEAD_DIM ** -0.5)
    qhs = [jnp.where(head_of_lane == hh, qf, 0.0).astype(BF16) for hh in range(NA_HEADS)]
    vts = [vb.astype(F32).T.astype(BF16) for vb in v_blocks]
    units = [(hh, j) for hh in range(NA_HEADS) for j in range(len(k_blocks))]

    def score(u):
        hh, j = u
        s = _dot_nt(k_blocks[j], qhs[hh])
        return s if biases[j] is None else s + biases[j][hh]

    def value_rows(hh, j):
        ones = jnp.ones((BF16_SUBLANES, vts[j].shape[1]), BF16)
        return jnp.concatenate([vts[j][HEAD_DIM * hh:HEAD_DIM * (hh + 1), :], ones], axis=0)

    lookahead = SCORE_LOOKAHEAD if seeded else len(k_blocks)
    pending = [score(u) for u in units[:lookahead]]
    accs, m = [], None
    for n, (hh, j) in enumerate(units):
        if j == 0:
            acc = jnp.zeros((V_AUG, tq), F32)
            head = [pending.pop(0) for _ in range(1 if seeded else len(k_blocks))]
            m = _col_max(head[0])
            for s in head[1:]:
                m = jnp.maximum(m, _col_max(s))
        else:
            head = head[1:] if not seeded else [pending.pop(0)]
        if n + lookahead < len(units):
            pending.append(score(units[n + lookahead]))
        acc = acc + _dot(value_rows(hh, j), jnp.exp(head[0] - m).astype(BF16))
        if j == len(k_blocks) - 1:
            accs.append(acc)
    out = jnp.concatenate([a[0:HEAD_DIM] * (1.0 / a[HEAD_DIM:HEAD_DIM + 1]) for a in accs], axis=0).T
    return out, accs


def _na_kernel(q_ref, kp_ref, kc_ref, kn_ref, vp_ref, vc_ref, vn_ref, kx_ref, vx_ref, bias_ref, o_ref):
    band = lambda refs: jnp.concatenate([r[...] for r in refs], axis=0)
    operands = lambda: (q_ref[...], [kx_ref[...], band((kp_ref, kc_ref, kn_ref))],
                        [vx_ref[...], band((vp_ref, vc_ref, vn_ref))], [None, bias_ref])
    out, accs = _na_heads(*operands(), seeded=True)
    o_ref[...] = out.astype(BF16)
    unusable = jnp.zeros((V_AUG, q_ref.shape[0]), F32)
    for acc in accs:
        unusable = jnp.maximum(unusable, jnp.where(jnp.abs(acc) < F32_SAFE_MAX, 0.0, 1.0))

    @pl.when(jnp.max(unusable) > 0.0)
    def _():
        o_ref[...] = _na_heads(*operands(), seeded=False)[0].astype(BF16)


def _na_call(q, k, v, kx, vx, bias, layer):
    b, t, w = q.shape
    c = kx.shape[1]
    tq = 4 * GRID_W
    nt = t // tq
    cur = lambda bi, i: (bi, i, 0)
    prev = lambda bi, i: (bi, jnp.maximum(i - 1, 0), 0)
    nxt = lambda bi, i: (bi, jnp.minimum(i + 1, nt - 1), 0)
    ctx = lambda bi, i: (bi, 0, 0)
    variant = lambda bi, i: (layer, jnp.where(i == 0, 0, jnp.where(i == nt - 1, 2, 1)), 0, 0, 0)
    blk = lambda m: pl.BlockSpec((None, tq, w), m)
    return pl.pallas_call(
        _na_kernel, grid=(b, nt),
        in_specs=[blk(cur), blk(prev), blk(cur), blk(nxt), blk(prev), blk(cur), blk(nxt),
                  pl.BlockSpec((None, c, w), ctx), pl.BlockSpec((None, c, w), ctx),
                  pl.BlockSpec((None, None, NA_HEADS, 3 * tq, tq), variant)],
        out_specs=blk(cur),
        out_shape=jax.ShapeDtypeStruct((b, t, w), BF16),
        compiler_params=_params(48, 2), name="na_attn",
    )(q, k, k, k, v, v, v, kx, vx, bias)


def _na_ctx_kernel(q_ref, k_ref, v_ref, o_ref):
    o_ref[...] = _na_heads(q_ref[...], [k_ref[...]], [v_ref[...]], [None], seeded=False)[0].astype(BF16)


def _na_ctx_call(q, k, v):
    b, c, w = q.shape
    spec = pl.BlockSpec((None, c, w), lambda bi: (bi, 0, 0))
    return pl.pallas_call(
        _na_ctx_kernel, grid=(b,), in_specs=[spec, spec, spec], out_specs=spec,
        out_shape=jax.ShapeDtypeStruct((b, c, w), BF16),
        compiler_params=_params(32, 1), name="na_ctx_attn",
    )(q, k, v)


def _na_bias(rpb, nt):
    rows = 4 * nt
    col = np.arange(GRID_W)
    c_start = np.clip(col - NA_COLS // 2, 0, GRID_W - NA_COLS)
    valid_c = (col[None, :] >= c_start[:, None]) & (col[None, :] < c_start[:, None] + NA_COLS)
    dc = np.clip(col[None, :] - col[:, None] + (NA_COLS - 1), 0, 2 * NA_COLS - 2)
    pick_dr, ok = [], []
    for i in (0, min(1, nt - 1), nt - 1):
        rq = 4 * i + np.arange(4)
        start = np.clip(rq - NA_ROWS // 2, 0, rows - NA_ROWS)
        blocks = np.array([i - 1, i, i + 1])
        rk = (4 * blocks[:, None] + np.arange(4)[None, :]).reshape(-1)
        block_ok = np.repeat((blocks >= 0) & (blocks < nt), 4)
        valid_r = (rk[None, :] >= start[:, None]) & (rk[None, :] < start[:, None] + NA_ROWS) & block_ok[None, :]
        dr = np.clip(rk[None, :] - rq[:, None] + (NA_ROWS - 1), 0, 2 * NA_ROWS - 2)
        pick_dr.append(dr[:, :, None] == np.arange(2 * NA_ROWS - 1))
        ok.append(valid_r[:, None, :, None] & valid_c[None, :, None, :])
    pick_dc = jnp.asarray(dc[:, :, None] == np.arange(2 * NA_COLS - 1), F32)
    by_col = jnp.einsum("lhrd,qkd->lhrqk", rpb, pick_dc, precision=lax.Precision.HIGHEST)
    vals = jnp.einsum("vabr,lhrqk->lvhbkaq", jnp.asarray(np.stack(pick_dr), F32), by_col,
                      precision=lax.Precision.HIGHEST)
    tile_q, tile_k = 4 * GRID_W, 12 * GRID_W
    mask = jnp.asarray(np.stack(ok).transpose(0, 3, 4, 1, 2).reshape(1, 3, 1, tile_k, tile_q))
    return jnp.where(mask, vals.reshape(rpb.shape[0], 3, NA_HEADS, tile_k, tile_q), NEG_INF)


TOKEN_TILE = 512
QUERY_BLOCK = 512
KEY_CHUNK = 512
FLASH_COL = MXU_DIM
STREAM_UNROLL = 16
SCORE_LOOKAHEAD = 2
SEED_KEYS = 64
F32_SAFE_MAX = 3e38
MIN_DENOMINATOR = 2.0 ** -40


def _col_max(s):
    keys = s.shape[0]
    if keys % 64 == 0 and keys > 64:
        s = jnp.max(s.reshape(keys // 64, 64, s.shape[1]), axis=0)
    return jnp.max(s, axis=0, keepdims=True)


def _flash_kernel(*refs, n_heads, group, tq, tk, has_ctx):
    if has_ctx:
        q_ref, k_ref, vt_ref, kx_ref, vtx_ref, o_ref, acc_scr, ot_scr = refs
    else:
        q_ref, k_ref, vt_ref, o_ref, acc_scr, ot_scr = refs
    n_sub = q_ref.shape[2] // tq
    n_chunks = k_ref.shape[1] // tk
    n_slabs = n_heads // group
    cols = [(hh, qi) for hh in range(n_heads) for qi in range(n_sub)]
    heads = range(len(cols))
    slab_of = [hh // group for hh, _ in cols]

    def q_col(c):
        hh, qi = cols[c]
        return q_ref[hh, :, qi * tq:(qi + 1) * tq]

    def load_chunk(j):
        off = pl.multiple_of(j * tk, tk)
        return ([k_ref[sl, pl.ds(off, tk), :] for sl in range(n_slabs)],
                [vt_ref[sl, :, pl.ds(off, tk)] for sl in range(n_slabs)])

    def pv(vt, p):
        return _dot(vt, p.astype(BF16))

    def exact_step(ks, vts, ms):
        new_ms = []
        for hh in heads:
            s = _dot(ks[slab_of[hh]], q_col(hh))
            m_new = jnp.maximum(ms[hh], _col_max(s))
            p = jnp.exp2(s - m_new)
            acc_scr[hh] = acc_scr[hh] * jnp.exp2(ms[hh] - m_new) + pv(vts[slab_of[hh]], p)
            new_ms.append(m_new)
        return tuple(new_ms)

    def stream(chunks, ms):
        units = [(i, hh) for i in range(len(chunks)) for hh in heads]
        score = lambda u: _dot(chunks[u[0]][0][slab_of[u[1]]], q_col(u[1]))
        pending = [score(u) for u in units[:SCORE_LOOKAHEAD]]
        for n, (i, hh) in enumerate(units):
            s = pending.pop(0)
            if n + SCORE_LOOKAHEAD < len(units):
                pending.append(score(units[n + SCORE_LOOKAHEAD]))
            acc_scr[hh] += pv(chunks[i][1][slab_of[hh]], jnp.exp2(s - ms[hh]))

    ctx_chunk = ([kx_ref[sl] for sl in range(n_slabs)], [vtx_ref[sl] for sl in range(n_slabs)]) if has_ctx else None
    seed_ref = kx_ref if has_ctx else k_ref
    ms = tuple(_col_max(_dot(seed_ref[slab_of[hh], 0:SEED_KEYS, :], q_col(hh))) for hh in heads)
    acc_scr[...] = jnp.zeros_like(acc_scr)
    if has_ctx:
        stream([ctx_chunk], ms)
    per_iter = max(u for u in (STREAM_UNROLL, 8, 4, 2, 1) if n_chunks % u == 0)

    def stream_body(j, carry):
        stream([load_chunk(j * per_iter + i) for i in range(per_iter)], ms)
        return carry

    lax.fori_loop(0, n_chunks // per_iter, stream_body, 0)
    unusable = jnp.zeros((V_AUG, tq), F32)
    for hh in heads:
        acc = acc_scr[hh]
        unusable = jnp.maximum(unusable, jnp.where(jnp.abs(acc) < F32_SAFE_MAX, 0.0, 1.0))
        unusable = jnp.maximum(unusable, jnp.where(acc[HEAD_DIM:HEAD_DIM + 1] > MIN_DENOMINATOR, 0.0, 1.0))

    @pl.when(jnp.max(unusable) > 0.0)
    def _():
        acc_scr[...] = jnp.zeros_like(acc_scr)
        ms = tuple(jnp.full((1, tq), NEG_INF, F32) for _ in heads)
        if has_ctx:
            ms = exact_step(*ctx_chunk, ms)
        lax.fori_loop(0, n_chunks, lambda j, m: exact_step(*load_chunk(j), m), ms)

    for c, (hh, qi) in enumerate(cols):
        acc = acc_scr[c]
        ot_scr[HEAD_DIM * hh:HEAD_DIM * (hh + 1), qi * tq:(qi + 1) * tq] = (
            acc[0:HEAD_DIM] * (1.0 / acc[HEAD_DIM:HEAD_DIM + 1]))
    o_ref[...] = ot_scr[...].T.astype(BF16)


def _flash_call(q, k, vt, kx, vtx, tq, tk):
    b, n_heads, w, t_q = q.shape
    slabs, t_k = k.shape[1], k.shape[2]
    has_ctx = kx is not None
    full4 = lambda bi, i: (bi, 0, 0, 0)
    in_specs = [pl.BlockSpec((None, n_heads, w, tq), lambda bi, i: (bi, 0, 0, i)),
                pl.BlockSpec((None, slabs, t_k, w), full4),
                pl.BlockSpec((None, slabs, V_AUG, t_k), full4)]
    args = [q, k, vt]
    if has_ctx:
        c = kx.shape[2]
        in_specs += [pl.BlockSpec((None, slabs, c, w), full4), pl.BlockSpec((None, slabs, V_AUG, c), full4)]
        args += [kx, vtx]
    col = min(FLASH_COL, tq)
    kern = functools.partial(_flash_kernel, n_heads=n_heads, group=n_heads // slabs, tq=col, tk=tk,
                             has_ctx=has_ctx)
    return pl.pallas_call(
        kern, grid=(b, t_q // tq), in_specs=in_specs,
        out_specs=pl.BlockSpec((None, tq, n_heads * HEAD_DIM), lambda bi, i: (bi, i, 0)),
        out_shape=jax.ShapeDtypeStruct((b, t_q, n_heads * HEAD_DIM), BF16),
        scratch_shapes=[pltpu.VMEM((n_heads * (tq // col), V_AUG, col), F32),
                        pltpu.VMEM((n_heads * HEAD_DIM, tq), F32)],
        compiler_params=_params(56, 2), name="flash",
    )(*args)


def _merge_kernel(x_ref, mod_ref, o0_ref, o1_ref, o2_ref, o3_ref, wg_ref, bg_ref, wb_ref, wo_ref, out_ref):
    x = x_ref[...]
    h = (_rms(x) * (1.0 + mod_ref[1:2, :]) + mod_ref[0:1, :]).astype(BF16)
    y = None
    for i, o_ref in enumerate((o0_ref, o1_ref, o2_ref, o3_ref)):
        gate = jax.nn.sigmoid(_dot(h, wg_ref[i]) + bg_ref[i])
        term = gate * _dot(o_ref[...], wb_ref[i])
        y = term if y is None else y + term
    out_ref[...] = x + mod_ref[2:3, :] * _dot(y.astype(BF16), wo_ref[...])


def _merge_call(x, mod, mod_row, branches, lw, tm):
    b, t, d = x.shape
    if mod_row is None:
        mod_map = lambda i, j: (j, 0, 0)
    else:
        mod_map = lambda i, j: (mod_row, 0, 0)
    tok = lambda i, j: (j, i, 0)
    br = pl.BlockSpec((None, tm, BRANCH_WIDTH), tok)
    return pl.pallas_call(
        _merge_kernel, grid=(t // tm, b),
        in_specs=[pl.BlockSpec((None, tm, d), tok), pl.BlockSpec((None, 6, d), mod_map), br, br, br, br,
                  _layer_spec((N_BRANCH, d, d), lw["layer"]), _layer_spec((N_BRANCH, 1, d), lw["layer"]),
                  _layer_spec((N_BRANCH, BRANCH_WIDTH, d), lw["layer"]), _layer_spec((d, d), lw["layer"])],
        out_specs=pl.BlockSpec((None, tm, d), tok),
        out_shape=jax.ShapeDtypeStruct((b, t, d), F32),
        compiler_params=_params(56, 2), name="merge",
    )(x, mod, *branches, lw["w_gate"], lw["b_gate"], lw["w_branch"], lw["w_out"])


def _ffn_kernel(x_ref, xp_ref, xn_ref, mod_ref, wu_ref, cw_ref, cb_ref, wd_ref, out_ref, h_scr, act_scr):
    tm = x_ref.shape[0]
    i = pl.program_id(0)
    nt = pl.num_programs(0)
    sh, sc = mod_ref[3:4, :], mod_ref[4:5, :]

    def modulated(v):
        return _rms(v) * (1.0 + sc) + sh

    x = x_ref[...]
    h_scr[0:HALO, :] = jnp.where(i > 0, modulated(xp_ref[...]), 0.0).astype(BF16)
    h_scr[HALO:HALO + tm, :] = modulated(x).astype(BF16)
    h_scr[HALO + tm:, :] = jnp.where(i < nt - 1, modulated(xn_ref[...]), 0.0).astype(BF16)
    rows = tm + 2 * HALO
    main = slice(HALO, HALO + tm)

    def cols(j, gate):
        lo = gate * FFN_DIM + FFN_CHUNK * j
        return slice(lo, lo + FFN_CHUNK)

    def up_proj(j):
        return tuple(_dot(h_scr[...], wu_ref[:, cols(j, gate)]) for gate in (0, 1))

    def conv(up, sl):
        return (pltpu.roll(up, 1, 0)[main] * cw_ref[0:1, sl] + up[main] * cw_ref[1:2, sl]
                + pltpu.roll(up, rows - 1, 0)[main] * cw_ref[2:3, sl] + cb_ref[:, sl])

    ups = [up_proj(j) for j in range(FFN_LOOKAHEAD)]
    for j in range(N_FFN_CHUNKS):
        up_a, up_g = ups.pop(0)
        if j + FFN_LOOKAHEAD < N_FFN_CHUNKS:
            ups.append(up_proj(j + FFN_LOOKAHEAD))
        a, g = conv(up_a, cols(j, 0)), conv(up_g, cols(j, 1))
        act_scr[:, cols(j, 0)] = (g * jax.nn.sigmoid(g) * a).astype(BF16)
    out_ref[...] = x + mod_ref[5:6, :] * _dot(act_scr[...], wd_ref[...])


def _ffn_call(x, mod, mod_row, lw, tm):
    b, t, d = x.shape
    nt = t // tm
    hb = tm // HALO
    last_halo = t // HALO - 1
    if mod_row is None:
        mod_map = lambda i, j: (j, 0, 0)
    else:
        mod_map = lambda i, j: (mod_row, 0, 0)
    tok = lambda i, j: (j, i, 0)
    return pl.pallas_call(
        _ffn_kernel, grid=(nt, b),
        in_specs=[pl.BlockSpec((None, tm, d), tok),
                  pl.BlockSpec((None, HALO, d), lambda i, j: (j, jnp.maximum(i * hb - 1, 0), 0)),
                  pl.BlockSpec((None, HALO, d), lambda i, j: (j, jnp.minimum((i + 1) * hb, last_halo), 0)),
                  pl.BlockSpec((None, 6, d), mod_map),
                  _layer_spec((d, 2 * FFN_DIM), lw["layer"]),
                  _layer_spec((3, 2 * FFN_DIM), lw["layer"]),
                  _layer_spec((1, 2 * FFN_DIM), lw["layer"]),
                  _layer_spec((FFN_DIM, d), lw["layer"])],
        out_specs=pl.BlockSpec((None, tm, d), tok),
        out_shape=jax.ShapeDtypeStruct((b, t, d), F32),
        scratch_shapes=[pltpu.VMEM((tm + 2 * HALO, d), BF16), pltpu.VMEM((tm, FFN_DIM), BF16)],
        compiler_params=_params(56, 2), name="ffn",
    )(x, x, x, mod, lw["w_up"], lw["conv_w"], lw["conv_b"], lw["w_down"])


def _block_diag_ones(width, segs):
    m = np.zeros((width, width), np.float32)
    for lo, hi in segs:
        m[lo:hi, lo:hi] = 1.0
    return jnp.asarray(m, BF16)


def _rope_pattern(pos, width):
    half = width // 2
    inv = ROPE_THETA ** (-jnp.arange(half, dtype=F32) / half)
    ang = pos.astype(F32)[:, None] * inv[None, :]
    cos, sin = jnp.cos(ang), jnp.sin(ang)
    return jnp.concatenate([cos, cos], axis=1), jnp.concatenate([-sin, sin], axis=1)


def _rope_tables(t, identity):
    if identity:
        return {"gc": jnp.ones((t, 256), F32), "gs": jnp.zeros((t, 256), F32),
                "mc": jnp.ones((t, LANES), F32), "ms": jnp.zeros((t, LANES), F32)}
    n_rows = t // GRID_W

    def grid_tables(width):
        rc, rs = _rope_pattern(jnp.arange(n_rows), width)
        cc, cs = _rope_pattern(jnp.arange(GRID_W), width)
        expand = lambda r, c: jnp.concatenate([jnp.repeat(r, GRID_W, axis=0), jnp.tile(c, (n_rows, 1))], axis=1)
        return expand(rc, cc), expand(rs, cs)

    gcos, gsin = grid_tables(HEAD_DIM // 2)
    mcos, msin = grid_tables(MLA_ROPE // 2)
    one, zero = jnp.ones((t, MLA_NOPE), F32), jnp.zeros((t, MLA_NOPE), F32)
    pad1, pad0 = jnp.ones((t, LANES - MLA_NOPE - MLA_ROPE), F32), jnp.zeros((t, LANES - MLA_NOPE - MLA_ROPE), F32)
    return {"gc": jnp.tile(gcos, (1, 4)), "gs": jnp.tile(gsin, (1, 4)),
            "mc": jnp.concatenate([one, mcos, pad1], axis=1),
            "ms": jnp.concatenate([zero, msin, pad0], axis=1)}


def _arrange(p):
    d = D_MODEL
    n_layers = p["w_in"].shape[0]
    cat = lambda parts: jnp.concatenate(parts, axis=-1)
    w_in = p["w_in"]
    na, mla = w_in[..., :NA_IN], w_in[..., NA_IN:NA_IN + MLA_IN]
    sg = w_in[..., NA_IN + MLA_IN:NA_IN + MLA_IN + SG_IN]
    gqa = w_in[..., NA_IN + MLA_IN + SG_IN:]
    z = lambda n: jnp.zeros((n_layers, d, n), F32)
    cq, ckv, kr = (mla[..., :MLA_Q_LORA], mla[..., MLA_Q_LORA:MLA_Q_LORA + MLA_KV_LORA],
                   mla[..., MLA_Q_LORA + MLA_KV_LORA:])
    gq, gk, gv = gqa[..., :256], gqa[..., 256:384], gqa[..., 384:]
    gk_dup = cat([gk[..., :64], gk[..., :64], gk[..., 64:], gk[..., 64:]])
    w_in_arr = cat([na, cq, ckv, z(256 - MLA_KV_LORA), z(MLA_NOPE), kr, z(LANES - MLA_NOPE - MLA_ROPE),
                    sg, gq, gk_dup, gv]).astype(BF16)

    lane_pad = lambda v, n: jnp.pad(v, ((0, 0),) * (v.ndim - 1) + ((0, n),))
    w_uq = p["mla_w_uq"].reshape(n_layers, MLA_Q_LORA, MLA_HEADS, MLA_NOPE + MLA_ROPE)
    w_uq = lane_pad(w_uq, LANES - MLA_NOPE - MLA_ROPE).reshape(n_layers, MLA_Q_LORA, MLA_HEADS * LANES)
    w_ukv = p["mla_w_ukv"].reshape(n_layers, MLA_KV_LORA, MLA_HEADS, MLA_NOPE + MLA_V)
    k_part = lane_pad(w_ukv[..., :MLA_NOPE], LANES - MLA_NOPE).reshape(n_layers, MLA_KV_LORA, MLA_HEADS * LANES)
    v_part = w_ukv[..., MLA_NOPE:].reshape(n_layers, MLA_KV_LORA, MLA_HEADS * MLA_V)
    w_ukv = jnp.pad(cat([k_part, v_part]), ((0, 0), (0, 256 - MLA_KV_LORA), (0, 0)))

    qg, kg = p["mla_q_norm"], p["mla_k_norm"]
    z1 = lambda n: jnp.zeros((n_layers, n), F32)
    tile4 = lambda v: jnp.tile(v, (1, 4))
    gains = cat([
        tile4(p["na_q_norm"]), tile4(p["na_k_norm"]),
        p["mla_cq_norm"],
        p["mla_ckv_norm"], z1(256 - MLA_KV_LORA),
        tile4(cat([qg * MLA_LOGIT_SCALE, z1(LANES - MLA_NOPE - MLA_ROPE)])),
        tile4(cat([kg[:, :MLA_NOPE], z1(LANES - MLA_NOPE)])),
        z1(MLA_NOPE), kg[:, MLA_NOPE:], z1(LANES - MLA_NOPE - MLA_ROPE),
        p["sg_v_norm"],
        tile4(p["gqa_q_norm"] * GQA_LOGIT_SCALE), tile4(p["gqa_k_norm"]),
    ]).reshape(n_layers, 1, G_TOT)

    invq = np.tile(np.concatenate([np.full(MLA_NOPE, 1.0 / MLA_NOPE), np.full(MLA_ROPE, 1.0 / MLA_ROPE),
                                   np.ones(LANES - MLA_NOPE - MLA_ROPE)]), 4).astype(np.float32).reshape(1, 512)
    sgb = jnp.repeat(jnp.swapaxes(p["sg_b_s"], -1, -2), SG_WIDTH // SG_GROUPS, axis=-1)

    return {
        "w_in": w_in_arr, "w_uq": w_uq.astype(BF16), "w_ukv": w_ukv.astype(BF16), "gains": gains,
        "s64": _block_diag_ones(MXU_DIM, [(64 * i, 64 * i + 64) for i in range(4)]),
        "smq": _block_diag_ones(MXU_DIM, [(0, 64), (64, 96), (128, 192), (192, 224)]),
        "invq": jnp.asarray(invq),
        "sgw": p["sg_w_s"].astype(BF16), "sgb": sgb,
        "w_gate": p["w_gate"].astype(BF16), "b_gate": p["b_gate"].reshape(n_layers, N_BRANCH, 1, d),
        "w_branch": p["w_branch"].astype(BF16), "w_out": p["w_out"].astype(BF16),
        "w_up": p["w_up"].astype(BF16), "conv_w": p["conv_w"], "conv_b": p["conv_b"].reshape(n_layers, 1, -1),
        "w_down": p["w_down"].astype(BF16),
    }


def kernel(x, c, ctx, c_ctx, w_ada, b_ada, w_in, na_q_norm, na_k_norm, na_rpb, mla_cq_norm, mla_ckv_norm,
           mla_w_uq, mla_w_ukv, mla_q_norm, mla_k_norm, sg_v_norm, sg_w_s, sg_b_s, gqa_q_norm, gqa_k_norm,
           w_branch, w_gate, b_gate, w_out, w_up, conv_w, conv_b, w_down):
    b, t, d = x.shape
    n_ctx = ctx.shape[1]
    depth = w_in.shape[0]
    ctx_row = b
    cvec = jnp.zeros((8, d), F32).at[:b].set(c).at[ctx_row].set(c_ctx)
    mod_all = _ada_call(cvec, w_ada, b_ada).reshape(depth, 8, 6, d)

    assert b < 8 and t % (4 * GRID_W) == 0 and n_ctx % SG_CHUNK == 0 and n_ctx >= SEED_KEYS
    tm = min(TOKEN_TILE, t)
    tq = min(QUERY_BLOCK, t)
    tk = min(KEY_CHUNK, t)
    assert t % tm == 0 and t % tq == 0 and t % tk == 0
    tabs_lat = _rope_tables(t, identity=False)
    tabs_ctx = _rope_tables(n_ctx, identity=True)
    na_bias = _na_bias(na_rpb, t // (4 * GRID_W))

    lw = _arrange({
        "w_in": w_in, "na_q_norm": na_q_norm, "na_k_norm": na_k_norm,
        "mla_cq_norm": mla_cq_norm, "mla_ckv_norm": mla_ckv_norm, "mla_w_uq": mla_w_uq,
        "mla_w_ukv": mla_w_ukv, "mla_q_norm": mla_q_norm, "mla_k_norm": mla_k_norm,
        "sg_v_norm": sg_v_norm, "sg_w_s": sg_w_s, "sg_b_s": sg_b_s,
        "gqa_q_norm": gqa_q_norm, "gqa_k_norm": gqa_k_norm,
        "w_branch": w_branch, "w_gate": w_gate, "b_gate": b_gate, "w_out": w_out,
        "w_up": w_up, "conv_w": conv_w, "conv_b": conv_b, "w_down": w_down})
    cx = ctx
    for l in range(depth):
        lw = dict(lw, layer=l)
        mod = mod_all[l]
        pc = _in_call(cx, mod, ctx_row, lw, tabs_ctx, n_ctx)
        pz = _in_call(x, mod, None, lw, tabs_lat, tm)

        o_na = _na_call(pz["na_q"], pz["na_k"], pz["na_v"], pc["na_k"], pc["na_v"], na_bias, l)
        o_mla = _flash_call(pz["mla_q"], pz["mla_k"], pz["mla_vt"], pc["mla_k"], pc["mla_vt"], tq, tk)
        o_gqa = _flash_call(pz["gqa_q"], pz["gqa_k"], pz["gqa_vt"], pc["gqa_k"], pc["gqa_vt"], tq, tk)
        x = _merge_call(x, mod, None, (o_na, o_mla, pz["sg_o"], o_gqa), lw, tm)
        x = _ffn_call(x, mod, None, lw, tm)

        if l < depth - 1:
            oc_na = _na_ctx_call(pc["na_q"], pc["na_k"], pc["na_v"])
            oc_mla = _flash_call(pc["mla_q"], pc["mla_k"], pc["mla_vt"], None, None, n_ctx, n_ctx)
            oc_gqa = _flash_call(pc["gqa_q"], pc["gqa_k"], pc["gqa_vt"], None, None, n_ctx, n_ctx)
            cx = _merge_call(cx, mod, ctx_row, (oc_na, oc_mla, pc["sg_o"], oc_gqa), lw, n_ctx)
            cx = _ffn_call(cx, mod, ctx_row, lw, n_ctx)
    return x
```

```python
import functools
import math

import numpy as np
import jax
import jax.numpy as jnp
from jax import lax
from jax.experimental import pallas as pl
from jax.experimental.pallas import tpu as pltpu

F32 = jnp.float32
BF16 = jnp.bfloat16

D_MODEL = 1024
GRID_W = 64
HEAD_DIM = 64
NA_HEADS = 4
NA_ROWS = 8
NA_COLS = 16
MLA_HEADS = 4
MLA_Q_LORA = 256
MLA_KV_LORA = 192
MLA_NOPE = 64
MLA_ROPE = 32
MLA_V = 64
SG_GROUPS = 4
SG_CHUNK = 128
SG_WIDTH = 256
GQA_HEADS = 4
GQA_KV_HEADS = 2
N_BRANCH = 4
BRANCH_WIDTH = 256
FFN_DIM = 2816
ROPE_THETA = 10000.0
EPS = 1e-6
NEG_INF = -1e30

NA_IN = 3 * NA_HEADS * HEAD_DIM
MLA_IN = MLA_Q_LORA + MLA_KV_LORA + MLA_ROPE
SG_IN = 2 * SG_WIDTH
GQA_IN = (GQA_HEADS + 2 * GQA_KV_HEADS) * HEAD_DIM

LANES = 128
MXU_DIM = 256
BF16_SUBLANES = 16
V_AUG = HEAD_DIM + BF16_SUBLANES
LOG2E = math.log2(math.e)
MLA_LOGIT_SCALE = (MLA_NOPE + MLA_ROPE) ** -0.5 * LOG2E
GQA_LOGIT_SCALE = HEAD_DIM ** -0.5 * LOG2E

C_NA = 0
C_MLA = C_NA + NA_IN
C_SG = C_MLA + 640
C_GQA = C_SG + SG_IN
IN_ARR = C_GQA + 640

G_NA, G_CQ, G_CKV, G_MQ, G_MKN, G_MKR, G_SG, G_GQA, G_TOT = 0, 512, 768, 1024, 1536, 2048, 2176, 2432, 2944

FFN_CHUNK = 256
N_FFN_CHUNKS = FFN_DIM // FFN_CHUNK
FFN_LOOKAHEAD = 2
HALO = BF16_SUBLANES


def _const_spec(shape):
    nd = len(shape)
    return pl.BlockSpec(shape, lambda *_: (0,) * nd, pipeline_mode=pl.Buffered(1))


def _layer_spec(shape, layer):
    nd = len(shape)
    return pl.BlockSpec((None,) + tuple(shape), lambda *_: (layer,) + (0,) * nd, pipeline_mode=pl.Buffered(1))


def _params(vmem_mb, n_grid):
    return pltpu.CompilerParams(dimension_semantics=("arbitrary",) * n_grid,
                                vmem_limit_bytes=vmem_mb * 1024 * 1024)


def _dot(a, b):
    return jnp.dot(a, b, preferred_element_type=F32)


def _dot_nt(a, b):
    return lax.dot_general(a, b, (((1,), (1,)), ((), ())), preferred_element_type=F32)


def _rms(xf):
    return xf * lax.rsqrt(jnp.mean(xf * xf, axis=-1, keepdims=True) + EPS)


def _segsum(x2, s_ref):
    hi = x2.astype(BF16)
    lo = (x2 - hi.astype(F32)).astype(BF16)
    s = s_ref[...]
    parts = []
    for j in range(x2.shape[1] // MXU_DIM):
        sl = slice(MXU_DIM * j, MXU_DIM * (j + 1))
        parts.append(_dot(hi[:, sl], s) + _dot(lo[:, sl], s))
    return parts[0] if len(parts) == 1 else jnp.concatenate(parts, axis=1)


def _rope(x, cos, sin_signed, shift, first_half):
    w = x.shape[1]
    partner = jnp.where(first_half, pltpu.roll(x, w - shift, 1), pltpu.roll(x, shift, 1))
    return x * cos + partner * sin_signed


def _ada_kernel(c_ref, w_ref, b_ref, o_ref):
    c = c_ref[...]
    a = (c * jax.nn.sigmoid(c)).astype(BF16)
    o_ref[...] = _dot(a, w_ref[...].astype(BF16)) + b_ref[...]


def _ada_call(cvec, w_ada, b_ada):
    n_layers, d, n = w_ada.shape
    tn = 1536
    return pl.pallas_call(
        _ada_kernel,
        grid=(n_layers, n // tn),
        in_specs=[pl.BlockSpec((8, d), lambda l, j: (0, 0)),
                  pl.BlockSpec((None, d, tn), lambda l, j: (l, 0, j)),
                  pl.BlockSpec((None, 1, tn), lambda l, j: (l, 0, j))],
        out_specs=pl.BlockSpec((None, 8, tn), lambda l, j: (l, 0, j)),
        out_shape=jax.ShapeDtypeStruct((n_layers, 8, n), F32),
        compiler_params=_params(32, 2),
        name="ada",
    )(cvec, w_ada, b_ada.reshape(n_layers, 1, n))


def _in_kernel(x_ref, mod_ref, w_in_ref, w_uq_ref, w_ukv_ref, g_ref, s64_ref, smq_ref, invq_ref,
               gc_ref, gs_ref, mc_ref, ms_ref, sgw_ref, sgb_ref,
               naq_ref, nak_ref, nav_ref, mq_ref, mk_ref, mvt_ref, sgo_ref, gq_ref, gk_ref, gvt_ref):
    tm = x_ref.shape[0]
    x = x_ref[...]
    h = (_rms(x) * (1.0 + mod_ref[1:2, :]) + mod_ref[0:1, :]).astype(BF16)

    def gain(off, width):
        return g_ref[:, off:off + width]

    inv_head = 1.0 / HEAD_DIM

    z = _dot(h, w_in_ref[:, C_NA:C_NA + NA_IN])
    qk = z[:, 0:512]
    qk = qk * lax.rsqrt(_segsum(qk * qk, s64_ref) * inv_head + EPS) * gain(G_NA, 512)
    naq_ref[...] = qk[:, 0:256].astype(BF16)
    nak_ref[...] = qk[:, 256:512].astype(BF16)
    nav_ref[...] = z[:, 512:768].astype(BF16)

    z = _dot(h, w_in_ref[:, C_MLA:C_MLA + 640])
    cq, ckv, kr = z[:, 0:256], z[:, 256:512], z[:, 512:640]
    cq = cq * lax.rsqrt(jnp.mean(cq * cq, axis=-1, keepdims=True) + EPS) * gain(G_CQ, 256)
    ckv = ckv * lax.rsqrt(jnp.sum(ckv * ckv, axis=-1, keepdims=True) * (1.0 / MLA_KV_LORA) + EPS) * gain(G_CKV, 256)
    q = _dot(cq.astype(BF16), w_uq_ref[...])
    q = q * lax.rsqrt(_segsum(q * q, smq_ref) * invq_ref[...] + EPS) * gain(G_MQ, 512)
    kv = _dot(ckv.astype(BF16), w_ukv_ref[...])
    kn = kv[:, 0:512]
    kn = kn * lax.rsqrt(_segsum(kn * kn, s64_ref) * inv_head + EPS) * gain(G_MKN, 512)
    kr = kr * lax.rsqrt(jnp.sum(kr * kr, axis=-1, keepdims=True) * (1.0 / MLA_ROPE) + EPS) * gain(G_MKR, 128)
    lane = lax.broadcasted_iota(jnp.int32, (tm, LANES), 1)
    first8 = (lane & 15) < 8
    mc, ms = mc_ref[...], ms_ref[...]
    kr = _rope(kr, mc, ms, 8, first8)
    for hh in range(MLA_HEADS):
        sl = slice(LANES * hh, LANES * (hh + 1))
        mq_ref[hh] = _rope(q[:, sl], mc, ms, 8, first8).astype(BF16)
        mk_ref[hh] = (kn[:, sl] + kr).astype(BF16)
    vt = kv[:, 512:768].T
    ones = jnp.ones((BF16_SUBLANES, tm), BF16)
    for hh in range(MLA_HEADS):
        mvt_ref[hh, 0:HEAD_DIM, :] = vt[HEAD_DIM * hh:HEAD_DIM * (hh + 1), :].astype(BF16)
        mvt_ref[hh, HEAD_DIM:V_AUG, :] = ones

    z = _dot(h, w_in_ref[:, C_SG:C_SG + SG_IN])
    uv = jax.nn.gelu(z)
    u, v = uv[:, 0:SG_WIDTH], uv[:, SG_WIDTH:]
    v = v * lax.rsqrt(jnp.mean(v * v, axis=-1, keepdims=True) + EPS) * gain(G_SG, 256)
    grp = lax.broadcasted_iota(jnp.int32, (SG_CHUNK, SG_WIDTH), 1) >> 6
    for c in range(tm // SG_CHUNK):
        rows = slice(SG_CHUNK * c, SG_CHUNK * (c + 1))
        vc = v[rows, :]
        mixed = sgb_ref[...]
        for gi in range(SG_GROUPS):
            mixed = mixed + _dot(sgw_ref[gi], jnp.where(grp == gi, vc, 0.0).astype(BF16))
        sgo_ref[rows, :] = (u[rows, :] * mixed).astype(BF16)

    z = _dot(h, w_in_ref[:, C_GQA:C_GQA + 640])
    qk = z[:, 0:512]
    qk = qk * lax.rsqrt(_segsum(qk * qk, s64_ref) * inv_head + EPS) * gain(G_GQA, 512)
    lane2 = lax.broadcasted_iota(jnp.int32, (tm, 2 * LANES), 1)
    first16 = (lane2 & 31) < 16
    gc, gs = gc_ref[...], gs_ref[...]
    qn = _rope(qk[:, 0:256], gc, gs, 16, first16)
    kn = _rope(qk[:, 256:512], gc, gs, 16, first16)
    half = lane >> 6
    for g in range(GQA_KV_HEADS):
        sl = slice(LANES * g, LANES * (g + 1))
        for r in range(GQA_HEADS // GQA_KV_HEADS):
            gq_ref[2 * g + r] = jnp.where(half == r, qn[:, sl], 0.0).astype(BF16)
        gk_ref[g] = kn[:, sl].astype(BF16)
    vt = z[:, 512:640].T
    for g in range(GQA_KV_HEADS):
        gvt_ref[g, 0:HEAD_DIM, :] = vt[HEAD_DIM * g:HEAD_DIM * (g + 1), :].astype(BF16)
        gvt_ref[g, HEAD_DIM:V_AUG, :] = ones


def _in_call(x, mod, mod_row, lw, tabs, tm):
    b, t, d = x.shape
    nt = t // tm
    if mod_row is None:
        mod_map = lambda i, j: (j, 0, 0)
    else:
        mod_map = lambda i, j: (mod_row, 0, 0)
    tok = lambda i, j: (j, i, 0)
    tab = lambda i, j: (i, 0)
    head_tok = lambda i, j: (j, 0, i, 0)
    head_t = lambda i, j: (j, 0, 0, i)
    in_specs = [
        pl.BlockSpec((None, tm, d), tok),
        pl.BlockSpec((None, 6, d), mod_map),
        _layer_spec((d, IN_ARR), lw["layer"]),
        _layer_spec((MLA_Q_LORA, 512), lw["layer"]),
        _layer_spec((256, 768), lw["layer"]),
        _layer_spec((1, G_TOT), lw["layer"]),
        _const_spec((MXU_DIM, MXU_DIM)),
        _const_spec((MXU_DIM, MXU_DIM)),
        _const_spec((1, 512)),
        pl.BlockSpec((tm, 256), tab), pl.BlockSpec((tm, 256), tab),
        pl.BlockSpec((tm, LANES), tab), pl.BlockSpec((tm, LANES), tab),
        _layer_spec((SG_GROUPS, SG_CHUNK, SG_CHUNK), lw["layer"]),
        _layer_spec((SG_CHUNK, SG_WIDTH), lw["layer"]),
    ]
    out_specs = [
        pl.BlockSpec((None, tm, 256), tok), pl.BlockSpec((None, tm, 256), tok), pl.BlockSpec((None, tm, 256), tok),
        pl.BlockSpec((None, MLA_HEADS, tm, LANES), head_tok),
        pl.BlockSpec((None, MLA_HEADS, tm, LANES), head_tok),
        pl.BlockSpec((None, MLA_HEADS, V_AUG, tm), head_t),
        pl.BlockSpec((None, tm, 256), tok),
        pl.BlockSpec((None, GQA_HEADS, tm, LANES), head_tok),
        pl.BlockSpec((None, GQA_KV_HEADS, tm, LANES), head_tok),
        pl.BlockSpec((None, GQA_KV_HEADS, V_AUG, tm), head_t),
    ]
    sds = jax.ShapeDtypeStruct
    out_shape = [
        sds((b, t, 256), BF16), sds((b, t, 256), BF16), sds((b, t, 256), BF16),
        sds((b, MLA_HEADS, t, LANES), BF16), sds((b, MLA_HEADS, t, LANES), BF16),
        sds((b, MLA_HEADS, V_AUG, t), BF16),
        sds((b, t, 256), BF16),
        sds((b, GQA_HEADS, t, LANES), BF16), sds((b, GQA_KV_HEADS, t, LANES), BF16),
        sds((b, GQA_KV_HEADS, V_AUG, t), BF16),
    ]
    outs = pl.pallas_call(
        _in_kernel, grid=(nt, b), in_specs=in_specs, out_specs=out_specs, out_shape=out_shape,
        compiler_params=_params(48, 2), name="in_proj",
    )(x, mod, lw["w_in"], lw["w_uq"], lw["w_ukv"], lw["gains"], lw["s64"], lw["smq"], lw["invq"],
      tabs["gc"], tabs["gs"], tabs["mc"], tabs["ms"], lw["sgw"], lw["sgb"])
    names = ("na_q", "na_k", "na_v", "mla_q", "mla_k", "mla_vt", "sg_o", "gqa_q", "gqa_k", "gqa_vt")
    return dict(zip(names, outs))


def _na_heads(q, k_blocks, v_blocks, biases, seeded):
    tq = q.shape[0]
    head_of_lane = lax.broadcasted_iota(jnp.int32, (tq, 256), 1) >> 6
    qf = q.astype(F32) * (HEAD_DIM ** -0.5)
    qhs = [jnp.where(head_of_lane == hh, qf, 0.0).astype(BF16) for hh in range(NA_HEADS)]
    vts = [vb.astype(F32).T.astype(BF16) for vb in v_blocks]
    units = [(hh, j) for hh in range(NA_HEADS) for j in range(len(k_blocks))]

    def score(u):
        hh, j = u
        s = _dot_nt(k_blocks[j], qhs[hh])
        return s if biases[j] is None else s + biases[j][hh]

    def value_rows(hh, j):
        ones = jnp.ones((BF16_SUBLANES, vts[j].shape[1]), BF16)
        return jnp.concatenate([vts[j][HEAD_DIM * hh:HEAD_DIM * (hh + 1), :], ones], axis=0)

    lookahead = SCORE_LOOKAHEAD if seeded else len(k_blocks)
    pending = [score(u) for u in units[:lookahead]]
    accs, m = [], None
    for n, (hh, j) in enumerate(units):
        if j == 0:
            acc = jnp.zeros((V_AUG, tq), F32)
            head = [pending.pop(0) for _ in range(1 if seeded else len(k_blocks))]
            m = _col_max(head[0])
            for s in head[1:]:
                m = jnp.maximum(m, _col_max(s))
        else:
            head = head[1:] if not seeded else [pending.pop(0)]
        if n + lookahead < len(units):
            pending.append(score(units[n + lookahead]))
        acc = acc + _dot(value_rows(hh, j), jnp.exp(head[0] - m).astype(BF16))
        if j == len(k_blocks) - 1:
            accs.append(acc)
    out = jnp.concatenate([a[0:HEAD_DIM] * (1.0 / a[HEAD_DIM:HEAD_DIM + 1]) for a in accs], axis=0).T
    return out, accs


def _na_kernel(q_ref, kp_ref, kc_ref, kn_ref, vp_ref, vc_ref, vn_ref, kx_ref, vx_ref, bias_ref, o_ref):
    band = lambda refs: jnp.concatenate([r[...] for r in refs], axis=0)
    operands = lambda: (q_ref[...], [kx_ref[...], band((kp_ref, kc_ref, kn_ref))],
                        [vx_ref[...], band((vp_ref, vc_ref, vn_ref))], [None, bias_ref])
    out, accs = _na_heads(*operands(), seeded=True)
    o_ref[...] = out.astype(BF16)
    unusable = jnp.zeros((V_AUG, q_ref.shape[0]), F32)
    for acc in accs:
        unusable = jnp.maximum(unusable, jnp.where(jnp.abs(acc) < F32_SAFE_MAX, 0.0, 1.0))

    @pl.when(jnp.max(unusable) > 0.0)
    def _():
        o_ref[...] = _na_heads(*operands(), seeded=False)[0].astype(BF16)


def _na_call(q, k, v, kx, vx, bias, layer):
    b, t, w = q.shape
    c = kx.shape[1]
    tq = 4 * GRID_W
    nt = t // tq
    cur = lambda bi, i: (bi, i, 0)
    prev = lambda bi, i: (bi, jnp.maximum(i - 1, 0), 0)
    nxt = lambda bi, i: (bi, jnp.minimum(i + 1, nt - 1), 0)
    ctx = lambda bi, i: (bi, 0, 0)
    variant = lambda bi, i: (layer, jnp.where(i == 0, 0, jnp.where(i == nt - 1, 2, 1)), 0, 0, 0)
    blk = lambda m: pl.BlockSpec((None, tq, w), m)
    return pl.pallas_call(
        _na_kernel, grid=(b, nt),
        in_specs=[blk(cur), blk(prev), blk(cur), blk(nxt), blk(prev), blk(cur), blk(nxt),
                  pl.BlockSpec((None, c, w), ctx), pl.BlockSpec((None, c, w), ctx),
                  pl.BlockSpec((None, None, NA_HEADS, 3 * tq, tq), variant)],
        out_specs=blk(cur),
        out_shape=jax.ShapeDtypeStruct((b, t, w), BF16),
        compiler_params=_params(48, 2), name="na_attn",
    )(q, k, k, k, v, v, v, kx, vx, bias)


def _na_ctx_kernel(q_ref, k_ref, v_ref, o_ref):
    o_ref[...] = _na_heads(q_ref[...], [k_ref[...]], [v_ref[...]], [None], seeded=False)[0].astype(BF16)


def _na_ctx_call(q, k, v):
    b, c, w = q.shape
    spec = pl.BlockSpec((None, c, w), lambda bi: (bi, 0, 0))
    return pl.pallas_call(
        _na_ctx_kernel, grid=(b,), in_specs=[spec, spec, spec], out_specs=spec,
        out_shape=jax.ShapeDtypeStruct((b, c, w), BF16),
        compiler_params=_params(32, 1), name="na_ctx_attn",
    )(q, k, v)


def _na_bias(rpb, nt):
    rows = 4 * nt
    col = np.arange(GRID_W)
    c_start = np.clip(col - NA_COLS // 2, 0, GRID_W - NA_COLS)
    valid_c = (col[None, :] >= c_start[:, None]) & (col[None, :] < c_start[:, None] + NA_COLS)
    dc = np.clip(col[None, :] - col[:, None] + (NA_COLS - 1), 0, 2 * NA_COLS - 2)
    pick_dr, ok = [], []
    for i in (0, min(1, nt - 1), nt - 1):
        rq = 4 * i + np.arange(4)
        start = np.clip(rq - NA_ROWS // 2, 0, rows - NA_ROWS)
        blocks = np.array([i - 1, i, i + 1])
        rk = (4 * blocks[:, None] + np.arange(4)[None, :]).reshape(-1)
        block_ok = np.repeat((blocks >= 0) & (blocks < nt), 4)
        valid_r = (rk[None, :] >= start[:, None]) & (rk[None, :] < start[:, None] + NA_ROWS) & block_ok[None, :]
        dr = np.clip(rk[None, :] - rq[:, None] + (NA_ROWS - 1), 0, 2 * NA_ROWS - 2)
        pick_dr.append(dr[:, :, None] == np.arange(2 * NA_ROWS - 1))
        ok.append(valid_r[:, None, :, None] & valid_c[None, :, None, :])
    pick_dc = jnp.asarray(dc[:, :, None] == np.arange(2 * NA_COLS - 1), F32)
    by_col = jnp.einsum("lhrd,qkd->lhrqk", rpb, pick_dc, precision=lax.Precision.HIGHEST)
    vals = jnp.einsum("vabr,lhrqk->lvhbkaq", jnp.asarray(np.stack(pick_dr), F32), by_col,
                      precision=lax.Precision.HIGHEST)
    tile_q, tile_k = 4 * GRID_W, 12 * GRID_W
    mask = jnp.asarray(np.stack(ok).transpose(0, 3, 4, 1, 2).reshape(1, 3, 1, tile_k, tile_q))
    return jnp.where(mask, vals.reshape(rpb.shape[0], 3, NA_HEADS, tile_k, tile_q), NEG_INF)


TOKEN_TILE = 512
QUERY_BLOCK = 512
KEY_CHUNK = 512
FLASH_COL = MXU_DIM
STREAM_UNROLL = 16
SCORE_LOOKAHEAD = 2
SEED_KEYS = 64
F32_SAFE_MAX = 3e38
MIN_DENOMINATOR = 2.0 ** -40


def _col_max(s):
    keys = s.shape[0]
    if keys % 64 == 0 and keys > 64:
        s = jnp.max(s.reshape(keys // 64, 64, s.shape[1]), axis=0)
    return jnp.max(s, axis=0, keepdims=True)


def _flash_kernel(*refs, n_heads, group, tq, tk, has_ctx):
    if has_ctx:
        q_ref, k_ref, vt_ref, kx_ref, vtx_ref, o_ref, acc_scr, ot_scr = refs
    else:
        q_ref, k_ref, vt_ref, o_ref, acc_scr, ot_scr = refs
    n_sub = q_ref.shape[1] // tq
    n_chunks = k_ref.shape[1] // tk
    n_slabs = n_heads // group
    cols = [(hh, qi) for hh in range(n_heads) for qi in range(n_sub)]
    heads = range(len(cols))
    slab_of = [hh // group for hh, _ in cols]

    def q_col(c):
        hh, qi = cols[c]
        return q_ref[hh, qi * tq:(qi + 1) * tq, :]

    def load_chunk(j):
        off = pl.multiple_of(j * tk, tk)
        return ([k_ref[sl, pl.ds(off, tk), :] for sl in range(n_slabs)],
                [vt_ref[sl, :, pl.ds(off, tk)] for sl in range(n_slabs)])

    def pv(vt, p):
        return _dot(vt, p.astype(BF16))

    def exact_step(ks, vts, ms):
        new_ms = []
        for hh in heads:
            s = _dot_nt(ks[slab_of[hh]], q_col(hh))
            m_new = jnp.maximum(ms[hh], _col_max(s))
            p = jnp.exp2(s - m_new)
            acc_scr[hh] = acc_scr[hh] * jnp.exp2(ms[hh] - m_new) + pv(vts[slab_of[hh]], p)
            new_ms.append(m_new)
        return tuple(new_ms)

    def stream(chunks, ms):
        units = [(i, hh) for i in range(len(chunks)) for hh in heads]
        score = lambda u: _dot_nt(chunks[u[0]][0](slab_of[u[1]]), q_col(u[1]))
        pending = [score(u) for u in units[:SCORE_LOOKAHEAD]]
        for n, (i, hh) in enumerate(units):
            s = pending.pop(0)
            if n + SCORE_LOOKAHEAD < len(units):
                pending.append(score(units[n + SCORE_LOOKAHEAD]))
            acc_scr[hh] += pv(chunks[i][1](slab_of[hh]), jnp.exp2(s - ms[hh]))

    def lazy_chunk(j):
        off = pl.multiple_of(j * tk, tk)
        return (lambda sl: k_ref[sl, pl.ds(off, tk), :]), (lambda sl: vt_ref[sl, :, pl.ds(off, tk)])

    lazy_ctx = ((lambda sl: kx_ref[sl]), (lambda sl: vtx_ref[sl])) if has_ctx else None
    ctx_chunk = ([kx_ref[sl] for sl in range(n_slabs)], [vtx_ref[sl] for sl in range(n_slabs)]) if has_ctx else None
    seed_ref = kx_ref if has_ctx else k_ref
    ms = tuple(_col_max(_dot_nt(seed_ref[slab_of[hh], 0:SEED_KEYS, :], q_col(hh))) for hh in heads)
    acc_scr[...] = jnp.zeros_like(acc_scr)
    if has_ctx:
        stream([lazy_ctx], ms)
    per_iter = max(u for u in (STREAM_UNROLL, 8, 4, 2, 1) if n_chunks % u == 0)

    def stream_body(j, carry):
        stream([lazy_chunk(j * per_iter + i) for i in range(per_iter)], ms)
        return carry

    lax.fori_loop(0, n_chunks // per_iter, stream_body, 0)
    unusable = jnp.zeros((V_AUG, tq), F32)
    for hh in heads:
        acc = acc_scr[hh]
        unusable = jnp.maximum(unusable, jnp.where(jnp.abs(acc) < F32_SAFE_MAX, 0.0, 1.0))
        unusable = jnp.maximum(unusable, jnp.where(acc[HEAD_DIM:HEAD_DIM + 1] > MIN_DENOMINATOR, 0.0, 1.0))

    @pl.when(jnp.max(unusable) > 0.0)
    def _():
        acc_scr[...] = jnp.zeros_like(acc_scr)
        ms = tuple(jnp.full((1, tq), NEG_INF, F32) for _ in heads)
        if has_ctx:
            ms = exact_step(*ctx_chunk, ms)
        lax.fori_loop(0, n_chunks, lambda j, m: exact_step(*load_chunk(j), m), ms)

    for c, (hh, qi) in enumerate(cols):
        acc = acc_scr[c]
        ot_scr[HEAD_DIM * hh:HEAD_DIM * (hh + 1), qi * tq:(qi + 1) * tq] = (
            acc[0:HEAD_DIM] * (1.0 / acc[HEAD_DIM:HEAD_DIM + 1]))
    o_ref[...] = ot_scr[...].T.astype(BF16)


def _flash_call(q, k, vt, kx, vtx, tq, tk):
    b, n_heads, t_q, w = q.shape
    slabs, t_k = k.shape[1], k.shape[2]
    has_ctx = kx is not None
    full4 = lambda bi, i: (bi, 0, 0, 0)
    in_specs = [pl.BlockSpec((None, n_heads, tq, w), lambda bi, i: (bi, 0, i, 0)),
                pl.BlockSpec((None, slabs, t_k, w), full4),
                pl.BlockSpec((None, slabs, V_AUG, t_k), full4)]
    args = [q, k, vt]
    if has_ctx:
        c = kx.shape[2]
        in_specs += [pl.BlockSpec((None, slabs, c, w), full4), pl.BlockSpec((None, slabs, V_AUG, c), full4)]
        args += [kx, vtx]
    col = min(FLASH_COL, tq)
    kern = functools.partial(_flash_kernel, n_heads=n_heads, group=n_heads // slabs, tq=col, tk=tk,
                             has_ctx=has_ctx)
    return pl.pallas_call(
        kern, grid=(b, t_q // tq), in_specs=in_specs,
        out_specs=pl.BlockSpec((None, tq, n_heads * HEAD_DIM), lambda bi, i: (bi, i, 0)),
        out_shape=jax.ShapeDtypeStruct((b, t_q, n_heads * HEAD_DIM), BF16),
        scratch_shapes=[pltpu.VMEM((n_heads * (tq // col), V_AUG, col), F32),
                        pltpu.VMEM((n_heads * HEAD_DIM, tq), F32)],
        compiler_params=_params(56, 2), name="flash",
    )(*args)


def _merge_kernel(x_ref, mod_ref, o0_ref, o1_ref, o2_ref, o3_ref, wg_ref, bg_ref, wb_ref, wo_ref, out_ref):
    x = x_ref[...]
    h = (_rms(x) * (1.0 + mod_ref[1:2, :]) + mod_ref[0:1, :]).astype(BF16)
    y = None
    for i, o_ref in enumerate((o0_ref, o1_ref, o2_ref, o3_ref)):
        gate = jax.nn.sigmoid(_dot(h, wg_ref[i]) + bg_ref[i])
        term = gate * _dot(o_ref[...], wb_ref[i])
        y = term if y is None else y + term
    out_ref[...] = x + mod_ref[2:3, :] * _dot(y.astype(BF16), wo_ref[...])


def _merge_call(x, mod, mod_row, branches, lw, tm):
    b, t, d = x.shape
    if mod_row is None:
        mod_map = lambda i, j: (j, 0, 0)
    else:
        mod_map = lambda i, j: (mod_row, 0, 0)
    tok = lambda i, j: (j, i, 0)
    br = pl.BlockSpec((None, tm, BRANCH_WIDTH), tok)
    return pl.pallas_call(
        _merge_kernel, grid=(t // tm, b),
        in_specs=[pl.BlockSpec((None, tm, d), tok), pl.BlockSpec((None, 6, d), mod_map), br, br, br, br,
                  _layer_spec((N_BRANCH, d, d), lw["layer"]), _layer_spec((N_BRANCH, 1, d), lw["layer"]),
                  _layer_spec((N_BRANCH, BRANCH_WIDTH, d), lw["layer"]), _layer_spec((d, d), lw["layer"])],
        out_specs=pl.BlockSpec((None, tm, d), tok),
        out_shape=jax.ShapeDtypeStruct((b, t, d), F32),
        compiler_params=_params(56, 2), name="merge",
    )(x, mod, *branches, lw["w_gate"], lw["b_gate"], lw["w_branch"], lw["w_out"])


def _ffn_kernel(x_ref, xp_ref, xn_ref, mod_ref, wu_ref, cw_ref, cb_ref, wd_ref, out_ref, h_scr, act_scr):
    tm = x_ref.shape[0]
    i = pl.program_id(0)
    nt = pl.num_programs(0)
    sh, sc = mod_ref[3:4, :], mod_ref[4:5, :]

    def modulated(v):
        return _rms(v) * (1.0 + sc) + sh

    x = x_ref[...]
    h_scr[0:HALO, :] = jnp.where(i > 0, modulated(xp_ref[...]), 0.0).astype(BF16)
    h_scr[HALO:HALO + tm, :] = modulated(x).astype(BF16)
    h_scr[HALO + tm:, :] = jnp.where(i < nt - 1, modulated(xn_ref[...]), 0.0).astype(BF16)
    rows = tm + 2 * HALO
    main = slice(HALO, HALO + tm)

    def cols(j, gate):
        lo = gate * FFN_DIM + FFN_CHUNK * j
        return slice(lo, lo + FFN_CHUNK)

    def up_proj(j):
        return tuple(_dot(h_scr[...], wu_ref[:, cols(j, gate)]) for gate in (0, 1))

    def conv(up, sl):
        return (pltpu.roll(up, 1, 0)[main] * cw_ref[0:1, sl] + up[main] * cw_ref[1:2, sl]
                + pltpu.roll(up, rows - 1, 0)[main] * cw_ref[2:3, sl] + cb_ref[:, sl])

    ups = [up_proj(j) for j in range(FFN_LOOKAHEAD)]
    for j in range(N_FFN_CHUNKS):
        up_a, up_g = ups.pop(0)
        if j + FFN_LOOKAHEAD < N_FFN_CHUNKS:
            ups.append(up_proj(j + FFN_LOOKAHEAD))
        a, g = conv(up_a, cols(j, 0)), conv(up_g, cols(j, 1))
        act_scr[:, cols(j, 0)] = (g * jax.nn.sigmoid(g) * a).astype(BF16)
    out_ref[...] = x + mod_ref[5:6, :] * _dot(act_scr[...], wd_ref[...])


def _ffn_call(x, mod, mod_row, lw, tm):
    b, t, d = x.shape
    nt = t // tm
    hb = tm // HALO
    last_halo = t // HALO - 1
    if mod_row is None:
        mod_map = lambda i, j: (j, 0, 0)
    else:
        mod_map = lambda i, j: (mod_row, 0, 0)
    tok = lambda i, j: (j, i, 0)
    return pl.pallas_call(
        _ffn_kernel, grid=(nt, b),
        in_specs=[pl.BlockSpec((None, tm, d), tok),
                  pl.BlockSpec((None, HALO, d), lambda i, j: (j, jnp.maximum(i * hb - 1, 0), 0)),
                  pl.BlockSpec((None, HALO, d), lambda i, j: (j, jnp.minimum((i + 1) * hb, last_halo), 0)),
                  pl.BlockSpec((None, 6, d), mod_map),
                  _layer_spec((d, 2 * FFN_DIM), lw["layer"]),
                  _layer_spec((3, 2 * FFN_DIM), lw["layer"]),
                  _layer_spec((1, 2 * FFN_DIM), lw["layer"]),
                  _layer_spec((FFN_DIM, d), lw["layer"])],
        out_specs=pl.BlockSpec((None, tm, d), tok),
        out_shape=jax.ShapeDtypeStruct((b, t, d), F32),
        scratch_shapes=[pltpu.VMEM((tm + 2 * HALO, d), BF16), pltpu.VMEM((tm, FFN_DIM), BF16)],
        compiler_params=_params(56, 2), name="ffn",
    )(x, x, x, mod, lw["w_up"], lw["conv_w"], lw["conv_b"], lw["w_down"])


def _block_diag_ones(width, segs):
    m = np.zeros((width, width), np.float32)
    for lo, hi in segs:
        m[lo:hi, lo:hi] = 1.0
    return jnp.asarray(m, BF16)


def _rope_pattern(pos, width):
    half = width // 2
    inv = ROPE_THETA ** (-jnp.arange(half, dtype=F32) / half)
    ang = pos.astype(F32)[:, None] * inv[None, :]
    cos, sin = jnp.cos(ang), jnp.sin(ang)
    return jnp.concatenate([cos, cos], axis=1), jnp.concatenate([-sin, sin], axis=1)


def _rope_tables(t, identity):
    if identity:
        return {"gc": jnp.ones((t, 256), F32), "gs": jnp.zeros((t, 256), F32),
                "mc": jnp.ones((t, LANES), F32), "ms": jnp.zeros((t, LANES), F32)}
    n_rows = t // GRID_W

    def grid_tables(width):
        rc, rs = _rope_pattern(jnp.arange(n_rows), width)
        cc, cs = _rope_pattern(jnp.arange(GRID_W), width)
        expand = lambda r, c: jnp.concatenate([jnp.repeat(r, GRID_W, axis=0), jnp.tile(c, (n_rows, 1))], axis=1)
        return expand(rc, cc), expand(rs, cs)

    gcos, gsin = grid_tables(HEAD_DIM // 2)
    mcos, msin = grid_tables(MLA_ROPE // 2)
    one, zero = jnp.ones((t, MLA_NOPE), F32), jnp.zeros((t, MLA_NOPE), F32)
    pad1, pad0 = jnp.ones((t, LANES - MLA_NOPE - MLA_ROPE), F32), jnp.zeros((t, LANES - MLA_NOPE - MLA_ROPE), F32)
    return {"gc": jnp.tile(gcos, (1, 4)), "gs": jnp.tile(gsin, (1, 4)),
            "mc": jnp.concatenate([one, mcos, pad1], axis=1),
            "ms": jnp.concatenate([zero, msin, pad0], axis=1)}


def _arrange(p):
    d = D_MODEL
    n_layers = p["w_in"].shape[0]
    cat = lambda parts: jnp.concatenate(parts, axis=-1)
    w_in = p["w_in"]
    na, mla = w_in[..., :NA_IN], w_in[..., NA_IN:NA_IN + MLA_IN]
    sg = w_in[..., NA_IN + MLA_IN:NA_IN + MLA_IN + SG_IN]
    gqa = w_in[..., NA_IN + MLA_IN + SG_IN:]
    z = lambda n: jnp.zeros((n_layers, d, n), F32)
    cq, ckv, kr = (mla[..., :MLA_Q_LORA], mla[..., MLA_Q_LORA:MLA_Q_LORA + MLA_KV_LORA],
                   mla[..., MLA_Q_LORA + MLA_KV_LORA:])
    gq, gk, gv = gqa[..., :256], gqa[..., 256:384], gqa[..., 384:]
    gk_dup = cat([gk[..., :64], gk[..., :64], gk[..., 64:], gk[..., 64:]])
    w_in_arr = cat([na, cq, ckv, z(256 - MLA_KV_LORA), z(MLA_NOPE), kr, z(LANES - MLA_NOPE - MLA_ROPE),
                    sg, gq, gk_dup, gv]).astype(BF16)

    lane_pad = lambda v, n: jnp.pad(v, ((0, 0),) * (v.ndim - 1) + ((0, n),))
    w_uq = p["mla_w_uq"].reshape(n_layers, MLA_Q_LORA, MLA_HEADS, MLA_NOPE + MLA_ROPE)
    w_uq = lane_pad(w_uq, LANES - MLA_NOPE - MLA_ROPE).reshape(n_layers, MLA_Q_LORA, MLA_HEADS * LANES)
    w_ukv = p["mla_w_ukv"].reshape(n_layers, MLA_KV_LORA, MLA_HEADS, MLA_NOPE + MLA_V)
    k_part = lane_pad(w_ukv[..., :MLA_NOPE], LANES - MLA_NOPE).reshape(n_layers, MLA_KV_LORA, MLA_HEADS * LANES)
    v_part = w_ukv[..., MLA_NOPE:].reshape(n_layers, MLA_KV_LORA, MLA_HEADS * MLA_V)
    w_ukv = jnp.pad(cat([k_part, v_part]), ((0, 0), (0, 256 - MLA_KV_LORA), (0, 0)))

    qg, kg = p["mla_q_norm"], p["mla_k_norm"]
    z1 = lambda n: jnp.zeros((n_layers, n), F32)
    tile4 = lambda v: jnp.tile(v, (1, 4))
    gains = cat([
        tile4(p["na_q_norm"]), tile4(p["na_k_norm"]),
        p["mla_cq_norm"],
        p["mla_ckv_norm"], z1(256 - MLA_KV_LORA),
        tile4(cat([qg * MLA_LOGIT_SCALE, z1(LANES - MLA_NOPE - MLA_ROPE)])),
        tile4(cat([kg[:, :MLA_NOPE], z1(LANES - MLA_NOPE)])),
        z1(MLA_NOPE), kg[:, MLA_NOPE:], z1(LANES - MLA_NOPE - MLA_ROPE),
        p["sg_v_norm"],
        tile4(p["gqa_q_norm"] * GQA_LOGIT_SCALE), tile4(p["gqa_k_norm"]),
    ]).reshape(n_layers, 1, G_TOT)

    invq = np.tile(np.concatenate([np.full(MLA_NOPE, 1.0 / MLA_NOPE), np.full(MLA_ROPE, 1.0 / MLA_ROPE),
                                   np.ones(LANES - MLA_NOPE - MLA_ROPE)]), 4).astype(np.float32).reshape(1, 512)
    sgb = jnp.repeat(jnp.swapaxes(p["sg_b_s"], -1, -2), SG_WIDTH // SG_GROUPS, axis=-1)

    return {
        "w_in": w_in_arr, "w_uq": w_uq.astype(BF16), "w_ukv": w_ukv.astype(BF16), "gains": gains,
        "s64": _block_diag_ones(MXU_DIM, [(64 * i, 64 * i + 64) for i in range(4)]),
        "smq": _block_diag_ones(MXU_DIM, [(0, 64), (64, 96), (128, 192), (192, 224)]),
        "invq": jnp.asarray(invq),
        "sgw": p["sg_w_s"].astype(BF16), "sgb": sgb,
        "w_gate": p["w_gate"].astype(BF16), "b_gate": p["b_gate"].reshape(n_layers, N_BRANCH, 1, d),
        "w_branch": p["w_branch"].astype(BF16), "w_out": p["w_out"].astype(BF16),
        "w_up": p["w_up"].astype(BF16), "conv_w": p["conv_w"], "conv_b": p["conv_b"].reshape(n_layers, 1, -1),
        "w_down": p["w_down"].astype(BF16),
    }


def kernel(x, c, ctx, c_ctx, w_ada, b_ada, w_in, na_q_norm, na_k_norm, na_rpb, mla_cq_norm, mla_ckv_norm,
           mla_w_uq, mla_w_ukv, mla_q_norm, mla_k_norm, sg_v_norm, sg_w_s, sg_b_s, gqa_q_norm, gqa_k_norm,
           w_branch, w_gate, b_gate, w_out, w_up, conv_w, conv_b, w_down):
    b, t, d = x.shape
    n_ctx = ctx.shape[1]
    depth = w_in.shape[0]
    ctx_row = b
    cvec = jnp.zeros((8, d), F32).at[:b].set(c).at[ctx_row].set(c_ctx)
    mod_all = _ada_call(cvec, w_ada, b_ada).reshape(depth, 8, 6, d)

    assert b < 8 and t % (4 * GRID_W) == 0 and n_ctx % SG_CHUNK == 0 and n_ctx >= SEED_KEYS
    tm = min(TOKEN_TILE, t)
    tq = min(QUERY_BLOCK, t)
    tk = min(KEY_CHUNK, t)
    assert t % tm == 0 and t % tq == 0 and t % tk == 0
    tabs_lat = _rope_tables(t, identity=False)
    tabs_ctx = _rope_tables(n_ctx, identity=True)
    na_bias = _na_bias(na_rpb, t // (4 * GRID_W))

    lw = _arrange({
        "w_in": w_in, "na_q_norm": na_q_norm, "na_k_norm": na_k_norm,
        "mla_cq_norm": mla_cq_norm, "mla_ckv_norm": mla_ckv_norm, "mla_w_uq": mla_w_uq,
        "mla_w_ukv": mla_w_ukv, "mla_q_norm": mla_q_norm, "mla_k_norm": mla_k_norm,
        "sg_v_norm": sg_v_norm, "sg_w_s": sg_w_s, "sg_b_s": sg_b_s,
        "gqa_q_norm": gqa_q_norm, "gqa_k_norm": gqa_k_norm,
        "w_branch": w_branch, "w_gate": w_gate, "b_gate": b_gate, "w_out": w_out,
        "w_up": w_up, "conv_w": conv_w, "conv_b": conv_b, "w_down": w_down})
    cx = ctx
    for l in range(depth):
        lw = dict(lw, layer=l)
        mod = mod_all[l]
        pc = _in_call(cx, mod, ctx_row, lw, tabs_ctx, n_ctx)
        pz = _in_call(x, mod, None, lw, tabs_lat, tm)

        o_na = _na_call(pz["na_q"], pz["na_k"], pz["na_v"], pc["na_k"], pc["na_v"], na_bias, l)
        o_mla = _flash_call(pz["mla_q"], pz["mla_k"], pz["mla_vt"], pc["mla_k"], pc["mla_vt"], tq, tk)
        o_gqa = _flash_call(pz["gqa_q"], pz["gqa_k"], pz["gqa_vt"], pc["gqa_k"], pc["gqa_vt"], tq, tk)
        x = _merge_call(x, mod, None, (o_na, o_mla, pz["sg_o"], o_gqa), lw, tm)
        x = _ffn_call(x, mod, None, lw, tm)

        if l < depth - 1:
            oc_na = _na_ctx_call(pc["na_q"], pc["na_k"], pc["na_v"])
            oc_mla = _flash_call(pc["mla_q"], pc["mla_k"], pc["mla_vt"], None, None, n_ctx, n_ctx)
            oc_gqa = _flash_call(pc["gqa_q"], pc["gqa_k"], pc["gqa_vt"], None, None, n_ctx, n_ctx)
            cx = _merge_call(cx, mod, ctx_row, (oc_na, oc_mla, pc["sg_o"], oc_gqa), lw, n_ctx)
            cx = _ffn_call(cx, mod, ctx_row, lw, n_ctx)
    return x
```

```python
import functools
import math

import numpy as np
import jax
import jax.numpy as jnp
from jax import lax
from jax.experimental import pallas as pl
from jax.experimental.pallas import tpu as pltpu

F32 = jnp.float32
BF16 = jnp.bfloat16

D_MODEL = 1024
GRID_W = 64
HEAD_DIM = 64
NA_HEADS = 4
NA_ROWS = 8
NA_COLS = 16
MLA_HEADS = 4
MLA_Q_LORA = 256
MLA_KV_LORA = 192
MLA_NOPE = 64
MLA_ROPE = 32
MLA_V = 64
SG_GROUPS = 4
SG_CHUNK = 128
SG_WIDTH = 256
GQA_HEADS = 4
GQA_KV_HEADS = 2
N_BRANCH = 4
BRANCH_WIDTH = 256
FFN_DIM = 2816
ROPE_THETA = 10000.0
EPS = 1e-6
NEG_INF = -1e30

NA_IN = 3 * NA_HEADS * HEAD_DIM
MLA_IN = MLA_Q_LORA + MLA_KV_LORA + MLA_ROPE
SG_IN = 2 * SG_WIDTH
GQA_IN = (GQA_HEADS + 2 * GQA_KV_HEADS) * HEAD_DIM

LANES = 128
MXU_DIM = 256
BF16_SUBLANES = 16
V_AUG = HEAD_DIM + BF16_SUBLANES
LOG2E = math.log2(math.e)
MLA_LOGIT_SCALE = (MLA_NOPE + MLA_ROPE) ** -0.5 * LOG2E
GQA_LOGIT_SCALE = HEAD_DIM ** -0.5 * LOG2E

C_NA = 0
C_MLA = C_NA + NA_IN
C_SG = C_MLA + 640
C_GQA = C_SG + SG_IN
IN_ARR = C_GQA + 640

G_NA, G_CQ, G_CKV, G_MQ, G_MKN, G_MKR, G_SG, G_GQA, G_TOT = 0, 512, 768, 1024, 1536, 2048, 2176, 2432, 2944

FFN_CHUNK = 256
N_FFN_CHUNKS = FFN_DIM // FFN_CHUNK
FFN_LOOKAHEAD = 2
HALO = BF16_SUBLANES


def _const_spec(shape):
    nd = len(shape)
    return pl.BlockSpec(shape, lambda *_: (0,) * nd, pipeline_mode=pl.Buffered(1))


def _layer_spec(shape, layer):
    nd = len(shape)
    return pl.BlockSpec((None,) + tuple(shape), lambda *_: (layer,) + (0,) * nd, pipeline_mode=pl.Buffered(1))


def _params(vmem_mb, n_grid):
    return pltpu.CompilerParams(dimension_semantics=("arbitrary",) * n_grid,
                                vmem_limit_bytes=vmem_mb * 1024 * 1024)


def _dot(a, b):
    return jnp.dot(a, b, preferred_element_type=F32)


def _dot_nt(a, b):
    return lax.dot_general(a, b, (((1,), (1,)), ((), ())), preferred_element_type=F32)


def _rms(xf):
    return xf * lax.rsqrt(jnp.mean(xf * xf, axis=-1, keepdims=True) + EPS)


def _segsum(x2, s_ref):
    hi = x2.astype(BF16)
    lo = (x2 - hi.astype(F32)).astype(BF16)
    s = s_ref[...]
    parts = []
    for j in range(x2.shape[1] // MXU_DIM):
        sl = slice(MXU_DIM * j, MXU_DIM * (j + 1))
        parts.append(_dot(hi[:, sl], s) + _dot(lo[:, sl], s))
    return parts[0] if len(parts) == 1 else jnp.concatenate(parts, axis=1)


def _rope(x, cos, sin_signed, shift, first_half):
    w = x.shape[1]
    partner = jnp.where(first_half, pltpu.roll(x, w - shift, 1), pltpu.roll(x, shift, 1))
    return x * cos + partner * sin_signed


def _ada_kernel(c_ref, w_ref, b_ref, o_ref):
    c = c_ref[...]
    a = (c * jax.nn.sigmoid(c)).astype(BF16)
    o_ref[...] = _dot(a, w_ref[...].astype(BF16)) + b_ref[...]


def _ada_call(cvec, w_ada, b_ada):
    n_layers, d, n = w_ada.shape
    tn = 1536
    return pl.pallas_call(
        _ada_kernel,
        grid=(n_layers, n // tn),
        in_specs=[pl.BlockSpec((8, d), lambda l, j: (0, 0)),
                  pl.BlockSpec((None, d, tn), lambda l, j: (l, 0, j)),
                  pl.BlockSpec((None, 1, tn), lambda l, j: (l, 0, j))],
        out_specs=pl.BlockSpec((None, 8, tn), lambda l, j: (l, 0, j)),
        out_shape=jax.ShapeDtypeStruct((n_layers, 8, n), F32),
        compiler_params=_params(32, 2),
        name="ada",
    )(cvec, w_ada, b_ada.reshape(n_layers, 1, n))


def _in_kernel(x_ref, mod_ref, w_in_ref, w_uq_ref, w_ukv_ref, g_ref, s64_ref, smq_ref, invq_ref,
               gc_ref, gs_ref, mc_ref, ms_ref, sgw_ref, sgb_ref,
               naq_ref, nak_ref, nav_ref, mq_ref, mk_ref, mvt_ref, sgo_ref, gq_ref, gk_ref, gvt_ref):
    tm = x_ref.shape[0]
    x = x_ref[...]
    h = (_rms(x) * (1.0 + mod_ref[1:2, :]) + mod_ref[0:1, :]).astype(BF16)

    def gain(off, width):
        return g_ref[:, off:off + width]

    inv_head = 1.0 / HEAD_DIM

    z = _dot(h, w_in_ref[:, C_NA:C_NA + NA_IN])
    qk = z[:, 0:512]
    qk = qk * lax.rsqrt(_segsum(qk * qk, s64_ref) * inv_head + EPS) * gain(G_NA, 512)
    naq_ref[...] = qk[:, 0:256].astype(BF16)
    nak_ref[...] = qk[:, 256:512].astype(BF16)
    nav_ref[...] = z[:, 512:768].astype(BF16)

    z = _dot(h, w_in_ref[:, C_MLA:C_MLA + 640])
    cq, ckv, kr = z[:, 0:256], z[:, 256:512], z[:, 512:640]
    cq = cq * lax.rsqrt(jnp.mean(cq * cq, axis=-1, keepdims=True) + EPS) * gain(G_CQ, 256)
    ckv = ckv * lax.rsqrt(jnp.sum(ckv * ckv, axis=-1, keepdims=True) * (1.0 / MLA_KV_LORA) + EPS) * gain(G_CKV, 256)
    q = _dot(cq.astype(BF16), w_uq_ref[...])
    q = q * lax.rsqrt(_segsum(q * q, smq_ref) * invq_ref[...] + EPS) * gain(G_MQ, 512)
    kv = _dot(ckv.astype(BF16), w_ukv_ref[...])
    kn = kv[:, 0:512]
    kn = kn * lax.rsqrt(_segsum(kn * kn, s64_ref) * inv_head + EPS) * gain(G_MKN, 512)
    kr = kr * lax.rsqrt(jnp.sum(kr * kr, axis=-1, keepdims=True) * (1.0 / MLA_ROPE) + EPS) * gain(G_MKR, 128)
    lane = lax.broadcasted_iota(jnp.int32, (tm, LANES), 1)
    first8 = (lane & 15) < 8
    mc, ms = mc_ref[...], ms_ref[...]
    kr = _rope(kr, mc, ms, 8, first8)
    for hh in range(MLA_HEADS):
        sl = slice(LANES * hh, LANES * (hh + 1))
        mq_ref[hh] = _rope(q[:, sl], mc, ms, 8, first8).astype(BF16)
        mk_ref[hh] = (kn[:, sl] + kr).astype(BF16)
    vt = kv[:, 512:768].T
    ones = jnp.ones((BF16_SUBLANES, tm), BF16)
    for hh in range(MLA_HEADS):
        mvt_ref[hh, 0:HEAD_DIM, :] = vt[HEAD_DIM * hh:HEAD_DIM * (hh + 1), :].astype(BF16)
        mvt_ref[hh, HEAD_DIM:V_AUG, :] = ones

    z = _dot(h, w_in_ref[:, C_SG:C_SG + SG_IN])
    uv = jax.nn.gelu(z)
    u, v = uv[:, 0:SG_WIDTH], uv[:, SG_WIDTH:]
    v = v * lax.rsqrt(jnp.mean(v * v, axis=-1, keepdims=True) + EPS) * gain(G_SG, 256)
    grp = lax.broadcasted_iota(jnp.int32, (SG_CHUNK, SG_WIDTH), 1) >> 6
    for c in range(tm // SG_CHUNK):
        rows = slice(SG_CHUNK * c, SG_CHUNK * (c + 1))
        vc = v[rows, :]
        mixed = sgb_ref[...]
        for gi in range(SG_GROUPS):
            mixed = mixed + _dot(sgw_ref[gi], jnp.where(grp == gi, vc, 0.0).astype(BF16))
        sgo_ref[rows, :] = (u[rows, :] * mixed).astype(BF16)

    z = _dot(h, w_in_ref[:, C_GQA:C_GQA + 640])
    qk = z[:, 0:512]
    qk = qk * lax.rsqrt(_segsum(qk * qk, s64_ref) * inv_head + EPS) * gain(G_GQA, 512)
    lane2 = lax.broadcasted_iota(jnp.int32, (tm, 2 * LANES), 1)
    first16 = (lane2 & 31) < 16
    gc, gs = gc_ref[...], gs_ref[...]
    qn = _rope(qk[:, 0:256], gc, gs, 16, first16)
    kn = _rope(qk[:, 256:512], gc, gs, 16, first16)
    half = lane >> 6
    for g in range(GQA_KV_HEADS):
        sl = slice(LANES * g, LANES * (g + 1))
        for r in range(GQA_HEADS // GQA_KV_HEADS):
            gq_ref[2 * g + r] = jnp.where(half == r, qn[:, sl], 0.0).astype(BF16)
        gk_ref[g] = kn[:, sl].astype(BF16)
    vt = z[:, 512:640].T
    for g in range(GQA_KV_HEADS):
        gvt_ref[g, 0:HEAD_DIM, :] = vt[HEAD_DIM * g:HEAD_DIM * (g + 1), :].astype(BF16)
        gvt_ref[g, HEAD_DIM:V_AUG, :] = ones


def _in_call(x, mod, mod_row, lw, tabs, tm):
    b, t, d = x.shape
    nt = t // tm
    if mod_row is None:
        mod_map = lambda i, j: (j, 0, 0)
    else:
        mod_map = lambda i, j: (mod_row, 0, 0)
    tok = lambda i, j: (j, i, 0)
    tab = lambda i, j: (i, 0)
    head_tok = lambda i, j: (j, 0, i, 0)
    head_t = lambda i, j: (j, 0, 0, i)
    in_specs = [
        pl.BlockSpec((None, tm, d), tok),
        pl.BlockSpec((None, 6, d), mod_map),
        _layer_spec((d, IN_ARR), lw["layer"]),
        _layer_spec((MLA_Q_LORA, 512), lw["layer"]),
        _layer_spec((256, 768), lw["layer"]),
        _layer_spec((1, G_TOT), lw["layer"]),
        _const_spec((MXU_DIM, MXU_DIM)),
        _const_spec((MXU_DIM, MXU_DIM)),
        _const_spec((1, 512)),
        pl.BlockSpec((tm, 256), tab), pl.BlockSpec((tm, 256), tab),
        pl.BlockSpec((tm, LANES), tab), pl.BlockSpec((tm, LANES), tab),
        _layer_spec((SG_GROUPS, SG_CHUNK, SG_CHUNK), lw["layer"]),
        _layer_spec((SG_CHUNK, SG_WIDTH), lw["layer"]),
    ]
    out_specs = [
        pl.BlockSpec((None, tm, 256), tok), pl.BlockSpec((None, tm, 256), tok), pl.BlockSpec((None, tm, 256), tok),
        pl.BlockSpec((None, MLA_HEADS, tm, LANES), head_tok),
        pl.BlockSpec((None, MLA_HEADS, tm, LANES), head_tok),
        pl.BlockSpec((None, MLA_HEADS, V_AUG, tm), head_t),
        pl.BlockSpec((None, tm, 256), tok),
        pl.BlockSpec((None, GQA_HEADS, tm, LANES), head_tok),
        pl.BlockSpec((None, GQA_KV_HEADS, tm, LANES), head_tok),
        pl.BlockSpec((None, GQA_KV_HEADS, V_AUG, tm), head_t),
    ]
    sds = jax.ShapeDtypeStruct
    out_shape = [
        sds((b, t, 256), BF16), sds((b, t, 256), BF16), sds((b, t, 256), BF16),
        sds((b, MLA_HEADS, t, LANES), BF16), sds((b, MLA_HEADS, t, LANES), BF16),
        sds((b, MLA_HEADS, V_AUG, t), BF16),
        sds((b, t, 256), BF16),
        sds((b, GQA_HEADS, t, LANES), BF16), sds((b, GQA_KV_HEADS, t, LANES), BF16),
        sds((b, GQA_KV_HEADS, V_AUG, t), BF16),
    ]
    outs = pl.pallas_call(
        _in_kernel, grid=(nt, b), in_specs=in_specs, out_specs=out_specs, out_shape=out_shape,
        compiler_params=_params(48, 2), name="in_proj",
    )(x, mod, lw["w_in"], lw["w_uq"], lw["w_ukv"], lw["gains"], lw["s64"], lw["smq"], lw["invq"],
      tabs["gc"], tabs["gs"], tabs["mc"], tabs["ms"], lw["sgw"], lw["sgb"])
    names = ("na_q", "na_k", "na_v", "mla_q", "mla_k", "mla_vt", "sg_o", "gqa_q", "gqa_k", "gqa_vt")
    return dict(zip(names, outs))


def _na_heads(q, k_blocks, v_blocks, biases, seeded):
    tq = q.shape[0]
    head_of_lane = lax.broadcasted_iota(jnp.int32, (tq, 256), 1) >> 6
    qf = q.astype(F32) * (HEAD_DIM ** -0.5)
    qhs = [jnp.where(head_of_lane == hh, qf, 0.0).astype(BF16) for hh in range(NA_HEADS)]
    vts = [vb.astype(F32).T.astype(BF16) for vb in v_blocks]
    units = [(hh, j) for hh in range(NA_HEADS) for j in range(len(k_blocks))]

    def score(u):
        hh, j = u
        s = _dot_nt(k_blocks[j], qhs[hh])
        return s if biases[j] is None else s + biases[j][hh]

    def value_rows(hh, j):
        ones = jnp.ones((BF16_SUBLANES, vts[j].shape[1]), BF16)
        return jnp.concatenate([vts[j][HEAD_DIM * hh:HEAD_DIM * (hh + 1), :], ones], axis=0)

    lookahead = SCORE_LOOKAHEAD if seeded else len(k_blocks)
    pending = [score(u) for u in units[:lookahead]]
    accs, m = [], None
    for n, (hh, j) in enumerate(units):
        if j == 0:
            acc = jnp.zeros((V_AUG, tq), F32)
            head = [pending.pop(0) for _ in range(1 if seeded else len(k_blocks))]
            m = _col_max(head[0])
            for s in head[1:]:
                m = jnp.maximum(m, _col_max(s))
        else:
            head = head[1:] if not seeded else [pending.pop(0)]
        if n + lookahead < len(units):
            pending.append(score(units[n + lookahead]))
        acc = acc + _dot(value_rows(hh, j), jnp.exp(head[0] - m).astype(BF16))
        if j == len(k_blocks) - 1:
            accs.append(acc)
    out = jnp.concatenate([a[0:HEAD_DIM] * (1.0 / a[HEAD_DIM:HEAD_DIM + 1]) for a in accs], axis=0).T
    return out, accs


def _na_kernel(q_ref, kp_ref, kc_ref, kn_ref, vp_ref, vc_ref, vn_ref, kx_ref, vx_ref, bias_ref, o_ref):
    band = lambda refs: jnp.concatenate([r[...] for r in refs], axis=0)
    operands = lambda: (q_ref[...], [kx_ref[...], band((kp_ref, kc_ref, kn_ref))],
                        [vx_ref[...], band((vp_ref, vc_ref, vn_ref))], [None, bias_ref])
    out, accs = _na_heads(*operands(), seeded=True)
    o_ref[...] = out.astype(BF16)
    unusable = jnp.zeros((V_AUG, q_ref.shape[0]), F32)
    for acc in accs:
        unusable = jnp.maximum(unusable, jnp.where(jnp.abs(acc) < F32_SAFE_MAX, 0.0, 1.0))

    @pl.when(jnp.max(unusable) > 0.0)
    def _():
        o_ref[...] = _na_heads(*operands(), seeded=False)[0].astype(BF16)


def _na_call(q, k, v, kx, vx, bias, layer):
    b, t, w = q.shape
    c = kx.shape[1]
    tq = 4 * GRID_W
    nt = t // tq
    cur = lambda bi, i: (bi, i, 0)
    prev = lambda bi, i: (bi, jnp.maximum(i - 1, 0), 0)
    nxt = lambda bi, i: (bi, jnp.minimum(i + 1, nt - 1), 0)
    ctx = lambda bi, i: (bi, 0, 0)
    variant = lambda bi, i: (layer, jnp.where(i == 0, 0, jnp.where(i == nt - 1, 2, 1)), 0, 0, 0)
    blk = lambda m: pl.BlockSpec((None, tq, w), m)
    return pl.pallas_call(
        _na_kernel, grid=(b, nt),
        in_specs=[blk(cur), blk(prev), blk(cur), blk(nxt), blk(prev), blk(cur), blk(nxt),
                  pl.BlockSpec((None, c, w), ctx), pl.BlockSpec((None, c, w), ctx),
                  pl.BlockSpec((None, None, NA_HEADS, 3 * tq, tq), variant)],
        out_specs=blk(cur),
        out_shape=jax.ShapeDtypeStruct((b, t, w), BF16),
        compiler_params=_params(48, 2), name="na_attn",
    )(q, k, k, k, v, v, v, kx, vx, bias)


def _na_ctx_kernel(q_ref, k_ref, v_ref, o_ref):
    o_ref[...] = _na_heads(q_ref[...], [k_ref[...]], [v_ref[...]], [None], seeded=False)[0].astype(BF16)


def _na_ctx_call(q, k, v):
    b, c, w = q.shape
    spec = pl.BlockSpec((None, c, w), lambda bi: (bi, 0, 0))
    return pl.pallas_call(
        _na_ctx_kernel, grid=(b,), in_specs=[spec, spec, spec], out_specs=spec,
        out_shape=jax.ShapeDtypeStruct((b, c, w), BF16),
        compiler_params=_params(32, 1), name="na_ctx_attn",
    )(q, k, v)


def _na_bias(rpb, nt):
    rows = 4 * nt
    col = np.arange(GRID_W)
    c_start = np.clip(col - NA_COLS // 2, 0, GRID_W - NA_COLS)
    valid_c = (col[None, :] >= c_start[:, None]) & (col[None, :] < c_start[:, None] + NA_COLS)
    dc = np.clip(col[None, :] - col[:, None] + (NA_COLS - 1), 0, 2 * NA_COLS - 2)
    pick_dc = jnp.asarray(dc.T[:, :, None] == np.arange(2 * NA_COLS - 1), F32)
    by_col = jnp.einsum("lhrd,kqd->lhrkq", rpb, pick_dc, precision=lax.Precision.HIGHEST)
    by_col = jnp.where(jnp.asarray(valid_c.T), by_col, NEG_INF)
    masked = jnp.full(by_col.shape[:2] + (GRID_W, GRID_W), NEG_INF, F32)
    variants = []
    for i in (0, min(1, nt - 1), nt - 1):
        rq = 4 * i + np.arange(4)
        start = np.clip(rq - NA_ROWS // 2, 0, rows - NA_ROWS)
        blocks = np.array([i - 1, i, i + 1])
        rk = (4 * blocks[:, None] + np.arange(4)[None, :]).reshape(-1)
        block_ok = np.repeat((blocks >= 0) & (blocks < nt), 4)
        valid_r = (rk[None, :] >= start[:, None]) & (rk[None, :] < start[:, None] + NA_ROWS) & block_ok[None, :]
        dr = np.clip(rk[None, :] - rq[:, None] + (NA_ROWS - 1), 0, 2 * NA_ROWS - 2)
        key_rows = [jnp.concatenate([by_col[:, :, dr[a, kr]] if valid_r[a, kr] else masked for a in range(4)], axis=-1)
                    for kr in range(len(rk))]
        variants.append(jnp.concatenate(key_rows, axis=-2))
    return jnp.stack(variants, axis=1)


TOKEN_TILE = 512
QUERY_BLOCK = 512
KEY_CHUNK = 512
FLASH_COL = MXU_DIM
STREAM_UNROLL = 16
SCORE_LOOKAHEAD = 2
SEED_KEYS = 64
F32_SAFE_MAX = 3e38
MIN_DENOMINATOR = 2.0 ** -40


def _col_max(s):
    keys = s.shape[0]
    if keys % 64 == 0 and keys > 64:
        s = jnp.max(s.reshape(keys // 64, 64, s.shape[1]), axis=0)
    return jnp.max(s, axis=0, keepdims=True)


def _flash_kernel(*refs, n_heads, group, tq, tk, has_ctx):
    if has_ctx:
        q_ref, k_ref, vt_ref, kx_ref, vtx_ref, o_ref, acc_scr, ot_scr = refs
    else:
        q_ref, k_ref, vt_ref, o_ref, acc_scr, ot_scr = refs
    n_sub = q_ref.shape[1] // tq
    n_chunks = k_ref.shape[1] // tk
    n_slabs = n_heads // group
    cols = [(hh, qi) for hh in range(n_heads) for qi in range(n_sub)]
    heads = range(len(cols))
    slab_of = [hh // group for hh, _ in cols]

    def q_col(c):
        hh, qi = cols[c]
        return q_ref[hh, qi * tq:(qi + 1) * tq, :]

    def load_chunk(j):
        off = pl.multiple_of(j * tk, tk)
        return ([k_ref[sl, pl.ds(off, tk), :] for sl in range(n_slabs)],
                [vt_ref[sl, :, pl.ds(off, tk)] for sl in range(n_slabs)])

    def pv(vt, p):
        return _dot(vt, p.astype(BF16))

    def exact_step(ks, vts, ms):
        new_ms = []
        for hh in heads:
            s = _dot_nt(ks[slab_of[hh]], q_col(hh))
            m_new = jnp.maximum(ms[hh], _col_max(s))
            p = jnp.exp2(s - m_new)
            acc_scr[hh] = acc_scr[hh] * jnp.exp2(ms[hh] - m_new) + pv(vts[slab_of[hh]], p)
            new_ms.append(m_new)
        return tuple(new_ms)

    def stream(chunks, ms):
        units = [(i, hh) for i in range(len(chunks)) for hh in heads]
        score = lambda u: _dot_nt(chunks[u[0]][0][slab_of[u[1]]], q_col(u[1]))
        pending = [score(u) for u in units[:SCORE_LOOKAHEAD]]
        for n, (i, hh) in enumerate(units):
            s = pending.pop(0)
            if n + SCORE_LOOKAHEAD < len(units):
                pending.append(score(units[n + SCORE_LOOKAHEAD]))
            acc_scr[hh] += pv(chunks[i][1][slab_of[hh]], jnp.exp2(s - ms[hh]))

    ctx_chunk = ([kx_ref[sl] for sl in range(n_slabs)], [vtx_ref[sl] for sl in range(n_slabs)]) if has_ctx else None
    seed_ref = kx_ref if has_ctx else k_ref
    ms = tuple(_col_max(_dot_nt(seed_ref[slab_of[hh], 0:SEED_KEYS, :], q_col(hh))) for hh in heads)
    acc_scr[...] = jnp.zeros_like(acc_scr)
    if has_ctx:
        stream([ctx_chunk], ms)
    per_iter = max(u for u in (STREAM_UNROLL, 8, 4, 2, 1) if n_chunks % u == 0)

    def stream_body(j, carry):
        stream([load_chunk(j * per_iter + i) for i in range(per_iter)], ms)
        return carry

    lax.fori_loop(0, n_chunks // per_iter, stream_body, 0)
    unusable = jnp.zeros((V_AUG, tq), F32)
    for hh in heads:
        acc = acc_scr[hh]
        unusable = jnp.maximum(unusable, jnp.where(jnp.abs(acc) < F32_SAFE_MAX, 0.0, 1.0))
        unusable = jnp.maximum(unusable, jnp.where(acc[HEAD_DIM:HEAD_DIM + 1] > MIN_DENOMINATOR, 0.0, 1.0))

    @pl.when(jnp.max(unusable) > 0.0)
    def _():
        acc_scr[...] = jnp.zeros_like(acc_scr)
        ms = tuple(jnp.full((1, tq), NEG_INF, F32) for _ in heads)
        if has_ctx:
            ms = exact_step(*ctx_chunk, ms)
        lax.fori_loop(0, n_chunks, lambda j, m: exact_step(*load_chunk(j), m), ms)

    for c, (hh, qi) in enumerate(cols):
        acc = acc_scr[c]
        ot_scr[HEAD_DIM * hh:HEAD_DIM * (hh + 1), qi * tq:(qi + 1) * tq] = (
            acc[0:HEAD_DIM] * (1.0 / acc[HEAD_DIM:HEAD_DIM + 1]))
    o_ref[...] = ot_scr[...].T.astype(BF16)


def _flash_call(q, k, vt, kx, vtx, tq, tk):
    b, n_heads, t_q, w = q.shape
    slabs, t_k = k.shape[1], k.shape[2]
    has_ctx = kx is not None
    full4 = lambda bi, i: (bi, 0, 0, 0)
    in_specs = [pl.BlockSpec((None, n_heads, tq, w), lambda bi, i: (bi, 0, i, 0)),
                pl.BlockSpec((None, slabs, t_k, w), full4),
                pl.BlockSpec((None, slabs, V_AUG, t_k), full4)]
    args = [q, k, vt]
    if has_ctx:
        c = kx.shape[2]
        in_specs += [pl.BlockSpec((None, slabs, c, w), full4), pl.BlockSpec((None, slabs, V_AUG, c), full4)]
        args += [kx, vtx]
    col = min(FLASH_COL, tq)
    kern = functools.partial(_flash_kernel, n_heads=n_heads, group=n_heads // slabs, tq=col, tk=tk,
                             has_ctx=has_ctx)
    return pl.pallas_call(
        kern, grid=(b, t_q // tq), in_specs=in_specs,
        out_specs=pl.BlockSpec((None, tq, n_heads * HEAD_DIM), lambda bi, i: (bi, i, 0)),
        out_shape=jax.ShapeDtypeStruct((b, t_q, n_heads * HEAD_DIM), BF16),
        scratch_shapes=[pltpu.VMEM((n_heads * (tq // col), V_AUG, col), F32),
                        pltpu.VMEM((n_heads * HEAD_DIM, tq), F32)],
        compiler_params=_params(56, 2), name="flash",
    )(*args)


def _merge_kernel(x_ref, mod_ref, o0_ref, o1_ref, o2_ref, o3_ref, wg_ref, bg_ref, wb_ref, wo_ref, out_ref):
    x = x_ref[...]
    h = (_rms(x) * (1.0 + mod_ref[1:2, :]) + mod_ref[0:1, :]).astype(BF16)
    y = None
    for i, o_ref in enumerate((o0_ref, o1_ref, o2_ref, o3_ref)):
        gate = jax.nn.sigmoid(_dot(h, wg_ref[i]) + bg_ref[i])
        term = gate * _dot(o_ref[...], wb_ref[i])
        y = term if y is None else y + term
    out_ref[...] = x + mod_ref[2:3, :] * _dot(y.astype(BF16), wo_ref[...])


def _merge_call(x, mod, mod_row, branches, lw, tm):
    b, t, d = x.shape
    if mod_row is None:
        mod_map = lambda i, j: (j, 0, 0)
    else:
        mod_map = lambda i, j: (mod_row, 0, 0)
    tok = lambda i, j: (j, i, 0)
    br = pl.BlockSpec((None, tm, BRANCH_WIDTH), tok)
    return pl.pallas_call(
        _merge_kernel, grid=(t // tm, b),
        in_specs=[pl.BlockSpec((None, tm, d), tok), pl.BlockSpec((None, 6, d), mod_map), br, br, br, br,
                  _layer_spec((N_BRANCH, d, d), lw["layer"]), _layer_spec((N_BRANCH, 1, d), lw["layer"]),
                  _layer_spec((N_BRANCH, BRANCH_WIDTH, d), lw["layer"]), _layer_spec((d, d), lw["layer"])],
        out_specs=pl.BlockSpec((None, tm, d), tok),
        out_shape=jax.ShapeDtypeStruct((b, t, d), F32),
        compiler_params=_params(56, 2), name="merge",
    )(x, mod, *branches, lw["w_gate"], lw["b_gate"], lw["w_branch"], lw["w_out"])


def _ffn_kernel(x_ref, xp_ref, xn_ref, mod_ref, wu_ref, cw_ref, cb_ref, wd_ref, out_ref, h_scr, act_scr):
    tm = x_ref.shape[0]
    i = pl.program_id(0)
    nt = pl.num_programs(0)
    sh, sc = mod_ref[3:4, :], mod_ref[4:5, :]

    def modulated(v):
        return _rms(v) * (1.0 + sc) + sh

    x = x_ref[...]
    h_scr[0:HALO, :] = jnp.where(i > 0, modulated(xp_ref[...]), 0.0).astype(BF16)
    h_scr[HALO:HALO + tm, :] = modulated(x).astype(BF16)
    h_scr[HALO + tm:, :] = jnp.where(i < nt - 1, modulated(xn_ref[...]), 0.0).astype(BF16)
    rows = tm + 2 * HALO
    main = slice(HALO, HALO + tm)

    def cols(j, gate):
        lo = gate * FFN_DIM + FFN_CHUNK * j
        return slice(lo, lo + FFN_CHUNK)

    def up_proj(j):
        return tuple(_dot(h_scr[...], wu_ref[:, cols(j, gate)]) for gate in (0, 1))

    def conv(up, sl):
        return (pltpu.roll(up, 1, 0)[main] * cw_ref[0:1, sl] + up[main] * cw_ref[1:2, sl]
                + pltpu.roll(up, rows - 1, 0)[main] * cw_ref[2:3, sl] + cb_ref[:, sl])

    ups = [up_proj(j) for j in range(FFN_LOOKAHEAD)]
    for j in range(N_FFN_CHUNKS):
        up_a, up_g = ups.pop(0)
        if j + FFN_LOOKAHEAD < N_FFN_CHUNKS:
            ups.append(up_proj(j + FFN_LOOKAHEAD))
        a, g = conv(up_a, cols(j, 0)), conv(up_g, cols(j, 1))
        act_scr[:, cols(j, 0)] = (g * jax.nn.sigmoid(g) * a).astype(BF16)
    out_ref[...] = x + mod_ref[5:6, :] * _dot(act_scr[...], wd_ref[...])


def _ffn_call(x, mod, mod_row, lw, tm):
    b, t, d = x.shape
    nt = t // tm
    hb = tm // HALO
    last_halo = t // HALO - 1
    if mod_row is None:
        mod_map = lambda i, j: (j, 0, 0)
    else:
        mod_map = lambda i, j: (mod_row, 0, 0)
    tok = lambda i, j: (j, i, 0)
    return pl.pallas_call(
        _ffn_kernel, grid=(nt, b),
        in_specs=[pl.BlockSpec((None, tm, d), tok),
                  pl.BlockSpec((None, HALO, d), lambda i, j: (j, jnp.maximum(i * hb - 1, 0), 0)),
                  pl.BlockSpec((None, HALO, d), lambda i, j: (j, jnp.minimum((i + 1) * hb, last_halo), 0)),
                  pl.BlockSpec((None, 6, d), mod_map),
                  _layer_spec((d, 2 * FFN_DIM), lw["layer"]),
                  _layer_spec((3, 2 * FFN_DIM), lw["layer"]),
                  _layer_spec((1, 2 * FFN_DIM), lw["layer"]),
                  _layer_spec((FFN_DIM, d), lw["layer"])],
        out_specs=pl.BlockSpec((None, tm, d), tok),
        out_shape=jax.ShapeDtypeStruct((b, t, d), F32),
        scratch_shapes=[pltpu.VMEM((tm + 2 * HALO, d), BF16), pltpu.VMEM((tm, FFN_DIM), BF16)],
        compiler_params=_params(56, 2), name="ffn",
    )(x, x, x, mod, lw["w_up"], lw["conv_w"], lw["conv_b"], lw["w_down"])


def _block_diag_ones(width, segs):
    m = np.zeros((width, width), np.float32)
    for lo, hi in segs:
        m[lo:hi, lo:hi] = 1.0
    return jnp.asarray(m, BF16)


def _rope_pattern(pos, width):
    half = width // 2
    inv = ROPE_THETA ** (-jnp.arange(half, dtype=F32) / half)
    ang = pos.astype(F32)[:, None] * inv[None, :]
    cos, sin = jnp.cos(ang), jnp.sin(ang)
    return jnp.concatenate([cos, cos], axis=1), jnp.concatenate([-sin, sin], axis=1)


def _rope_tables(t, identity):
    if identity:
        return {"gc": jnp.ones((t, 256), F32), "gs": jnp.zeros((t, 256), F32),
                "mc": jnp.ones((t, LANES), F32), "ms": jnp.zeros((t, LANES), F32)}
    n_rows = t // GRID_W

    def grid_tables(width):
        rc, rs = _rope_pattern(jnp.arange(n_rows), width)
        cc, cs = _rope_pattern(jnp.arange(GRID_W), width)
        expand = lambda r, c: jnp.concatenate([jnp.repeat(r, GRID_W, axis=0), jnp.tile(c, (n_rows, 1))], axis=1)
        return expand(rc, cc), expand(rs, cs)

    gcos, gsin = grid_tables(HEAD_DIM // 2)
    mcos, msin = grid_tables(MLA_ROPE // 2)
    one, zero = jnp.ones((t, MLA_NOPE), F32), jnp.zeros((t, MLA_NOPE), F32)
    pad1, pad0 = jnp.ones((t, LANES - MLA_NOPE - MLA_ROPE), F32), jnp.zeros((t, LANES - MLA_NOPE - MLA_ROPE), F32)
    return {"gc": jnp.tile(gcos, (1, 4)), "gs": jnp.tile(gsin, (1, 4)),
            "mc": jnp.concatenate([one, mcos, pad1], axis=1),
            "ms": jnp.concatenate([zero, msin, pad0], axis=1)}


def _arrange(p):
    d = D_MODEL
    n_layers = p["w_in"].shape[0]
    cat = lambda parts: jnp.concatenate(parts, axis=-1)
    w_in = p["w_in"]
    na, mla = w_in[..., :NA_IN], w_in[..., NA_IN:NA_IN + MLA_IN]
    sg = w_in[..., NA_IN + MLA_IN:NA_IN + MLA_IN + SG_IN]
    gqa = w_in[..., NA_IN + MLA_IN + SG_IN:]
    z = lambda n: jnp.zeros((n_layers, d, n), F32)
    cq, ckv, kr = (mla[..., :MLA_Q_LORA], mla[..., MLA_Q_LORA:MLA_Q_LORA + MLA_KV_LORA],
                   mla[..., MLA_Q_LORA + MLA_KV_LORA:])
    gq, gk, gv = gqa[..., :256], gqa[..., 256:384], gqa[..., 384:]
    gk_dup = cat([gk[..., :64], gk[..., :64], gk[..., 64:], gk[..., 64:]])
    w_in_arr = cat([na, cq, ckv, z(256 - MLA_KV_LORA), z(MLA_NOPE), kr, z(LANES - MLA_NOPE - MLA_ROPE),
                    sg, gq, gk_dup, gv]).astype(BF16)

    lane_pad = lambda v, n: jnp.pad(v, ((0, 0),) * (v.ndim - 1) + ((0, n),))
    w_uq = p["mla_w_uq"].reshape(n_layers, MLA_Q_LORA, MLA_HEADS, MLA_NOPE + MLA_ROPE)
    w_uq = lane_pad(w_uq, LANES - MLA_NOPE - MLA_ROPE).reshape(n_layers, MLA_Q_LORA, MLA_HEADS * LANES)
    w_ukv = p["mla_w_ukv"].reshape(n_layers, MLA_KV_LORA, MLA_HEADS, MLA_NOPE + MLA_V)
    k_part = lane_pad(w_ukv[..., :MLA_NOPE], LANES - MLA_NOPE).reshape(n_layers, MLA_KV_LORA, MLA_HEADS * LANES)
    v_part = w_ukv[..., MLA_NOPE:].reshape(n_layers, MLA_KV_LORA, MLA_HEADS * MLA_V)
    w_ukv = jnp.pad(cat([k_part, v_part]), ((0, 0), (0, 256 - MLA_KV_LORA), (0, 0)))

    qg, kg = p["mla_q_norm"], p["mla_k_norm"]
    z1 = lambda n: jnp.zeros((n_layers, n), F32)
    tile4 = lambda v: jnp.tile(v, (1, 4))
    gains = cat([
        tile4(p["na_q_norm"]), tile4(p["na_k_norm"]),
        p["mla_cq_norm"],
        p["mla_ckv_norm"], z1(256 - MLA_KV_LORA),
        tile4(cat([qg * MLA_LOGIT_SCALE, z1(LANES - MLA_NOPE - MLA_ROPE)])),
        tile4(cat([kg[:, :MLA_NOPE], z1(LANES - MLA_NOPE)])),
        z1(MLA_NOPE), kg[:, MLA_NOPE:], z1(LANES - MLA_NOPE - MLA_ROPE),
        p["sg_v_norm"],
        tile4(p["gqa_q_norm"] * GQA_LOGIT_SCALE), tile4(p["gqa_k_norm"]),
    ]).reshape(n_layers, 1, G_TOT)

    invq = np.tile(np.concatenate([np.full(MLA_NOPE, 1.0 / MLA_NOPE), np.full(MLA_ROPE, 1.0 / MLA_ROPE),
                                   np.ones(LANES - MLA_NOPE - MLA_ROPE)]), 4).astype(np.float32).reshape(1, 512)
    sgb = jnp.repeat(jnp.swapaxes(p["sg_b_s"], -1, -2), SG_WIDTH // SG_GROUPS, axis=-1)

    return {
        "w_in": w_in_arr, "w_uq": w_uq.astype(BF16), "w_ukv": w_ukv.astype(BF16), "gains": gains,
        "s64": _block_diag_ones(MXU_DIM, [(64 * i, 64 * i + 64) for i in range(4)]),
        "smq": _block_diag_ones(MXU_DIM, [(0, 64), (64, 96), (128, 192), (192, 224)]),
        "invq": jnp.asarray(invq),
        "sgw": p["sg_w_s"].astype(BF16), "sgb": sgb,
        "w_gate": p["w_gate"].astype(BF16), "b_gate": p["b_gate"].reshape(n_layers, N_BRANCH, 1, d),
        "w_branch": p["w_branch"].astype(BF16), "w_out": p["w_out"].astype(BF16),
        "w_up": p["w_up"].astype(BF16), "conv_w": p["conv_w"], "conv_b": p["conv_b"].reshape(n_layers, 1, -1),
        "w_down": p["w_down"].astype(BF16),
    }


def kernel(x, c, ctx, c_ctx, w_ada, b_ada, w_in, na_q_norm, na_k_norm, na_rpb, mla_cq_norm, mla_ckv_norm,
           mla_w_uq, mla_w_ukv, mla_q_norm, mla_k_norm, sg_v_norm, sg_w_s, sg_b_s, gqa_q_norm, gqa_k_norm,
           w_branch, w_gate, b_gate, w_out, w_up, conv_w, conv_b, w_down):
    b, t, d = x.shape
    n_ctx = ctx.shape[1]
    depth = w_in.shape[0]
    ctx_row = b
    cvec = jnp.zeros((8, d), F32).at[:b].set(c).at[ctx_row].set(c_ctx)
    mod_all = _ada_call(cvec, w_ada, b_ada).reshape(depth, 8, 6, d)

    assert b < 8 and t % (4 * GRID_W) == 0 and n_ctx % SG_CHUNK == 0 and n_ctx >= SEED_KEYS
    tm = min(TOKEN_TILE, t)
    tq = min(QUERY_BLOCK, t)
    tk = min(KEY_CHUNK, t)
    assert t % tm == 0 and t % tq == 0 and t % tk == 0
    tabs_lat = _rope_tables(t, identity=False)
    tabs_ctx = _rope_tables(n_ctx, identity=True)
    na_bias = _na_bias(na_rpb, t // (4 * GRID_W))

    lw = _arrange({
        "w_in": w_in, "na_q_norm": na_q_norm, "na_k_norm": na_k_norm,
        "mla_cq_norm": mla_cq_norm, "mla_ckv_norm": mla_ckv_norm, "mla_w_uq": mla_w_uq,
        "mla_w_ukv": mla_w_ukv, "mla_q_norm": mla_q_norm, "mla_k_norm": mla_k_norm,
        "sg_v_norm": sg_v_norm, "sg_w_s": sg_w_s, "sg_b_s": sg_b_s,
        "gqa_q_norm": gqa_q_norm, "gqa_k_norm": gqa_k_norm,
        "w_branch": w_branch, "w_gate": w_gate, "b_gate": b_gate, "w_out": w_out,
        "w_up": w_up, "conv_w": conv_w, "conv_b": conv_b, "w_down": w_down})
    cx = ctx
    for l in range(depth):
        lw = dict(lw, layer=l)
        mod = mod_all[l]
        pc = _in_call(cx, mod, ctx_row, lw, tabs_ctx, n_ctx)
        pz = _in_call(x, mod, None, lw, tabs_lat, tm)

        o_na = _na_call(pz["na_q"], pz["na_k"], pz["na_v"], pc["na_k"], pc["na_v"], na_bias, l)
        o_mla = _flash_call(pz["mla_q"], pz["mla_k"], pz["mla_vt"], pc["mla_k"], pc["mla_vt"], tq, tk)
        o_gqa = _flash_call(pz["gqa_q"], pz["gqa_k"], pz["gqa_vt"], pc["gqa_k"], pc["gqa_vt"], tq, tk)
        x = _merge_call(x, mod, None, (o_na, o_mla, pz["sg_o"], o_gqa), lw, tm)
        x = _ffn_call(x, mod, None, lw, tm)

        if l < depth - 1:
            oc_na = _na_ctx_call(pc["na_q"], pc["na_k"], pc["na_v"])
            oc_mla = _flash_call(pc["mla_q"], pc["mla_k"], pc["mla_vt"], None, None, n_ctx, n_ctx)
            oc_gqa = _flash_call(pc["gqa_q"], pc["gqa_k"], pc["gqa_vt"], None, None, n_ctx, n_ctx)
            cx = _merge_call(cx, mod, ctx_row, (oc_na, oc_mla, pc["sg_o"], oc_gqa), lw, n_ctx)
            cx = _ffn_call(cx, mod, ctx_row, lw, n_ctx)
    return x
```

```python
import functools
import math

import numpy as np
import jax
import jax.numpy as jnp
from jax import lax
from jax.experimental import pallas as pl
from jax.experimental.pallas import tpu as pltpu

F32 = jnp.float32
BF16 = jnp.bfloat16

D_MODEL = 1024
GRID_W = 64
HEAD_DIM = 64
NA_HEADS = 4
NA_ROWS = 8
NA_COLS = 16
MLA_HEADS = 4
MLA_Q_LORA = 256
MLA_KV_LORA = 192
MLA_NOPE = 64
MLA_ROPE = 32
MLA_V = 64
SG_GROUPS = 4
SG_CHUNK = 128
SG_WIDTH = 256
GQA_HEADS = 4
GQA_KV_HEADS = 2
N_BRANCH = 4
BRANCH_WIDTH = 256
FFN_DIM = 2816
ROPE_THETA = 10000.0
EPS = 1e-6
NEG_INF = -1e30

NA_IN = 3 * NA_HEADS * HEAD_DIM
MLA_IN = MLA_Q_LORA + MLA_KV_LORA + MLA_ROPE
SG_IN = 2 * SG_WIDTH
GQA_IN = (GQA_HEADS + 2 * GQA_KV_HEADS) * HEAD_DIM

LANES = 128
MXU_DIM = 256
BF16_SUBLANES = 16
V_AUG = HEAD_DIM + BF16_SUBLANES
LOG2E = math.log2(math.e)
MLA_LOGIT_SCALE = (MLA_NOPE + MLA_ROPE) ** -0.5 * LOG2E
GQA_LOGIT_SCALE = HEAD_DIM ** -0.5 * LOG2E

C_NA = 0
C_MLA = C_NA + NA_IN
C_SG = C_MLA + 640
C_GQA = C_SG + SG_IN
IN_ARR = C_GQA + 640

G_NA, G_CQ, G_CKV, G_MQ, G_MKN, G_MKR, G_SG, G_GQA, G_TOT = 0, 512, 768, 1024, 1536, 2048, 2176, 2432, 2944

FFN_CHUNK = 256
N_FFN_CHUNKS = FFN_DIM // FFN_CHUNK
FFN_LOOKAHEAD = 2
HALO = BF16_SUBLANES


def _const_spec(shape):
    nd = len(shape)
    return pl.BlockSpec(shape, lambda *_: (0,) * nd, pipeline_mode=pl.Buffered(1))


def _layer_spec(shape, layer):
    nd = len(shape)
    return pl.BlockSpec((None,) + tuple(shape), lambda *_: (layer,) + (0,) * nd, pipeline_mode=pl.Buffered(1))


def _params(vmem_mb, n_grid):
    return pltpu.CompilerParams(dimension_semantics=("arbitrary",) * n_grid,
                                vmem_limit_bytes=vmem_mb * 1024 * 1024)


def _dot(a, b):
    return jnp.dot(a, b, preferred_element_type=F32)


def _dot_nt(a, b):
    return lax.dot_general(a, b, (((1,), (1,)), ((), ())), preferred_element_type=F32)


def _rms(xf):
    return xf * lax.rsqrt(jnp.mean(xf * xf, axis=-1, keepdims=True) + EPS)


def _segsum(x2, s_ref):
    hi = x2.astype(BF16)
    lo = (x2 - hi.astype(F32)).astype(BF16)
    s = s_ref[...]
    parts = []
    for j in range(x2.shape[1] // MXU_DIM):
        sl = slice(MXU_DIM * j, MXU_DIM * (j + 1))
        parts.append(_dot(hi[:, sl], s) + _dot(lo[:, sl], s))
    return parts[0] if len(parts) == 1 else jnp.concatenate(parts, axis=1)


def _rope(x, cos, sin_signed, shift, first_half):
    w = x.shape[1]
    partner = jnp.where(first_half, pltpu.roll(x, w - shift, 1), pltpu.roll(x, shift, 1))
    return x * cos + partner * sin_signed


def _ada_kernel(c_ref, w_ref, b_ref, o_ref):
    c = c_ref[...]
    a = (c * jax.nn.sigmoid(c)).astype(BF16)
    o_ref[...] = _dot(a, w_ref[...].astype(BF16)) + b_ref[...]


def _ada_call(cvec, w_ada, b_ada):
    n_layers, d, n = w_ada.shape
    tn = 1536
    return pl.pallas_call(
        _ada_kernel,
        grid=(n_layers, n // tn),
        in_specs=[pl.BlockSpec((8, d), lambda l, j: (0, 0)),
                  pl.BlockSpec((None, d, tn), lambda l, j: (l, 0, j)),
                  pl.BlockSpec((None, 1, tn), lambda l, j: (l, 0, j))],
        out_specs=pl.BlockSpec((None, 8, tn), lambda l, j: (l, 0, j)),
        out_shape=jax.ShapeDtypeStruct((n_layers, 8, n), F32),
        compiler_params=_params(32, 2),
        name="ada",
    )(cvec, w_ada, b_ada.reshape(n_layers, 1, n))


def _in_kernel(x_ref, mod_ref, w_in_ref, w_uq_ref, w_ukv_ref, g_ref, s64_ref, smq_ref, invq_ref,
               gc_ref, gs_ref, mc_ref, ms_ref, sgw_ref, sgb_ref,
               naq_ref, nak_ref, nav_ref, mq_ref, mk_ref, mvt_ref, sgo_ref, gq_ref, gk_ref, gvt_ref):
    tm = x_ref.shape[0]
    x = x_ref[...]
    h = (_rms(x) * (1.0 + mod_ref[1:2, :]) + mod_ref[0:1, :]).astype(BF16)

    def gain(off, width):
        return g_ref[:, off:off + width]

    inv_head = 1.0 / HEAD_DIM

    z = _dot(h, w_in_ref[:, C_NA:C_NA + NA_IN])
    qk = z[:, 0:512]
    qk = qk * lax.rsqrt(_segsum(qk * qk, s64_ref) * inv_head + EPS) * gain(G_NA, 512)
    naq_ref[...] = qk[:, 0:256].astype(BF16)
    nak_ref[...] = qk[:, 256:512].astype(BF16)
    nav_ref[...] = z[:, 512:768].astype(BF16)

    z = _dot(h, w_in_ref[:, C_MLA:C_MLA + 640])
    cq, ckv, kr = z[:, 0:256], z[:, 256:512], z[:, 512:640]
    cq = cq * lax.rsqrt(jnp.mean(cq * cq, axis=-1, keepdims=True) + EPS) * gain(G_CQ, 256)
    ckv = ckv * lax.rsqrt(jnp.sum(ckv * ckv, axis=-1, keepdims=True) * (1.0 / MLA_KV_LORA) + EPS) * gain(G_CKV, 256)
    q = _dot(cq.astype(BF16), w_uq_ref[...])
    q = q * lax.rsqrt(_segsum(q * q, smq_ref) * invq_ref[...] + EPS) * gain(G_MQ, 512)
    kv = _dot(ckv.astype(BF16), w_ukv_ref[...])
    kn = kv[:, 0:512]
    kn = kn * lax.rsqrt(_segsum(kn * kn, s64_ref) * inv_head + EPS) * gain(G_MKN, 512)
    kr = kr * lax.rsqrt(jnp.sum(kr * kr, axis=-1, keepdims=True) * (1.0 / MLA_ROPE) + EPS) * gain(G_MKR, 128)
    lane = lax.broadcasted_iota(jnp.int32, (tm, LANES), 1)
    first8 = (lane & 15) < 8
    mc, ms = mc_ref[...], ms_ref[...]
    kr = _rope(kr, mc, ms, 8, first8)
    for hh in range(MLA_HEADS):
        sl = slice(LANES * hh, LANES * (hh + 1))
        mq_ref[hh] = _rope(q[:, sl], mc, ms, 8, first8).astype(BF16)
        mk_ref[hh] = (kn[:, sl] + kr).astype(BF16)
    vt = kv[:, 512:768].T
    ones = jnp.ones((BF16_SUBLANES, tm), BF16)
    for hh in range(MLA_HEADS):
        mvt_ref[hh, 0:HEAD_DIM, :] = vt[HEAD_DIM * hh:HEAD_DIM * (hh + 1), :].astype(BF16)
        mvt_ref[hh, HEAD_DIM:V_AUG, :] = ones

    z = _dot(h, w_in_ref[:, C_SG:C_SG + SG_IN])
    uv = jax.nn.gelu(z)
    u, v = uv[:, 0:SG_WIDTH], uv[:, SG_WIDTH:]
    v = v * lax.rsqrt(jnp.mean(v * v, axis=-1, keepdims=True) + EPS) * gain(G_SG, 256)
    grp = lax.broadcasted_iota(jnp.int32, (SG_CHUNK, SG_WIDTH), 1) >> 6
    for c in range(tm // SG_CHUNK):
        rows = slice(SG_CHUNK * c, SG_CHUNK * (c + 1))
        vc = v[rows, :]
        mixed = sgb_ref[...]
        for gi in range(SG_GROUPS):
            mixed = mixed + _dot(sgw_ref[gi], jnp.where(grp == gi, vc, 0.0).astype(BF16))
        sgo_ref[rows, :] = (u[rows, :] * mixed).astype(BF16)

    z = _dot(h, w_in_ref[:, C_GQA:C_GQA + 640])
    qk = z[:, 0:512]
    qk = qk * lax.rsqrt(_segsum(qk * qk, s64_ref) * inv_head + EPS) * gain(G_GQA, 512)
    lane2 = lax.broadcasted_iota(jnp.int32, (tm, 2 * LANES), 1)
    first16 = (lane2 & 31) < 16
    gc, gs = gc_ref[...], gs_ref[...]
    qn = _rope(qk[:, 0:256], gc, gs, 16, first16)
    kn = _rope(qk[:, 256:512], gc, gs, 16, first16)
    half = lane >> 6
    for g in range(GQA_KV_HEADS):
        sl = slice(LANES * g, LANES * (g + 1))
        for r in range(GQA_HEADS // GQA_KV_HEADS):
            gq_ref[2 * g + r] = jnp.where(half == r, qn[:, sl], 0.0).astype(BF16)
        gk_ref[g] = kn[:, sl].astype(BF16)
    vt = z[:, 512:640].T
    for g in range(GQA_KV_HEADS):
        gvt_ref[g, 0:HEAD_DIM, :] = vt[HEAD_DIM * g:HEAD_DIM * (g + 1), :].astype(BF16)
        gvt_ref[g, HEAD_DIM:V_AUG, :] = ones


def _in_call(x, mod, mod_row, lw, tabs, tm):
    b, t, d = x.shape
    nt = t // tm
    if mod_row is None:
        mod_map = lambda i, j: (j, 0, 0)
    else:
        mod_map = lambda i, j: (mod_row, 0, 0)
    tok = lambda i, j: (j, i, 0)
    tab = lambda i, j: (i, 0)
    head_tok = lambda i, j: (j, 0, i, 0)
    head_t = lambda i, j: (j, 0, 0, i)
    in_specs = [
        pl.BlockSpec((None, tm, d), tok),
        pl.BlockSpec((None, 6, d), mod_map),
        _layer_spec((d, IN_ARR), lw["layer"]),
        _layer_spec((MLA_Q_LORA, 512), lw["layer"]),
        _layer_spec((256, 768), lw["layer"]),
        _layer_spec((1, G_TOT), lw["layer"]),
        _const_spec((MXU_DIM, MXU_DIM)),
        _const_spec((MXU_DIM, MXU_DIM)),
        _const_spec((1, 512)),
        pl.BlockSpec((tm, 256), tab), pl.BlockSpec((tm, 256), tab),
        pl.BlockSpec((tm, LANES), tab), pl.BlockSpec((tm, LANES), tab),
        _layer_spec((SG_GROUPS, SG_CHUNK, SG_CHUNK), lw["layer"]),
        _layer_spec((SG_CHUNK, SG_WIDTH), lw["layer"]),
    ]
    out_specs = [
        pl.BlockSpec((None, tm, 256), tok), pl.BlockSpec((None, tm, 256), tok), pl.BlockSpec((None, tm, 256), tok),
        pl.BlockSpec((None, MLA_HEADS, tm, LANES), head_tok),
        pl.BlockSpec((None, MLA_HEADS, tm, LANES), head_tok),
        pl.BlockSpec((None, MLA_HEADS, V_AUG, tm), head_t),
        pl.BlockSpec((None, tm, 256), tok),
        pl.BlockSpec((None, GQA_HEADS, tm, LANES), head_tok),
        pl.BlockSpec((None, GQA_KV_HEADS, tm, LANES), head_tok),
        pl.BlockSpec((None, GQA_KV_HEADS, V_AUG, tm), head_t),
    ]
    sds = jax.ShapeDtypeStruct
    out_shape = [
        sds((b, t, 256), BF16), sds((b, t, 256), BF16), sds((b, t, 256), BF16),
        sds((b, MLA_HEADS, t, LANES), BF16), sds((b, MLA_HEADS, t, LANES), BF16),
        sds((b, MLA_HEADS, V_AUG, t), BF16),
        sds((b, t, 256), BF16),
        sds((b, GQA_HEADS, t, LANES), BF16), sds((b, GQA_KV_HEADS, t, LANES), BF16),
        sds((b, GQA_KV_HEADS, V_AUG, t), BF16),
    ]
    outs = pl.pallas_call(
        _in_kernel, grid=(nt, b), in_specs=in_specs, out_specs=out_specs, out_shape=out_shape,
        compiler_params=_params(48, 2), name="in_proj",
    )(x, mod, lw["w_in"], lw["w_uq"], lw["w_ukv"], lw["gains"], lw["s64"], lw["smq"], lw["invq"],
      tabs["gc"], tabs["gs"], tabs["mc"], tabs["ms"], lw["sgw"], lw["sgb"])
    names = ("na_q", "na_k", "na_v", "mla_q", "mla_k", "mla_vt", "sg_o", "gqa_q", "gqa_k", "gqa_vt")
    return dict(zip(names, outs))


def _na_heads(q, k_blocks, v_blocks, biases, seeded):
    tq = q.shape[0]
    head_of_lane = lax.broadcasted_iota(jnp.int32, (tq, 256), 1) >> 6
    qf = q.astype(F32) * (HEAD_DIM ** -0.5)
    qhs = [jnp.where(head_of_lane == hh, qf, 0.0).astype(BF16) for hh in range(NA_HEADS)]
    vts = [vb.astype(F32).T.astype(BF16) for vb in v_blocks]
    units = [(hh, j) for hh in range(NA_HEADS) for j in range(len(k_blocks))]

    def score(u):
        hh, j = u
        s = _dot_nt(k_blocks[j], qhs[hh])
        return s if biases[j] is None else s + biases[j][hh]

    def value_rows(hh, j):
        ones = jnp.ones((BF16_SUBLANES, vts[j].shape[1]), BF16)
        return jnp.concatenate([vts[j][HEAD_DIM * hh:HEAD_DIM * (hh + 1), :], ones], axis=0)

    lookahead = SCORE_LOOKAHEAD if seeded else len(k_blocks)
    pending = [score(u) for u in units[:lookahead]]
    accs, m = [], None
    for n, (hh, j) in enumerate(units):
        if j == 0:
            acc = jnp.zeros((V_AUG, tq), F32)
            head = [pending.pop(0) for _ in range(1 if seeded else len(k_blocks))]
            m = _col_max(head[0])
            for s in head[1:]:
                m = jnp.maximum(m, _col_max(s))
        else:
            head = head[1:] if not seeded else [pending.pop(0)]
        if n + lookahead < len(units):
            pending.append(score(units[n + lookahead]))
        acc = acc + _dot(value_rows(hh, j), jnp.exp(head[0] - m).astype(BF16))
        if j == len(k_blocks) - 1:
            accs.append(acc)
    out = jnp.concatenate([a[0:HEAD_DIM] * (1.0 / a[HEAD_DIM:HEAD_DIM + 1]) for a in accs], axis=0).T
    return out, accs


def _na_kernel(q_ref, k0_ref, k1_ref, k2_ref, k3_ref, v0_ref, v1_ref, v2_ref, v3_ref, kx_ref, vx_ref,
               bias_a_ref, bias_b_ref, o_ref):
    tq = q_ref.shape[0] // 2
    band = lambda refs: jnp.concatenate([r[...] for r in refs], axis=0)
    tiles = [(slice(0, tq), (k0_ref, k1_ref, k2_ref), (v0_ref, v1_ref, v2_ref), bias_a_ref),
             (slice(tq, 2 * tq), (k1_ref, k2_ref, k3_ref), (v1_ref, v2_ref, v3_ref), bias_b_ref)]
    operands = lambda rows, ks, vs, bias: (q_ref[rows, :], [kx_ref[...], band(ks)], [vx_ref[...], band(vs)],
                                           [None, bias])
    unusable = jnp.zeros((V_AUG, tq), F32)
    for rows, ks, vs, bias in tiles:
        out, accs = _na_heads(*operands(rows, ks, vs, bias), seeded=True)
        o_ref[rows, :] = out.astype(BF16)
        for acc in accs:
            unusable = jnp.maximum(unusable, jnp.where(jnp.abs(acc) < F32_SAFE_MAX, 0.0, 1.0))

    @pl.when(jnp.max(unusable) > 0.0)
    def _():
        for rows, ks, vs, bias in tiles:
            o_ref[rows, :] = _na_heads(*operands(rows, ks, vs, bias), seeded=False)[0].astype(BF16)


def _na_call(q, k, v, kx, vx, bias, layer):
    b, t, w = q.shape
    c = kx.shape[1]
    tq = 4 * GRID_W
    nt = t // tq
    assert nt % 2 == 0
    block = lambda off: (lambda bi, i: (bi, jnp.clip(2 * i + off, 0, nt - 1), 0))
    ctx = lambda bi, i: (bi, 0, 0)
    which = lambda s: jnp.where(s == 0, 0, jnp.where(s == nt - 1, 2, 1))
    variant = lambda off: (lambda bi, i: (layer, which(2 * i + off), 0, 0, 0))
    blk = lambda off: pl.BlockSpec((None, tq, w), block(off))
    bias_spec = lambda off: pl.BlockSpec((None, None, NA_HEADS, 3 * tq, tq), variant(off))
    pair = pl.BlockSpec((None, 2 * tq, w), lambda bi, i: (bi, i, 0))
    return pl.pallas_call(
        _na_kernel, grid=(b, nt // 2),
        in_specs=[pair, blk(-1), blk(0), blk(1), blk(2), blk(-1), blk(0), blk(1), blk(2),
                  pl.BlockSpec((None, c, w), ctx), pl.BlockSpec((None, c, w), ctx), bias_spec(0), bias_spec(1)],
        out_specs=pair,
        out_shape=jax.ShapeDtypeStruct((b, t, w), BF16),
        compiler_params=_params(48, 2), name="na_attn",
    )(q, k, k, k, k, v, v, v, v, kx, vx, bias, bias)


def _na_ctx_kernel(q_ref, k_ref, v_ref, o_ref):
    o_ref[...] = _na_heads(q_ref[...], [k_ref[...]], [v_ref[...]], [None], seeded=False)[0].astype(BF16)


def _na_ctx_call(q, k, v):
    b, c, w = q.shape
    spec = pl.BlockSpec((None, c, w), lambda bi: (bi, 0, 0))
    return pl.pallas_call(
        _na_ctx_kernel, grid=(b,), in_specs=[spec, spec, spec], out_specs=spec,
        out_shape=jax.ShapeDtypeStruct((b, c, w), BF16),
        compiler_params=_params(32, 1), name="na_ctx_attn",
    )(q, k, v)


def _na_bias(rpb, nt):
    rows = 4 * nt
    col = np.arange(GRID_W)
    c_start = np.clip(col - NA_COLS // 2, 0, GRID_W - NA_COLS)
    valid_c = (col[None, :] >= c_start[:, None]) & (col[None, :] < c_start[:, None] + NA_COLS)
    dc = np.clip(col[None, :] - col[:, None] + (NA_COLS - 1), 0, 2 * NA_COLS - 2)
    pick_dc = jnp.asarray(dc.T[:, :, None] == np.arange(2 * NA_COLS - 1), F32)
    by_col = jnp.einsum("lhrd,kqd->lhrkq", rpb, pick_dc, precision=lax.Precision.HIGHEST)
    by_col = jnp.where(jnp.asarray(valid_c.T), by_col, NEG_INF)
    masked = jnp.full(by_col.shape[:2] + (GRID_W, GRID_W), NEG_INF, F32)
    variants = []
    for i in (0, min(1, nt - 1), nt - 1):
        rq = 4 * i + np.arange(4)
        start = np.clip(rq - NA_ROWS // 2, 0, rows - NA_ROWS)
        blocks = np.array([i - 1, i, i + 1])
        rk = (4 * blocks[:, None] + np.arange(4)[None, :]).reshape(-1)
        block_ok = np.repeat((blocks >= 0) & (blocks < nt), 4)
        valid_r = (rk[None, :] >= start[:, None]) & (rk[None, :] < start[:, None] + NA_ROWS) & block_ok[None, :]
        dr = np.clip(rk[None, :] - rq[:, None] + (NA_ROWS - 1), 0, 2 * NA_ROWS - 2)
        key_rows = [jnp.concatenate([by_col[:, :, dr[a, kr]] if valid_r[a, kr] else masked for a in range(4)], axis=-1)
                    for kr in range(len(rk))]
        variants.append(jnp.concatenate(key_rows, axis=-2))
    return jnp.stack(variants, axis=1)


TOKEN_TILE = 512
QUERY_BLOCK = 512
KEY_CHUNK = 512
FLASH_COL = MXU_DIM
STREAM_UNROLL = 16
SCORE_LOOKAHEAD = 2
SEED_KEYS = 64
F32_SAFE_MAX = 3e38
MIN_DENOMINATOR = 2.0 ** -40


def _col_max(s):
    keys = s.shape[0]
    if keys % 64 == 0 and keys > 64:
        s = jnp.max(s.reshape(keys // 64, 64, s.shape[1]), axis=0)
    return jnp.max(s, axis=0, keepdims=True)


def _flash_kernel(*refs, n_heads, group, tq, tk, has_ctx):
    if has_ctx:
        q_ref, k_ref, vt_ref, kx_ref, vtx_ref, o_ref, acc_scr, ot_scr = refs
    else:
        q_ref, k_ref, vt_ref, o_ref, acc_scr, ot_scr = refs
    n_sub = q_ref.shape[1] // tq
    n_chunks = k_ref.shape[1] // tk
    n_slabs = n_heads // group
    cols = [(hh, qi) for hh in range(n_heads) for qi in range(n_sub)]
    heads = range(len(cols))
    slab_of = [hh // group for hh, _ in cols]

    def q_col(c):
        hh, qi = cols[c]
        return q_ref[hh, qi * tq:(qi + 1) * tq, :]

    def load_chunk(j):
        off = pl.multiple_of(j * tk, tk)
        return ([k_ref[sl, pl.ds(off, tk), :] for sl in range(n_slabs)],
                [vt_ref[sl, :, pl.ds(off, tk)] for sl in range(n_slabs)])

    def pv(vt, p):
        return _dot(vt, p.astype(BF16))

    def exact_step(ks, vts, ms):
        new_ms = []
        for hh in heads:
            s = _dot_nt(ks[slab_of[hh]], q_col(hh))
            m_new = jnp.maximum(ms[hh], _col_max(s))
            p = jnp.exp2(s - m_new)
            acc_scr[hh] = acc_scr[hh] * jnp.exp2(ms[hh] - m_new) + pv(vts[slab_of[hh]], p)
            new_ms.append(m_new)
        return tuple(new_ms)

    def stream(chunks, ms):
        units = [(i, hh) for i in range(len(chunks)) for hh in heads]
        score = lambda u: _dot_nt(chunks[u[0]][0][slab_of[u[1]]], q_col(u[1]))
        pending = [score(u) for u in units[:SCORE_LOOKAHEAD]]
        for n, (i, hh) in enumerate(units):
            s = pending.pop(0)
            if n + SCORE_LOOKAHEAD < len(units):
                pending.append(score(units[n + SCORE_LOOKAHEAD]))
            acc_scr[hh] += pv(chunks[i][1][slab_of[hh]], jnp.exp2(s - ms[hh]))

    ctx_chunk = ([kx_ref[sl] for sl in range(n_slabs)], [vtx_ref[sl] for sl in range(n_slabs)]) if has_ctx else None
    seed_ref = kx_ref if has_ctx else k_ref
    ms = tuple(_col_max(_dot_nt(seed_ref[slab_of[hh], 0:SEED_KEYS, :], q_col(hh))) for hh in heads)
    acc_scr[...] = jnp.zeros_like(acc_scr)
    if has_ctx:
        stream([ctx_chunk], ms)
    per_iter = max(u for u in (STREAM_UNROLL, 8, 4, 2, 1) if n_chunks % u == 0)

    def stream_body(j, carry):
        stream([load_chunk(j * per_iter + i) for i in range(per_iter)], ms)
        return carry

    lax.fori_loop(0, n_chunks // per_iter, stream_body, 0)
    unusable = jnp.zeros((V_AUG, tq), F32)
    for hh in heads:
        acc = acc_scr[hh]
        unusable = jnp.maximum(unusable, jnp.where(jnp.abs(acc) < F32_SAFE_MAX, 0.0, 1.0))
        unusable = jnp.maximum(unusable, jnp.where(acc[HEAD_DIM:HEAD_DIM + 1] > MIN_DENOMINATOR, 0.0, 1.0))

    @pl.when(jnp.max(unusable) > 0.0)
    def _():
        acc_scr[...] = jnp.zeros_like(acc_scr)
        ms = tuple(jnp.full((1, tq), NEG_INF, F32) for _ in heads)
        if has_ctx:
            ms = exact_step(*ctx_chunk, ms)
        lax.fori_loop(0, n_chunks, lambda j, m: exact_step(*load_chunk(j), m), ms)

    for c, (hh, qi) in enumerate(cols):
        acc = acc_scr[c]
        ot_scr[HEAD_DIM * hh:HEAD_DIM * (hh + 1), qi * tq:(qi + 1) * tq] = (
            acc[0:HEAD_DIM] * (1.0 / acc[HEAD_DIM:HEAD_DIM + 1]))
    o_ref[...] = ot_scr[...].T.astype(BF16)


def _flash_call(q, k, vt, kx, vtx, tq, tk):
    b, n_heads, t_q, w = q.shape
    slabs, t_k = k.shape[1], k.shape[2]
    has_ctx = kx is not None
    full4 = lambda bi, i: (bi, 0, 0, 0)
    in_specs = [pl.BlockSpec((None, n_heads, tq, w), lambda bi, i: (bi, 0, i, 0)),
                pl.BlockSpec((None, slabs, t_k, w), full4),
                pl.BlockSpec((None, slabs, V_AUG, t_k), full4)]
    args = [q, k, vt]
    if has_ctx:
        c = kx.shape[2]
        in_specs += [pl.BlockSpec((None, slabs, c, w), full4), pl.BlockSpec((None, slabs, V_AUG, c), full4)]
        args += [kx, vtx]
    col = min(FLASH_COL, tq)
    kern = functools.partial(_flash_kernel, n_heads=n_heads, group=n_heads // slabs, tq=col, tk=tk,
                             has_ctx=has_ctx)
    return pl.pallas_call(
        kern, grid=(b, t_q // tq), in_specs=in_specs,
        out_specs=pl.BlockSpec((None, tq, n_heads * HEAD_DIM), lambda bi, i: (bi, i, 0)),
        out_shape=jax.ShapeDtypeStruct((b, t_q, n_heads * HEAD_DIM), BF16),
        scratch_shapes=[pltpu.VMEM((n_heads * (tq // col), V_AUG, col), F32),
                        pltpu.VMEM((n_heads * HEAD_DIM, tq), F32)],
        compiler_params=_params(56, 2), name="flash",
    )(*args)


def _merge_kernel(x_ref, mod_ref, o0_ref, o1_ref, o2_ref, o3_ref, wg_ref, bg_ref, wb_ref, wo_ref, out_ref):
    x = x_ref[...]
    h = (_rms(x) * (1.0 + mod_ref[1:2, :]) + mod_ref[0:1, :]).astype(BF16)
    y = None
    for i, o_ref in enumerate((o0_ref, o1_ref, o2_ref, o3_ref)):
        gate = jax.nn.sigmoid(_dot(h, wg_ref[i]) + bg_ref[i])
        term = gate * _dot(o_ref[...], wb_ref[i])
        y = term if y is None else y + term
    out_ref[...] = x + mod_ref[2:3, :] * _dot(y.astype(BF16), wo_ref[...])


def _merge_call(x, mod, mod_row, branches, lw, tm):
    b, t, d = x.shape
    if mod_row is None:
        mod_map = lambda i, j: (j, 0, 0)
    else:
        mod_map = lambda i, j: (mod_row, 0, 0)
    tok = lambda i, j: (j, i, 0)
    br = pl.BlockSpec((None, tm, BRANCH_WIDTH), tok)
    return pl.pallas_call(
        _merge_kernel, grid=(t // tm, b),
        in_specs=[pl.BlockSpec((None, tm, d), tok), pl.BlockSpec((None, 6, d), mod_map), br, br, br, br,
                  _layer_spec((N_BRANCH, d, d), lw["layer"]), _layer_spec((N_BRANCH, 1, d), lw["layer"]),
                  _layer_spec((N_BRANCH, BRANCH_WIDTH, d), lw["layer"]), _layer_spec((d, d), lw["layer"])],
        out_specs=pl.BlockSpec((None, tm, d), tok),
        out_shape=jax.ShapeDtypeStruct((b, t, d), F32),
        compiler_params=_params(56, 2), name="merge",
    )(x, mod, *branches, lw["w_gate"], lw["b_gate"], lw["w_branch"], lw["w_out"])


def _ffn_kernel(x_ref, xp_ref, xn_ref, mod_ref, wu_ref, cw_ref, cb_ref, wd_ref, out_ref, h_scr, act_scr):
    tm = x_ref.shape[0]
    i = pl.program_id(0)
    nt = pl.num_programs(0)
    sh, sc = mod_ref[3:4, :], mod_ref[4:5, :]

    def modulated(v):
        return _rms(v) * (1.0 + sc) + sh

    x = x_ref[...]
    h_scr[0:HALO, :] = jnp.where(i > 0, modulated(xp_ref[...]), 0.0).astype(BF16)
    h_scr[HALO:HALO + tm, :] = modulated(x).astype(BF16)
    h_scr[HALO + tm:, :] = jnp.where(i < nt - 1, modulated(xn_ref[...]), 0.0).astype(BF16)
    rows = tm + 2 * HALO
    main = slice(HALO, HALO + tm)

    def cols(j, gate):
        lo = gate * FFN_DIM + FFN_CHUNK * j
        return slice(lo, lo + FFN_CHUNK)

    def up_proj(j):
        return tuple(_dot(h_scr[...], wu_ref[:, cols(j, gate)]) for gate in (0, 1))

    def conv(up, sl):
        return (pltpu.roll(up, 1, 0)[main] * cw_ref[0:1, sl] + up[main] * cw_ref[1:2, sl]
                + pltpu.roll(up, rows - 1, 0)[main] * cw_ref[2:3, sl] + cb_ref[:, sl])

    ups = [up_proj(j) for j in range(FFN_LOOKAHEAD)]
    for j in range(N_FFN_CHUNKS):
        up_a, up_g = ups.pop(0)
        if j + FFN_LOOKAHEAD < N_FFN_CHUNKS:
            ups.append(up_proj(j + FFN_LOOKAHEAD))
        a, g = conv(up_a, cols(j, 0)), conv(up_g, cols(j, 1))
        act_scr[:, cols(j, 0)] = (g * jax.nn.sigmoid(g) * a).astype(BF16)
    out_ref[...] = x + mod_ref[5:6, :] * _dot(act_scr[...], wd_ref[...])


def _ffn_call(x, mod, mod_row, lw, tm):
    b, t, d = x.shape
    nt = t // tm
    hb = tm // HALO
    last_halo = t // HALO - 1
    if mod_row is None:
        mod_map = lambda i, j: (j, 0, 0)
    else:
        mod_map = lambda i, j: (mod_row, 0, 0)
    tok = lambda i, j: (j, i, 0)
    return pl.pallas_call(
        _ffn_kernel, grid=(nt, b),
        in_specs=[pl.BlockSpec((None, tm, d), tok),
                  pl.BlockSpec((None, HALO, d), lambda i, j: (j, jnp.maximum(i * hb - 1, 0), 0)),
                  pl.BlockSpec((None, HALO, d), lambda i, j: (j, jnp.minimum((i + 1) * hb, last_halo), 0)),
                  pl.BlockSpec((None, 6, d), mod_map),
                  _layer_spec((d, 2 * FFN_DIM), lw["layer"]),
                  _layer_spec((3, 2 * FFN_DIM), lw["layer"]),
                  _layer_spec((1, 2 * FFN_DIM), lw["layer"]),
                  _layer_spec((FFN_DIM, d), lw["layer"])],
        out_specs=pl.BlockSpec((None, tm, d), tok),
        out_shape=jax.ShapeDtypeStruct((b, t, d), F32),
        scratch_shapes=[pltpu.VMEM((tm + 2 * HALO, d), BF16), pltpu.VMEM((tm, FFN_DIM), BF16)],
        compiler_params=_params(56, 2), name="ffn",
    )(x, x, x, mod, lw["w_up"], lw["conv_w"], lw["conv_b"], lw["w_down"])


def _block_diag_ones(width, segs):
    m = np.zeros((width, width), np.float32)
    for lo, hi in segs:
        m[lo:hi, lo:hi] = 1.0
    return jnp.asarray(m, BF16)


def _rope_pattern(pos, width):
    half = width // 2
    inv = ROPE_THETA ** (-jnp.arange(half, dtype=F32) / half)
    ang = pos.astype(F32)[:, None] * inv[None, :]
    cos, sin = jnp.cos(ang), jnp.sin(ang)
    return jnp.concatenate([cos, cos], axis=1), jnp.concatenate([-sin, sin], axis=1)


def _rope_tables(t, identity):
    if identity:
        return {"gc": jnp.ones((t, 256), F32), "gs": jnp.zeros((t, 256), F32),
                "mc": jnp.ones((t, LANES), F32), "ms": jnp.zeros((t, LANES), F32)}
    n_rows = t // GRID_W

    def grid_tables(width):
        rc, rs = _rope_pattern(jnp.arange(n_rows), width)
        cc, cs = _rope_pattern(jnp.arange(GRID_W), width)
        expand = lambda r, c: jnp.concatenate([jnp.repeat(r, GRID_W, axis=0), jnp.tile(c, (n_rows, 1))], axis=1)
        return expand(rc, cc), expand(rs, cs)

    gcos, gsin = grid_tables(HEAD_DIM // 2)
    mcos, msin = grid_tables(MLA_ROPE // 2)
    one, zero = jnp.ones((t, MLA_NOPE), F32), jnp.zeros((t, MLA_NOPE), F32)
    pad1, pad0 = jnp.ones((t, LANES - MLA_NOPE - MLA_ROPE), F32), jnp.zeros((t, LANES - MLA_NOPE - MLA_ROPE), F32)
    return {"gc": jnp.tile(gcos, (1, 4)), "gs": jnp.tile(gsin, (1, 4)),
            "mc": jnp.concatenate([one, mcos, pad1], axis=1),
            "ms": jnp.concatenate([zero, msin, pad0], axis=1)}


def _arrange(p):
    d = D_MODEL
    n_layers = p["w_in"].shape[0]
    cat = lambda parts: jnp.concatenate(parts, axis=-1)
    w_in = p["w_in"]
    na, mla = w_in[..., :NA_IN], w_in[..., NA_IN:NA_IN + MLA_IN]
    sg = w_in[..., NA_IN + MLA_IN:NA_IN + MLA_IN + SG_IN]
    gqa = w_in[..., NA_IN + MLA_IN + SG_IN:]
    z = lambda n: jnp.zeros((n_layers, d, n), F32)
    cq, ckv, kr = (mla[..., :MLA_Q_LORA], mla[..., MLA_Q_LORA:MLA_Q_LORA + MLA_KV_LORA],
                   mla[..., MLA_Q_LORA + MLA_KV_LORA:])
    gq, gk, gv = gqa[..., :256], gqa[..., 256:384], gqa[..., 384:]
    gk_dup = cat([gk[..., :64], gk[..., :64], gk[..., 64:], gk[..., 64:]])
    w_in_arr = cat([na, cq, ckv, z(256 - MLA_KV_LORA), z(MLA_NOPE), kr, z(LANES - MLA_NOPE - MLA_ROPE),
                    sg, gq, gk_dup, gv]).astype(BF16)

    lane_pad = lambda v, n: jnp.pad(v, ((0, 0),) * (v.ndim - 1) + ((0, n),))
    w_uq = p["mla_w_uq"].reshape(n_layers, MLA_Q_LORA, MLA_HEADS, MLA_NOPE + MLA_ROPE)
    w_uq = lane_pad(w_uq, LANES - MLA_NOPE - MLA_ROPE).reshape(n_layers, MLA_Q_LORA, MLA_HEADS * LANES)
    w_ukv = p["mla_w_ukv"].reshape(n_layers, MLA_KV_LORA, MLA_HEADS, MLA_NOPE + MLA_V)
    k_part = lane_pad(w_ukv[..., :MLA_NOPE], LANES - MLA_NOPE).reshape(n_layers, MLA_KV_LORA, MLA_HEADS * LANES)
    v_part = w_ukv[..., MLA_NOPE:].reshape(n_layers, MLA_KV_LORA, MLA_HEADS * MLA_V)
    w_ukv = jnp.pad(cat([k_part, v_part]), ((0, 0), (0, 256 - MLA_KV_LORA), (0, 0)))

    qg, kg = p["mla_q_norm"], p["mla_k_norm"]
    z1 = lambda n: jnp.zeros((n_layers, n), F32)
    tile4 = lambda v: jnp.tile(v, (1, 4))
    gains = cat([
        tile4(p["na_q_norm"]), tile4(p["na_k_norm"]),
        p["mla_cq_norm"],
        p["mla_ckv_norm"], z1(256 - MLA_KV_LORA),
        tile4(cat([qg * MLA_LOGIT_SCALE, z1(LANES - MLA_NOPE - MLA_ROPE)])),
        tile4(cat([kg[:, :MLA_NOPE], z1(LANES - MLA_NOPE)])),
        z1(MLA_NOPE), kg[:, MLA_NOPE:], z1(LANES - MLA_NOPE - MLA_ROPE),
        p["sg_v_norm"],
        tile4(p["gqa_q_norm"] * GQA_LOGIT_SCALE), tile4(p["gqa_k_norm"]),
    ]).reshape(n_layers, 1, G_TOT)

    invq = np.tile(np.concatenate([np.full(MLA_NOPE, 1.0 / MLA_NOPE), np.full(MLA_ROPE, 1.0 / MLA_ROPE),
                                   np.ones(LANES - MLA_NOPE - MLA_ROPE)]), 4).astype(np.float32).reshape(1, 512)
    sgb = jnp.repeat(jnp.swapaxes(p["sg_b_s"], -1, -2), SG_WIDTH // SG_GROUPS, axis=-1)

    return {
        "w_in": w_in_arr, "w_uq": w_uq.astype(BF16), "w_ukv": w_ukv.astype(BF16), "gains": gains,
        "s64": _block_diag_ones(MXU_DIM, [(64 * i, 64 * i + 64) for i in range(4)]),
        "smq": _block_diag_ones(MXU_DIM, [(0, 64), (64, 96), (128, 192), (192, 224)]),
        "invq": jnp.asarray(invq),
        "sgw": p["sg_w_s"].astype(BF16), "sgb": sgb,
        "w_gate": p["w_gate"].astype(BF16), "b_gate": p["b_gate"].reshape(n_layers, N_BRANCH, 1, d),
        "w_branch": p["w_branch"].astype(BF16), "w_out": p["w_out"].astype(BF16),
        "w_up": p["w_up"].astype(BF16), "conv_w": p["conv_w"], "conv_b": p["conv_b"].reshape(n_layers, 1, -1),
        "w_down": p["w_down"].astype(BF16),
    }


def kernel(x, c, ctx, c_ctx, w_ada, b_ada, w_in, na_q_norm, na_k_norm, na_rpb, mla_cq_norm, mla_ckv_norm,
           mla_w_uq, mla_w_ukv, mla_q_norm, mla_k_norm, sg_v_norm, sg_w_s, sg_b_s, gqa_q_norm, gqa_k_norm,
           w_branch, w_gate, b_gate, w_out, w_up, conv_w, conv_b, w_down):
    b, t, d = x.shape
    n_ctx = ctx.shape[1]
    depth = w_in.shape[0]
    ctx_row = b
    cvec = jnp.zeros((8, d), F32).at[:b].set(c).at[ctx_row].set(c_ctx)
    mod_all = _ada_call(cvec, w_ada, b_ada).reshape(depth, 8, 6, d)

    assert b < 8 and t % (4 * GRID_W) == 0 and n_ctx % SG_CHUNK == 0 and n_ctx >= SEED_KEYS
    tm = min(TOKEN_TILE, t)
    tq = min(QUERY_BLOCK, t)
    tk = min(KEY_CHUNK, t)
    assert t % tm == 0 and t % tq == 0 and t % tk == 0
    tabs_lat = _rope_tables(t, identity=False)
    tabs_ctx = _rope_tables(n_ctx, identity=True)
    na_bias = _na_bias(na_rpb, t // (4 * GRID_W))

    lw = _arrange({
        "w_in": w_in, "na_q_norm": na_q_norm, "na_k_norm": na_k_norm,
        "mla_cq_norm": mla_cq_norm, "mla_ckv_norm": mla_ckv_norm, "mla_w_uq": mla_w_uq,
        "mla_w_ukv": mla_w_ukv, "mla_q_norm": mla_q_norm, "mla_k_norm": mla_k_norm,
        "sg_v_norm": sg_v_norm, "sg_w_s": sg_w_s, "sg_b_s": sg_b_s,
        "gqa_q_norm": gqa_q_norm, "gqa_k_norm": gqa_k_norm,
        "w_branch": w_branch, "w_gate": w_gate, "b_gate": b_gate, "w_out": w_out,
        "w_up": w_up, "conv_w": conv_w, "conv_b": conv_b, "w_down": w_down})
    cx = ctx
    for l in range(depth):
        lw = dict(lw, layer=l)
        mod = mod_all[l]
        pc = _in_call(cx, mod, ctx_row, lw, tabs_ctx, n_ctx)
        pz = _in_call(x, mod, None, lw, tabs_lat, tm)

        o_na = _na_call(pz["na_q"], pz["na_k"], pz["na_v"], pc["na_k"], pc["na_v"], na_bias, l)
        o_mla = _flash_call(pz["mla_q"], pz["mla_k"], pz["mla_vt"], pc["mla_k"], pc["mla_vt"], tq, tk)
        o_gqa = _flash_call(pz["gqa_q"], pz["gqa_k"], pz["gqa_vt"], pc["gqa_k"], pc["gqa_vt"], tq, tk)
        x = _merge_call(x, mod, None, (o_na, o_mla, pz["sg_o"], o_gqa), lw, tm)
        x = _ffn_call(x, mod, None, lw, tm)

        if l < depth - 1:
            oc_na = _na_ctx_call(pc["na_q"], pc["na_k"], pc["na_v"])
            oc_mla = _flash_call(pc["mla_q"], pc["mla_k"], pc["mla_vt"], None, None, n_ctx, n_ctx)
            oc_gqa = _flash_call(pc["gqa_q"], pc["gqa_k"], pc["gqa_vt"], None, None, n_ctx, n_ctx)
            cx = _merge_call(cx, mod, ctx_row, (oc_na, oc_mla, pc["sg_o"], oc_gqa), lw, n_ctx)
            cx = _ffn_call(cx, mod, ctx_row, lw, n_ctx)
    return x
```

```python
import functools
import math

import numpy as np
import jax
import jax.numpy as jnp
from jax import lax
from jax.experimental import pallas as pl
from jax.experimental.pallas import tpu as pltpu

F32 = jnp.float32
BF16 = jnp.bfloat16

D_MODEL = 1024
GRID_W = 64
HEAD_DIM = 64
NA_HEADS = 4
NA_ROWS = 8
NA_COLS = 16
MLA_HEADS = 4
MLA_Q_LORA = 256
MLA_KV_LORA = 192
MLA_NOPE = 64
MLA_ROPE = 32
MLA_V = 64
SG_GROUPS = 4
SG_CHUNK = 128
SG_WIDTH = 256
GQA_HEADS = 4
GQA_KV_HEADS = 2
N_BRANCH = 4
BRANCH_WIDTH = 256
FFN_DIM = 2816
ROPE_THETA = 10000.0
EPS = 1e-6
NEG_INF = -1e30

NA_IN = 3 * NA_HEADS * HEAD_DIM
MLA_IN = MLA_Q_LORA + MLA_KV_LORA + MLA_ROPE
SG_IN = 2 * SG_WIDTH
GQA_IN = (GQA_HEADS + 2 * GQA_KV_HEADS) * HEAD_DIM

LANES = 128
MXU_DIM = 256
BF16_SUBLANES = 16
V_AUG = HEAD_DIM + BF16_SUBLANES
LOG2E = math.log2(math.e)
MLA_LOGIT_SCALE = (MLA_NOPE + MLA_ROPE) ** -0.5 * LOG2E
GQA_LOGIT_SCALE = HEAD_DIM ** -0.5 * LOG2E

C_NA = 0
C_MLA = C_NA + NA_IN
C_SG = C_MLA + 640
C_GQA = C_SG + SG_IN
IN_ARR = C_GQA + 640

G_NA, G_CQ, G_CKV, G_MQ, G_MKN, G_MKR, G_SG, G_GQA, G_TOT = 0, 512, 768, 1024, 1536, 2048, 2176, 2432, 2944

FFN_CHUNK = 256
N_FFN_CHUNKS = FFN_DIM // FFN_CHUNK
FFN_LOOKAHEAD = 2
HALO = BF16_SUBLANES


def _const_spec(shape):
    nd = len(shape)
    return pl.BlockSpec(shape, lambda *_: (0,) * nd, pipeline_mode=pl.Buffered(1))


def _layer_spec(shape, layer):
    nd = len(shape)
    return pl.BlockSpec((None,) + tuple(shape), lambda *_: (layer,) + (0,) * nd, pipeline_mode=pl.Buffered(1))


def _params(vmem_mb, n_grid):
    return pltpu.CompilerParams(dimension_semantics=("arbitrary",) * n_grid,
                                vmem_limit_bytes=vmem_mb * 1024 * 1024)


def _dot(a, b):
    return jnp.dot(a, b, preferred_element_type=F32)


def _dot_nt(a, b):
    return lax.dot_general(a, b, (((1,), (1,)), ((), ())), preferred_element_type=F32)


def _rms(xf):
    return xf * lax.rsqrt(jnp.mean(xf * xf, axis=-1, keepdims=True) + EPS)


def _segsum(x2, s_ref):
    hi = x2.astype(BF16)
    lo = (x2 - hi.astype(F32)).astype(BF16)
    s = s_ref[...]
    parts = []
    for j in range(x2.shape[1] // MXU_DIM):
        sl = slice(MXU_DIM * j, MXU_DIM * (j + 1))
        parts.append(_dot(hi[:, sl], s) + _dot(lo[:, sl], s))
    return parts[0] if len(parts) == 1 else jnp.concatenate(parts, axis=1)


def _rope(x, cos, sin_signed, shift, first_half):
    w = x.shape[1]
    partner = jnp.where(first_half, pltpu.roll(x, w - shift, 1), pltpu.roll(x, shift, 1))
    return x * cos + partner * sin_signed


def _ada_kernel(c_ref, w_ref, b_ref, o_ref):
    c = c_ref[...]
    a = (c * jax.nn.sigmoid(c)).astype(BF16)
    o_ref[...] = _dot(a, w_ref[...].astype(BF16)) + b_ref[...]


def _ada_call(cvec, w_ada, b_ada):
    n_layers, d, n = w_ada.shape
    tn = 1536
    return pl.pallas_call(
        _ada_kernel,
        grid=(n_layers, n // tn),
        in_specs=[pl.BlockSpec((8, d), lambda l, j: (0, 0)),
                  pl.BlockSpec((None, d, tn), lambda l, j: (l, 0, j)),
                  pl.BlockSpec((None, 1, tn), lambda l, j: (l, 0, j))],
        out_specs=pl.BlockSpec((None, 8, tn), lambda l, j: (l, 0, j)),
        out_shape=jax.ShapeDtypeStruct((n_layers, 8, n), F32),
        compiler_params=_params(32, 2),
        name="ada",
    )(cvec, w_ada, b_ada.reshape(n_layers, 1, n))


def _in_kernel(x_ref, mod_ref, w_in_ref, w_uq_ref, w_ukv_ref, g_ref, s64_ref, smq_ref, invq_ref,
               gc_ref, gs_ref, mc_ref, ms_ref, sgw_ref, sgb_ref,
               naq_ref, nak_ref, nav_ref, mq_ref, mk_ref, mvt_ref, sgo_ref, gq_ref, gk_ref, gvt_ref):
    tm = x_ref.shape[0]
    x = x_ref[...]
    h = (_rms(x) * (1.0 + mod_ref[1:2, :]) + mod_ref[0:1, :]).astype(BF16)

    def gain(off, width):
        return g_ref[:, off:off + width]

    inv_head = 1.0 / HEAD_DIM

    z = _dot(h, w_in_ref[:, C_NA:C_NA + NA_IN])
    qk = z[:, 0:512]
    qk = qk * lax.rsqrt(_segsum(qk * qk, s64_ref) * inv_head + EPS) * gain(G_NA, 512)
    naq_ref[...] = qk[:, 0:256].astype(BF16)
    nak_ref[...] = qk[:, 256:512].astype(BF16)
    nav_ref[...] = z[:, 512:768].astype(BF16)

    z = _dot(h, w_in_ref[:, C_MLA:C_MLA + 640])
    cq, ckv, kr = z[:, 0:256], z[:, 256:512], z[:, 512:640]
    cq = cq * lax.rsqrt(jnp.mean(cq * cq, axis=-1, keepdims=True) + EPS) * gain(G_CQ, 256)
    ckv = ckv * lax.rsqrt(jnp.sum(ckv * ckv, axis=-1, keepdims=True) * (1.0 / MLA_KV_LORA) + EPS) * gain(G_CKV, 256)
    q = _dot(cq.astype(BF16), w_uq_ref[...])
    q = q * lax.rsqrt(_segsum(q * q, smq_ref) * invq_ref[...] + EPS) * gain(G_MQ, 512)
    kv = _dot(ckv.astype(BF16), w_ukv_ref[...])
    kn = kv[:, 0:512]
    kn = kn * lax.rsqrt(_segsum(kn * kn, s64_ref) * inv_head + EPS) * gain(G_MKN, 512)
    kr = kr * lax.rsqrt(jnp.sum(kr * kr, axis=-1, keepdims=True) * (1.0 / MLA_ROPE) + EPS) * gain(G_MKR, 128)
    lane = lax.broadcasted_iota(jnp.int32, (tm, LANES), 1)
    first8 = (lane & 15) < 8
    mc, ms = mc_ref[...], ms_ref[...]
    kr = _rope(kr, mc, ms, 8, first8)
    for hh in range(MLA_HEADS):
        sl = slice(LANES * hh, LANES * (hh + 1))
        mq_ref[hh] = _rope(q[:, sl], mc, ms, 8, first8).astype(BF16)
        mk_ref[hh] = (kn[:, sl] + kr).astype(BF16)
    vt = kv[:, 512:768].T
    ones = jnp.ones((BF16_SUBLANES, tm), BF16)
    for hh in range(MLA_HEADS):
        mvt_ref[hh, 0:HEAD_DIM, :] = vt[HEAD_DIM * hh:HEAD_DIM * (hh + 1), :].astype(BF16)
        mvt_ref[hh, HEAD_DIM:V_AUG, :] = ones

    z = _dot(h, w_in_ref[:, C_SG:C_SG + SG_IN])
    uv = jax.nn.gelu(z)
    u, v = uv[:, 0:SG_WIDTH], uv[:, SG_WIDTH:]
    v = v * lax.rsqrt(jnp.mean(v * v, axis=-1, keepdims=True) + EPS) * gain(G_SG, 256)
    grp = lax.broadcasted_iota(jnp.int32, (SG_CHUNK, SG_WIDTH), 1) >> 6
    for c in range(tm // SG_CHUNK):
        rows = slice(SG_CHUNK * c, SG_CHUNK * (c + 1))
        vc = v[rows, :]
        mixed = sgb_ref[...]
        for gi in range(SG_GROUPS):
            mixed = mixed + _dot(sgw_ref[gi], jnp.where(grp == gi, vc, 0.0).astype(BF16))
        sgo_ref[rows, :] = (u[rows, :] * mixed).astype(BF16)

    z = _dot(h, w_in_ref[:, C_GQA:C_GQA + 640])
    qk = z[:, 0:512]
    qk = qk * lax.rsqrt(_segsum(qk * qk, s64_ref) * inv_head + EPS) * gain(G_GQA, 512)
    lane2 = lax.broadcasted_iota(jnp.int32, (tm, 2 * LANES), 1)
    first16 = (lane2 & 31) < 16
    gc, gs = gc_ref[...], gs_ref[...]
    qn = _rope(qk[:, 0:256], gc, gs, 16, first16)
    kn = _rope(qk[:, 256:512], gc, gs, 16, first16)
    half = lane >> 6
    for g in range(GQA_KV_HEADS):
        sl = slice(LANES * g, LANES * (g + 1))
        for r in range(GQA_HEADS // GQA_KV_HEADS):
            gq_ref[2 * g + r] = jnp.where(half == r, qn[:, sl], 0.0).astype(BF16)
        gk_ref[g] = kn[:, sl].astype(BF16)
    vt = z[:, 512:640].T
    for g in range(GQA_KV_HEADS):
        gvt_ref[g, 0:HEAD_DIM, :] = vt[HEAD_DIM * g:HEAD_DIM * (g + 1), :].astype(BF16)
        gvt_ref[g, HEAD_DIM:V_AUG, :] = ones


def _in_call(x, mod, mod_row, lw, tabs, tm):
    b, t, d = x.shape
    nt = t // tm
    if mod_row is None:
        mod_map = lambda i, j: (j, 0, 0)
    else:
        mod_map = lambda i, j: (mod_row, 0, 0)
    tok = lambda i, j: (j, i, 0)
    tab = lambda i, j: (i, 0)
    head_tok = lambda i, j: (j, 0, i, 0)
    head_t = lambda i, j: (j, 0, 0, i)
    in_specs = [
        pl.BlockSpec((None, tm, d), tok),
        pl.BlockSpec((None, 6, d), mod_map),
        _layer_spec((d, IN_ARR), lw["layer"]),
        _layer_spec((MLA_Q_LORA, 512), lw["layer"]),
        _layer_spec((256, 768), lw["layer"]),
        _layer_spec((1, G_TOT), lw["layer"]),
        _const_spec((MXU_DIM, MXU_DIM)),
        _const_spec((MXU_DIM, MXU_DIM)),
        _const_spec((1, 512)),
        pl.BlockSpec((tm, 256), tab), pl.BlockSpec((tm, 256), tab),
        pl.BlockSpec((tm, LANES), tab), pl.BlockSpec((tm, LANES), tab),
        _layer_spec((SG_GROUPS, SG_CHUNK, SG_CHUNK), lw["layer"]),
        _layer_spec((SG_CHUNK, SG_WIDTH), lw["layer"]),
    ]
    out_specs = [
        pl.BlockSpec((None, tm, 256), tok), pl.BlockSpec((None, tm, 256), tok), pl.BlockSpec((None, tm, 256), tok),
        pl.BlockSpec((None, MLA_HEADS, tm, LANES), head_tok),
        pl.BlockSpec((None, MLA_HEADS, tm, LANES), head_tok),
        pl.BlockSpec((None, MLA_HEADS, V_AUG, tm), head_t),
        pl.BlockSpec((None, tm, 256), tok),
        pl.BlockSpec((None, GQA_HEADS, tm, LANES), head_tok),
        pl.BlockSpec((None, GQA_KV_HEADS, tm, LANES), head_tok),
        pl.BlockSpec((None, GQA_KV_HEADS, V_AUG, tm), head_t),
    ]
    sds = jax.ShapeDtypeStruct
    out_shape = [
        sds((b, t, 256), BF16), sds((b, t, 256), BF16), sds((b, t, 256), BF16),
        sds((b, MLA_HEADS, t, LANES), BF16), sds((b, MLA_HEADS, t, LANES), BF16),
        sds((b, MLA_HEADS, V_AUG, t), BF16),
        sds((b, t, 256), BF16),
        sds((b, GQA_HEADS, t, LANES), BF16), sds((b, GQA_KV_HEADS, t, LANES), BF16),
        sds((b, GQA_KV_HEADS, V_AUG, t), BF16),
    ]
    outs = pl.pallas_call(
        _in_kernel, grid=(nt, b), in_specs=in_specs, out_specs=out_specs, out_shape=out_shape,
        compiler_params=_params(48, 2), name="in_proj",
    )(x, mod, lw["w_in"], lw["w_uq"], lw["w_ukv"], lw["gains"], lw["s64"], lw["smq"], lw["invq"],
      tabs["gc"], tabs["gs"], tabs["mc"], tabs["ms"], lw["sgw"], lw["sgb"])
    names = ("na_q", "na_k", "na_v", "mla_q", "mla_k", "mla_vt", "sg_o", "gqa_q", "gqa_k", "gqa_vt")
    return dict(zip(names, outs))


def _na_heads(q, k_blocks, v_blocks, biases, seeded):
    tq = q.shape[0]
    head_of_lane = lax.broadcasted_iota(jnp.int32, (tq, 256), 1) >> 6
    qf = q.astype(F32) * (HEAD_DIM ** -0.5)
    qhs = [jnp.where(head_of_lane == hh, qf, 0.0).astype(BF16) for hh in range(NA_HEADS)]
    vts = [vb.astype(F32).T.astype(BF16) for vb in v_blocks]
    units = [(hh, j) for hh in range(NA_HEADS) for j in range(len(k_blocks))]

    def score(u):
        hh, j = u
        s = _dot_nt(k_blocks[j], qhs[hh])
        return s if biases[j] is None else s + biases[j][hh]

    def value_rows(hh, j):
        ones = jnp.ones((BF16_SUBLANES, vts[j].shape[1]), BF16)
        return jnp.concatenate([vts[j][HEAD_DIM * hh:HEAD_DIM * (hh + 1), :], ones], axis=0)

    lookahead = SCORE_LOOKAHEAD if seeded else len(k_blocks)
    pending = [score(u) for u in units[:lookahead]]
    accs, m = [], None
    for n, (hh, j) in enumerate(units):
        if j == 0:
            acc = jnp.zeros((V_AUG, tq), F32)
            head = [pending.pop(0) for _ in range(1 if seeded else len(k_blocks))]
            m = _col_max(head[0])
            for s in head[1:]:
                m = jnp.maximum(m, _col_max(s))
        else:
            head = head[1:] if not seeded else [pending.pop(0)]
        if n + lookahead < len(units):
            pending.append(score(units[n + lookahead]))
        acc = acc + _dot(value_rows(hh, j), jnp.exp(head[0] - m).astype(BF16))
        if j == len(k_blocks) - 1:
            accs.append(acc)
    out = jnp.concatenate([a[0:HEAD_DIM] * (1.0 / a[HEAD_DIM:HEAD_DIM + 1]) for a in accs], axis=0).T
    return out, accs


def _na_kernel(q_ref, k0_ref, k1_ref, k2_ref, k3_ref, v0_ref, v1_ref, v2_ref, v3_ref, kx_ref, vx_ref,
               bias_a_ref, bias_b_ref, o_ref):
    tq = q_ref.shape[0] // 2
    band = lambda refs: jnp.concatenate([r[...] for r in refs], axis=0)
    tiles = [(slice(0, tq), (k0_ref, k1_ref, k2_ref), (v0_ref, v1_ref, v2_ref), bias_a_ref),
             (slice(tq, 2 * tq), (k1_ref, k2_ref, k3_ref), (v1_ref, v2_ref, v3_ref), bias_b_ref)]
    operands = lambda rows, ks, vs, bias: (q_ref[rows, :], [kx_ref[...], band(ks)], [vx_ref[...], band(vs)],
                                           [None, bias])
    unusable = jnp.zeros((V_AUG, tq), F32)
    for rows, ks, vs, bias in tiles:
        out, accs = _na_heads(*operands(rows, ks, vs, bias), seeded=True)
        o_ref[rows, :] = out.astype(BF16)
        for acc in accs:
            unusable = jnp.maximum(unusable, jnp.where(jnp.abs(acc) < F32_SAFE_MAX, 0.0, 1.0))

    @pl.when(jnp.max(unusable) > 0.0)
    def _():
        for rows, ks, vs, bias in tiles:
            o_ref[rows, :] = _na_heads(*operands(rows, ks, vs, bias), seeded=False)[0].astype(BF16)


def _na_call(q, k, v, kx, vx, bias, layer):
    b, t, w = q.shape
    c = kx.shape[1]
    tq = 4 * GRID_W
    nt = t // tq
    assert nt % 2 == 0
    block = lambda off: (lambda bi, i: (bi, jnp.clip(2 * i + off, 0, nt - 1), 0))
    ctx = lambda bi, i: (bi, 0, 0)
    which = lambda s: jnp.where(s == 0, 0, jnp.where(s == nt - 1, 2, 1))
    variant = lambda off: (lambda bi, i: (layer, which(2 * i + off), 0, 0, 0))
    blk = lambda off: pl.BlockSpec((None, tq, w), block(off))
    bias_spec = lambda off: pl.BlockSpec((None, None, NA_HEADS, 3 * tq, tq), variant(off))
    pair = pl.BlockSpec((None, 2 * tq, w), lambda bi, i: (bi, i, 0))
    return pl.pallas_call(
        _na_kernel, grid=(b, nt // 2),
        in_specs=[pair, blk(-1), blk(0), blk(1), blk(2), blk(-1), blk(0), blk(1), blk(2),
                  pl.BlockSpec((None, c, w), ctx), pl.BlockSpec((None, c, w), ctx), bias_spec(0), bias_spec(1)],
        out_specs=pair,
        out_shape=jax.ShapeDtypeStruct((b, t, w), BF16),
        compiler_params=_params(48, 2), name="na_attn",
    )(q, k, k, k, k, v, v, v, v, kx, vx, bias, bias)


def _na_ctx_kernel(q_ref, k_ref, v_ref, o_ref):
    o_ref[...] = _na_heads(q_ref[...], [k_ref[...]], [v_ref[...]], [None], seeded=False)[0].astype(BF16)


def _na_ctx_call(q, k, v):
    b, c, w = q.shape
    spec = pl.BlockSpec((None, c, w), lambda bi: (bi, 0, 0))
    return pl.pallas_call(
        _na_ctx_kernel, grid=(b,), in_specs=[spec, spec, spec], out_specs=spec,
        out_shape=jax.ShapeDtypeStruct((b, c, w), BF16),
        compiler_params=_params(32, 1), name="na_ctx_attn",
    )(q, k, v)


def _na_bias(rpb, nt):
    rows = 4 * nt
    col = np.arange(GRID_W)
    c_start = np.clip(col - NA_COLS // 2, 0, GRID_W - NA_COLS)
    valid_c = (col[None, :] >= c_start[:, None]) & (col[None, :] < c_start[:, None] + NA_COLS)
    dc = np.clip(col[None, :] - col[:, None] + (NA_COLS - 1), 0, 2 * NA_COLS - 2)
    pick_dc = jnp.asarray(dc.T[:, :, None] == np.arange(2 * NA_COLS - 1), F32)
    by_col = jnp.einsum("lhrd,kqd->lhrkq", rpb, pick_dc, precision=lax.Precision.HIGHEST)
    by_col = jnp.where(jnp.asarray(valid_c.T), by_col, NEG_INF)
    masked = jnp.full(by_col.shape[:2] + (GRID_W, GRID_W), NEG_INF, F32)
    variants = []
    for i in (0, min(1, nt - 1), nt - 1):
        rq = 4 * i + np.arange(4)
        start = np.clip(rq - NA_ROWS // 2, 0, rows - NA_ROWS)
        blocks = np.array([i - 1, i, i + 1])
        rk = (4 * blocks[:, None] + np.arange(4)[None, :]).reshape(-1)
        block_ok = np.repeat((blocks >= 0) & (blocks < nt), 4)
        valid_r = (rk[None, :] >= start[:, None]) & (rk[None, :] < start[:, None] + NA_ROWS) & block_ok[None, :]
        dr = np.clip(rk[None, :] - rq[:, None] + (NA_ROWS - 1), 0, 2 * NA_ROWS - 2)
        key_rows = [jnp.concatenate([by_col[:, :, dr[a, kr]] if valid_r[a, kr] else masked for a in range(4)], axis=-1)
                    for kr in range(len(rk))]
        variants.append(jnp.concatenate(key_rows, axis=-2))
    return jnp.stack(variants, axis=1)


TOKEN_TILE = 512
QUERY_BLOCK = 512
KEY_CHUNK = 512
FLASH_COL = MXU_DIM
STREAM_UNROLL = 16
SCORE_LOOKAHEAD = 2
SEED_KEYS = 64
F32_SAFE_MAX = 3e38
MIN_DENOMINATOR = 2.0 ** -40


def _col_max(s):
    keys = s.shape[0]
    if keys % 64 == 0 and keys > 64:
        s = jnp.max(s.reshape(keys // 64, 64, s.shape[1]), axis=0)
    return jnp.max(s, axis=0, keepdims=True)


def _flash_kernel(*refs, n_heads, group, tq, tk, has_ctx):
    if has_ctx:
        q_ref, k_ref, vt_ref, kx_ref, vtx_ref, o_ref, acc_scr, ot_scr = refs
    else:
        q_ref, k_ref, vt_ref, o_ref, acc_scr, ot_scr = refs
    n_sub = q_ref.shape[1] // tq
    n_chunks = k_ref.shape[1] // tk
    n_slabs = n_heads // group
    cols = [(hh, qi) for hh in range(n_heads) for qi in range(n_sub)]
    heads = range(len(cols))
    slab_of = [hh // group for hh, _ in cols]

    def q_col(c):
        hh, qi = cols[c]
        return q_ref[hh, qi * tq:(qi + 1) * tq, :]

    def load_chunk(j):
        off = pl.multiple_of(j * tk, tk)
        return ([k_ref[sl, pl.ds(off, tk), :] for sl in range(n_slabs)],
                [vt_ref[sl, :, pl.ds(off, tk)] for sl in range(n_slabs)])

    def pv(vt, p):
        return _dot(vt, p.astype(BF16))

    def exact_step(ks, vts, ms):
        new_ms = []
        for hh in heads:
            s = _dot_nt(ks[slab_of[hh]], q_col(hh))
            m_new = jnp.maximum(ms[hh], _col_max(s))
            p = jnp.exp2(s - m_new)
            acc_scr[hh] = acc_scr[hh] * jnp.exp2(ms[hh] - m_new) + pv(vts[slab_of[hh]], p)
            new_ms.append(m_new)
        return tuple(new_ms)

    def stream(chunks, ms):
        units = [(i, hh) for i in range(len(chunks)) for hh in heads]
        score = lambda u: _dot_nt(chunks[u[0]][0][slab_of[u[1]]], q_col(u[1]))
        pending = [score(u) for u in units[:SCORE_LOOKAHEAD]]
        for n, (i, hh) in enumerate(units):
            s = pending.pop(0)
            if n + SCORE_LOOKAHEAD < len(units):
                pending.append(score(units[n + SCORE_LOOKAHEAD]))
            acc_scr[hh] += pv(chunks[i][1][slab_of[hh]], jnp.exp2(s - ms[hh]))

    ctx_chunk = ([kx_ref[sl] for sl in range(n_slabs)], [vtx_ref[sl] for sl in range(n_slabs)]) if has_ctx else None
    seed_ref = kx_ref if has_ctx else k_ref
    ms = tuple(_col_max(_dot_nt(seed_ref[slab_of[hh], 0:SEED_KEYS, :], q_col(hh))) for hh in heads)
    acc_scr[...] = jnp.zeros_like(acc_scr)
    if has_ctx:
        stream([ctx_chunk], ms)
    per_iter = max(u for u in (STREAM_UNROLL, 8, 4, 2, 1) if n_chunks % u == 0)

    def stream_body(j, carry):
        stream([load_chunk(j * per_iter + i) for i in range(per_iter)], ms)
        return carry

    lax.fori_loop(0, n_chunks // per_iter, stream_body, 0)
    unusable = jnp.zeros((V_AUG, tq), F32)
    for hh in heads:
        acc = acc_scr[hh]
        unusable = jnp.maximum(unusable, jnp.where(jnp.abs(acc) < F32_SAFE_MAX, 0.0, 1.0))
        unusable = jnp.maximum(unusable, jnp.where(acc[HEAD_DIM:HEAD_DIM + 1] > MIN_DENOMINATOR, 0.0, 1.0))

    def finish():
        for c, (hh, qi) in enumerate(cols):
            acc = acc_scr[c]
            ot_scr[HEAD_DIM * hh:HEAD_DIM * (hh + 1), qi * tq:(qi + 1) * tq] = (
                acc[0:HEAD_DIM] * (1.0 / acc[HEAD_DIM:HEAD_DIM + 1]))
        o_ref[...] = ot_scr[...].T.astype(BF16)

    finish()

    @pl.when(jnp.max(unusable) > 0.0)
    def _():
        acc_scr[...] = jnp.zeros_like(acc_scr)
        ms = tuple(jnp.full((1, tq), NEG_INF, F32) for _ in heads)
        if has_ctx:
            ms = exact_step(*ctx_chunk, ms)
        lax.fori_loop(0, n_chunks, lambda j, m: exact_step(*load_chunk(j), m), ms)
        finish()


def _flash_call(q, k, vt, kx, vtx, tq, tk):
    b, n_heads, t_q, w = q.shape
    slabs, t_k = k.shape[1], k.shape[2]
    has_ctx = kx is not None
    full4 = lambda bi, i: (bi, 0, 0, 0)
    in_specs = [pl.BlockSpec((None, n_heads, tq, w), lambda bi, i: (bi, 0, i, 0)),
                pl.BlockSpec((None, slabs, t_k, w), full4),
                pl.BlockSpec((None, slabs, V_AUG, t_k), full4)]
    args = [q, k, vt]
    if has_ctx:
        c = kx.shape[2]
        in_specs += [pl.BlockSpec((None, slabs, c, w), full4), pl.BlockSpec((None, slabs, V_AUG, c), full4)]
        args += [kx, vtx]
    col = min(FLASH_COL, tq)
    kern = functools.partial(_flash_kernel, n_heads=n_heads, group=n_heads // slabs, tq=col, tk=tk,
                             has_ctx=has_ctx)
    return pl.pallas_call(
        kern, grid=(b, t_q // tq), in_specs=in_specs,
        out_specs=pl.BlockSpec((None, tq, n_heads * HEAD_DIM), lambda bi, i: (bi, i, 0)),
        out_shape=jax.ShapeDtypeStruct((b, t_q, n_heads * HEAD_DIM), BF16),
        scratch_shapes=[pltpu.VMEM((n_heads * (tq // col), V_AUG, col), F32),
                        pltpu.VMEM((n_heads * HEAD_DIM, tq), F32)],
        compiler_params=_params(56, 2), name="flash",
    )(*args)


def _merge_kernel(x_ref, mod_ref, o0_ref, o1_ref, o2_ref, o3_ref, wg_ref, bg_ref, wb_ref, wo_ref, out_ref):
    x = x_ref[...]
    h = (_rms(x) * (1.0 + mod_ref[1:2, :]) + mod_ref[0:1, :]).astype(BF16)
    y = None
    for i, o_ref in enumerate((o0_ref, o1_ref, o2_ref, o3_ref)):
        gate = jax.nn.sigmoid(_dot(h, wg_ref[i]) + bg_ref[i])
        term = gate * _dot(o_ref[...], wb_ref[i])
        y = term if y is None else y + term
    out_ref[...] = x + mod_ref[2:3, :] * _dot(y.astype(BF16), wo_ref[...])


def _merge_call(x, mod, mod_row, branches, lw, tm):
    b, t, d = x.shape
    if mod_row is None:
        mod_map = lambda i, j: (j, 0, 0)
    else:
        mod_map = lambda i, j: (mod_row, 0, 0)
    tok = lambda i, j: (j, i, 0)
    br = pl.BlockSpec((None, tm, BRANCH_WIDTH), tok)
    return pl.pallas_call(
        _merge_kernel, grid=(t // tm, b),
        in_specs=[pl.BlockSpec((None, tm, d), tok), pl.BlockSpec((None, 6, d), mod_map), br, br, br, br,
                  _layer_spec((N_BRANCH, d, d), lw["layer"]), _layer_spec((N_BRANCH, 1, d), lw["layer"]),
                  _layer_spec((N_BRANCH, BRANCH_WIDTH, d), lw["layer"]), _layer_spec((d, d), lw["layer"])],
        out_specs=pl.BlockSpec((None, tm, d), tok),
        out_shape=jax.ShapeDtypeStruct((b, t, d), F32),
        compiler_params=_params(56, 2), name="merge",
    )(x, mod, *branches, lw["w_gate"], lw["b_gate"], lw["w_branch"], lw["w_out"])


def _ffn_kernel(x_ref, xp_ref, xn_ref, mod_ref, wu_ref, cw_ref, cb_ref, wd_ref, out_ref, h_scr, act_scr):
    tm = x_ref.shape[0]
    i = pl.program_id(0)
    nt = pl.num_programs(0)
    sh, sc = mod_ref[3:4, :], mod_ref[4:5, :]

    def modulated(v):
        return _rms(v) * (1.0 + sc) + sh

    x = x_ref[...]
    h_scr[0:HALO, :] = jnp.where(i > 0, modulated(xp_ref[...]), 0.0).astype(BF16)
    h_scr[HALO:HALO + tm, :] = modulated(x).astype(BF16)
    h_scr[HALO + tm:, :] = jnp.where(i < nt - 1, modulated(xn_ref[...]), 0.0).astype(BF16)
    rows = tm + 2 * HALO
    main = slice(HALO, HALO + tm)

    def cols(j, gate):
        lo = gate * FFN_DIM + FFN_CHUNK * j
        return slice(lo, lo + FFN_CHUNK)

    def up_proj(j):
        return tuple(_dot(h_scr[...], wu_ref[:, cols(j, gate)]) for gate in (0, 1))

    def conv(up, sl):
        return (pltpu.roll(up, 1, 0)[main] * cw_ref[0:1, sl] + up[main] * cw_ref[1:2, sl]
                + pltpu.roll(up, rows - 1, 0)[main] * cw_ref[2:3, sl] + cb_ref[:, sl])

    ups = [up_proj(j) for j in range(FFN_LOOKAHEAD)]
    for j in range(N_FFN_CHUNKS):
        up_a, up_g = ups.pop(0)
        if j + FFN_LOOKAHEAD < N_FFN_CHUNKS:
            ups.append(up_proj(j + FFN_LOOKAHEAD))
        a, g = conv(up_a, cols(j, 0)), conv(up_g, cols(j, 1))
        act_scr[:, cols(j, 0)] = (g * jax.nn.sigmoid(g) * a).astype(BF16)
    out_ref[...] = x + mod_ref[5:6, :] * _dot(act_scr[...], wd_ref[...])


def _ffn_call(x, mod, mod_row, lw, tm):
    b, t, d = x.shape
    nt = t // tm
    hb = tm // HALO
    last_halo = t // HALO - 1
    if mod_row is None:
        mod_map = lambda i, j: (j, 0, 0)
    else:
        mod_map = lambda i, j: (mod_row, 0, 0)
    tok = lambda i, j: (j, i, 0)
    return pl.pallas_call(
        _ffn_kernel, grid=(nt, b),
        in_specs=[pl.BlockSpec((None, tm, d), tok),
                  pl.BlockSpec((None, HALO, d), lambda i, j: (j, jnp.maximum(i * hb - 1, 0), 0)),
                  pl.BlockSpec((None, HALO, d), lambda i, j: (j, jnp.minimum((i + 1) * hb, last_halo), 0)),
                  pl.BlockSpec((None, 6, d), mod_map),
                  _layer_spec((d, 2 * FFN_DIM), lw["layer"]),
                  _layer_spec((3, 2 * FFN_DIM), lw["layer"]),
                  _layer_spec((1, 2 * FFN_DIM), lw["layer"]),
                  _layer_spec((FFN_DIM, d), lw["layer"])],
        out_specs=pl.BlockSpec((None, tm, d), tok),
        out_shape=jax.ShapeDtypeStruct((b, t, d), F32),
        scratch_shapes=[pltpu.VMEM((tm + 2 * HALO, d), BF16), pltpu.VMEM((tm, FFN_DIM), BF16)],
        compiler_params=_params(56, 2), name="ffn",
    )(x, x, x, mod, lw["w_up"], lw["conv_w"], lw["conv_b"], lw["w_down"])


def _block_diag_ones(width, segs):
    m = np.zeros((width, width), np.float32)
    for lo, hi in segs:
        m[lo:hi, lo:hi] = 1.0
    return jnp.asarray(m, BF16)


def _rope_pattern(pos, width):
    half = width // 2
    inv = ROPE_THETA ** (-jnp.arange(half, dtype=F32) / half)
    ang = pos.astype(F32)[:, None] * inv[None, :]
    cos, sin = jnp.cos(ang), jnp.sin(ang)
    return jnp.concatenate([cos, cos], axis=1), jnp.concatenate([-sin, sin], axis=1)


def _rope_tables(t, identity):
    if identity:
        return {"gc": jnp.ones((t, 256), F32), "gs": jnp.zeros((t, 256), F32),
                "mc": jnp.ones((t, LANES), F32), "ms": jnp.zeros((t, LANES), F32)}
    n_rows = t // GRID_W

    def grid_tables(width):
        rc, rs = _rope_pattern(jnp.arange(n_rows), width)
        cc, cs = _rope_pattern(jnp.arange(GRID_W), width)
        expand = lambda r, c: jnp.concatenate([jnp.repeat(r, GRID_W, axis=0), jnp.tile(c, (n_rows, 1))], axis=1)
        return expand(rc, cc), expand(rs, cs)

    gcos, gsin = grid_tables(HEAD_DIM // 2)
    mcos, msin = grid_tables(MLA_ROPE // 2)
    one, zero = jnp.ones((t, MLA_NOPE), F32), jnp.zeros((t, MLA_NOPE), F32)
    pad1, pad0 = jnp.ones((t, LANES - MLA_NOPE - MLA_ROPE), F32), jnp.zeros((t, LANES - MLA_NOPE - MLA_ROPE), F32)
    return {"gc": jnp.tile(gcos, (1, 4)), "gs": jnp.tile(gsin, (1, 4)),
            "mc": jnp.concatenate([one, mcos, pad1], axis=1),
            "ms": jnp.concatenate([zero, msin, pad0], axis=1)}


def _arrange(p):
    d = D_MODEL
    n_layers = p["w_in"].shape[0]
    cat = lambda parts: jnp.concatenate(parts, axis=-1)
    w_in = p["w_in"]
    na, mla = w_in[..., :NA_IN], w_in[..., NA_IN:NA_IN + MLA_IN]
    sg = w_in[..., NA_IN + MLA_IN:NA_IN + MLA_IN + SG_IN]
    gqa = w_in[..., NA_IN + MLA_IN + SG_IN:]
    z = lambda n: jnp.zeros((n_layers, d, n), F32)
    cq, ckv, kr = (mla[..., :MLA_Q_LORA], mla[..., MLA_Q_LORA:MLA_Q_LORA + MLA_KV_LORA],
                   mla[..., MLA_Q_LORA + MLA_KV_LORA:])
    gq, gk, gv = gqa[..., :256], gqa[..., 256:384], gqa[..., 384:]
    gk_dup = cat([gk[..., :64], gk[..., :64], gk[..., 64:], gk[..., 64:]])
    w_in_arr = cat([na, cq, ckv, z(256 - MLA_KV_LORA), z(MLA_NOPE), kr, z(LANES - MLA_NOPE - MLA_ROPE),
                    sg, gq, gk_dup, gv]).astype(BF16)

    lane_pad = lambda v, n: jnp.pad(v, ((0, 0),) * (v.ndim - 1) + ((0, n),))
    w_uq = p["mla_w_uq"].reshape(n_layers, MLA_Q_LORA, MLA_HEADS, MLA_NOPE + MLA_ROPE)
    w_uq = lane_pad(w_uq, LANES - MLA_NOPE - MLA_ROPE).reshape(n_layers, MLA_Q_LORA, MLA_HEADS * LANES)
    w_ukv = p["mla_w_ukv"].reshape(n_layers, MLA_KV_LORA, MLA_HEADS, MLA_NOPE + MLA_V)
    k_part = lane_pad(w_ukv[..., :MLA_NOPE], LANES - MLA_NOPE).reshape(n_layers, MLA_KV_LORA, MLA_HEADS * LANES)
    v_part = w_ukv[..., MLA_NOPE:].reshape(n_layers, MLA_KV_LORA, MLA_HEADS * MLA_V)
    w_ukv = jnp.pad(cat([k_part, v_part]), ((0, 0), (0, 256 - MLA_KV_LORA), (0, 0)))

    qg, kg = p["mla_q_norm"], p["mla_k_norm"]
    z1 = lambda n: jnp.zeros((n_layers, n), F32)
    tile4 = lambda v: jnp.tile(v, (1, 4))
    gains = cat([
        tile4(p["na_q_norm"]), tile4(p["na_k_norm"]),
        p["mla_cq_norm"],
        p["mla_ckv_norm"], z1(256 - MLA_KV_LORA),
        tile4(cat([qg * MLA_LOGIT_SCALE, z1(LANES - MLA_NOPE - MLA_ROPE)])),
        tile4(cat([kg[:, :MLA_NOPE], z1(LANES - MLA_NOPE)])),
        z1(MLA_NOPE), kg[:, MLA_NOPE:], z1(LANES - MLA_NOPE - MLA_ROPE),
        p["sg_v_norm"],
        tile4(p["gqa_q_norm"] * GQA_LOGIT_SCALE), tile4(p["gqa_k_norm"]),
    ]).reshape(n_layers, 1, G_TOT)

    invq = np.tile(np.concatenate([np.full(MLA_NOPE, 1.0 / MLA_NOPE), np.full(MLA_ROPE, 1.0 / MLA_ROPE),
                                   np.ones(LANES - MLA_NOPE - MLA_ROPE)]), 4).astype(np.float32).reshape(1, 512)
    sgb = jnp.repeat(jnp.swapaxes(p["sg_b_s"], -1, -2), SG_WIDTH // SG_GROUPS, axis=-1)

    return {
        "w_in": w_in_arr, "w_uq": w_uq.astype(BF16), "w_ukv": w_ukv.astype(BF16), "gains": gains,
        "s64": _block_diag_ones(MXU_DIM, [(64 * i, 64 * i + 64) for i in range(4)]),
        "smq": _block_diag_ones(MXU_DIM, [(0, 64), (64, 96), (128, 192), (192, 224)]),
        "invq": jnp.asarray(invq),
        "sgw": p["sg_w_s"].astype(BF16), "sgb": sgb,
        "w_gate": p["w_gate"].astype(BF16), "b_gate": p["b_gate"].reshape(n_layers, N_BRANCH, 1, d),
        "w_branch": p["w_branch"].astype(BF16), "w_out": p["w_out"].astype(BF16),
        "w_up": p["w_up"].astype(BF16), "conv_w": p["conv_w"], "conv_b": p["conv_b"].reshape(n_layers, 1, -1),
        "w_down": p["w_down"].astype(BF16),
    }


def kernel(x, c, ctx, c_ctx, w_ada, b_ada, w_in, na_q_norm, na_k_norm, na_rpb, mla_cq_norm, mla_ckv_norm,
           mla_w_uq, mla_w_ukv, mla_q_norm, mla_k_norm, sg_v_norm, sg_w_s, sg_b_s, gqa_q_norm, gqa_k_norm,
           w_branch, w_gate, b_gate, w_out, w_up, conv_w, conv_b, w_down):
    b, t, d = x.shape
    n_ctx = ctx.shape[1]
    depth = w_in.shape[0]
    ctx_row = b
    cvec = jnp.zeros((8, d), F32).at[:b].set(c).at[ctx_row].set(c_ctx)
    mod_all = _ada_call(cvec, w_ada, b_ada).reshape(depth, 8, 6, d)

    assert b < 8 and t % (4 * GRID_W) == 0 and n_ctx % SG_CHUNK == 0 and n_ctx >= SEED_KEYS
    tm = min(TOKEN_TILE, t)
    tq = min(QUERY_BLOCK, t)
    tk = min(KEY_CHUNK, t)
    assert t % tm == 0 and t % tq == 0 and t % tk == 0
    tabs_lat = _rope_tables(t, identity=False)
    tabs_ctx = _rope_tables(n_ctx, identity=True)
    na_bias = _na_bias(na_rpb, t // (4 * GRID_W))

    lw = _arrange({
        "w_in": w_in, "na_q_norm": na_q_norm, "na_k_norm": na_k_norm,
        "mla_cq_norm": mla_cq_norm, "mla_ckv_norm": mla_ckv_norm, "mla_w_uq": mla_w_uq,
        "mla_w_ukv": mla_w_ukv, "mla_q_norm": mla_q_norm, "mla_k_norm": mla_k_norm,
        "sg_v_norm": sg_v_norm, "sg_w_s": sg_w_s, "sg_b_s": sg_b_s,
        "gqa_q_norm": gqa_q_norm, "gqa_k_norm": gqa_k_norm,
        "w_branch": w_branch, "w_gate": w_gate, "b_gate": b_gate, "w_out": w_out,
        "w_up": w_up, "conv_w": conv_w, "conv_b": conv_b, "w_down": w_down})
    cx = ctx
    for l in range(depth):
        lw = dict(lw, layer=l)
        mod = mod_all[l]
        pc = _in_call(cx, mod, ctx_row, lw, tabs_ctx, n_ctx)
        pz = _in_call(x, mod, None, lw, tabs_lat, tm)

        o_na = _na_call(pz["na_q"], pz["na_k"], pz["na_v"], pc["na_k"], pc["na_v"], na_bias, l)
        o_mla = _flash_call(pz["mla_q"], pz["mla_k"], pz["mla_vt"], pc["mla_k"], pc["mla_vt"], tq, tk)
        o_gqa = _flash_call(pz["gqa_q"], pz["gqa_k"], pz["gqa_vt"], pc["gqa_k"], pc["gqa_vt"], tq, tk)
        x = _merge_call(x, mod, None, (o_na, o_mla, pz["sg_o"], o_gqa), lw, tm)
        x = _ffn_call(x, mod, None, lw, tm)

        if l < depth - 1:
            oc_na = _na_ctx_call(pc["na_q"], pc["na_k"], pc["na_v"])
            oc_mla = _flash_call(pc["mla_q"], pc["mla_k"], pc["mla_vt"], None, None, n_ctx, n_ctx)
            oc_gqa = _flash_call(pc["gqa_q"], pc["gqa_k"], pc["gqa_vt"], None, None, n_ctx, n_ctx)
            cx = _merge_call(cx, mod, ctx_row, (oc_na, oc_mla, pc["sg_o"], oc_gqa), lw, n_ctx)
            cx = _ffn_call(cx, mod, ctx_row, lw, n_ctx)
    return x
```
